```python
import math
import jax
import jax.numpy as jnp
from jax import lax
import numpy as np

D_MODEL = 1024
BATCH = 32
SEQ = 256
DEPTH = 2
DEC_BATCH = 4
DEC_SEQ = 4096
PAST_LEN = 256

GRID_W = 64
D_MIX = D_MODEL
SSM_WIDTH = D_MIX // 4
SSM_CH_PER_GROUP = 16
SSM_GROUPS = SSM_WIDTH // SSM_CH_PER_GROUP
SSM_STATE = 64
NA_WIDTH = D_MIX // 2
NA_HEAD_DIM = 64
NA_HEADS = NA_WIDTH // NA_HEAD_DIM
NA_MAX_ROWS = 8
NA_COLS = 16
GM_WIDTH = D_MIX - SSM_WIDTH - NA_WIDTH
GM_GROUPS = 4
GM_CHUNK = 128
D_FF = 2816
N_MOD = 9
IN_COLS = SSM_WIDTH + 3 * NA_WIDTH + 2 * GM_WIDTH
IN_SPLITS = (SSM_WIDTH, SSM_WIDTH + NA_WIDTH, SSM_WIDTH + 2 * NA_WIDTH, SSM_WIDTH + 3 * NA_WIDTH)
ATTN_Q_BLOCK = 128
RMS_EPS = 1e-6
LN_EPS = 1e-5
NEG_INF = -1e30

kernel_name = 'hybrid_diffusion_prefix_step'


def rmsnorm(x, g):
    x32 = x.astype(jnp.float32)
    y = x32 * lax.rsqrt(jnp.mean(x32 * x32, axis=-1, keepdims=True) + RMS_EPS)
    return (y * g.astype(jnp.float32)).astype(x.dtype)


def layernorm(x):
    x32 = x.astype(jnp.float32)
    xc = x32 - jnp.mean(x32, axis=-1, keepdims=True)
    var = jnp.mean(xc * xc, axis=-1, keepdims=True)
    return (xc * lax.rsqrt(var + LN_EPS)).astype(x.dtype)


def modulate(h, shift, scale):
    return h * (1.0 + scale) + shift


def swiglu_ffn(h, w_in, w_out):
    gate, up = jnp.split(h @ w_in, 2, axis=-1)
    return (jax.nn.silu(gate) * up) @ w_out


def adaln(cond, w_ada, b_ada):
    mod = jax.nn.silu(cond) @ w_ada + b_ada
    return mod.reshape(cond.shape[0], N_MOD, D_MODEL)


def _linear_recurrence(e1, e2):
    a1, b1 = e1
    a2, b2 = e2
    return a1 * a2, a2 * b1 + b2


def ssm_mixer(x_ssm, lp, init_state):
    f32 = jnp.float32
    b_, seq_len, _ = x_ssm.shape
    u = x_ssm.astype(f32).reshape(b_, seq_len, SSM_GROUPS, SSM_CH_PER_GROUP)
    uc = u.astype(jnp.complex64)
    y = lp['ssm_d'].astype(f32).reshape(SSM_GROUPS, SSM_CH_PER_GROUP) * u
    finals = []
    for d in range(2):
        reverse = d == 1
        lam = lax.complex(lp['ssm_lambda_re'][d].astype(f32), lp['ssm_lambda_im'][d].astype(f32))
        dt = jnp.exp(lp['ssm_log_dt'][d].astype(f32))[:, None]
        lbar = jnp.exp(lam * dt)
        b_mat = lax.complex(lp['ssm_b_re'][d].astype(f32), lp['ssm_b_im'][d].astype(f32))
        bbar = ((lbar - 1.0) / lam)[..., None] * b_mat
        c_mat = lax.complex(lp['ssm_c_re'][d].astype(f32), lp['ssm_c_im'][d].astype(f32))
        bu = jnp.einsum('gpc,blgc->blgp', bbar, uc)
        if init_state is not None:
            s0 = lax.complex(init_state[:, d, ..., 0].astype(f32), init_state[:, d, ..., 1].astype(f32))
            edge = seq_len - 1 if reverse else 0
            bu = bu.at[:, edge].add(lbar * s0)
        a = jnp.broadcast_to(lbar, bu.shape)
        _, s = lax.associative_scan(_linear_recurrence, (a, bu), axis=1, reverse=reverse)
        y = y + jnp.real(jnp.einsum('gcp,blgp->blgc', c_mat, s))
        if init_state is None:
            fin = s[:, 0] if reverse else s[:, seq_len - 1]
            finals.append(jnp.stack([jnp.real(fin), jnp.imag(fin)], axis=-1))
    y = jax.nn.gelu(y.reshape(b_, seq_len, SSM_WIDTH))
    y = y * jax.nn.sigmoid(y @ lp['ssm_glu_w'].astype(f32) + lp['ssm_glu_b'].astype(f32))
    final_state = jnp.stack(finals, axis=1) if init_state is None else None
    return y.astype(x_ssm.dtype), final_state


def dense_attention(q, k, v):
    b_, s_len, h_, dh = q.shape
    nb = s_len // ATTN_Q_BLOCK
    qb = jnp.moveaxis(q.reshape(b_, nb, ATTN_Q_BLOCK, h_, dh), 1, 0)
    scale = dh ** -0.5

    def block(qi):
        s = jnp.einsum('bqhd,bkhd->bhqk', qi, k).astype(jnp.float32) * scale
        p = jax.nn.softmax(s, axis=-1).astype(v.dtype)
        return jnp.einsum('bhqk,bkhd->bqhd', p, v)

    out = lax.map(block, qb)
    return jnp.moveaxis(out, 0, 1).reshape(b_, s_len, h_ * dh)


def neighbourhood_attention(q, k, v, k_ctx, v_ctx, rpb):
    b_, seq_len, h_, dh = q.shape
    rows = seq_len // GRID_W
    kh = min(NA_MAX_ROWS, rows)
    kw = NA_COLS
    kb = 2 * kw
    nb = GRID_W // kw
    qcol = np.arange(GRID_W).reshape(nb, kw)
    kc0 = np.clip(np.arange(nb) * kw - kw // 2, 0, GRID_W - kb)
    kcol = kc0[:, None] + np.arange(kb)[None, :]
    cs = np.clip(qcol - kw // 2, 0, GRID_W - kw)
    win = (kcol[:, None, :] >= cs[:, :, None]) & (kcol[:, None, :] < cs[:, :, None] + kw)
    dc = np.clip(kcol[:, None, :] - qcol[:, :, None], -(kw - 1), kw - 1) + (kw - 1)
    bias_c = rpb[:, :, dc]
    q_g = q.reshape(b_, rows, GRID_W, h_, dh)
    k_g = k.reshape(b_, rows, GRID_W, h_, dh)
    v_g = v.reshape(b_, rows, GRID_W, h_, dh)
    scale = dh ** -0.5

    def row_block(r):
        rs = jnp.clip(r - kh // 2, 0, rows - kh)
        k_blk = lax.dynamic_slice_in_dim(k_g, rs, kh, axis=1)[:, :, kcol]
        v_blk = lax.dynamic_slice_in_dim(v_g, rs, kh, axis=1)[:, :, kcol]
        q_r = lax.dynamic_index_in_dim(q_g, r, axis=1, keepdims=False).reshape(b_, nb, kw, h_, dh)
        s_win = jnp.einsum('bnqhd,bknchd->bhnqkc', q_r, k_blk).astype(jnp.float32) * scale
        dr = rs + jnp.arange(kh) - r + (NA_MAX_ROWS - 1)
        bias = jnp.transpose(jnp.take(bias_c, dr, axis=1), (0, 2, 3, 1, 4))
        s_win = jnp.where(win[None, None, :, :, None, :], s_win + bias[None].astype(jnp.float32), NEG_INF)
        s_ctx = jnp.einsum('bnqhd,bkhd->bhnqk', q_r, k_ctx).astype(jnp.float32) * scale
        logits = jnp.concatenate([s_win.reshape(b_, h_, nb, kw, kh * kb), s_ctx], axis=-1)
        p = jax.nn.softmax(logits, axis=-1).astype(v.dtype)
        p_win = p[..., :kh * kb].reshape(b_, h_, nb, kw, kh, kb)
        p_ctx = p[..., kh * kb:]
        o = jnp.einsum('bhnqkc,bknchd->bnqhd', p_win, v_blk) + jnp.einsum('bhnqk,bkhd->bnqhd', p_ctx, v_ctx)
        return o.reshape(b_, GRID_W, h_, dh)

    out = lax.map(row_block, jnp.arange(rows))
    return jnp.moveaxis(out, 0, 1).reshape(b_, seq_len, h_ * dh)


def spatial_gating(uv, ws, bs):
    b_, seq_len, _ = uv.shape
    u, v = jnp.split(jax.nn.gelu(uv), 2, axis=-1)
    v = layernorm(v)
    vc = v.reshape(b_, seq_len // GM_CHUNK, GM_CHUNK, GM_GROUPS, GM_WIDTH // GM_GROUPS)
    sp = jnp.einsum('gij,bnjgc->bnigc', ws, vc) + jnp.transpose(bs)[:, :, None]
    return u * sp.reshape(b_, seq_len, GM_WIDTH)


def token_mixing(h, lp, ctx_kv, ssm_init):
    b_, seq_len, _ = h.shape
    x_ssm, q, k, v, uv = jnp.split(h @ lp['w_in'], IN_SPLITS, axis=-1)
    y_ssm, ssm_final = ssm_mixer(x_ssm, lp, ssm_init)
    q = rmsnorm(q.reshape(b_, seq_len, NA_HEADS, NA_HEAD_DIM), lp['na_q_norm'])
    k = rmsnorm(k.reshape(b_, seq_len, NA_HEADS, NA_HEAD_DIM), lp['na_k_norm'])
    v = v.reshape(b_, seq_len, NA_HEADS, NA_HEAD_DIM)
    if ctx_kv is None:
        y_na = dense_attention(q, k, v)
    else:
        y_na = neighbourhood_attention(q, k, v, ctx_kv[0], ctx_kv[1], lp['na_rpb'])
    y_gm = spatial_gating(uv, lp['gm_ws'], lp['gm_bs'])
    out = jnp.concatenate([y_ssm, y_na, y_gm], axis=-1) @ lp['w_out']
    ctx_state = (k, v, ssm_final) if ctx_kv is None else None
    return out, ctx_state


def trunk_layer(x, mod, lp, ctx_kv, ssm_init):
    mod = mod.astype(x.dtype)
    sh1, sc1, g1, sh2, sc2, g2, sh3, sc3, g3 = [mod[:, i][:, None, :] for i in range(N_MOD)]
    h = modulate(rmsnorm(x, lp['norm_ffn1']), sh1, sc1)
    x = x + 0.5 * g1 * swiglu_ffn(h, lp['ffn1_w_in'], lp['ffn1_w_out'])
    h = modulate(rmsnorm(x, lp['norm_mix']), sh2, sc2)
    mix, ctx_state = token_mixing(h, lp, ctx_kv, ssm_init)
    x = x + g2 * mix
    h = modulate(rmsnorm(x, lp['norm_ffn2']), sh3, sc3)
    x = x + 0.5 * g3 * swiglu_ffn(h, lp['ffn2_w_in'], lp['ffn2_w_out'])
    return x, ctx_state


def setup_inputs(seed: int = 0) -> dict:
    key = jax.random.key(seed)
    ks = jax.random.split(key, 40)
    f32 = jnp.float32

    def nrm(k, shape, scale):
        return jax.random.normal(k, shape, f32) * scale

    def gain(k, shape):
        return 1.0 + 0.01 * jax.random.normal(k, shape, f32)

    ssm_shape = (DEPTH, 2, SSM_GROUPS, SSM_STATE)
    n_idx = jnp.arange(SSM_STATE, dtype=f32)
    return {
        'x_prompt': nrm(ks[0], (BATCH, SEQ, D_MODEL), 1.0),
        'x_sample': nrm(ks[1], (DEC_BATCH, DEC_SEQ, D_MODEL), 1.0),
        'c': nrm(ks[2], (DEC_BATCH, D_MODEL), 1.0),
        'cache_k': nrm(ks[3], (DEC_BATCH, DEPTH, PAST_LEN, NA_HEADS, NA_HEAD_DIM), 1.0),
        'cache_v': nrm(ks[4], (DEC_BATCH, DEPTH, PAST_LEN, NA_HEADS, NA_HEAD_DIM), 1.0),
        'state_ssm': nrm(ks[5], (DEC_BATCH, DEPTH, 2, SSM_GROUPS, SSM_STATE, 2), 0.1),
        'c_ctx': nrm(ks[6], (D_MODEL,), 1.0),
        'w_ada': nrm(ks[7], (DEPTH, D_MODEL, N_MOD * D_MODEL), 0.02),
        'b_ada': nrm(ks[8], (DEPTH, N_MOD * D_MODEL), 0.02),
        'norm_ffn1': gain(ks[9], (DEPTH, D_MODEL)),
        'ffn1_w_in': nrm(ks[10], (DEPTH, D_MODEL, 2 * D_FF), D_MODEL ** -0.5),
        'ffn1_w_out': nrm(ks[11], (DEPTH, D_FF, D_MODEL), D_FF ** -0.5),
        'norm_mix': gain(ks[12], (DEPTH, D_MODEL)),
        'w_in': nrm(ks[13], (DEPTH, D_MODEL, IN_COLS), D_MODEL ** -0.5),
        'w_out': nrm(ks[14], (DEPTH, D_MIX, D_MODEL), D_MIX ** -0.5),
        'ssm_lambda_re': -0.5 + nrm(ks[15], ssm_shape, 0.01),
        'ssm_lambda_im': math.pi * n_idx + nrm(ks[16], ssm_shape, 0.01),
        'ssm_log_dt': jax.random.uniform(ks[17], (DEPTH, 2, SSM_GROUPS), f32, math.log(1e-3), math.log(1e-1)),
        'ssm_b_re': nrm(ks[18], (DEPTH, 2, SSM_GROUPS, SSM_STATE, SSM_CH_PER_GROUP), (2 * SSM_CH_PER_GROUP) ** -0.5),
        'ssm_b_im': nrm(ks[19], (DEPTH, 2, SSM_GROUPS, SSM_STATE, SSM_CH_PER_GROUP), (2 * SSM_CH_PER_GROUP) ** -0.5),
        'ssm_c_re': nrm(ks[20], (DEPTH, 2, SSM_GROUPS, SSM_CH_PER_GROUP, SSM_STATE), (2 * SSM_STATE) ** -0.5),
        'ssm_c_im': nrm(ks[21], (DEPTH, 2, SSM_GROUPS, SSM_CH_PER_GROUP, SSM_STATE), (2 * SSM_STATE) ** -0.5),
        'ssm_d': nrm(ks[22], (DEPTH, SSM_WIDTH), 1.0),
        'ssm_glu_w': nrm(ks[23], (DEPTH, SSM_WIDTH, SSM_WIDTH), SSM_WIDTH ** -0.5),
        'ssm_glu_b': nrm(ks[24], (DEPTH, SSM_WIDTH), 0.02),
        'na_q_norm': gain(ks[25], (DEPTH, NA_HEAD_DIM)),
        'na_k_norm': gain(ks[26], (DEPTH, NA_HEAD_DIM)),
        'na_rpb': nrm(ks[27], (DEPTH, NA_HEADS, 2 * NA_MAX_ROWS - 1, 2 * NA_COLS - 1), 0.1),
        'gm_ws': nrm(ks[28], (DEPTH, GM_GROUPS, GM_CHUNK, GM_CHUNK), GM_CHUNK ** -0.5),
        'gm_bs': gain(ks[29], (DEPTH, GM_GROUPS, GM_CHUNK)),
        'norm_ffn2': gain(ks[30], (DEPTH, D_MODEL)),
        'ffn2_w_in': nrm(ks[31], (DEPTH, D_MODEL, 2 * D_FF), D_MODEL ** -0.5),
        'ffn2_w_out': nrm(ks[32], (DEPTH, D_FF, D_MODEL), D_FF ** -0.5),
    }


def reference(x_prompt, x_sample, c, cache_k, cache_v, state_ssm, c_ctx, w_ada, b_ada,
              norm_ffn1, ffn1_w_in, ffn1_w_out, norm_mix, w_in, w_out,
              ssm_lambda_re, ssm_lambda_im, ssm_log_dt, ssm_b_re, ssm_b_im, ssm_c_re, ssm_c_im,
              ssm_d, ssm_glu_w, ssm_glu_b, na_q_norm, na_k_norm, na_rpb, gm_ws, gm_bs,
              norm_ffn2, ffn2_w_in, ffn2_w_out):
    stacked = {
        'norm_ffn1': norm_ffn1, 'ffn1_w_in': ffn1_w_in, 'ffn1_w_out': ffn1_w_out,
        'norm_mix': norm_mix, 'w_in': w_in, 'w_out': w_out,
        'ssm_lambda_re': ssm_lambda_re, 'ssm_lambda_im': ssm_lambda_im, 'ssm_log_dt': ssm_log_dt,
        'ssm_b_re': ssm_b_re, 'ssm_b_im': ssm_b_im, 'ssm_c_re': ssm_c_re, 'ssm_c_im': ssm_c_im,
        'ssm_d': ssm_d, 'ssm_glu_w': ssm_glu_w, 'ssm_glu_b': ssm_glu_b,
        'na_q_norm': na_q_norm, 'na_k_norm': na_k_norm, 'na_rpb': na_rpb,
        'gm_ws': gm_ws, 'gm_bs': gm_bs,
        'norm_ffn2': norm_ffn2, 'ffn2_w_in': ffn2_w_in, 'ffn2_w_out': ffn2_w_out,
    }
    xp = x_prompt
    xs = x_sample
    new_k, new_v, new_s = [], [], []
    for l in range(DEPTH):
        lp = {name: arr[l] for name, arr in stacked.items()}
        mod_ctx = adaln(c_ctx[None, :], w_ada[l], b_ada[l])
        xp, (k_c, v_c, s_c) = trunk_layer(xp, mod_ctx, lp, None, None)
        new_k.append(k_c)
        new_v.append(v_c)
        new_s.append(s_c)
        mod_lat = adaln(c, w_ada[l], b_ada[l])
        xs, _ = trunk_layer(xs, mod_lat, lp, (cache_k[:, l], cache_v[:, l]), state_ssm[:, l])
    new_cache_k = jnp.stack(new_k, axis=1)
    new_cache_v = jnp.stack(new_v, axis=1)
    new_state_ssm = jnp.stack(new_s, axis=1)
    return (xp, xs, new_cache_k, new_cache_v, new_state_ssm)
```

```python
import functools
import math

import numpy as np
import jax
import jax.numpy as jnp
from jax import lax
from jax.experimental import pallas as pl
from jax.experimental.pallas import tpu as pltpu

D_MODEL = 1024
DEPTH = 2
SEQ = 256
DEC_SEQ = 4096
PAST_LEN = 256
GRID_W = 64
SSM_WIDTH = 256
SSM_CH = 16
SSM_GROUPS = 16
SSM_STATE = 64
NA_WIDTH = 512
NA_HEAD_DIM = 64
NA_HEADS = 8
NA_MAX_ROWS = 8
NA_COLS = 16
GM_WIDTH = 256
GM_GROUPS = 4
GM_CHUNK = 128
D_FF = 2816
N_MOD = 9
RMS_EPS = 1e-6
LN_EPS = 1e-5
NEG_INF = -1e30

F32 = jnp.float32
BF16 = jnp.bfloat16

LANES = 128
MXU_DIM = 256
VMEM_LIMIT_BYTES = 56 * 1024 * 1024

TOKEN_TILE = 512
FF_CHUNK = MXU_DIM
N_FF_CHUNKS = D_FF // FF_CHUNK
SSM_CHUNK = 16
SSM_ROW = SSM_CHUNK * SSM_CH
SSM_GROUP_BLOCK = 2
HEAD_PAIRS = NA_HEADS // 2
NA_Q_ROWS = 8
NA_K_ROWS = 16
NA_KCOLS = 2 * NA_COLS
NA_COL_BLOCKS = GRID_W // NA_COLS
NA_KC0 = (0, 8, 24, 32)


def _silu(x):
    return x * (1.0 / (1.0 + jnp.exp(-x)))


def _sigmoid(x):
    return 1.0 / (1.0 + jnp.exp(-x))


def _gelu(x):
    return 0.5 * x * (1.0 + jnp.tanh(0.7978845608028654 * (x + 0.044715 * (x * x * x))))


def _cparams(sem):
    return pltpu.CompilerParams(dimension_semantics=sem, vmem_limit_bytes=VMEM_LIMIT_BYTES)


def _const_spec(shape):
    nd = len(shape)
    return pl.BlockSpec(shape, lambda *_: (0,) * nd, pipeline_mode=pl.Buffered(1))


def _ada_kernel(c_ref, w_ref, b_ref, o_ref):
    s = _silu(c_ref[...]).astype(BF16)
    w = w_ref[0].astype(BF16)
    o_ref[0] = jnp.dot(s, w, preferred_element_type=F32) + b_ref[0]


def _adaln(cond8, w_ada, b_ada):
    tn = D_MODEL
    ncol = N_MOD * D_MODEL
    return pl.pallas_call(
        _ada_kernel,
        grid=(DEPTH, ncol // tn),
        in_specs=[
            pl.BlockSpec((8, D_MODEL), lambda l, j: (0, 0)),
            pl.BlockSpec((1, D_MODEL, tn), lambda l, j: (l, 0, j)),
            pl.BlockSpec((1, 1, tn), lambda l, j: (l, 0, j)),
        ],
        out_specs=pl.BlockSpec((1, 8, tn), lambda l, j: (l, 0, j)),
        out_shape=jax.ShapeDtypeStruct((DEPTH, 8, ncol), F32),
        compiler_params=_cparams(("arbitrary", "arbitrary")),
        name="adaln",
    )(cond8, w_ada, b_ada.reshape(DEPTH, 1, ncol))


def _normed(x, g_row, shift, scale):
    ms = jnp.mean(x * x, axis=-1, keepdims=True)
    h = x * lax.rsqrt(ms + RMS_EPS) * g_row
    return h * (1.0 + scale) + shift


def _seg_map(seg0, seg_tokens):
    tiles_per_seg = seg_tokens // TOKEN_TILE
    return lambda i: (seg0 + i // tiles_per_seg, 0, 0)


def _ffn_kernel(x_ref, mod_ref, g_ref, win_ref, wout_ref, o_ref, h_ref, act_ref, *, mod_base):
    x = x_ref[...]
    shift = mod_ref[0, mod_base:mod_base + 1, :]
    scale = mod_ref[0, mod_base + 1:mod_base + 2, :]
    gate = mod_ref[0, mod_base + 2:mod_base + 3, :]
    h_ref[...] = _normed(x, g_ref[...], shift, scale).astype(BF16)
    for j in range(N_FF_CHUNKS):
        gu = jnp.dot(h_ref[...], win_ref[j], preferred_element_type=F32)
        a = _silu(gu[:, :FF_CHUNK]) * gu[:, FF_CHUNK:]
        act_ref[:, j * FF_CHUNK:(j + 1) * FF_CHUNK] = a.astype(BF16)
    for n in range(D_MODEL // MXU_DIM):
        cols = slice(n * MXU_DIM, (n + 1) * MXU_DIM)
        y = jnp.dot(act_ref[...], wout_ref[:, cols], preferred_element_type=F32)
        o_ref[:, cols] = x_ref[:, cols] + 0.5 * gate[:, cols] * y


def _ffn(x, mod_l, g, w_in3, w_out, *, mod_base, seg0, seg_tokens):
    n_tok = x.shape[0]
    return pl.pallas_call(
        functools.partial(_ffn_kernel, mod_base=mod_base),
        grid=(n_tok // TOKEN_TILE,),
        in_specs=[
            pl.BlockSpec((TOKEN_TILE, D_MODEL), lambda i: (i, 0)),
            pl.BlockSpec((1, N_MOD, D_MODEL), _seg_map(seg0, seg_tokens)),
            _const_spec((1, D_MODEL)),
            _const_spec((N_FF_CHUNKS, D_MODEL, 2 * FF_CHUNK)),
            _const_spec((D_FF, D_MODEL)),
        ],
        out_specs=pl.BlockSpec((TOKEN_TILE, D_MODEL), lambda i: (i, 0)),
        out_shape=jax.ShapeDtypeStruct((n_tok, D_MODEL), F32),
        scratch_shapes=[pltpu.VMEM((TOKEN_TILE, D_MODEL), BF16), pltpu.VMEM((TOKEN_TILE, D_FF), BF16)],
        compiler_params=_cparams(("arbitrary",)),
        name="ffn",
    )(x, mod_l, g, w_in3, w_out)


def _inproj_kernel(x_ref, mod_ref, g_ref, w_ref, hsum_ref, qg_ref, kg_ref, ws_ref, gmb_ref,
                   xs_ref, q_ref, k_ref, v_ref, gm_ref, *tok_refs, h_ref):
    x = x_ref[...]
    shift = mod_ref[0, 3:4, :]
    scale = mod_ref[0, 4:5, :]
    h_ref[...] = _normed(x, g_ref[...], shift, scale).astype(BF16)

    def proj(c0, width):
        return jnp.dot(h_ref[...], w_ref[:, c0:c0 + width], preferred_element_type=F32)

    xs_ref[...] = proj(0, SSM_WIDTH)

    def head_norm(t, gain_row):
        ms = jnp.dot((t * t).astype(BF16), hsum_ref[...], preferred_element_type=F32)
        return t * lax.rsqrt(ms + RMS_EPS) * gain_row

    q = head_norm(proj(SSM_WIDTH, NA_WIDTH), qg_ref[...]) * (NA_HEAD_DIM ** -0.5)
    k = head_norm(proj(SSM_WIDTH + NA_WIDTH, NA_WIDTH), kg_ref[...])
    v = proj(SSM_WIDTH + 2 * NA_WIDTH, NA_WIDTH)
    for p in range(HEAD_PAIRS):
        lanes = slice(p * LANES, (p + 1) * LANES)
        q_ref[p] = q[:, lanes].astype(q_ref.dtype)
        k_ref[p] = k[:, lanes].astype(k_ref.dtype)
        v_ref[p] = v[:, lanes].astype(v_ref.dtype)
    if tok_refs:
        tok_refs[0][...] = k
        tok_refs[1][...] = v

    a = _gelu(proj(SSM_WIDTH + 3 * NA_WIDTH, 2 * GM_WIDTH))
    u = a[:, :GM_WIDTH]
    vv = a[:, GM_WIDTH:]
    mu = jnp.mean(vv, axis=-1, keepdims=True)
    vc = vv - mu
    var = jnp.mean(vc * vc, axis=-1, keepdims=True)
    vln = (vc * lax.rsqrt(var + LN_EPS)).astype(BF16)
    lane = lax.broadcasted_iota(jnp.int32, (1, GM_WIDTH), 1)
    gw = GM_WIDTH // GM_GROUPS
    for ci in range(TOKEN_TILE // GM_CHUNK):
        rows = slice(ci * GM_CHUNK, (ci + 1) * GM_CHUNK)
        vch = vln[rows, :]
        sp = gmb_ref[...]
        for gi in range(GM_GROUPS):
            t = jnp.dot(ws_ref[gi], vch, preferred_element_type=F32)
            sp = sp + jnp.where((lane >= gi * gw) & (lane < (gi + 1) * gw), t, 0.0)
        gm_ref[rows, :] = (u[rows, :] * sp).astype(BF16)


def _inproj(x, mod_l, g, w, hsum, qg, kg, ws, gmb, *, seg0, seg_tokens, is_ctx):
    n_tok = x.shape[0]
    kv_dtype = BF16 if is_ctx else F32
    pair_spec = pl.BlockSpec((HEAD_PAIRS, TOKEN_TILE, LANES), lambda i: (0, i, 0))
    out_specs = [
        pl.BlockSpec((TOKEN_TILE, SSM_WIDTH), lambda i: (i, 0)),
        pair_spec, pair_spec, pair_spec,
        pl.BlockSpec((TOKEN_TILE, GM_WIDTH), lambda i: (i, 0)),
    ]
    out_shape = [
        jax.ShapeDtypeStruct((n_tok, SSM_WIDTH), F32),
        jax.ShapeDtypeStruct((HEAD_PAIRS, n_tok, LANES), BF16),
        jax.ShapeDtypeStruct((HEAD_PAIRS, n_tok, LANES), kv_dtype),
        jax.ShapeDtypeStruct((HEAD_PAIRS, n_tok, LANES), kv_dtype),
        jax.ShapeDtypeStruct((n_tok, GM_WIDTH), BF16),
    ]
    if is_ctx:
        out_specs += [pl.BlockSpec((TOKEN_TILE, NA_WIDTH), lambda i: (i, 0))] * 2
        out_shape += [jax.ShapeDtypeStruct((n_tok, NA_WIDTH), F32)] * 2
    in_cols = w.shape[1]
    return pl.pallas_call(
        functools.partial(_inproj_wrapper, n_out=len(out_shape)),
        grid=(n_tok // TOKEN_TILE,),
        in_specs=[
            pl.BlockSpec((TOKEN_TILE, D_MODEL), lambda i: (i, 0)),
            pl.BlockSpec((1, N_MOD, D_MODEL), _seg_map(seg0, seg_tokens)),
            _const_spec((1, D_MODEL)),
            _const_spec((D_MODEL, in_cols)),
            _const_spec((NA_WIDTH, NA_WIDTH)),
            _const_spec((1, NA_WIDTH)),
            _const_spec((1, NA_WIDTH)),
            _const_spec((GM_GROUPS, GM_CHUNK, GM_CHUNK)),
            _const_spec((GM_CHUNK, GM_WIDTH)),
        ],
        out_specs=out_specs,
        out_shape=out_shape,
        scratch_shapes=[pltpu.VMEM((TOKEN_TILE, D_MODEL), BF16)],
        compiler_params=_cparams(("arbitrary",)),
        name="inproj_ctx" if is_ctx else "inproj_lat",
    )(x, mod_l, g, w, hsum, qg, kg, ws, gmb)


def _inproj_wrapper(*refs, n_out):
    ins = refs[:9]
    outs = refs[9:9 + n_out]
    (h_ref,) = refs[9 + n_out:]
    _inproj_kernel(*ins, *outs, h_ref=h_ref)


def _ssm_kernel(u_ref, mp_ref, q_ref, ab_ref, dvec_ref, s0_ref, y_ref, fin_ref, w_ref, sq_ref, *, n_chunks, bp):
    gb = u_ref.shape[0]
    n_rows = n_chunks * bp
    row_blk = min(n_rows, 512)
    for g in range(gb):
        for r0 in range(0, n_rows, row_blk):
            rows = slice(r0, r0 + row_blk)
            ug = u_ref[g, rows, :]
            wy = jnp.dot(ug.astype(BF16), mp_ref[g], preferred_element_type=F32)
            y_ref[g, rows, :] = wy[:, :SSM_ROW] + ug * dvec_ref[g]
            w_ref[g, rows, :] = wy[:, SSM_ROW:]

    a_f, b_f, bp_f = ab_ref[:, 0:1, :], ab_ref[:, 1:2, :], ab_ref[:, 2:3, :]
    a_b, b_b, bp_b = ab_ref[:, 3:4, :], ab_ref[:, 4:5, :], ab_ref[:, 5:6, :]

    def step(n, carry):
        sf, sfp, sb, sbp = carry
        rf = pl.multiple_of(n * bp, bp)
        rb = pl.multiple_of((n_chunks - 1 - n) * bp, bp)
        sq_ref[:, pl.ds(rf, bp), 0:LANES] = sf
        sq_ref[:, pl.ds(rb, bp), LANES:2 * LANES] = sb
        wf = w_ref[:, pl.ds(rf, bp), 0:LANES]
        wfp = w_ref[:, pl.ds(rf, bp), LANES:2 * LANES]
        wb = w_ref[:, pl.ds(rb, bp), 2 * LANES:3 * LANES]
        wbp = w_ref[:, pl.ds(rb, bp), 3 * LANES:4 * LANES]
        return (a_f * sf + b_f * sfp + wf, a_f * sfp + bp_f * sf + wfp,
                a_b * sb + b_b * sbp + wb, a_b * sbp + bp_b * sb + wbp)

    init = (s0_ref[:, 0], s0_ref[:, 1], s0_ref[:, 2], s0_ref[:, 3])
    sf, _, sb, _ = lax.fori_loop(0, n_chunks, step, init, unroll=4)
    fin_ref[:, 0] = sf
    fin_ref[:, 1] = sb

    for g in range(gb):
        for r0 in range(0, n_rows, row_blk):
            rows = slice(r0, r0 + row_blk)
            y_ref[g, rows, :] += jnp.dot(sq_ref[g, rows, :].astype(BF16), q_ref[g], preferred_element_type=F32)


def _ssm_core(u, mp, q, ab, dvec, s0, *, n_chunks, bp):
    n_rows = n_chunks * bp
    gb = SSM_GROUP_BLOCK
    g3 = lambda i: (i, 0, 0)
    g4 = lambda i: (i, 0, 0, 0)
    return pl.pallas_call(
        functools.partial(_ssm_kernel, n_chunks=n_chunks, bp=bp),
        grid=(SSM_GROUPS // gb,),
        in_specs=[
            pl.BlockSpec((gb, n_rows, SSM_ROW), g3),
            pl.BlockSpec((gb, SSM_ROW, SSM_ROW + 4 * LANES), g3),
            pl.BlockSpec((gb, 2 * LANES, SSM_ROW), g3),
            pl.BlockSpec((gb, 8, LANES), g3),
            pl.BlockSpec((gb, 1, SSM_ROW), g3),
            pl.BlockSpec((gb, 4, bp, LANES), g4),
        ],
        out_specs=[
            pl.BlockSpec((gb, n_rows, SSM_ROW), g3),
            pl.BlockSpec((gb, 2, bp, LANES), g4),
        ],
        out_shape=[
            jax.ShapeDtypeStruct((SSM_GROUPS, n_rows, SSM_ROW), F32),
            jax.ShapeDtypeStruct((SSM_GROUPS, 2, bp, LANES), F32),
        ],
        scratch_shapes=[pltpu.VMEM((gb, n_rows, 4 * LANES), F32), pltpu.VMEM((gb, n_rows, 2 * LANES), F32)],
        compiler_params=_cparams(("arbitrary",)),
        name="ssm_core",
    )(u, mp, q, ab, dvec, s0)


def _ssm_params(lp):
    t = SSM_CHUNK
    mats, pcols, qrows, abrows = 0.0, [], [], []
    n_idx = np.arange(t)
    for d in range(2):
        lam = lax.complex(lp['ssm_lambda_re'][d].astype(F32), lp['ssm_lambda_im'][d].astype(F32))
        dt = jnp.exp(lp['ssm_log_dt'][d].astype(F32))[:, None]
        lbar = jnp.exp(lam * dt)
        b_mat = lax.complex(lp['ssm_b_re'][d].astype(F32), lp['ssm_b_im'][d].astype(F32))
        bbar = ((lbar - 1.0) / lam)[..., None] * b_mat
        c_mat = lax.complex(lp['ssm_c_re'][d].astype(F32), lp['ssm_c_im'][d].astype(F32))
        pw = jnp.exp((lam * dt)[None] * jnp.arange(t + 1, dtype=F32)[:, None, None])
        kern = jnp.real(jnp.einsum('gcp,ngp,gpd->ngcd', c_mat, pw[:t], bbar, precision='highest'))
        lag = (n_idx[None, :] - n_idx[:, None]) if d == 0 else (n_idx[:, None] - n_idx[None, :])
        m = jnp.where((lag >= 0)[:, :, None, None, None], kern[np.clip(lag, 0, t - 1)], 0.0)
        mats = mats + jnp.transpose(m, (2, 0, 4, 1, 3)).reshape(SSM_GROUPS, SSM_ROW, SSM_ROW)
        pexp = (t - 1 - n_idx) if d == 0 else n_idx
        pm = pw[pexp][:, :, :, None] * bbar[None]
        pm = jnp.transpose(pm, (1, 0, 3, 2)).reshape(SSM_GROUPS, SSM_ROW, SSM_STATE)
        pcols += [jnp.real(pm), jnp.imag(pm), jnp.imag(pm), jnp.real(pm)]
        qexp = (n_idx + 1) if d == 0 else (t - n_idx)
        qm = c_mat[None] * jnp.transpose(pw[qexp], (0, 1, 2))[:, :, None, :]
        qm = jnp.transpose(qm, (1, 3, 0, 2)).reshape(SSM_GROUPS, SSM_STATE, SSM_ROW)
        qrows += [jnp.real(qm), -jnp.imag(qm)]
        l16 = pw[t]
        ar, ai = jnp.real(l16), jnp.imag(l16)
        abrows += [jnp.concatenate([ar, ar], -1), jnp.concatenate([-ai, ai], -1), jnp.concatenate([ai, -ai], -1)]
    mp = jnp.concatenate([mats] + pcols, axis=-1).astype(BF16)
    q = jnp.concatenate(qrows, axis=1).astype(BF16)
    ab = jnp.stack(abrows + [jnp.zeros_like(abrows[0])] * 2, axis=1)
    dvec = jnp.tile(lp['ssm_d'].astype(F32).reshape(SSM_GROUPS, 1, SSM_CH), (1, 1, SSM_CHUNK))
    return mp, q, ab, dvec


def _ssm_mixer_core(x_ssm, params, n_batch, seq_len, init_state):
    mp, q, ab, dvec = params
    n_chunks = seq_len // SSM_CHUNK
    bp = max(8, n_batch)
    u = x_ssm.reshape(n_batch, n_chunks, SSM_CHUNK, SSM_GROUPS, SSM_CH)
    u = jnp.transpose(u, (3, 1, 0, 2, 4))
    if bp != n_batch:
        u = jnp.pad(u, ((0, 0), (0, 0), (0, bp - n_batch), (0, 0), (0, 0)))
    u = u.reshape(SSM_GROUPS, n_chunks * bp, SSM_ROW)
    if init_state is None:
        s0 = jnp.zeros((SSM_GROUPS, 4, bp, LANES), F32)
    else:
        st = jnp.transpose(init_state.astype(F32), (2, 1, 0, 3, 4))
        re, im = st[..., 0], st[..., 1]
        s0 = jnp.stack([jnp.concatenate([re[:, 0], im[:, 0]], -1), jnp.concatenate([im[:, 0], re[:, 0]], -1),
                        jnp.concatenate([re[:, 1], im[:, 1]], -1), jnp.concatenate([im[:, 1], re[:, 1]], -1)], axis=1)
        s0 = jnp.pad(s0, ((0, 0), (0, 0), (0, bp - n_batch), (0, 0)))
    y, fin = _ssm_core(u, mp, q, ab, dvec, s0, n_chunks=n_chunks, bp=bp)
    y = y.reshape(SSM_GROUPS, n_chunks, bp, SSM_CHUNK, SSM_CH)[:, :, :n_batch]
    y = jnp.transpose(y, (2, 1, 3, 0, 4)).reshape(n_batch * seq_len, SSM_WIDTH)
    fin = fin[:, :, :n_batch]
    fin = jnp.stack([fin[..., :SSM_STATE], fin[..., SSM_STATE:]], axis=-1)
    return y, jnp.transpose(fin, (2, 1, 0, 3, 4))


def _head_lane_mask(hh):
    lane = lax.broadcasted_iota(jnp.int32, (1, LANES), 1)
    return (lane >= hh * NA_HEAD_DIM) & (lane < (hh + 1) * NA_HEAD_DIM)


_NT_DIMS = (((1,), (1,)), ((), ()))


def _ctx_attn_kernel(q_ref, k_ref, v_ref, o_ref):
    for p in range(HEAD_PAIRS):
        qp, kp, vp = q_ref[p], k_ref[p], v_ref[p]
        acc = jnp.zeros((SEQ, LANES), F32)
        for hh in range(2):
            lm = _head_lane_mask(hh)
            qm = jnp.where(lm, qp, jnp.zeros_like(qp))
            s = lax.dot_general(qm, kp, _NT_DIMS, preferred_element_type=F32)
            e = jnp.exp(s - jnp.max(s, axis=-1, keepdims=True))
            o = jnp.dot(e.astype(BF16), vp, preferred_element_type=F32)
            acc = jnp.where(lm, o / jnp.sum(e, axis=-1, keepdims=True), acc)
        o_ref[:, p * LANES:(p + 1) * LANES] = acc.astype(BF16)


def _ctx_attn(q, k, v, n_batch):
    spec = pl.BlockSpec((HEAD_PAIRS, SEQ, LANES), lambda b: (0, b, 0))
    return pl.pallas_call(
        _ctx_attn_kernel,
        grid=(n_batch,),
        in_specs=[spec, spec, spec],
        out_specs=pl.BlockSpec((SEQ, NA_WIDTH), lambda b: (b, 0)),
        out_shape=jax.ShapeDtypeStruct((n_batch * SEQ, NA_WIDTH), BF16),
        compiler_params=_cparams(("arbitrary",)),
        name="ctx_attn",
    )(q, k, v)


def _na_kernel(q_ref, k_ref, v_ref, kc_ref, vc_ref, tb_ref, rm_ref, o_ref, qb_ref, os_ref):
    rt = pl.program_id(1)
    n_rt = pl.num_programs(1)
    r0 = rt * NA_Q_ROWS
    ttype = jnp.where(rt == 0, 0, jnp.where(rt == n_rt - 1, 2, 1))
    rmask = rm_ref[ttype]
    blk_q = NA_Q_ROWS * NA_COLS

    def pair_body(p, carry):
        for n in range(NA_COL_BLOCKS):
            for rl in range(NA_Q_ROWS):
                qb_ref[n * blk_q + rl * NA_COLS:n * blk_q + (rl + 1) * NA_COLS, :] = (
                    q_ref[p, rl * GRID_W + n * NA_COLS:rl * GRID_W + (n + 1) * NA_COLS, :])
        kcp, vcp = kc_ref[0, p], vc_ref[0, p]
        qall = qb_ref[...]
        sc_all = []
        for hh in range(2):
            qm = jnp.where(_head_lane_mask(hh), qall, jnp.zeros_like(qall))
            sc_all.append(lax.dot_general(qm, kcp, _NT_DIMS, preferred_element_type=F32))
        for n in range(NA_COL_BLOCKS):
            kparts, vparts = [], []
            for i in range(NA_K_ROWS):
                kr = jnp.clip(r0 - (NA_K_ROWS - NA_Q_ROWS) // 2 + i, 0, GRID_W - 1)
                start = pl.multiple_of(kr * GRID_W + NA_KC0[n], 8)
                kparts.append(k_ref[p, pl.ds(start, NA_KCOLS), :])
                vparts.append(v_ref[p, pl.ds(start, NA_KCOLS), :])
            kblk = jnp.concatenate(kparts, axis=0).astype(BF16)
            vblk = jnp.concatenate(vparts, axis=0).astype(BF16)
            qblk = qall[n * blk_q:(n + 1) * blk_q, :]
            oblk = jnp.zeros((blk_q, LANES), F32)
            for hh in range(2):
                lm = _head_lane_mask(hh)
                qm = jnp.where(lm, qblk, jnp.zeros_like(qblk))
                s = lax.dot_general(qm, kblk, _NT_DIMS, preferred_element_type=F32)
                s = s + tb_ref[n, 2 * p + hh] + rmask
                sc = sc_all[hh][n * blk_q:(n + 1) * blk_q, :]
                m = jnp.maximum(jnp.max(s, axis=-1, keepdims=True), jnp.max(sc, axis=-1, keepdims=True))
                e = jnp.exp(s - m)
                ec = jnp.exp(sc - m)
                den = jnp.sum(e, axis=-1, keepdims=True) + jnp.sum(ec, axis=-1, keepdims=True)
                o = (jnp.dot(e.astype(BF16), vblk, preferred_element_type=F32)
                     + jnp.dot(ec.astype(BF16), vcp, preferred_element_type=F32))
                oblk = jnp.where(lm, o / den, oblk)
            for rl in range(NA_Q_ROWS):
                os_ref[p, rl * GRID_W + n * NA_COLS:rl * GRID_W + (n + 1) * NA_COLS, :] = (
                    oblk[rl * NA_COLS:(rl + 1) * NA_COLS, :])
        return carry

    lax.fori_loop(0, HEAD_PAIRS, pair_body, 0)
    for p in range(HEAD_PAIRS):
        o_ref[:, p * LANES:(p + 1) * LANES] = os_ref[p].astype(BF16)


def _na_attn(q, k, v, kc, vc, tb, rm, n_batch):
    tile_tok = NA_Q_ROWS * GRID_W
    n_rt = DEC_SEQ // tile_tok
    return pl.pallas_call(
        _na_kernel,
        grid=(n_batch, n_rt),
        in_specs=[
            pl.BlockSpec((HEAD_PAIRS, tile_tok, LANES), lambda b, r: (0, b * n_rt + r, 0)),
            pl.BlockSpec((HEAD_PAIRS, DEC_SEQ, LANES), lambda b, r: (0, b, 0), pipeline_mode=pl.Buffered(1)),
            pl.BlockSpec((HEAD_PAIRS, DEC_SEQ, LANES), lambda b, r: (0, b, 0), pipeline_mode=pl.Buffered(1)),
            pl.BlockSpec((1, HEAD_PAIRS, PAST_LEN, LANES), lambda b, r: (b, 0, 0, 0)),
            pl.BlockSpec((1, HEAD_PAIRS, PAST_LEN, LANES), lambda b, r: (b, 0, 0, 0)),
            _const_spec(tb.shape),
            _const_spec(rm.shape),
        ],
        out_specs=pl.BlockSpec((tile_tok, NA_WIDTH), lambda b, r: (b * n_rt + r, 0)),
        out_shape=jax.ShapeDtypeStruct((n_batch * DEC_SEQ, NA_WIDTH), BF16),
        scratch_shapes=[pltpu.VMEM((tile_tok, LANES), BF16), pltpu.VMEM((HEAD_PAIRS, tile_tok, LANES), F32)],
        compiler_params=_cparams(("arbitrary", "arbitrary")),
        name="na_attn",
    )(q, k, v, kc, vc, tb, rm)


def _na_tables(rpb_l):
    n = np.arange(NA_COL_BLOCKS)
    cc = np.arange(NA_COLS)
    kk = np.arange(NA_KCOLS)
    rl = np.arange(NA_Q_ROWS)
    ki = np.arange(NA_K_ROWS)
    c = n[:, None] * NA_COLS + cc[None, :]
    kcol = np.asarray(NA_KC0)[:, None] + kk[None, :]
    dc = np.clip(kcol[:, None, :] - c[:, :, None], -(NA_COLS - 1), NA_COLS - 1) + (NA_COLS - 1)
    cs = np.clip(c - NA_COLS // 2, 0, GRID_W - NA_COLS)
    col_ok = (kcol[:, None, :] >= cs[:, :, None]) & (kcol[:, None, :] < cs[:, :, None] + NA_COLS)
    half = (NA_K_ROWS - NA_Q_ROWS) // 2
    dr = np.clip(ki[None, :] - half - rl[:, None] + (NA_MAX_ROWS - 1), 0, 2 * NA_MAX_ROWS - 2)
    vals = rpb_l.astype(F32)[:, dr[None, :, None, :, None], dc[:, None, :, None, :]]
    vals = jnp.where(col_ok[None, :, None, :, None, :], vals, NEG_INF)
    tb = jnp.transpose(vals, (1, 0, 2, 3, 4, 5)).reshape(
        NA_COL_BLOCKS, NA_HEADS, NA_Q_ROWS * NA_COLS, NA_K_ROWS * NA_KCOLS)
    rows = GRID_W
    masks = []
    for r0 in (0, NA_Q_ROWS, rows - NA_Q_ROWS):
        r = r0 + rl
        rs = np.clip(r - NA_MAX_ROWS // 2, 0, rows - NA_MAX_ROWS)
        kr = r0 - half + ki
        ok = (kr[None, :] >= rs[:, None]) & (kr[None, :] < rs[:, None] + NA_MAX_ROWS)
        full = np.broadcast_to(ok[:, None, :, None], (NA_Q_ROWS, NA_COLS, NA_K_ROWS, NA_KCOLS))
        masks.append(np.where(full, 0.0, NEG_INF).reshape(NA_Q_ROWS * NA_COLS, NA_K_ROWS * NA_KCOLS))
    rm = jnp.asarray(np.stack(masks), F32)
    return tb, rm


def _outproj_kernel(x_ref, mod_ref, ys_ref, na_ref, gm_ref, gw_ref, gb_ref, w_ref, o_ref):
    gate = mod_ref[0, 5:6, :]
    y = _gelu(ys_ref[...])
    z = jnp.dot(y.astype(BF16), gw_ref[...], preferred_element_type=F32) + gb_ref[...]
    ys = (y * _sigmoid(z)).astype(BF16)
    mix = jnp.dot(ys, w_ref[0:SSM_WIDTH, :], preferred_element_type=F32)
    mix = mix + jnp.dot(na_ref[...], w_ref[SSM_WIDTH:SSM_WIDTH + NA_WIDTH, :], preferred_element_type=F32)
    mix = mix + jnp.dot(gm_ref[...], w_ref[SSM_WIDTH + NA_WIDTH:, :], preferred_element_type=F32)
    o_ref[...] = x_ref[...] + gate * mix


def _outproj(x, mod_l, ys, na, gm, gw, gb, w, *, seg0, seg_tokens):
    n_tok = x.shape[0]
    tok = lambda width: pl.BlockSpec((TOKEN_TILE, width), lambda i: (i, 0))
    return pl.pallas_call(
        _outproj_kernel,
        grid=(n_tok // TOKEN_TILE,),
        in_specs=[
            tok(D_MODEL),
            pl.BlockSpec((1, N_MOD, D_MODEL), _seg_map(seg0, seg_tokens)),
            tok(SSM_WIDTH), tok(NA_WIDTH), tok(GM_WIDTH),
            _const_spec((SSM_WIDTH, SSM_WIDTH)),
            _const_spec((1, SSM_WIDTH)),
            _const_spec((D_MODEL, D_MODEL)),
        ],
        out_specs=tok(D_MODEL),
        out_shape=jax.ShapeDtypeStruct((n_tok, D_MODEL), F32),
        compiler_params=_cparams(("arbitrary",)),
        name="outproj",
    )(x, mod_l, ys, na, gm, gw, gb, w)


def _layer_weights(l, p):
    def ffn_in(w):
        gate = w[:, :D_FF].reshape(D_MODEL, N_FF_CHUNKS, FF_CHUNK)
        up = w[:, D_FF:].reshape(D_MODEL, N_FF_CHUNKS, FF_CHUNK)
        return jnp.transpose(jnp.concatenate([gate, up], axis=-1), (1, 0, 2)).astype(BF16)

    heads = np.arange(NA_WIDTH) // NA_HEAD_DIM
    hsum = jnp.asarray((heads[:, None] == heads[None, :]) / NA_HEAD_DIM, BF16)
    gmb = jnp.repeat(jnp.transpose(p['gm_bs'][l]).astype(F32), GM_WIDTH // GM_GROUPS, axis=1)
    lp = {k: p[k][l] for k in ('ssm_lambda_re', 'ssm_lambda_im', 'ssm_log_dt', 'ssm_b_re', 'ssm_b_im',
                               'ssm_c_re', 'ssm_c_im', 'ssm_d')}
    tb, rm = _na_tables(p['na_rpb'][l])
    return dict(
        g1=p['norm_ffn1'][l].reshape(1, D_MODEL), g2=p['norm_mix'][l].reshape(1, D_MODEL),
        g3=p['norm_ffn2'][l].reshape(1, D_MODEL),
        f1_in=ffn_in(p['ffn1_w_in'][l]), f1_out=p['ffn1_w_out'][l].astype(BF16),
        f2_in=ffn_in(p['ffn2_w_in'][l]), f2_out=p['ffn2_w_out'][l].astype(BF16),
        w_in=p['w_in'][l].astype(BF16), w_out=p['w_out'][l].astype(BF16),
        hsum=hsum,
        qg=jnp.tile(p['na_q_norm'][l].astype(F32), NA_HEADS).reshape(1, NA_WIDTH),
        kg=jnp.tile(p['na_k_norm'][l].astype(F32), NA_HEADS).reshape(1, NA_WIDTH),
        ws=p['gm_ws'][l].astype(BF16), gmb=gmb,
        ssm=_ssm_params(lp),
        glu_w=p['ssm_glu_w'][l].astype(BF16), glu_b=p['ssm_glu_b'][l].astype(F32).reshape(1, SSM_WIDTH),
        tb=tb, rm=rm,
    )


def _pairs_major(t):
    b, s = t.shape[0], t.shape[1]
    return jnp.transpose(t.reshape(b, s, HEAD_PAIRS, LANES), (0, 2, 1, 3)).astype(BF16)


def _trunk_layer(x, mod_l, lw, *, n_batch, seq_len, is_ctx, ctx_kv=None, ssm_init=None):
    seg = dict(seg0=0 if is_ctx else 1, seg_tokens=x.shape[0] if is_ctx else seq_len)
    x = _ffn(x, mod_l, lw['g1'], lw['f1_in'], lw['f1_out'], mod_base=0, **seg)
    outs = _inproj(x, mod_l, lw['g2'], lw['w_in'], lw['hsum'], lw['qg'], lw['kg'], lw['ws'], lw['gmb'],
                   is_ctx=is_ctx, **seg)
    xs, q, k, v, gm = outs[:5]
    ys, fin = _ssm_mixer_core(xs, lw['ssm'], n_batch, seq_len, ssm_init)
    if is_ctx:
        na = _ctx_attn(q, k, v, n_batch)
    else:
        na = _na_attn(q, k, v, ctx_kv[0], ctx_kv[1], lw['tb'], lw['rm'], n_batch)
    x = _outproj(x, mod_l, ys, na, gm, lw['glu_w'], lw['glu_b'], lw['w_out'], **seg)
    x = _ffn(x, mod_l, lw['g3'], lw['f2_in'], lw['f2_out'], mod_base=6, **seg)
    return x, (outs[5:], fin)


def kernel(x_prompt, x_sample, c, cache_k, cache_v, state_ssm, c_ctx, w_ada, b_ada, norm_ffn1, ffn1_w_in, ffn1_w_out, norm_mix, w_in, w_out, ssm_lambda_re, ssm_lambda_im, ssm_log_dt, ssm_b_re, ssm_b_im, ssm_c_re, ssm_c_im, ssm_d, ssm_glu_w, ssm_glu_b, na_q_norm, na_k_norm, na_rpb, gm_ws, gm_bs, norm_ffn2, ffn2_w_in, ffn2_w_out):
    p = dict(norm_ffn1=norm_ffn1, ffn1_w_in=ffn1_w_in, ffn1_w_out=ffn1_w_out, norm_mix=norm_mix, w_in=w_in,
             w_out=w_out, ssm_lambda_re=ssm_lambda_re, ssm_lambda_im=ssm_lambda_im, ssm_log_dt=ssm_log_dt,
             ssm_b_re=ssm_b_re, ssm_b_im=ssm_b_im, ssm_c_re=ssm_c_re, ssm_c_im=ssm_c_im, ssm_d=ssm_d,
             ssm_glu_w=ssm_glu_w, ssm_glu_b=ssm_glu_b, na_q_norm=na_q_norm, na_k_norm=na_k_norm, na_rpb=na_rpb,
             gm_ws=gm_ws, gm_bs=gm_bs, norm_ffn2=norm_ffn2, ffn2_w_in=ffn2_w_in, ffn2_w_out=ffn2_w_out)
    batch, dec_batch = x_prompt.shape[0], x_sample.shape[0]
    cond8 = jnp.zeros((8, D_MODEL), F32).at[0].set(c_ctx).at[1:1 + dec_batch].set(c)
    mod = _adaln(cond8, w_ada, b_ada).reshape(DEPTH, 8, N_MOD, D_MODEL)

    xp = x_prompt.reshape(batch * SEQ, D_MODEL)
    xs = x_sample.reshape(dec_batch * DEC_SEQ, D_MODEL)
    new_k, new_v, new_s = [], [], []
    for l in range(DEPTH):
        lw = _layer_weights(l, p)
        xp, ((k_c, v_c), s_c) = _trunk_layer(xp, mod[l], lw, n_batch=batch, seq_len=SEQ, is_ctx=True)
        new_k.append(k_c.reshape(batch, SEQ, NA_HEADS, NA_HEAD_DIM))
        new_v.append(v_c.reshape(batch, SEQ, NA_HEADS, NA_HEAD_DIM))
        new_s.append(s_c)
        ctx_kv = (_pairs_major(cache_k[:, l]), _pairs_major(cache_v[:, l]))
        xs, _ = _trunk_layer(xs, mod[l], lw, n_batch=dec_batch, seq_len=DEC_SEQ, is_ctx=False,
                             ctx_kv=ctx_kv, ssm_init=state_ssm[:, l])
    return (xp.reshape(batch, SEQ, D_MODEL), xs.reshape(dec_batch, DEC_SEQ, D_MODEL),
            jnp.stack(new_k, axis=1), jnp.stack(new_v, axis=1), jnp.stack(new_s, axis=1))
```

```python
import functools

import numpy as np
import jax
import jax.numpy as jnp
from jax import lax
from jax.experimental import pallas as pl
from jax.experimental.pallas import tpu as pltpu

D_MODEL = 1024
DEPTH = 2
SEQ = 256
DEC_SEQ = 4096
PAST_LEN = 256
GRID_W = 64
SSM_WIDTH = 256
SSM_CH = 16
SSM_GROUPS = 16
SSM_STATE = 64
NA_WIDTH = 512
NA_HEAD_DIM = 64
NA_HEADS = 8
NA_MAX_ROWS = 8
NA_COLS = 16
GM_WIDTH = 256
GM_GROUPS = 4
GM_CHUNK = 128
D_FF = 2816
N_MOD = 9
RMS_EPS = 1e-6
LN_EPS = 1e-5
NEG_INF = -1e30

F32 = jnp.float32
BF16 = jnp.bfloat16

LANES = 128
SUBLANES = 8
MXU_DIM = 256
VMEM_LIMIT_BYTES = 56 * 1024 * 1024

FFN_TILE = 512
MIX_TILE = 1024
FF_CHUNK = MXU_DIM
N_FF_CHUNKS = D_FF // FF_CHUNK
SSM_CHUNK = 16
SSM_ROW = SSM_CHUNK * SSM_CH
SSM_GROUP_BLOCK = 4
SSM_W_COLS = 4 * LANES
SCAN_SHIFTS = (1, 2, 4)
N_SCAN_CONST = 3 * len(SCAN_SHIFTS) + 3
HEAD_PAIRS = NA_HEADS // 2
NA_Q_ROWS = 8
NA_K_ROWS = 16
NA_KCOLS = 2 * NA_COLS
NA_COL_BLOCKS = GRID_W // NA_COLS
NA_KC0 = (0, 8, 24, 32)
SEGS_PER_TILE = LANES // SSM_CH


def _silu(x):
    return x * (1.0 / (1.0 + jnp.exp(-x)))


def _sigmoid(x):
    return 1.0 / (1.0 + jnp.exp(-x))


def _gelu(x):
    return 0.5 * x * (1.0 + jnp.tanh(0.7978845608028654 * (x + 0.044715 * (x * x * x))))


def _cparams(sem):
    return pltpu.CompilerParams(dimension_semantics=sem, vmem_limit_bytes=VMEM_LIMIT_BYTES)


def _const_spec(shape):
    nd = len(shape)
    return pl.BlockSpec(shape, lambda *_: (0,) * nd, pipeline_mode=pl.Buffered(1))


def _ada_kernel(c_ref, w_ref, b_ref, o_ref):
    s = _silu(c_ref[...]).astype(BF16)
    w = w_ref[0].astype(BF16)
    o_ref[0] = jnp.dot(s, w, preferred_element_type=F32) + b_ref[0]


def _adaln(cond8, w_ada, b_ada):
    tn = D_MODEL
    ncol = N_MOD * D_MODEL
    return pl.pallas_call(
        _ada_kernel,
        grid=(DEPTH, ncol // tn),
        in_specs=[
            pl.BlockSpec((8, D_MODEL), lambda l, j: (0, 0)),
            pl.BlockSpec((1, D_MODEL, tn), lambda l, j: (l, 0, j)),
            pl.BlockSpec((1, 1, tn), lambda l, j: (l, 0, j)),
        ],
        out_specs=pl.BlockSpec((1, 8, tn), lambda l, j: (l, 0, j)),
        out_shape=jax.ShapeDtypeStruct((DEPTH, 8, ncol), F32),
        compiler_params=_cparams(("arbitrary", "arbitrary")),
        name="adaln",
    )(cond8, w_ada, b_ada.reshape(DEPTH, 1, ncol))


def _normed(x, g_row, shift, scale):
    ms = jnp.mean(x * x, axis=-1, keepdims=True)
    h = x * lax.rsqrt(ms + RMS_EPS) * g_row
    return h * (1.0 + scale) + shift


def _seg_map(seg0, seg_tokens, tile):
    tiles_per_seg = seg_tokens // tile
    return lambda i: (seg0 + i // tiles_per_seg, 0, 0)


def _ffn_kernel(x_ref, mod_ref, g_ref, win_ref, wout_ref, o_ref, h_ref, act_ref, *, mod_base):
    x = x_ref[...]
    shift = mod_ref[0, mod_base:mod_base + 1, :]
    scale = mod_ref[0, mod_base + 1:mod_base + 2, :]
    gate = mod_ref[0, mod_base + 2:mod_base + 3, :]
    h_ref[...] = _normed(x, g_ref[...], shift, scale).astype(BF16)
    for j in range(N_FF_CHUNKS):
        gu = jnp.dot(h_ref[...], win_ref[j], preferred_element_type=F32)
        a = _silu(gu[:, :FF_CHUNK]) * gu[:, FF_CHUNK:]
        act_ref[:, j * FF_CHUNK:(j + 1) * FF_CHUNK] = a.astype(BF16)
    for n in range(D_MODEL // MXU_DIM):
        cols = slice(n * MXU_DIM, (n + 1) * MXU_DIM)
        y = jnp.dot(act_ref[...], wout_ref[:, cols], preferred_element_type=F32)
        o_ref[:, cols] = x_ref[:, cols] + 0.5 * gate[:, cols] * y


def _ffn(x, mod_l, g, w_in3, w_out, *, mod_base, seg0, seg_tokens):
    n_tok = x.shape[0]
    return pl.pallas_call(
        functools.partial(_ffn_kernel, mod_base=mod_base),
        grid=(n_tok // FFN_TILE,),
        in_specs=[
            pl.BlockSpec((FFN_TILE, D_MODEL), lambda i: (i, 0)),
            pl.BlockSpec((1, N_MOD, D_MODEL), _seg_map(seg0, seg_tokens, FFN_TILE)),
            _const_spec((1, D_MODEL)),
            _const_spec((N_FF_CHUNKS, D_MODEL, 2 * FF_CHUNK)),
            _const_spec((D_FF, D_MODEL)),
        ],
        out_specs=pl.BlockSpec((FFN_TILE, D_MODEL), lambda i: (i, 0)),
        out_shape=jax.ShapeDtypeStruct((n_tok, D_MODEL), F32),
        scratch_shapes=[pltpu.VMEM((FFN_TILE, D_MODEL), BF16), pltpu.VMEM((FFN_TILE, D_FF), BF16)],
        compiler_params=_cparams(("arbitrary",)),
        name="ffn",
    )(x, mod_l, g, w_in3, w_out)


def _segment_ids():
    return lax.broadcasted_iota(jnp.int32, (1, LANES), 1) // SSM_CH


def _merge_segments(load_piece, rot):
    seg = _segment_ids()
    tiles = []
    for lt in range(SSM_ROW // LANES):
        acc = None
        for jj in range(SEGS_PER_TILE):
            src = (lt * SEGS_PER_TILE + jj - rot) % SSM_CHUNK
            piece = load_piece(src, lt)
            acc = piece if acc is None else jnp.where(seg == jj, piece, acc)
        tiles.append(acc)
    return jnp.concatenate(tiles, axis=-1)


def _inproj_kernel(x_ref, mod_ref, g_ref, w_ref, wrot_ref, hsum_ref, qg_ref, kg_ref, ws_ref, gmb_ref,
                   u_ref, q_ref, k_ref, v_ref, gm_ref, *rest):
    *tok_refs, h_ref, hs_ref, z_ref = rest
    shift = mod_ref[0, 3:4, :]
    scale = mod_ref[0, 4:5, :]
    h = _normed(x_ref[...], g_ref[...], shift, scale)
    h_ref[...] = h.astype(BF16)
    n_rows = MIX_TILE // SSM_CHUNK
    n_lt = D_MODEL // LANES
    for j in range(n_lt):
        hs_ref[j] = h[:, j * LANES:(j + 1) * LANES]

    for t in range(SSM_CHUNK):
        ht = jnp.concatenate([hs_ref[j, pl.ds(t, n_rows, stride=SSM_CHUNK), :] for j in range(n_lt)], axis=-1)
        z_ref[t] = jnp.dot(ht.astype(BF16), wrot_ref[t], preferred_element_type=F32)
    for g in range(SSM_GROUPS):
        u_ref[g] = _merge_segments(lambda src, lt: z_ref[src, :, lt * LANES:(lt + 1) * LANES], g)

    def proj(c0, width):
        return jnp.dot(h_ref[...], w_ref[:, c0:c0 + width], preferred_element_type=F32)

    def head_norm(t, gain_row):
        ms = jnp.dot((t * t).astype(BF16), hsum_ref[...], preferred_element_type=F32)
        return t * lax.rsqrt(ms + RMS_EPS) * gain_row

    q = head_norm(proj(0, NA_WIDTH), qg_ref[...]) * (NA_HEAD_DIM ** -0.5)
    k = head_norm(proj(NA_WIDTH, NA_WIDTH), kg_ref[...])
    v = proj(2 * NA_WIDTH, NA_WIDTH)
    for p in range(HEAD_PAIRS):
        lanes = slice(p * LANES, (p + 1) * LANES)
        q_ref[p] = q[:, lanes].astype(q_ref.dtype)
        k_ref[p] = k[:, lanes].astype(k_ref.dtype)
        v_ref[p] = v[:, lanes].astype(v_ref.dtype)
    if tok_refs:
        tok_refs[0][...] = k
        tok_refs[1][...] = v

    a = _gelu(proj(3 * NA_WIDTH, 2 * GM_WIDTH))
    u = a[:, :GM_WIDTH]
    vv = a[:, GM_WIDTH:]
    mu = jnp.mean(vv, axis=-1, keepdims=True)
    vc = vv - mu
    var = jnp.mean(vc * vc, axis=-1, keepdims=True)
    vln = (vc * lax.rsqrt(var + LN_EPS)).astype(BF16)
    lane = lax.broadcasted_iota(jnp.int32, (1, GM_WIDTH), 1)
    gw = GM_WIDTH // GM_GROUPS
    for ci in range(MIX_TILE // GM_CHUNK):
        rows = slice(ci * GM_CHUNK, (ci + 1) * GM_CHUNK)
        vch = vln[rows, :]
        sp = gmb_ref[...]
        for gi in range(GM_GROUPS):
            t = jnp.dot(ws_ref[gi], vch, preferred_element_type=F32)
            sp = sp + jnp.where((lane >= gi * gw) & (lane < (gi + 1) * gw), t, 0.0)
        gm_ref[rows, :] = (u[rows, :] * sp).astype(BF16)


def _inproj(x, mod_l, g, w, wrot, hsum, qg, kg, ws, gmb, *, seg0, seg_tokens, is_ctx):
    n_tok = x.shape[0]
    kv_dtype = BF16 if is_ctx else F32
    n_rows = MIX_TILE // SSM_CHUNK
    pair_spec = pl.BlockSpec((HEAD_PAIRS, MIX_TILE, LANES), lambda i: (0, i, 0))
    out_specs = [
        pl.BlockSpec((SSM_GROUPS, n_rows, SSM_ROW), lambda i: (0, i, 0)),
        pair_spec, pair_spec, pair_spec,
        pl.BlockSpec((MIX_TILE, GM_WIDTH), lambda i: (i, 0)),
    ]
    out_shape = [
        jax.ShapeDtypeStruct((SSM_GROUPS, n_tok // SSM_CHUNK, SSM_ROW), F32),
        jax.ShapeDtypeStruct((HEAD_PAIRS, n_tok, LANES), BF16),
        jax.ShapeDtypeStruct((HEAD_PAIRS, n_tok, LANES), kv_dtype),
        jax.ShapeDtypeStruct((HEAD_PAIRS, n_tok, LANES), kv_dtype),
        jax.ShapeDtypeStruct((n_tok, GM_WIDTH), BF16),
    ]
    if is_ctx:
        out_specs += [pl.BlockSpec((MIX_TILE, NA_WIDTH), lambda i: (i, 0))] * 2
        out_shape += [jax.ShapeDtypeStruct((n_tok, NA_WIDTH), F32)] * 2
    return pl.pallas_call(
        _inproj_kernel,
        grid=(n_tok // MIX_TILE,),
        in_specs=[
            pl.BlockSpec((MIX_TILE, D_MODEL), lambda i: (i, 0)),
            pl.BlockSpec((1, N_MOD, D_MODEL), _seg_map(seg0, seg_tokens, MIX_TILE)),
            _const_spec((1, D_MODEL)),
            _const_spec(w.shape),
            _const_spec(wrot.shape),
            _const_spec((NA_WIDTH, NA_WIDTH)),
            _const_spec((1, NA_WIDTH)),
            _const_spec((1, NA_WIDTH)),
            _const_spec((GM_GROUPS, GM_CHUNK, GM_CHUNK)),
            _const_spec((GM_CHUNK, GM_WIDTH)),
        ],
        out_specs=out_specs,
        out_shape=out_shape,
        scratch_shapes=[pltpu.VMEM((MIX_TILE, D_MODEL), BF16), pltpu.VMEM((D_MODEL // LANES, MIX_TILE, LANES), F32),
                        pltpu.VMEM((SSM_CHUNK, n_rows, SSM_ROW), F32)],
        compiler_params=_cparams(("arbitrary",)),
        name="inproj_ctx" if is_ctx else "inproj_lat",
    )(x, mod_l, g, w, wrot, hsum, qg, kg, ws, gmb)


def _shift_rows(x, d, row, down):
    if down:
        return jnp.where(row >= d, pltpu.roll(x, d, axis=0), 0.0)
    return jnp.where(row < SUBLANES - d, pltpu.roll(x, SUBLANES - d, axis=0), 0.0)


def _scan_tile(w, wp, c, cp, cst, row, down):
    x, xp = w, wp
    for si, d in enumerate(SCAN_SHIFTS):
        a, b, bp = cst[3 * si], cst[3 * si + 1], cst[3 * si + 2]
        xs, xps = _shift_rows(x, d, row, down), _shift_rows(xp, d, row, down)
        x, xp = x + a * xs + b * xps, xp + a * xps + bp * xs
    a, b, bp = cst[N_SCAN_CONST - 3], cst[N_SCAN_CONST - 2], cst[N_SCAN_CONST - 1]
    s = x + a * c + b * cp
    sp = xp + a * cp + bp * c
    edge = 0 if down else SUBLANES - 1
    last = SUBLANES - 1 if down else 0
    seen = jnp.where(row == edge, c, _shift_rows(s, 1, row, down))
    c_new = jnp.broadcast_to(s[last:last + 1, :], s.shape)
    cp_new = jnp.broadcast_to(sp[last:last + 1, :], sp.shape)
    return seen, c_new, cp_new


def _ssm_kernel(u_ref, mp_ref, q_ref, cst_ref, dvec_ref, s0_ref, y_ref, fin_ref, w_ref, sq_ref, *, n_batch, n_chunks):
    gb = u_ref.shape[0]
    n_rows = n_batch * n_chunks
    row_blk = min(n_rows, 512)
    for g in range(gb):
        for r0 in range(0, n_rows, row_blk):
            rows = slice(r0, r0 + row_blk)
            ug = u_ref[g, rows, :]
            wy = jnp.dot(ug.astype(BF16), mp_ref[g], preferred_element_type=F32)
            y_ref[g, rows, :] = wy[:, :SSM_ROW] + ug * dvec_ref[g]
            w_ref[g, rows, :] = wy[:, SSM_ROW:]

    n_tiles = n_chunks // SUBLANES
    row = lax.broadcasted_iota(jnp.int32, (SUBLANES, LANES), 0)

    def tile_step(b, kt, carry):
        out = []
        for g in range(gb):
            cf, cfp, cb, cbp = carry[4 * g:4 * g + 4]
            rf = pl.multiple_of(b * n_chunks + kt * SUBLANES, SUBLANES)
            rb = pl.multiple_of(b * n_chunks + (n_tiles - 1 - kt) * SUBLANES, SUBLANES)
            wf = w_ref[g, pl.ds(rf, SUBLANES), :]
            wb = w_ref[g, pl.ds(rb, SUBLANES), :]
            cst_f = [cst_ref[g, 0, i] for i in range(N_SCAN_CONST)]
            cst_b = [cst_ref[g, 1, i] for i in range(N_SCAN_CONST)]
            seen_f, cf, cfp = _scan_tile(wf[:, 0:LANES], wf[:, LANES:2 * LANES], cf, cfp, cst_f, row, True)
            seen_b, cb, cbp = _scan_tile(wb[:, 2 * LANES:3 * LANES], wb[:, 3 * LANES:], cb, cbp, cst_b, row, False)
            sq_ref[g, pl.ds(rf, SUBLANES), 0:LANES] = seen_f
            sq_ref[g, pl.ds(rb, SUBLANES), LANES:2 * LANES] = seen_b
            out += [cf, cfp, cb, cbp]
        return tuple(out)

    def batch_body(b, _):
        carry = []
        for g in range(gb):
            for i in range(4):
                carry.append(jnp.broadcast_to(s0_ref[g, i, pl.ds(b, 1), :], (SUBLANES, LANES)))
        carry = tuple(carry)
        if n_tiles <= 2:
            for kt in range(n_tiles):
                carry = tile_step(b, kt, carry)
        else:
            carry = lax.fori_loop(0, n_tiles, lambda kt, c: tile_step(b, kt, c), carry)
        for g in range(gb):
            fin_ref[g, 0, pl.ds(b, 1), :] = carry[4 * g][0:1, :]
            fin_ref[g, 1, pl.ds(b, 1), :] = carry[4 * g + 2][0:1, :]
        return 0

    lax.fori_loop(0, n_batch, batch_body, 0)

    for g in range(gb):
        for r0 in range(0, n_rows, row_blk):
            rows = slice(r0, r0 + row_blk)
            y_ref[g, rows, :] += jnp.dot(sq_ref[g, rows, :].astype(BF16), q_ref[g], preferred_element_type=F32)


def _ssm_core(u, mp, q, cst, dvec, s0, *, n_batch, n_chunks):
    n_rows = n_batch * n_chunks
    gb = SSM_GROUP_BLOCK
    g3 = lambda i: (i, 0, 0)
    g4 = lambda i: (i, 0, 0, 0)
    g5 = lambda i: (i, 0, 0, 0, 0)
    return pl.pallas_call(
        functools.partial(_ssm_kernel, n_batch=n_batch, n_chunks=n_chunks),
        grid=(SSM_GROUPS // gb,),
        in_specs=[
            pl.BlockSpec((gb, n_rows, SSM_ROW), g3),
            pl.BlockSpec((gb, SSM_ROW, SSM_ROW + SSM_W_COLS), g3),
            pl.BlockSpec((gb, 2 * LANES, SSM_ROW), g3),
            pl.BlockSpec((gb, 2, N_SCAN_CONST, SUBLANES, LANES), g5),
            pl.BlockSpec((gb, 1, SSM_ROW), g3),
            pl.BlockSpec((gb, 4, n_batch, LANES), g4),
        ],
        out_specs=[
            pl.BlockSpec((gb, n_rows, SSM_ROW), g3),
            pl.BlockSpec((gb, 2, n_batch, LANES), g4),
        ],
        out_shape=[
            jax.ShapeDtypeStruct((SSM_GROUPS, n_rows, SSM_ROW), F32),
            jax.ShapeDtypeStruct((SSM_GROUPS, 2, n_batch, LANES), F32),
        ],
        scratch_shapes=[pltpu.VMEM((gb, n_rows, SSM_W_COLS), F32), pltpu.VMEM((gb, n_rows, 2 * LANES), F32)],
        compiler_params=_cparams(("arbitrary",)),
        name="ssm_core",
    )(u, mp, q, cst, dvec, s0)


def _cmul(a, b):
    return a[0] * b[0] - a[1] * b[1], a[0] * b[1] + a[1] * b[0]


def _toeplitz(kern, lower):
    t = kern.shape[0]
    pad = jnp.concatenate([jnp.zeros_like(kern[:t - 1]), kern], axis=0)
    rows = jnp.stack([pad[t - 1 - j:2 * t - 1 - j] for j in range(t)], axis=0)
    return rows if lower else jnp.swapaxes(rows, 0, 1)


def _rotate_groups(x, axis):
    return jnp.stack([jnp.roll(x[g], g, axis=axis - 1) for g in range(SSM_GROUPS)], axis=0)


def _ssm_params(lp):
    t = SSM_CHUNK
    hi = dict(precision='highest')
    mats, pcols, qrows, csts = 0.0, [], [], []
    steps = jnp.arange(t + 1, dtype=F32)[:, None, None]
    tile_steps = jnp.arange(1, SUBLANES + 1, dtype=F32)[:, None, None] * t
    for d in range(2):
        lam = (lp['ssm_lambda_re'][d].astype(F32), lp['ssm_lambda_im'][d].astype(F32))
        dt = jnp.exp(lp['ssm_log_dt'][d].astype(F32))[:, None]
        arg = (lam[0] * dt, lam[1] * dt)

        def powers(n):
            mag = jnp.exp(n * arg[0][None])
            return mag * jnp.cos(n * arg[1][None]), mag * jnp.sin(n * arg[1][None])

        pw = powers(steps)
        num = (pw[0][1] - 1.0, pw[1][1])
        den = lam[0] * lam[0] + lam[1] * lam[1]
        zoh = ((num[0] * lam[0] + num[1] * lam[1]) / den, (num[1] * lam[0] - num[0] * lam[1]) / den)
        b_mat = (lp['ssm_b_re'][d].astype(F32), lp['ssm_b_im'][d].astype(F32))
        bbar = _cmul((zoh[0][..., None], zoh[1][..., None]), b_mat)
        c_mat = (lp['ssm_c_re'][d].astype(F32), lp['ssm_c_im'][d].astype(F32))
        pb = _cmul((pw[0][:t, :, :, None], pw[1][:t, :, :, None]), (bbar[0][None], bbar[1][None]))
        kern = (jnp.einsum('gcp,ngpd->ngcd', c_mat[0], pb[0], **hi)
                - jnp.einsum('gcp,ngpd->ngcd', c_mat[1], pb[1], **hi))
        m = _toeplitz(kern, lower=(d == 0))
        mats = mats + jnp.transpose(m, (2, 0, 4, 1, 3))
        pm = [jnp.flip(x, 0) if d == 0 else x for x in pb]
        pm = [jnp.transpose(x, (1, 0, 3, 2)) for x in pm]
        pcols += [pm[0], pm[1], pm[1], pm[0]]
        qp = [x[1:] if d == 0 else jnp.flip(x[1:], 0) for x in pw]
        qm = _cmul((c_mat[0][None], c_mat[1][None]), (qp[0][:, :, None, :], qp[1][:, :, None, :]))
        qm = [jnp.transpose(x, (1, 3, 0, 2)) for x in qm]
        qrows += [qm[0], -qm[1]]
        def carry_rows(n, shape):
            pr, pi = powers(n)
            rows = [jnp.concatenate([pr, pr], -1), jnp.concatenate([-pi, pi], -1), jnp.concatenate([pi, -pi], -1)]
            return [jnp.broadcast_to(jnp.swapaxes(r, 0, 1), shape) for r in rows]

        full = (SSM_GROUPS, SUBLANES, LANES)
        cd = []
        for s in SCAN_SHIFTS:
            cd += carry_rows(jnp.full((1, 1, 1), float(s * t), F32), full)
        tile_pw = tile_steps if d == 0 else jnp.flip(tile_steps, 0)
        cd += carry_rows(tile_pw, full)
        csts.append(jnp.stack(cd, axis=1))
    g = SSM_GROUPS
    mats = _rotate_groups(_rotate_groups(mats, 1), 3).reshape(g, SSM_ROW, SSM_ROW)
    pcols = [_rotate_groups(x, 1).reshape(g, SSM_ROW, SSM_STATE) for x in pcols]
    qrows = [_rotate_groups(x, 2).reshape(g, SSM_STATE, SSM_ROW) for x in qrows]
    mp = jnp.concatenate([mats] + pcols, axis=-1).astype(BF16)
    q = jnp.concatenate(qrows, axis=1).astype(BF16)
    cst = jnp.stack(csts, axis=1)
    dvec = jnp.tile(lp['ssm_d'].astype(F32).reshape(SSM_GROUPS, 1, SSM_CH), (1, 1, SSM_CHUNK))
    return mp, q, cst, dvec


def _ssm_mixer_core(u, params, n_batch, seq_len, init_state):
    mp, q, cst, dvec = params
    n_chunks = seq_len // SSM_CHUNK
    if init_state is None:
        s0 = jnp.zeros((SSM_GROUPS, 4, n_batch, LANES), F32)
    else:
        st = jnp.transpose(init_state.astype(F32), (2, 1, 0, 3, 4))
        re, im = st[..., 0], st[..., 1]
        s0 = jnp.stack([jnp.concatenate([re[:, 0], im[:, 0]], -1), jnp.concatenate([im[:, 0], re[:, 0]], -1),
                        jnp.concatenate([re[:, 1], im[:, 1]], -1), jnp.concatenate([im[:, 1], re[:, 1]], -1)], axis=1)
    y, fin = _ssm_core(u, mp, q, cst, dvec, s0, n_batch=n_batch, n_chunks=n_chunks)
    fin = jnp.stack([fin[..., :SSM_STATE], fin[..., SSM_STATE:]], axis=-1)
    return y, jnp.transpose(fin, (2, 1, 0, 3, 4))


def _head_lane_mask(hh):
    lane = lax.broadcasted_iota(jnp.int32, (1, LANES), 1)
    return (lane >= hh * NA_HEAD_DIM) & (lane < (hh + 1) * NA_HEAD_DIM)


_NT_DIMS = (((1,), (1,)), ((), ()))


def _ctx_attn_kernel(q_ref, k_ref, v_ref, o_ref):
    for p in range(HEAD_PAIRS):
        qp, kp, vp = q_ref[p], k_ref[p], v_ref[p]
        acc = jnp.zeros((SEQ, LANES), F32)
        for hh in range(2):
            lm = _head_lane_mask(hh)
            qm = jnp.where(lm, qp, jnp.zeros_like(qp))
            s = lax.dot_general(qm, kp, _NT_DIMS, preferred_element_type=F32)
            e = jnp.exp(s - jnp.max(s, axis=-1, keepdims=True))
            o = jnp.dot(e.astype(BF16), vp, preferred_element_type=F32)
            acc = jnp.where(lm, o / jnp.sum(e, axis=-1, keepdims=True), acc)
        o_ref[:, p * LANES:(p + 1) * LANES] = acc.astype(BF16)


def _ctx_attn(q, k, v, n_batch):
    spec = pl.BlockSpec((HEAD_PAIRS, SEQ, LANES), lambda b: (0, b, 0))
    return pl.pallas_call(
        _ctx_attn_kernel,
        grid=(n_batch,),
        in_specs=[spec, spec, spec],
        out_specs=pl.BlockSpec((SEQ, NA_WIDTH), lambda b: (b, 0)),
        out_shape=jax.ShapeDtypeStruct((n_batch * SEQ, NA_WIDTH), BF16),
        compiler_params=_cparams(("arbitrary",)),
        name="ctx_attn",
    )(q, k, v)


def _na_kernel(q_ref, k_ref, v_ref, kc_ref, vc_ref, tb_ref, rm_ref, o_ref, qb_ref, os_ref):
    rt = pl.program_id(1)
    n_rt = pl.num_programs(1)
    r0 = rt * NA_Q_ROWS
    ttype = jnp.where(rt == 0, 0, jnp.where(rt == n_rt - 1, 2, 1))
    rmask = rm_ref[ttype]
    blk_q = NA_Q_ROWS * NA_COLS

    def pair_body(p, carry):
        for n in range(NA_COL_BLOCKS):
            for rl in range(NA_Q_ROWS):
                qb_ref[n * blk_q + rl * NA_COLS:n * blk_q + (rl + 1) * NA_COLS, :] = (
                    q_ref[p, rl * GRID_W + n * NA_COLS:rl * GRID_W + (n + 1) * NA_COLS, :])
        kcp, vcp = kc_ref[0, p], vc_ref[0, p]
        qall = qb_ref[...]
        sc_all = []
        for hh in range(2):
            qm = jnp.where(_head_lane_mask(hh), qall, jnp.zeros_like(qall))
            sc_all.append(lax.dot_general(qm, kcp, _NT_DIMS, preferred_element_type=F32))
        for n in range(NA_COL_BLOCKS):
            kparts, vparts = [], []
            for i in range(NA_K_ROWS):
                kr = jnp.clip(r0 - (NA_K_ROWS - NA_Q_ROWS) // 2 + i, 0, GRID_W - 1)
                start = pl.multiple_of(kr * GRID_W + NA_KC0[n], 8)
                kparts.append(k_ref[p, pl.ds(start, NA_KCOLS), :])
                vparts.append(v_ref[p, pl.ds(start, NA_KCOLS), :])
            kblk = jnp.concatenate(kparts, axis=0).astype(BF16)
            vblk = jnp.concatenate(vparts, axis=0).astype(BF16)
            qblk = qall[n * blk_q:(n + 1) * blk_q, :]
            oblk = jnp.zeros((blk_q, LANES), F32)
            for hh in range(2):
                lm = _head_lane_mask(hh)
                qm = jnp.where(lm, qblk, jnp.zeros_like(qblk))
                s = lax.dot_general(qm, kblk, _NT_DIMS, preferred_element_type=F32)
                s = s + tb_ref[n, 2 * p + hh] + rmask
                sc = sc_all[hh][n * blk_q:(n + 1) * blk_q, :]
                m = jnp.maximum(jnp.max(s, axis=-1, keepdims=True), jnp.max(sc, axis=-1, keepdims=True))
                e = jnp.exp(s - m)
                ec = jnp.exp(sc - m)
                den = jnp.sum(e, axis=-1, keepdims=True) + jnp.sum(ec, axis=-1, keepdims=True)
                o = (jnp.dot(e.astype(BF16), vblk, preferred_element_type=F32)
                     + jnp.dot(ec.astype(BF16), vcp, preferred_element_type=F32))
                oblk = jnp.where(lm, o / den, oblk)
            for rl in range(NA_Q_ROWS):
                os_ref[p, rl * GRID_W + n * NA_COLS:rl * GRID_W + (n + 1) * NA_COLS, :] = (
                    oblk[rl * NA_COLS:(rl + 1) * NA_COLS, :])
        return carry

    lax.fori_loop(0, HEAD_PAIRS, pair_body, 0)
    for p in range(HEAD_PAIRS):
        o_ref[:, p * LANES:(p + 1) * LANES] = os_ref[p].astype(BF16)


def _na_attn(q, k, v, kc, vc, tb, rm, n_batch):
    tile_tok = NA_Q_ROWS * GRID_W
    n_rt = DEC_SEQ // tile_tok
    return pl.pallas_call(
        _na_kernel,
        grid=(n_batch, n_rt),
        in_specs=[
            pl.BlockSpec((HEAD_PAIRS, tile_tok, LANES), lambda b, r: (0, b * n_rt + r, 0)),
            pl.BlockSpec((HEAD_PAIRS, DEC_SEQ, LANES), lambda b, r: (0, b, 0), pipeline_mode=pl.Buffered(1)),
            pl.BlockSpec((HEAD_PAIRS, DEC_SEQ, LANES), lambda b, r: (0, b, 0), pipeline_mode=pl.Buffered(1)),
            pl.BlockSpec((1, HEAD_PAIRS, PAST_LEN, LANES), lambda b, r: (b, 0, 0, 0)),
            pl.BlockSpec((1, HEAD_PAIRS, PAST_LEN, LANES), lambda b, r: (b, 0, 0, 0)),
            _const_spec(tb.shape),
            _const_spec(rm.shape),
        ],
        out_specs=pl.BlockSpec((tile_tok, NA_WIDTH), lambda b, r: (b * n_rt + r, 0)),
        out_shape=jax.ShapeDtypeStruct((n_batch * DEC_SEQ, NA_WIDTH), BF16),
        scratch_shapes=[pltpu.VMEM((tile_tok, LANES), BF16), pltpu.VMEM((HEAD_PAIRS, tile_tok, LANES), F32)],
        compiler_params=_cparams(("arbitrary", "arbitrary")),
        name="na_attn",
    )(q, k, v, kc, vc, tb, rm)


def _na_tables(rpb_l):
    n = np.arange(NA_COL_BLOCKS)
    cc = np.arange(NA_COLS)
    kk = np.arange(NA_KCOLS)
    rl = np.arange(NA_Q_ROWS)
    ki = np.arange(NA_K_ROWS)
    n_dr, n_dc = 2 * NA_MAX_ROWS - 1, 2 * NA_COLS - 1
    c = n[:, None] * NA_COLS + cc[None, :]
    kcol = np.asarray(NA_KC0)[:, None] + kk[None, :]
    dc = np.clip(kcol[:, None, :] - c[:, :, None], -(NA_COLS - 1), NA_COLS - 1) + (NA_COLS - 1)
    cs = np.clip(c - NA_COLS // 2, 0, GRID_W - NA_COLS)
    col_ok = (kcol[:, None, :] >= cs[:, :, None]) & (kcol[:, None, :] < cs[:, :, None] + NA_COLS)
    half = (NA_K_ROWS - NA_Q_ROWS) // 2
    dr = np.clip(ki[None, :] - half - rl[:, None] + (NA_MAX_ROWS - 1), 0, n_dr - 1)
    oh_dc = jnp.asarray(dc[None] == np.arange(n_dc)[:, None, None, None], F32)
    oh_dr = jnp.asarray(dr[None] == np.arange(n_dr)[:, None, None], F32)
    cols = jnp.einsum('hab,bnck->hanck', rpb_l.astype(F32), oh_dc, precision='highest')
    cols = jnp.where(jnp.asarray(col_ok)[None, None], cols, NEG_INF)
    tb = jnp.einsum('hanck,ari->nhrcik', cols, oh_dr, precision='highest')
    tb = jnp.maximum(tb, NEG_INF).reshape(NA_COL_BLOCKS, NA_HEADS, NA_Q_ROWS * NA_COLS, NA_K_ROWS * NA_KCOLS)
    rows = GRID_W
    masks = []
    for r0 in (0, NA_Q_ROWS, rows - NA_Q_ROWS):
        r = r0 + rl
        rs = np.clip(r - NA_MAX_ROWS // 2, 0, rows - NA_MAX_ROWS)
        kr = r0 - half + ki
        ok = (kr[None, :] >= rs[:, None]) & (kr[None, :] < rs[:, None] + NA_MAX_ROWS)
        full = np.broadcast_to(ok[:, None, :, None], (NA_Q_ROWS, NA_COLS, NA_K_ROWS, NA_KCOLS))
        masks.append(np.where(full, 0.0, NEG_INF).reshape(NA_Q_ROWS * NA_COLS, NA_K_ROWS * NA_KCOLS))
    rm = jnp.asarray(np.stack(masks), F32)
    return tb, rm


def _outproj_kernel(x_ref, mod_ref, y_ref, na_ref, gm_ref, gw_ref, gb_ref, wrot_ref, w_ref, o_ref, mix_ref):
    gate = mod_ref[0, 5:6, :]
    n_rows = MIX_TILE // SSM_CHUNK
    n_lt = D_MODEL // LANES
    for t in range(SSM_CHUNK):
        yt = _merge_segments(lambda src, lt: y_ref[src, :, lt * LANES:(lt + 1) * LANES], t)
        yt = _gelu(yt)
        z = jnp.dot(yt.astype(BF16), gw_ref[t], preferred_element_type=F32) + gb_ref[t]
        ys = (yt * _sigmoid(z)).astype(BF16)
        mt = jnp.dot(ys, wrot_ref[t], preferred_element_type=F32)
        for j in range(n_lt):
            mix_ref[j, pl.ds(t, n_rows, stride=SSM_CHUNK), :] = mt[:, j * LANES:(j + 1) * LANES]
    mix = jnp.concatenate([mix_ref[j] for j in range(n_lt)], axis=-1)
    mix = mix + jnp.dot(na_ref[...], w_ref[0:NA_WIDTH, :], preferred_element_type=F32)
    mix = mix + jnp.dot(gm_ref[...], w_ref[NA_WIDTH:, :], preferred_element_type=F32)
    o_ref[...] = x_ref[...] + gate * mix


def _outproj(x, mod_l, y, na, gm, gw, gb, wrot, w, *, seg0, seg_tokens):
    n_tok = x.shape[0]
    n_rows = MIX_TILE // SSM_CHUNK
    tok = lambda width: pl.BlockSpec((MIX_TILE, width), lambda i: (i, 0))
    return pl.pallas_call(
        _outproj_kernel,
        grid=(n_tok // MIX_TILE,),
        in_specs=[
            tok(D_MODEL),
            pl.BlockSpec((1, N_MOD, D_MODEL), _seg_map(seg0, seg_tokens, MIX_TILE)),
            pl.BlockSpec((SSM_GROUPS, n_rows, SSM_ROW), lambda i: (0, i, 0)),
            tok(NA_WIDTH), tok(GM_WIDTH),
            _const_spec(gw.shape),
            _const_spec(gb.shape),
            _const_spec(wrot.shape),
            _const_spec(w.shape),
        ],
        out_specs=tok(D_MODEL),
        out_shape=jax.ShapeDtypeStruct((n_tok, D_MODEL), F32),
        scratch_shapes=[pltpu.VMEM((D_MODEL // LANES, MIX_TILE, LANES), F32)],
        compiler_params=_cparams(("arbitrary",)),
        name="outproj",
    )(x, mod_l, y, na, gm, gw, gb, wrot, w)


def _rotations(x, axes):
    return jnp.stack([jnp.roll(x, (t,) * len(axes), axis=axes) for t in range(SSM_CHUNK)], axis=0)


def _layer_weights(l, p):
    def ffn_in(w):
        gate = w[:, :D_FF].reshape(D_MODEL, N_FF_CHUNKS, FF_CHUNK)
        up = w[:, D_FF:].reshape(D_MODEL, N_FF_CHUNKS, FF_CHUNK)
        return jnp.transpose(jnp.concatenate([gate, up], axis=-1), (1, 0, 2)).astype(BF16)

    heads = np.arange(NA_WIDTH) // NA_HEAD_DIM
    hsum = jnp.asarray((heads[:, None] == heads[None, :]) / NA_HEAD_DIM, BF16)
    gmb = jnp.repeat(jnp.transpose(p['gm_bs'][l]).astype(F32), GM_WIDTH // GM_GROUPS, axis=1)
    lp = {k: p[k][l] for k in ('ssm_lambda_re', 'ssm_lambda_im', 'ssm_log_dt', 'ssm_b_re', 'ssm_b_im',
                               'ssm_c_re', 'ssm_c_im', 'ssm_d')}
    tb, rm = _na_tables(p['na_rpb'][l])
    w_in = p['w_in'][l].astype(BF16)
    w_out = p['w_out'][l].astype(BF16)
    g, c = SSM_GROUPS, SSM_CH
    w_ssm_rot = _rotations(w_in[:, :SSM_WIDTH].reshape(D_MODEL, g, c), (1,)).reshape(SSM_CHUNK, D_MODEL, SSM_WIDTH)
    glu_rot = _rotations(p['ssm_glu_w'][l].astype(BF16).reshape(g, c, g, c), (0, 2)).reshape(
        SSM_CHUNK, SSM_WIDTH, SSM_WIDTH)
    glub_rot = _rotations(p['ssm_glu_b'][l].astype(F32).reshape(g, c), (0,)).reshape(SSM_CHUNK, 1, SSM_WIDTH)
    w_out_rot = _rotations(w_out[:SSM_WIDTH].reshape(g, c, D_MODEL), (0,)).reshape(SSM_CHUNK, SSM_WIDTH, D_MODEL)
    return dict(
        g1=p['norm_ffn1'][l].reshape(1, D_MODEL), g2=p['norm_mix'][l].reshape(1, D_MODEL),
        g3=p['norm_ffn2'][l].reshape(1, D_MODEL),
        f1_in=ffn_in(p['ffn1_w_in'][l]), f1_out=p['ffn1_w_out'][l].astype(BF16),
        f2_in=ffn_in(p['ffn2_w_in'][l]), f2_out=p['ffn2_w_out'][l].astype(BF16),
        w_in=w_in[:, SSM_WIDTH:], w_ssm_rot=w_ssm_rot,
        w_out=w_out[SSM_WIDTH:], w_out_rot=w_out_rot,
        hsum=hsum,
        qg=jnp.tile(p['na_q_norm'][l].astype(F32), NA_HEADS).reshape(1, NA_WIDTH),
        kg=jnp.tile(p['na_k_norm'][l].astype(F32), NA_HEADS).reshape(1, NA_WIDTH),
        ws=p['gm_ws'][l].astype(BF16), gmb=gmb,
        ssm=_ssm_params(lp),
        glu_rot=glu_rot, glub_rot=glub_rot,
        tb=tb, rm=rm,
    )


def _pairs_major(t):
    b, s = t.shape[0], t.shape[1]
    return jnp.transpose(t.reshape(b, s, HEAD_PAIRS, LANES), (0, 2, 1, 3)).astype(BF16)


def _trunk_layer(x, mod_l, lw, *, n_batch, seq_len, is_ctx, ctx_kv=None, ssm_init=None):
    seg = dict(seg0=0 if is_ctx else 1, seg_tokens=x.shape[0] if is_ctx else seq_len)
    x = _ffn(x, mod_l, lw['g1'], lw['f1_in'], lw['f1_out'], mod_base=0, **seg)
    outs = _inproj(x, mod_l, lw['g2'], lw['w_in'], lw['w_ssm_rot'], lw['hsum'], lw['qg'], lw['kg'], lw['ws'],
                   lw['gmb'], is_ctx=is_ctx, **seg)
    u, q, k, v, gm = outs[:5]
    y, fin = _ssm_mixer_core(u, lw['ssm'], n_batch, seq_len, ssm_init)
    if is_ctx:
        na = _ctx_attn(q, k, v, n_batch)
    else:
        na = _na_attn(q, k, v, ctx_kv[0], ctx_kv[1], lw['tb'], lw['rm'], n_batch)
    x = _outproj(x, mod_l, y, na, gm, lw['glu_rot'], lw['glub_rot'], lw['w_out_rot'], lw['w_out'], **seg)
    x = _ffn(x, mod_l, lw['g3'], lw['f2_in'], lw['f2_out'], mod_base=6, **seg)
    return x, (outs[5:], fin)


def kernel(x_prompt, x_sample, c, cache_k, cache_v, state_ssm, c_ctx, w_ada, b_ada, norm_ffn1, ffn1_w_in, ffn1_w_out, norm_mix, w_in, w_out, ssm_lambda_re, ssm_lambda_im, ssm_log_dt, ssm_b_re, ssm_b_im, ssm_c_re, ssm_c_im, ssm_d, ssm_glu_w, ssm_glu_b, na_q_norm, na_k_norm, na_rpb, gm_ws, gm_bs, norm_ffn2, ffn2_w_in, ffn2_w_out):
    p = dict(norm_ffn1=norm_ffn1, ffn1_w_in=ffn1_w_in, ffn1_w_out=ffn1_w_out, norm_mix=norm_mix, w_in=w_in,
             w_out=w_out, ssm_lambda_re=ssm_lambda_re, ssm_lambda_im=ssm_lambda_im, ssm_log_dt=ssm_log_dt,
             ssm_b_re=ssm_b_re, ssm_b_im=ssm_b_im, ssm_c_re=ssm_c_re, ssm_c_im=ssm_c_im, ssm_d=ssm_d,
             ssm_glu_w=ssm_glu_w, ssm_glu_b=ssm_glu_b, na_q_norm=na_q_norm, na_k_norm=na_k_norm, na_rpb=na_rpb,
             gm_ws=gm_ws, gm_bs=gm_bs, norm_ffn2=norm_ffn2, ffn2_w_in=ffn2_w_in, ffn2_w_out=ffn2_w_out)
    batch, dec_batch = x_prompt.shape[0], x_sample.shape[0]
    cond8 = jnp.zeros((8, D_MODEL), F32).at[0].set(c_ctx).at[1:1 + dec_batch].set(c)
    mod = _adaln(cond8, w_ada, b_ada).reshape(DEPTH, 8, N_MOD, D_MODEL)

    xp = x_prompt.reshape(batch * SEQ, D_MODEL)
    xs = x_sample.reshape(dec_batch * DEC_SEQ, D_MODEL)
    new_k, new_v, new_s = [], [], []
    for l in range(DEPTH):
        lw = _layer_weights(l, p)
        xp, ((k_c, v_c), s_c) = _trunk_layer(xp, mod[l], lw, n_batch=batch, seq_len=SEQ, is_ctx=True)
        new_k.append(k_c.reshape(batch, SEQ, NA_HEADS, NA_HEAD_DIM))
        new_v.append(v_c.reshape(batch, SEQ, NA_HEADS, NA_HEAD_DIM))
        new_s.append(s_c)
        ctx_kv = (_pairs_major(cache_k[:, l]), _pairs_major(cache_v[:, l]))
        xs, _ = _trunk_layer(xs, mod[l], lw, n_batch=dec_batch, seq_len=DEC_SEQ, is_ctx=False,
                             ctx_kv=ctx_kv, ssm_init=state_ssm[:, l])
    return (xp.reshape(batch, SEQ, D_MODEL), xs.reshape(dec_batch, DEC_SEQ, D_MODEL),
            jnp.stack(new_k, axis=1), jnp.stack(new_v, axis=1), jnp.stack(new_s, axis=1))
```

```python
import functools

import numpy as np
import jax
import jax.numpy as jnp
from jax import lax
from jax.experimental import pallas as pl
from jax.experimental.pallas import tpu as pltpu

D_MODEL = 1024
DEPTH = 2
SEQ = 256
DEC_SEQ = 4096
PAST_LEN = 256
GRID_W = 64
SSM_WIDTH = 256
SSM_CH = 16
SSM_GROUPS = 16
SSM_STATE = 64
NA_WIDTH = 512
NA_HEAD_DIM = 64
NA_HEADS = 8
NA_MAX_ROWS = 8
NA_COLS = 16
GM_WIDTH = 256
GM_GROUPS = 4
GM_CHUNK = 128
D_FF = 2816
N_MOD = 9
RMS_EPS = 1e-6
LN_EPS = 1e-5
NEG_INF = -1e30

F32 = jnp.float32
BF16 = jnp.bfloat16

LANES = 128
SUBLANES = 8
MXU_DIM = 256
VMEM_LIMIT_BYTES = 56 * 1024 * 1024

FFN_TILE = 512
MIX_TILE = 1024
FF_CHUNK = MXU_DIM
N_FF_CHUNKS = D_FF // FF_CHUNK
SSM_CHUNK = 16
SSM_ROW = SSM_CHUNK * SSM_CH
SSM_GROUP_BLOCK = 4
SSM_W_COLS = 4 * LANES
SCAN_SHIFTS = (1, 2, 4)
N_SCAN_CONST = 3 * len(SCAN_SHIFTS) + 3
HEAD_PAIRS = NA_HEADS // 2
NA_Q_ROWS = 8
NA_K_ROWS = 16
NA_KCOLS = 2 * NA_COLS
NA_COL_BLOCKS = GRID_W // NA_COLS
NA_KC0 = (0, 8, 24, 32)
SEGS_PER_TILE = LANES // SSM_CH


def _silu(x):
    return x * (1.0 / (1.0 + jnp.exp(-x)))


def _sigmoid(x):
    return 1.0 / (1.0 + jnp.exp(-x))


def _gelu(x):
    return 0.5 * x * (1.0 + jnp.tanh(0.7978845608028654 * (x + 0.044715 * (x * x * x))))


def _cparams(sem):
    return pltpu.CompilerParams(dimension_semantics=sem, vmem_limit_bytes=VMEM_LIMIT_BYTES)


def _const_spec(shape):
    nd = len(shape)
    return pl.BlockSpec(shape, lambda *_: (0,) * nd, pipeline_mode=pl.Buffered(1))


def _ada_kernel(c_ref, w_ref, b_ref, o_ref):
    s = _silu(c_ref[...]).astype(BF16)
    w = w_ref[0].astype(BF16)
    o_ref[0] = jnp.dot(s, w, preferred_element_type=F32) + b_ref[0]


def _adaln(cond8, w_ada, b_ada):
    tn = D_MODEL
    ncol = N_MOD * D_MODEL
    return pl.pallas_call(
        _ada_kernel,
        grid=(DEPTH, ncol // tn),
        in_specs=[
            pl.BlockSpec((8, D_MODEL), lambda l, j: (0, 0)),
            pl.BlockSpec((1, D_MODEL, tn), lambda l, j: (l, 0, j)),
            pl.BlockSpec((1, 1, tn), lambda l, j: (l, 0, j)),
        ],
        out_specs=pl.BlockSpec((1, 8, tn), lambda l, j: (l, 0, j)),
        out_shape=jax.ShapeDtypeStruct((DEPTH, 8, ncol), F32),
        compiler_params=_cparams(("arbitrary", "arbitrary")),
        name="adaln",
    )(cond8, w_ada, b_ada.reshape(DEPTH, 1, ncol))


def _normed(x, g_row, shift, scale):
    ms = jnp.mean(x * x, axis=-1, keepdims=True)
    h = x * lax.rsqrt(ms + RMS_EPS) * g_row
    return h * (1.0 + scale) + shift


def _seg_map(seg0, seg_tokens, tile):
    tiles_per_seg = seg_tokens // tile
    return lambda i: (seg0 + i // tiles_per_seg, 0, 0)


def _ffn_kernel(x_ref, mod_ref, g_ref, win_ref, wout_ref, o_ref, h_ref, act_ref, *, mod_base):
    x = x_ref[...]
    shift = mod_ref[0, mod_base:mod_base + 1, :]
    scale = mod_ref[0, mod_base + 1:mod_base + 2, :]
    gate = mod_ref[0, mod_base + 2:mod_base + 3, :]
    h_ref[...] = _normed(x, g_ref[...], shift, scale).astype(BF16)
    for j in range(N_FF_CHUNKS):
        gu = jnp.dot(h_ref[...], win_ref[j], preferred_element_type=F32)
        a = _silu(gu[:, :FF_CHUNK]) * gu[:, FF_CHUNK:]
        act_ref[:, j * FF_CHUNK:(j + 1) * FF_CHUNK] = a.astype(BF16)
    for n in range(D_MODEL // MXU_DIM):
        cols = slice(n * MXU_DIM, (n + 1) * MXU_DIM)
        y = jnp.dot(act_ref[...], wout_ref[:, cols], preferred_element_type=F32)
        o_ref[:, cols] = x_ref[:, cols] + 0.5 * gate[:, cols] * y


def _ffn(x, mod_l, g, w_in3, w_out, *, mod_base, seg0, seg_tokens):
    n_tok = x.shape[0]
    return pl.pallas_call(
        functools.partial(_ffn_kernel, mod_base=mod_base),
        grid=(n_tok // FFN_TILE,),
        in_specs=[
            pl.BlockSpec((FFN_TILE, D_MODEL), lambda i: (i, 0)),
            pl.BlockSpec((1, N_MOD, D_MODEL), _seg_map(seg0, seg_tokens, FFN_TILE)),
            _const_spec((1, D_MODEL)),
            _const_spec((N_FF_CHUNKS, D_MODEL, 2 * FF_CHUNK)),
            _const_spec((D_FF, D_MODEL)),
        ],
        out_specs=pl.BlockSpec((FFN_TILE, D_MODEL), lambda i: (i, 0)),
        out_shape=jax.ShapeDtypeStruct((n_tok, D_MODEL), F32),
        scratch_shapes=[pltpu.VMEM((FFN_TILE, D_MODEL), BF16), pltpu.VMEM((FFN_TILE, D_FF), BF16)],
        compiler_params=_cparams(("arbitrary",)),
        name="ffn",
    )(x, mod_l, g, w_in3, w_out)


def _segment_ids():
    return lax.broadcasted_iota(jnp.int32, (1, LANES), 1) // SSM_CH


def _merge_segments(load_piece, rot):
    seg = _segment_ids()
    tiles = []
    for lt in range(SSM_ROW // LANES):
        acc = None
        for jj in range(SEGS_PER_TILE):
            src = (lt * SEGS_PER_TILE + jj - rot) % SSM_CHUNK
            piece = load_piece(src, lt)
            acc = piece if acc is None else jnp.where(seg == jj, piece, acc)
        tiles.append(acc)
    return jnp.concatenate(tiles, axis=-1)


def _inproj_kernel(x_ref, mod_ref, g_ref, w_ref, wrot_ref, hsum_ref, qg_ref, kg_ref, ws_ref, gmb_ref,
                   u_ref, q_ref, k_ref, v_ref, gm_ref, *rest):
    *tok_refs, h_ref, hs_ref, z_ref = rest
    shift = mod_ref[0, 3:4, :]
    scale = mod_ref[0, 4:5, :]
    h = _normed(x_ref[...], g_ref[...], shift, scale)
    h_ref[...] = h.astype(BF16)
    n_rows = MIX_TILE // SSM_CHUNK
    n_lt = D_MODEL // LANES
    for j in range(n_lt):
        hs_ref[j] = h[:, j * LANES:(j + 1) * LANES]

    for t in range(SSM_CHUNK):
        ht = jnp.concatenate([hs_ref[j, pl.ds(t, n_rows, stride=SSM_CHUNK), :] for j in range(n_lt)], axis=-1)
        z_ref[t] = jnp.dot(ht.astype(BF16), wrot_ref[t], preferred_element_type=F32)
    for g in range(SSM_GROUPS):
        u_ref[g] = _merge_segments(lambda src, lt: z_ref[src, :, lt * LANES:(lt + 1) * LANES], g)

    def proj(c0, width):
        return jnp.dot(h_ref[...], w_ref[:, c0:c0 + width], preferred_element_type=F32)

    def head_norm(t, gain_row):
        ms = jnp.dot((t * t).astype(BF16), hsum_ref[...], preferred_element_type=F32)
        return t * lax.rsqrt(ms + RMS_EPS) * gain_row

    q = head_norm(proj(0, NA_WIDTH), qg_ref[...]) * (NA_HEAD_DIM ** -0.5)
    k = head_norm(proj(NA_WIDTH, NA_WIDTH), kg_ref[...])
    v = proj(2 * NA_WIDTH, NA_WIDTH)
    for p in range(HEAD_PAIRS):
        lanes = slice(p * LANES, (p + 1) * LANES)
        q_ref[p] = q[:, lanes].astype(q_ref.dtype)
        k_ref[p] = k[:, lanes].astype(k_ref.dtype)
        v_ref[p] = v[:, lanes].astype(v_ref.dtype)
    if tok_refs:
        tok_refs[0][...] = k
        tok_refs[1][...] = v

    a = _gelu(proj(3 * NA_WIDTH, 2 * GM_WIDTH))
    u = a[:, :GM_WIDTH]
    vv = a[:, GM_WIDTH:]
    mu = jnp.mean(vv, axis=-1, keepdims=True)
    vc = vv - mu
    var = jnp.mean(vc * vc, axis=-1, keepdims=True)
    vln = (vc * lax.rsqrt(var + LN_EPS)).astype(BF16)
    lane = lax.broadcasted_iota(jnp.int32, (1, GM_WIDTH), 1)
    gw = GM_WIDTH // GM_GROUPS
    for ci in range(MIX_TILE // GM_CHUNK):
        rows = slice(ci * GM_CHUNK, (ci + 1) * GM_CHUNK)
        vch = vln[rows, :]
        sp = gmb_ref[...]
        for gi in range(GM_GROUPS):
            t = jnp.dot(ws_ref[gi], vch, preferred_element_type=F32)
            sp = sp + jnp.where((lane >= gi * gw) & (lane < (gi + 1) * gw), t, 0.0)
        gm_ref[rows, :] = (u[rows, :] * sp).astype(BF16)


def _inproj(x, mod_l, g, w, wrot, hsum, qg, kg, ws, gmb, *, seg0, seg_tokens, is_ctx):
    n_tok = x.shape[0]
    kv_dtype = BF16 if is_ctx else F32
    n_rows = MIX_TILE // SSM_CHUNK
    pair_spec = pl.BlockSpec((HEAD_PAIRS, MIX_TILE, LANES), lambda i: (0, i, 0))
    out_specs = [
        pl.BlockSpec((SSM_GROUPS, n_rows, SSM_ROW), lambda i: (0, i, 0)),
        pair_spec, pair_spec, pair_spec,
        pl.BlockSpec((MIX_TILE, GM_WIDTH), lambda i: (i, 0)),
    ]
    out_shape = [
        jax.ShapeDtypeStruct((SSM_GROUPS, n_tok // SSM_CHUNK, SSM_ROW), F32),
        jax.ShapeDtypeStruct((HEAD_PAIRS, n_tok, LANES), BF16),
        jax.ShapeDtypeStruct((HEAD_PAIRS, n_tok, LANES), kv_dtype),
        jax.ShapeDtypeStruct((HEAD_PAIRS, n_tok, LANES), kv_dtype),
        jax.ShapeDtypeStruct((n_tok, GM_WIDTH), BF16),
    ]
    if is_ctx:
        out_specs += [pl.BlockSpec((MIX_TILE, NA_WIDTH), lambda i: (i, 0))] * 2
        out_shape += [jax.ShapeDtypeStruct((n_tok, NA_WIDTH), F32)] * 2
    return pl.pallas_call(
        _inproj_kernel,
        grid=(n_tok // MIX_TILE,),
        in_specs=[
            pl.BlockSpec((MIX_TILE, D_MODEL), lambda i: (i, 0)),
            pl.BlockSpec((1, N_MOD, D_MODEL), _seg_map(seg0, seg_tokens, MIX_TILE)),
            _const_spec((1, D_MODEL)),
            _const_spec(w.shape),
            _const_spec(wrot.shape),
            _const_spec((NA_WIDTH, NA_WIDTH)),
            _const_spec((1, NA_WIDTH)),
            _const_spec((1, NA_WIDTH)),
            _const_spec((GM_GROUPS, GM_CHUNK, GM_CHUNK)),
            _const_spec((GM_CHUNK, GM_WIDTH)),
        ],
        out_specs=out_specs,
        out_shape=out_shape,
        scratch_shapes=[pltpu.VMEM((MIX_TILE, D_MODEL), BF16), pltpu.VMEM((D_MODEL // LANES, MIX_TILE, LANES), F32),
                        pltpu.VMEM((SSM_CHUNK, n_rows, SSM_ROW), F32)],
        compiler_params=_cparams(("arbitrary",)),
        name="inproj_ctx" if is_ctx else "inproj_lat",
    )(x, mod_l, g, w, wrot, hsum, qg, kg, ws, gmb)


def _shift_rows(x, d, row, down):
    if down:
        return jnp.where(row >= d, pltpu.roll(x, d, axis=0), 0.0)
    return jnp.where(row < SUBLANES - d, pltpu.roll(x, SUBLANES - d, axis=0), 0.0)


def _scan_tile(w, wp, c, cp, cst, row, down):
    x, xp = w, wp
    for si, d in enumerate(SCAN_SHIFTS):
        a, b, bp = cst[3 * si], cst[3 * si + 1], cst[3 * si + 2]
        xs, xps = _shift_rows(x, d, row, down), _shift_rows(xp, d, row, down)
        x, xp = x + a * xs + b * xps, xp + a * xps + bp * xs
    a, b, bp = cst[N_SCAN_CONST - 3], cst[N_SCAN_CONST - 2], cst[N_SCAN_CONST - 1]
    s = x + a * c + b * cp
    sp = xp + a * cp + bp * c
    edge = 0 if down else SUBLANES - 1
    last = SUBLANES - 1 if down else 0
    seen = jnp.where(row == edge, c, _shift_rows(s, 1, row, down))
    c_new = jnp.broadcast_to(s[last:last + 1, :], s.shape)
    cp_new = jnp.broadcast_to(sp[last:last + 1, :], sp.shape)
    return seen, c_new, cp_new


def _ssm_kernel(u_ref, mp_ref, q_ref, cst_ref, dvec_ref, s0_ref, y_ref, fin_ref, w_ref, sq_ref, *, n_batch, n_chunks):
    gb = u_ref.shape[0]
    n_rows = n_batch * n_chunks
    row_blk = min(n_rows, 512)
    for g in range(gb):
        for r0 in range(0, n_rows, row_blk):
            rows = slice(r0, r0 + row_blk)
            ug = u_ref[g, rows, :]
            wy = jnp.dot(ug.astype(BF16), mp_ref[g], preferred_element_type=F32)
            y_ref[g, rows, :] = wy[:, :SSM_ROW] + ug * dvec_ref[g]
            w_ref[g, rows, :] = wy[:, SSM_ROW:]

    n_tiles = n_chunks // SUBLANES
    row = lax.broadcasted_iota(jnp.int32, (SUBLANES, LANES), 0)

    def tile_step(b, kt, carry):
        out = []
        for g in range(gb):
            cf, cfp, cb, cbp = carry[4 * g:4 * g + 4]
            rf = pl.multiple_of(b * n_chunks + kt * SUBLANES, SUBLANES)
            rb = pl.multiple_of(b * n_chunks + (n_tiles - 1 - kt) * SUBLANES, SUBLANES)
            wf = w_ref[g, pl.ds(rf, SUBLANES), :]
            wb = w_ref[g, pl.ds(rb, SUBLANES), :]
            cst_f = [cst_ref[g, 0, i] for i in range(N_SCAN_CONST)]
            cst_b = [cst_ref[g, 1, i] for i in range(N_SCAN_CONST)]
            seen_f, cf, cfp = _scan_tile(wf[:, 0:LANES], wf[:, LANES:2 * LANES], cf, cfp, cst_f, row, True)
            seen_b, cb, cbp = _scan_tile(wb[:, 2 * LANES:3 * LANES], wb[:, 3 * LANES:], cb, cbp, cst_b, row, False)
            sq_ref[g, pl.ds(rf, SUBLANES), 0:LANES] = seen_f
            sq_ref[g, pl.ds(rb, SUBLANES), LANES:2 * LANES] = seen_b
            out += [cf, cfp, cb, cbp]
        return tuple(out)

    def batch_body(b, _):
        carry = []
        for g in range(gb):
            for i in range(4):
                carry.append(jnp.broadcast_to(s0_ref[g, i, pl.ds(b, 1), :], (SUBLANES, LANES)))
        carry = tuple(carry)
        if n_tiles <= 2:
            for kt in range(n_tiles):
                carry = tile_step(b, kt, carry)
        else:
            carry = lax.fori_loop(0, n_tiles, lambda kt, c: tile_step(b, kt, c), carry)
        for g in range(gb):
            fin_ref[g, 0, pl.ds(b, 1), :] = carry[4 * g][0:1, :]
            fin_ref[g, 1, pl.ds(b, 1), :] = carry[4 * g + 2][0:1, :]
        return 0

    lax.fori_loop(0, n_batch, batch_body, 0)

    for g in range(gb):
        for r0 in range(0, n_rows, row_blk):
            rows = slice(r0, r0 + row_blk)
            y_ref[g, rows, :] += jnp.dot(sq_ref[g, rows, :].astype(BF16), q_ref[g], preferred_element_type=F32)


def _ssm_core(u, mp, q, cst, dvec, s0, *, n_batch, n_chunks):
    n_rows = n_batch * n_chunks
    gb = SSM_GROUP_BLOCK
    g3 = lambda i: (i, 0, 0)
    g4 = lambda i: (i, 0, 0, 0)
    g5 = lambda i: (i, 0, 0, 0, 0)
    return pl.pallas_call(
        functools.partial(_ssm_kernel, n_batch=n_batch, n_chunks=n_chunks),
        grid=(SSM_GROUPS // gb,),
        in_specs=[
            pl.BlockSpec((gb, n_rows, SSM_ROW), g3),
            pl.BlockSpec((gb, SSM_ROW, SSM_ROW + SSM_W_COLS), g3),
            pl.BlockSpec((gb, 2 * LANES, SSM_ROW), g3),
            pl.BlockSpec((gb, 2, N_SCAN_CONST, SUBLANES, LANES), g5),
            pl.BlockSpec((gb, 1, SSM_ROW), g3),
            pl.BlockSpec((gb, 4, n_batch, LANES), g4),
        ],
        out_specs=[
            pl.BlockSpec((gb, n_rows, SSM_ROW), g3),
            pl.BlockSpec((gb, 2, n_batch, LANES), g4),
        ],
        out_shape=[
            jax.ShapeDtypeStruct((SSM_GROUPS, n_rows, SSM_ROW), F32),
            jax.ShapeDtypeStruct((SSM_GROUPS, 2, n_batch, LANES), F32),
        ],
        scratch_shapes=[pltpu.VMEM((gb, n_rows, SSM_W_COLS), F32), pltpu.VMEM((gb, n_rows, 2 * LANES), F32)],
        compiler_params=_cparams(("arbitrary",)),
        name="ssm_core",
    )(u, mp, q, cst, dvec, s0)


def _cmul(a, b):
    return a[0] * b[0] - a[1] * b[1], a[0] * b[1] + a[1] * b[0]


_CHUNK_IDX = np.arange(SSM_CHUNK)
_ROT = ((_CHUNK_IDX[None, None, :] - _CHUNK_IDX[:, None, None]) % SSM_CHUNK == _CHUNK_IDX[None, :, None])
_LAG = (_CHUNK_IDX[None, None, :] - _CHUNK_IDX[None, :, None] == _CHUNK_IDX[:, None, None],
        _CHUNK_IDX[None, :, None] - _CHUNK_IDX[None, None, :] == _CHUNK_IDX[:, None, None])


def _ssm_params(lp):
    t = SSM_CHUNK
    hi = dict(precision='highest')
    rot = jnp.asarray(_ROT, BF16)
    mats, pcols, qrows, csts = 0.0, [], [], []
    steps = jnp.arange(t + 1, dtype=F32)[:, None, None]
    tile_steps = np.arange(1, SUBLANES + 1, dtype=np.float32) * t
    for d in range(2):
        lam = (lp['ssm_lambda_re'][d].astype(F32), lp['ssm_lambda_im'][d].astype(F32))
        dt = jnp.exp(lp['ssm_log_dt'][d].astype(F32))[:, None]
        arg = (lam[0] * dt, lam[1] * dt)

        def powers(n):
            mag = jnp.exp(n * arg[0][None])
            return mag * jnp.cos(n * arg[1][None]), mag * jnp.sin(n * arg[1][None])

        pw = powers(steps)
        num = (pw[0][1] - 1.0, pw[1][1])
        den = lam[0] * lam[0] + lam[1] * lam[1]
        zoh = ((num[0] * lam[0] + num[1] * lam[1]) / den, (num[1] * lam[0] - num[0] * lam[1]) / den)
        b_mat = (lp['ssm_b_re'][d].astype(F32), lp['ssm_b_im'][d].astype(F32))
        bbar = _cmul((zoh[0][..., None], zoh[1][..., None]), b_mat)
        c_mat = (lp['ssm_c_re'][d].astype(F32), lp['ssm_c_im'][d].astype(F32))
        pb = _cmul((pw[0][:t, :, :, None], pw[1][:t, :, :, None]), (bbar[0][None], bbar[1][None]))
        kern = (jnp.einsum('gcp,ngpd->ngcd', c_mat[0], pb[0], **hi)
                - jnp.einsum('gcp,ngpd->ngcd', c_mat[1], pb[1], **hi))
        mats = mats + jnp.einsum('ngcd,nji->gjdic', kern, jnp.asarray(_LAG[d], F32), **hi)
        pm = [jnp.flip(x, 0) if d == 0 else x for x in pb]
        pm = [jnp.transpose(x, (1, 0, 3, 2)) for x in pm]
        pcols += [pm[0], pm[1], pm[1], pm[0]]
        qp = [x[1:] if d == 0 else jnp.flip(x[1:], 0) for x in pw]
        qm = _cmul((c_mat[0][None], c_mat[1][None]), (qp[0][:, :, None, :], qp[1][:, :, None, :]))
        qm = [jnp.transpose(x, (1, 3, 0, 2)) for x in qm]
        qrows += [qm[0], -qm[1]]
        n_shift = len(SCAN_SHIFTS)
        tile_pw = tile_steps if d == 0 else tile_steps[::-1]
        expo = np.concatenate([np.asarray(SCAN_SHIFTS, np.float32) * t, tile_pw])
        pr, pi = powers(jnp.asarray(expo, F32)[:, None, None])
        forms = jnp.stack([jnp.concatenate([pr, pr], -1), jnp.concatenate([-pi, pi], -1),
                           jnp.concatenate([pi, -pi], -1)], axis=1)
        per_shift = jnp.broadcast_to(forms[:n_shift].reshape(3 * n_shift, SSM_GROUPS, 1, LANES),
                                     (3 * n_shift, SSM_GROUPS, SUBLANES, LANES))
        per_row = jnp.transpose(forms[n_shift:], (1, 2, 0, 3))
        csts.append(jnp.transpose(jnp.concatenate([per_shift, per_row], axis=0), (1, 0, 2, 3)))
    g = SSM_GROUPS
    mats = jnp.einsum('gjdic,gjJ,giI->gJdIc', mats.astype(BF16), rot, rot).reshape(g, SSM_ROW, SSM_ROW)
    pcat = jnp.einsum('gjdp,gjJ->gJdp', jnp.concatenate(pcols, axis=-1).astype(BF16), rot)
    q = jnp.einsum('gpic,giI->gpIc', jnp.concatenate(qrows, axis=1).astype(BF16), rot).reshape(g, 2 * LANES, SSM_ROW)
    mp = jnp.concatenate([mats, pcat.reshape(g, SSM_ROW, SSM_W_COLS)], axis=-1)
    cst = jnp.stack(csts, axis=1)
    dvec = jnp.tile(lp['ssm_d'].astype(F32).reshape(SSM_GROUPS, 1, SSM_CH), (1, 1, SSM_CHUNK))
    return mp, q, cst, dvec


def _ssm_mixer_core(u, params, n_batch, seq_len, init_state):
    mp, q, cst, dvec = params
    n_chunks = seq_len // SSM_CHUNK
    if init_state is None:
        s0 = jnp.zeros((SSM_GROUPS, 4, n_batch, LANES), F32)
    else:
        st = jnp.transpose(init_state.astype(F32), (2, 1, 0, 3, 4))
        re, im = st[..., 0], st[..., 1]
        s0 = jnp.stack([jnp.concatenate([re[:, 0], im[:, 0]], -1), jnp.concatenate([im[:, 0], re[:, 0]], -1),
                        jnp.concatenate([re[:, 1], im[:, 1]], -1), jnp.concatenate([im[:, 1], re[:, 1]], -1)], axis=1)
    y, fin = _ssm_core(u, mp, q, cst, dvec, s0, n_batch=n_batch, n_chunks=n_chunks)
    fin = jnp.stack([fin[..., :SSM_STATE], fin[..., SSM_STATE:]], axis=-1)
    return y, jnp.transpose(fin, (2, 1, 0, 3, 4))


def _head_lane_mask(hh):
    lane = lax.broadcasted_iota(jnp.int32, (1, LANES), 1)
    return (lane >= hh * NA_HEAD_DIM) & (lane < (hh + 1) * NA_HEAD_DIM)


_NT_DIMS = (((1,), (1,)), ((), ()))


def _ctx_attn_kernel(q_ref, k_ref, v_ref, o_ref):
    for p in range(HEAD_PAIRS):
        qp, kp, vp = q_ref[p], k_ref[p], v_ref[p]
        acc = jnp.zeros((SEQ, LANES), F32)
        for hh in range(2):
            lm = _head_lane_mask(hh)
            qm = jnp.where(lm, qp, jnp.zeros_like(qp))
            s = lax.dot_general(qm, kp, _NT_DIMS, preferred_element_type=F32)
            e = jnp.exp(s - jnp.max(s, axis=-1, keepdims=True))
            o = jnp.dot(e.astype(BF16), vp, preferred_element_type=F32)
            acc = jnp.where(lm, o / jnp.sum(e, axis=-1, keepdims=True), acc)
        o_ref[:, p * LANES:(p + 1) * LANES] = acc.astype(BF16)


def _ctx_attn(q, k, v, n_batch):
    spec = pl.BlockSpec((HEAD_PAIRS, SEQ, LANES), lambda b: (0, b, 0))
    return pl.pallas_call(
        _ctx_attn_kernel,
        grid=(n_batch,),
        in_specs=[spec, spec, spec],
        out_specs=pl.BlockSpec((SEQ, NA_WIDTH), lambda b: (b, 0)),
        out_shape=jax.ShapeDtypeStruct((n_batch * SEQ, NA_WIDTH), BF16),
        compiler_params=_cparams(("arbitrary",)),
        name="ctx_attn",
    )(q, k, v)


def _na_kernel(q_ref, k_ref, v_ref, kc_ref, vc_ref, tb_ref, rm_ref, o_ref, qb_ref, os_ref):
    rt = pl.program_id(1)
    n_rt = pl.num_programs(1)
    r0 = rt * NA_Q_ROWS
    ttype = jnp.where(rt == 0, 0, jnp.where(rt == n_rt - 1, 2, 1))
    rmask = rm_ref[ttype]
    blk_q = NA_Q_ROWS * NA_COLS

    def pair_body(p, carry):
        for n in range(NA_COL_BLOCKS):
            for rl in range(NA_Q_ROWS):
                qb_ref[n * blk_q + rl * NA_COLS:n * blk_q + (rl + 1) * NA_COLS, :] = (
                    q_ref[p, rl * GRID_W + n * NA_COLS:rl * GRID_W + (n + 1) * NA_COLS, :])
        kcp, vcp = kc_ref[0, p], vc_ref[0, p]
        qall = qb_ref[...]
        sc_all = []
        for hh in range(2):
            qm = jnp.where(_head_lane_mask(hh), qall, jnp.zeros_like(qall))
            sc_all.append(lax.dot_general(qm, kcp, _NT_DIMS, preferred_element_type=F32))
        for n in range(NA_COL_BLOCKS):
            kparts, vparts = [], []
            for i in range(NA_K_ROWS):
                kr = jnp.clip(r0 - (NA_K_ROWS - NA_Q_ROWS) // 2 + i, 0, GRID_W - 1)
                start = pl.multiple_of(kr * GRID_W + NA_KC0[n], 8)
                kparts.append(k_ref[p, pl.ds(start, NA_KCOLS), :])
                vparts.append(v_ref[p, pl.ds(start, NA_KCOLS), :])
            kblk = jnp.concatenate(kparts, axis=0).astype(BF16)
            vblk = jnp.concatenate(vparts, axis=0).astype(BF16)
            qblk = qall[n * blk_q:(n + 1) * blk_q, :]
            oblk = jnp.zeros((blk_q, LANES), F32)
            for hh in range(2):
                lm = _head_lane_mask(hh)
                qm = jnp.where(lm, qblk, jnp.zeros_like(qblk))
                s = lax.dot_general(qm, kblk, _NT_DIMS, preferred_element_type=F32)
                s = s + tb_ref[n, 2 * p + hh] + rmask
                sc = sc_all[hh][n * blk_q:(n + 1) * blk_q, :]
                m = jnp.maximum(jnp.max(s, axis=-1, keepdims=True), jnp.max(sc, axis=-1, keepdims=True))
                e = jnp.exp(s - m)
                ec = jnp.exp(sc - m)
                den = jnp.sum(e, axis=-1, keepdims=True) + jnp.sum(ec, axis=-1, keepdims=True)
                o = (jnp.dot(e.astype(BF16), vblk, preferred_element_type=F32)
                     + jnp.dot(ec.astype(BF16), vcp, preferred_element_type=F32))
                oblk = jnp.where(lm, o / den, oblk)
            for rl in range(NA_Q_ROWS):
                os_ref[p, rl * GRID_W + n * NA_COLS:rl * GRID_W + (n + 1) * NA_COLS, :] = (
                    oblk[rl * NA_COLS:(rl + 1) * NA_COLS, :])
        return carry

    lax.fori_loop(0, HEAD_PAIRS, pair_body, 0)
    for p in range(HEAD_PAIRS):
        o_ref[:, p * LANES:(p + 1) * LANES] = os_ref[p].astype(BF16)


def _na_attn(q, k, v, kc, vc, tb, rm, n_batch):
    tile_tok = NA_Q_ROWS * GRID_W
    n_rt = DEC_SEQ // tile_tok
    return pl.pallas_call(
        _na_kernel,
        grid=(n_batch, n_rt),
        in_specs=[
            pl.BlockSpec((HEAD_PAIRS, tile_tok, LANES), lambda b, r: (0, b * n_rt + r, 0)),
            pl.BlockSpec((HEAD_PAIRS, DEC_SEQ, LANES), lambda b, r: (0, b, 0), pipeline_mode=pl.Buffered(1)),
            pl.BlockSpec((HEAD_PAIRS, DEC_SEQ, LANES), lambda b, r: (0, b, 0), pipeline_mode=pl.Buffered(1)),
            pl.BlockSpec((1, HEAD_PAIRS, PAST_LEN, LANES), lambda b, r: (b, 0, 0, 0)),
            pl.BlockSpec((1, HEAD_PAIRS, PAST_LEN, LANES), lambda b, r: (b, 0, 0, 0)),
            _const_spec(tb.shape),
            _const_spec(rm.shape),
        ],
        out_specs=pl.BlockSpec((tile_tok, NA_WIDTH), lambda b, r: (b * n_rt + r, 0)),
        out_shape=jax.ShapeDtypeStruct((n_batch * DEC_SEQ, NA_WIDTH), BF16),
        scratch_shapes=[pltpu.VMEM((tile_tok, LANES), BF16), pltpu.VMEM((HEAD_PAIRS, tile_tok, LANES), F32)],
        compiler_params=_cparams(("arbitrary", "arbitrary")),
        name="na_attn",
    )(q, k, v, kc, vc, tb, rm)


def _na_tables(rpb_l):
    n = np.arange(NA_COL_BLOCKS)
    cc = np.arange(NA_COLS)
    kk = np.arange(NA_KCOLS)
    rl = np.arange(NA_Q_ROWS)
    ki = np.arange(NA_K_ROWS)
    n_dr, n_dc = 2 * NA_MAX_ROWS - 1, 2 * NA_COLS - 1
    c = n[:, None] * NA_COLS + cc[None, :]
    kcol = np.asarray(NA_KC0)[:, None] + kk[None, :]
    dc = np.clip(kcol[:, None, :] - c[:, :, None], -(NA_COLS - 1), NA_COLS - 1) + (NA_COLS - 1)
    cs = np.clip(c - NA_COLS // 2, 0, GRID_W - NA_COLS)
    col_ok = (kcol[:, None, :] >= cs[:, :, None]) & (kcol[:, None, :] < cs[:, :, None] + NA_COLS)
    half = (NA_K_ROWS - NA_Q_ROWS) // 2
    dr = np.clip(ki[None, :] - half - rl[:, None] + (NA_MAX_ROWS - 1), 0, n_dr - 1)
    oh_dc = jnp.asarray(dc[None] == np.arange(n_dc)[:, None, None, None], F32)
    oh_dr = jnp.asarray(dr[None] == np.arange(n_dr)[:, None, None], F32)
    cols = jnp.einsum('hab,bnck->hanck', rpb_l.astype(F32), oh_dc, precision='highest')
    cols = jnp.where(jnp.asarray(col_ok)[None, None], cols, NEG_INF)
    tb = jnp.einsum('hanck,ari->nhrcik', cols, oh_dr, precision='highest')
    tb = tb.reshape(NA_COL_BLOCKS, NA_HEADS, NA_Q_ROWS * NA_COLS, NA_K_ROWS * NA_KCOLS)
    rows = GRID_W
    masks = []
    for r0 in (0, NA_Q_ROWS, rows - NA_Q_ROWS):
        r = r0 + rl
        rs = np.clip(r - NA_MAX_ROWS // 2, 0, rows - NA_MAX_ROWS)
        kr = r0 - half + ki
        ok = (kr[None, :] >= rs[:, None]) & (kr[None, :] < rs[:, None] + NA_MAX_ROWS)
        full = np.broadcast_to(ok[:, None, :, None], (NA_Q_ROWS, NA_COLS, NA_K_ROWS, NA_KCOLS))
        masks.append(np.where(full, 0.0, NEG_INF).reshape(NA_Q_ROWS * NA_COLS, NA_K_ROWS * NA_KCOLS))
    rm = jnp.asarray(np.stack(masks), F32)
    return tb, rm


def _outproj_kernel(x_ref, mod_ref, y_ref, na_ref, gm_ref, gw_ref, gb_ref, wrot_ref, w_ref, o_ref, mix_ref):
    gate = mod_ref[0, 5:6, :]
    n_rows = MIX_TILE // SSM_CHUNK
    n_lt = D_MODEL // LANES
    for t in range(SSM_CHUNK):
        yt = _merge_segments(lambda src, lt: y_ref[src, :, lt * LANES:(lt + 1) * LANES], t)
        yt = _gelu(yt)
        z = jnp.dot(yt.astype(BF16), gw_ref[t], preferred_element_type=F32) + gb_ref[t]
        ys = (yt * _sigmoid(z)).astype(BF16)
        mt = jnp.dot(ys, wrot_ref[t], preferred_element_type=F32)
        for j in range(n_lt):
            mix_ref[j, pl.ds(t, n_rows, stride=SSM_CHUNK), :] = mt[:, j * LANES:(j + 1) * LANES]
    mix = jnp.concatenate([mix_ref[j] for j in range(n_lt)], axis=-1)
    mix = mix + jnp.dot(na_ref[...], w_ref[0:NA_WIDTH, :], preferred_element_type=F32)
    mix = mix + jnp.dot(gm_ref[...], w_ref[NA_WIDTH:, :], preferred_element_type=F32)
    o_ref[...] = x_ref[...] + gate * mix


def _outproj(x, mod_l, y, na, gm, gw, gb, wrot, w, *, seg0, seg_tokens):
    n_tok = x.shape[0]
    n_rows = MIX_TILE // SSM_CHUNK
    tok = lambda width: pl.BlockSpec((MIX_TILE, width), lambda i: (i, 0))
    return pl.pallas_call(
        _outproj_kernel,
        grid=(n_tok // MIX_TILE,),
        in_specs=[
            tok(D_MODEL),
            pl.BlockSpec((1, N_MOD, D_MODEL), _seg_map(seg0, seg_tokens, MIX_TILE)),
            pl.BlockSpec((SSM_GROUPS, n_rows, SSM_ROW), lambda i: (0, i, 0)),
            tok(NA_WIDTH), tok(GM_WIDTH),
            _const_spec(gw.shape),
            _const_spec(gb.shape),
            _const_spec(wrot.shape),
            _const_spec(w.shape),
        ],
        out_specs=tok(D_MODEL),
        out_shape=jax.ShapeDtypeStruct((n_tok, D_MODEL), F32),
        scratch_shapes=[pltpu.VMEM((D_MODEL // LANES, MIX_TILE, LANES), F32)],
        compiler_params=_cparams(("arbitrary",)),
        name="outproj",
    )(x, mod_l, y, na, gm, gw, gb, wrot, w)


def _layer_weights(l, p):
    def ffn_in(w):
        gate = w[:, :D_FF].reshape(D_MODEL, N_FF_CHUNKS, FF_CHUNK)
        up = w[:, D_FF:].reshape(D_MODEL, N_FF_CHUNKS, FF_CHUNK)
        return jnp.transpose(jnp.concatenate([gate, up], axis=-1), (1, 0, 2)).astype(BF16)

    heads = np.arange(NA_WIDTH) // NA_HEAD_DIM
    hsum = jnp.asarray((heads[:, None] == heads[None, :]) / NA_HEAD_DIM, BF16)
    gmb = jnp.repeat(jnp.transpose(p['gm_bs'][l]).astype(F32), GM_WIDTH // GM_GROUPS, axis=1)
    lp = {k: p[k][l] for k in ('ssm_lambda_re', 'ssm_lambda_im', 'ssm_log_dt', 'ssm_b_re', 'ssm_b_im',
                               'ssm_c_re', 'ssm_c_im', 'ssm_d')}
    tb, rm = _na_tables(p['na_rpb'][l])
    w_in = p['w_in'][l].astype(BF16)
    w_out = p['w_out'][l].astype(BF16)
    g, c = SSM_GROUPS, SSM_CH
    rot = jnp.asarray(_ROT, BF16)
    w_ssm_rot = jnp.einsum('dgc,tgj->tdjc', w_in[:, :SSM_WIDTH].reshape(D_MODEL, g, c), rot).reshape(
        SSM_CHUNK, D_MODEL, SSM_WIDTH)
    glu_rot = jnp.einsum('gchd,tgj,thk->tjckd', p['ssm_glu_w'][l].astype(BF16).reshape(g, c, g, c), rot, rot).reshape(
        SSM_CHUNK, SSM_WIDTH, SSM_WIDTH)
    glub_rot = jnp.einsum('gc,tgj->tjc', p['ssm_glu_b'][l].astype(F32).reshape(g, c), jnp.asarray(_ROT, F32),
                          precision='highest').reshape(SSM_CHUNK, 1, SSM_WIDTH)
    w_out_rot = jnp.einsum('gcn,tgj->tjcn', w_out[:SSM_WIDTH].reshape(g, c, D_MODEL), rot).reshape(
        SSM_CHUNK, SSM_WIDTH, D_MODEL)
    return dict(
        g1=p['norm_ffn1'][l].reshape(1, D_MODEL), g2=p['norm_mix'][l].reshape(1, D_MODEL),
        g3=p['norm_ffn2'][l].reshape(1, D_MODEL),
        f1_in=ffn_in(p['ffn1_w_in'][l]), f1_out=p['ffn1_w_out'][l].astype(BF16),
        f2_in=ffn_in(p['ffn2_w_in'][l]), f2_out=p['ffn2_w_out'][l].astype(BF16),
        w_in=w_in[:, SSM_WIDTH:], w_ssm_rot=w_ssm_rot,
        w_out=w_out[SSM_WIDTH:], w_out_rot=w_out_rot,
        hsum=hsum,
        qg=jnp.tile(p['na_q_norm'][l].astype(F32), NA_HEADS).reshape(1, NA_WIDTH),
        kg=jnp.tile(p['na_k_norm'][l].astype(F32), NA_HEADS).reshape(1, NA_WIDTH),
        ws=p['gm_ws'][l].astype(BF16), gmb=gmb,
        ssm=_ssm_params(lp),
        glu_rot=glu_rot, glub_rot=glub_rot,
        tb=tb, rm=rm,
    )


def _pairs_major(t):
    b, s = t.shape[0], t.shape[1]
    return jnp.transpose(t.reshape(b, s, HEAD_PAIRS, LANES), (0, 2, 1, 3)).astype(BF16)


def _trunk_layer(x, mod_l, lw, *, n_batch, seq_len, is_ctx, ctx_kv=None, ssm_init=None):
    seg = dict(seg0=0 if is_ctx else 1, seg_tokens=x.shape[0] if is_ctx else seq_len)
    x = _ffn(x, mod_l, lw['g1'], lw['f1_in'], lw['f1_out'], mod_base=0, **seg)
    outs = _inproj(x, mod_l, lw['g2'], lw['w_in'], lw['w_ssm_rot'], lw['hsum'], lw['qg'], lw['kg'], lw['ws'],
                   lw['gmb'], is_ctx=is_ctx, **seg)
    u, q, k, v, gm = outs[:5]
    y, fin = _ssm_mixer_core(u, lw['ssm'], n_batch, seq_len, ssm_init)
    if is_ctx:
        na = _ctx_attn(q, k, v, n_batch)
    else:
        na = _na_attn(q, k, v, ctx_kv[0], ctx_kv[1], lw['tb'], lw['rm'], n_batch)
    x = _outproj(x, mod_l, y, na, gm, lw['glu_rot'], lw['glub_rot'], lw['w_out_rot'], lw['w_out'], **seg)
    x = _ffn(x, mod_l, lw['g3'], lw['f2_in'], lw['f2_out'], mod_base=6, **seg)
    return x, (outs[5:], fin)


def kernel(x_prompt, x_sample, c, cache_k, cache_v, state_ssm, c_ctx, w_ada, b_ada, norm_ffn1, ffn1_w_in, ffn1_w_out, norm_mix, w_in, w_out, ssm_lambda_re, ssm_lambda_im, ssm_log_dt, ssm_b_re, ssm_b_im, ssm_c_re, ssm_c_im, ssm_d, ssm_glu_w, ssm_glu_b, na_q_norm, na_k_norm, na_rpb, gm_ws, gm_bs, norm_ffn2, ffn2_w_in, ffn2_w_out):
    p = dict(norm_ffn1=norm_ffn1, ffn1_w_in=ffn1_w_in, ffn1_w_out=ffn1_w_out, norm_mix=norm_mix, w_in=w_in,
             w_out=w_out, ssm_lambda_re=ssm_lambda_re, ssm_lambda_im=ssm_lambda_im, ssm_log_dt=ssm_log_dt,
             ssm_b_re=ssm_b_re, ssm_b_im=ssm_b_im, ssm_c_re=ssm_c_re, ssm_c_im=ssm_c_im, ssm_d=ssm_d,
             ssm_glu_w=ssm_glu_w, ssm_glu_b=ssm_glu_b, na_q_norm=na_q_norm, na_k_norm=na_k_norm, na_rpb=na_rpb,
             gm_ws=gm_ws, gm_bs=gm_bs, norm_ffn2=norm_ffn2, ffn2_w_in=ffn2_w_in, ffn2_w_out=ffn2_w_out)
    batch, dec_batch = x_prompt.shape[0], x_sample.shape[0]
    cond8 = jnp.zeros((8, D_MODEL), F32).at[0].set(c_ctx).at[1:1 + dec_batch].set(c)
    mod = _adaln(cond8, w_ada, b_ada).reshape(DEPTH, 8, N_MOD, D_MODEL)

    xp = x_prompt.reshape(batch * SEQ, D_MODEL)
    xs = x_sample.reshape(dec_batch * DEC_SEQ, D_MODEL)
    new_k, new_v, new_s = [], [], []
    for l in range(DEPTH):
        lw = _layer_weights(l, p)
        xp, ((k_c, v_c), s_c) = _trunk_layer(xp, mod[l], lw, n_batch=batch, seq_len=SEQ, is_ctx=True)
        new_k.append(k_c.reshape(batch, SEQ, NA_HEADS, NA_HEAD_DIM))
        new_v.append(v_c.reshape(batch, SEQ, NA_HEADS, NA_HEAD_DIM))
        new_s.append(s_c)
        ctx_kv = (_pairs_major(cache_k[:, l]), _pairs_major(cache_v[:, l]))
        xs, _ = _trunk_layer(xs, mod[l], lw, n_batch=dec_batch, seq_len=DEC_SEQ, is_ctx=False,
                             ctx_kv=ctx_kv, ssm_init=state_ssm[:, l])
    return (xp.reshape(batch, SEQ, D_MODEL), xs.reshape(dec_batch, DEC_SEQ, D_MODEL),
            jnp.stack(new_k, axis=1), jnp.stack(new_v, axis=1), jnp.stack(new_s, axis=1))
```

```python
import functools

import numpy as np
import jax
import jax.numpy as jnp
from jax import lax
from jax.experimental import pallas as pl
from jax.experimental.pallas import tpu as pltpu

D_MODEL = 1024
DEPTH = 2
SEQ = 256
DEC_SEQ = 4096
PAST_LEN = 256
GRID_W = 64
SSM_WIDTH = 256
SSM_CH = 16
SSM_GROUPS = 16
SSM_STATE = 64
NA_WIDTH = 512
NA_HEAD_DIM = 64
NA_HEADS = 8
NA_MAX_ROWS = 8
NA_COLS = 16
GM_WIDTH = 256
GM_GROUPS = 4
GM_CHUNK = 128
D_FF = 2816
N_MOD = 9
RMS_EPS = 1e-6
LN_EPS = 1e-5
NEG_INF = -1e30

F32 = jnp.float32
BF16 = jnp.bfloat16

LANES = 128
SUBLANES = 8
MXU_DIM = 256
VMEM_LIMIT_BYTES = 56 * 1024 * 1024

FFN_TILE = 512
MIX_TILE = 1024
FF_CHUNK = MXU_DIM
N_FF_CHUNKS = D_FF // FF_CHUNK
SSM_CHUNK = 16
SSM_ROW = SSM_CHUNK * SSM_CH
SSM_GROUP_BLOCK = 4
SSM_W_COLS = 4 * LANES
SCAN_SHIFTS = (1, 2, 4)
N_SCAN_CONST = 3 * len(SCAN_SHIFTS) + 3
HEAD_PAIRS = NA_HEADS // 2
NA_Q_ROWS = 8
NA_K_ROWS = 16
NA_KCOLS = 2 * NA_COLS
NA_COL_BLOCKS = GRID_W // NA_COLS
NA_KC0 = (0, 8, 24, 32)
SEGS_PER_TILE = LANES // SSM_CH


def _silu(x):
    return x * (1.0 / (1.0 + jnp.exp(-x)))


def _sigmoid(x):
    return 1.0 / (1.0 + jnp.exp(-x))


def _gelu(x):
    return 0.5 * x * (1.0 + jnp.tanh(0.7978845608028654 * (x + 0.044715 * (x * x * x))))


def _cparams(sem):
    return pltpu.CompilerParams(dimension_semantics=sem, vmem_limit_bytes=VMEM_LIMIT_BYTES)


def _const_spec(shape):
    nd = len(shape)
    return pl.BlockSpec(shape, lambda *_: (0,) * nd, pipeline_mode=pl.Buffered(1))


def _ada_kernel(c_ref, w_ref, b_ref, o_ref):
    s = _silu(c_ref[...]).astype(BF16)
    w = w_ref[0].astype(BF16)
    o_ref[0] = jnp.dot(s, w, preferred_element_type=F32) + b_ref[0]


def _adaln(cond8, w_ada, b_ada):
    tn = D_MODEL
    ncol = N_MOD * D_MODEL
    return pl.pallas_call(
        _ada_kernel,
        grid=(DEPTH, ncol // tn),
        in_specs=[
            pl.BlockSpec((8, D_MODEL), lambda l, j: (0, 0)),
            pl.BlockSpec((1, D_MODEL, tn), lambda l, j: (l, 0, j)),
            pl.BlockSpec((1, 1, tn), lambda l, j: (l, 0, j)),
        ],
        out_specs=pl.BlockSpec((1, 8, tn), lambda l, j: (l, 0, j)),
        out_shape=jax.ShapeDtypeStruct((DEPTH, 8, ncol), F32),
        compiler_params=_cparams(("arbitrary", "arbitrary")),
        name="adaln",
    )(cond8, w_ada, b_ada.reshape(DEPTH, 1, ncol))


def _normed(x, g_row, shift, scale):
    ms = jnp.mean(x * x, axis=-1, keepdims=True)
    h = x * lax.rsqrt(ms + RMS_EPS) * g_row
    return h * (1.0 + scale) + shift


def _seg_map(seg0, seg_tokens, tile):
    tiles_per_seg = seg_tokens // tile
    return lambda i: (seg0 + i // tiles_per_seg, 0, 0)


def _ffn_kernel(x_ref, mod_ref, g_ref, win_ref, wout_ref, o_ref, h_ref, act_ref, *, mod_base):
    x = x_ref[...]
    shift = mod_ref[0, mod_base:mod_base + 1, :]
    scale = mod_ref[0, mod_base + 1:mod_base + 2, :]
    gate = mod_ref[0, mod_base + 2:mod_base + 3, :]
    h_ref[...] = _normed(x, g_ref[...], shift, scale).astype(BF16)
    for j in range(N_FF_CHUNKS):
        gu = jnp.dot(h_ref[...], win_ref[:, 2 * j * FF_CHUNK:2 * (j + 1) * FF_CHUNK], preferred_element_type=F32)
        a = _silu(gu[:, :FF_CHUNK]) * gu[:, FF_CHUNK:]
        act_ref[:, j * FF_CHUNK:(j + 1) * FF_CHUNK] = a.astype(BF16)
    for n in range(D_MODEL // MXU_DIM):
        cols = slice(n * MXU_DIM, (n + 1) * MXU_DIM)
        y = jnp.dot(act_ref[...], wout_ref[:, cols], preferred_element_type=F32)
        o_ref[:, cols] = x_ref[:, cols] + 0.5 * gate[:, cols] * y


def _ffn(x, mod_l, g, w_in3, w_out, *, mod_base, seg0, seg_tokens):
    n_tok = x.shape[0]
    return pl.pallas_call(
        functools.partial(_ffn_kernel, mod_base=mod_base),
        grid=(n_tok // FFN_TILE,),
        in_specs=[
            pl.BlockSpec((FFN_TILE, D_MODEL), lambda i: (i, 0)),
            pl.BlockSpec((1, N_MOD, D_MODEL), _seg_map(seg0, seg_tokens, FFN_TILE)),
            _const_spec((1, D_MODEL)),
            _const_spec((D_MODEL, 2 * D_FF)),
            _const_spec((D_FF, D_MODEL)),
        ],
        out_specs=pl.BlockSpec((FFN_TILE, D_MODEL), lambda i: (i, 0)),
        out_shape=jax.ShapeDtypeStruct((n_tok, D_MODEL), F32),
        scratch_shapes=[pltpu.VMEM((FFN_TILE, D_MODEL), BF16), pltpu.VMEM((FFN_TILE, D_FF), BF16)],
        compiler_params=_cparams(("arbitrary",)),
        name="ffn",
    )(x, mod_l, g, w_in3, w_out)


def _segment_ids():
    return lax.broadcasted_iota(jnp.int32, (1, LANES), 1) // SSM_CH


def _merge_segments(load_piece, rot):
    seg = _segment_ids()
    tiles = []
    for lt in range(SSM_ROW // LANES):
        acc = None
        for jj in range(SEGS_PER_TILE):
            src = (lt * SEGS_PER_TILE + jj - rot) % SSM_CHUNK
            piece = load_piece(src, lt)
            acc = piece if acc is None else jnp.where(seg == jj, piece, acc)
        tiles.append(acc)
    return jnp.concatenate(tiles, axis=-1)


def _roll_lanes(x, shift):
    shift %= x.shape[-1]
    return pltpu.roll(x, shift, axis=x.ndim - 1) if shift else x


def _inproj_kernel(*refs, n_in, n_out):
    x_ref, mod_ref, g_ref, w_ref, hsum_ref, qg_ref, kg_ref, ws_ref, gmb_ref = refs[:9]
    u_ref, q_ref, k_ref, v_ref, gm_ref, *tok_refs = refs[n_in:n_in + n_out]
    h_ref, xs_ref, z_ref = refs[n_in + n_out:]
    shift = mod_ref[0, 3:4, :]
    scale = mod_ref[0, 4:5, :]
    h_ref[...] = _normed(x_ref[...], g_ref[...], shift, scale).astype(BF16)
    n_rows = MIX_TILE // SSM_CHUNK

    def proj(c0, width):
        return jnp.dot(h_ref[...], w_ref[:, c0:c0 + width], preferred_element_type=F32)

    xs = proj(0, SSM_WIDTH)
    for j in range(SSM_WIDTH // LANES):
        xs_ref[j] = xs[:, j * LANES:(j + 1) * LANES]
    for t in range(SSM_CHUNK):
        zt = jnp.concatenate([xs_ref[j, pl.ds(t, n_rows, stride=SSM_CHUNK), :] for j in range(SSM_WIDTH // LANES)],
                             axis=-1)
        z_ref[t] = _roll_lanes(zt, SSM_CH * t)
    for g in range(SSM_GROUPS):
        ug = _merge_segments(lambda src, lt: z_ref[src, :, lt * LANES:(lt + 1) * LANES], g)
        u_ref[g] = _roll_lanes(ug, -SSM_CH * g)

    def head_norm(t, gain_row):
        ms = jnp.dot((t * t).astype(BF16), hsum_ref[...], preferred_element_type=F32)
        return t * lax.rsqrt(ms + RMS_EPS) * gain_row

    c0 = SSM_WIDTH
    q = head_norm(proj(c0, NA_WIDTH), qg_ref[...]) * (NA_HEAD_DIM ** -0.5)
    k = head_norm(proj(c0 + NA_WIDTH, NA_WIDTH), kg_ref[...])
    v = proj(c0 + 2 * NA_WIDTH, NA_WIDTH)
    for p in range(HEAD_PAIRS):
        lanes = slice(p * LANES, (p + 1) * LANES)
        q_ref[p] = q[:, lanes].astype(q_ref.dtype)
        k_ref[p] = k[:, lanes].astype(k_ref.dtype)
        v_ref[p] = v[:, lanes].astype(v_ref.dtype)
    if tok_refs:
        for slot in range(tok_refs[0].shape[1]):
            tok_refs[0][:, slot] = k.reshape(MIX_TILE // SEQ, SEQ, NA_WIDTH)
            tok_refs[1][:, slot] = v.reshape(MIX_TILE // SEQ, SEQ, NA_WIDTH)

    a = _gelu(proj(c0 + 3 * NA_WIDTH, 2 * GM_WIDTH))
    u = a[:, :GM_WIDTH]
    vv = a[:, GM_WIDTH:]
    mu = jnp.mean(vv, axis=-1, keepdims=True)
    vc = vv - mu
    var = jnp.mean(vc * vc, axis=-1, keepdims=True)
    vln = (vc * lax.rsqrt(var + LN_EPS)).astype(BF16)
    lane = lax.broadcasted_iota(jnp.int32, (1, GM_WIDTH), 1)
    gw = GM_WIDTH // GM_GROUPS
    for ci in range(MIX_TILE // GM_CHUNK):
        rows = slice(ci * GM_CHUNK, (ci + 1) * GM_CHUNK)
        vch = vln[rows, :]
        sp = gmb_ref[...]
        for gi in range(GM_GROUPS):
            t = jnp.dot(ws_ref[gi], vch, preferred_element_type=F32)
            sp = sp + jnp.where((lane >= gi * gw) & (lane < (gi + 1) * gw), t, 0.0)
        gm_ref[rows, :] = (u[rows, :] * sp).astype(BF16)


def _inproj(x, mod_l, g, w, hsum, qg, kg, ws, gmb, *, seg0, seg_tokens, layer=None, caches=None):
    n_tok = x.shape[0]
    is_ctx = layer is not None
    kv_dtype = BF16 if is_ctx else F32
    n_rows = MIX_TILE // SSM_CHUNK
    pair_spec = pl.BlockSpec((HEAD_PAIRS, MIX_TILE, LANES), lambda i: (0, i, 0))
    out_specs = [
        pl.BlockSpec((SSM_GROUPS, n_rows, SSM_ROW), lambda i: (0, i, 0)),
        pair_spec, pair_spec, pair_spec,
        pl.BlockSpec((MIX_TILE, GM_WIDTH), lambda i: (i, 0)),
    ]
    out_shape = [
        jax.ShapeDtypeStruct((SSM_GROUPS, n_tok // SSM_CHUNK, SSM_ROW), F32),
        jax.ShapeDtypeStruct((HEAD_PAIRS, n_tok, LANES), BF16),
        jax.ShapeDtypeStruct((HEAD_PAIRS, n_tok, LANES), kv_dtype),
        jax.ShapeDtypeStruct((HEAD_PAIRS, n_tok, LANES), kv_dtype),
        jax.ShapeDtypeStruct((n_tok, GM_WIDTH), BF16),
    ]
    args = [x, mod_l, g, w, hsum, qg, kg, ws, gmb]
    in_specs = [
        pl.BlockSpec((MIX_TILE, D_MODEL), lambda i: (i, 0)),
        pl.BlockSpec((1, N_MOD, D_MODEL), _seg_map(seg0, seg_tokens, MIX_TILE)),
        _const_spec((1, D_MODEL)),
        _const_spec(w.shape),
        _const_spec((NA_WIDTH, NA_WIDTH)),
        _const_spec((1, NA_WIDTH)),
        _const_spec((1, NA_WIDTH)),
        _const_spec((GM_GROUPS, GM_CHUNK, GM_CHUNK)),
        _const_spec((GM_CHUNK, GM_WIDTH)),
    ]
    aliases = {}
    if is_ctx:
        bt = MIX_TILE // SEQ
        cache_shape = jax.ShapeDtypeStruct((n_tok // SEQ, DEPTH, SEQ, NA_WIDTH), F32)
        if caches is None:
            out_specs += [pl.BlockSpec((bt, DEPTH, SEQ, NA_WIDTH), lambda i: (i, 0, 0, 0))] * 2
        else:
            out_specs += [pl.BlockSpec((bt, 1, SEQ, NA_WIDTH), lambda i: (i, layer, 0, 0))] * 2
        out_shape += [cache_shape] * 2
        if caches is not None:
            aliases = {len(args): len(out_shape) - 2, len(args) + 1: len(out_shape) - 1}
            args += list(caches)
            in_specs += [pl.BlockSpec(memory_space=pl.ANY)] * 2
    return pl.pallas_call(
        functools.partial(_inproj_kernel, n_in=len(args), n_out=len(out_shape)),
        grid=(n_tok // MIX_TILE,),
        in_specs=in_specs,
        out_specs=out_specs,
        out_shape=out_shape,
        input_output_aliases=aliases,
        scratch_shapes=[pltpu.VMEM((MIX_TILE, D_MODEL), BF16), pltpu.VMEM((SSM_WIDTH // LANES, MIX_TILE, LANES), F32),
                        pltpu.VMEM((SSM_CHUNK, n_rows, SSM_ROW), F32)],
        compiler_params=_cparams(("arbitrary",)),
        name="inproj_ctx" if is_ctx else "inproj_lat",
    )(*args)


def _shift_rows(x, d, row, down):
    if down:
        return jnp.where(row >= d, pltpu.roll(x, d, axis=0), 0.0)
    return jnp.where(row < SUBLANES - d, pltpu.roll(x, SUBLANES - d, axis=0), 0.0)


def _scan_tile(w, wp, c, cp, cst, row, down):
    x, xp = w, wp
    for si, d in enumerate(SCAN_SHIFTS):
        a, b, bp = cst[3 * si], cst[3 * si + 1], cst[3 * si + 2]
        xs, xps = _shift_rows(x, d, row, down), _shift_rows(xp, d, row, down)
        x, xp = x + a * xs + b * xps, xp + a * xps + bp * xs
    a, b, bp = cst[N_SCAN_CONST - 3], cst[N_SCAN_CONST - 2], cst[N_SCAN_CONST - 1]
    s = x + a * c + b * cp
    sp = xp + a * cp + bp * c
    edge = 0 if down else SUBLANES - 1
    last = SUBLANES - 1 if down else 0
    seen = jnp.where(row == edge, c, _shift_rows(s, 1, row, down))
    c_new = jnp.broadcast_to(s[last:last + 1, :], s.shape)
    cp_new = jnp.broadcast_to(sp[last:last + 1, :], sp.shape)
    return seen, c_new, cp_new


def _ssm_kernel(u_ref, mp_ref, q_ref, cst_ref, dvec_ref, s0_ref, y_ref, fin_ref, w_ref, sq_ref, *, n_batch, n_chunks):
    gb = u_ref.shape[0]
    n_rows = n_batch * n_chunks
    row_blk = min(n_rows, 512)
    for g in range(gb):
        for r0 in range(0, n_rows, row_blk):
            rows = slice(r0, r0 + row_blk)
            ug = u_ref[g, rows, :]
            wy = jnp.dot(ug.astype(BF16), mp_ref[g], preferred_element_type=F32)
            y_ref[g, rows, :] = wy[:, :SSM_ROW] + ug * dvec_ref[g]
            w_ref[g, rows, :] = wy[:, SSM_ROW:]

    n_tiles = n_chunks // SUBLANES
    row = lax.broadcasted_iota(jnp.int32, (SUBLANES, LANES), 0)

    def tile_step(b, kt, carry):
        out = []
        for g in range(gb):
            cf, cfp, cb, cbp = carry[4 * g:4 * g + 4]
            rf = pl.multiple_of(b * n_chunks + kt * SUBLANES, SUBLANES)
            rb = pl.multiple_of(b * n_chunks + (n_tiles - 1 - kt) * SUBLANES, SUBLANES)
            wf = w_ref[g, pl.ds(rf, SUBLANES), :]
            wb = w_ref[g, pl.ds(rb, SUBLANES), :]
            cst_f = [cst_ref[g, 0, i] for i in range(N_SCAN_CONST)]
            cst_b = [cst_ref[g, 1, i] for i in range(N_SCAN_CONST)]
            seen_f, cf, cfp = _scan_tile(wf[:, 0:LANES], wf[:, LANES:2 * LANES], cf, cfp, cst_f, row, True)
            seen_b, cb, cbp = _scan_tile(wb[:, 2 * LANES:3 * LANES], wb[:, 3 * LANES:], cb, cbp, cst_b, row, False)
            sq_ref[g, pl.ds(rf, SUBLANES), 0:LANES] = seen_f
            sq_ref[g, pl.ds(rb, SUBLANES), LANES:2 * LANES] = seen_b
            out += [cf, cfp, cb, cbp]
        return tuple(out)

    def batch_body(b, _):
        carry = []
        for g in range(gb):
            for i in range(4):
                carry.append(jnp.broadcast_to(s0_ref[g, i, pl.ds(b, 1), :], (SUBLANES, LANES)))
        carry = tuple(carry)
        if n_tiles <= 2:
            for kt in range(n_tiles):
                carry = tile_step(b, kt, carry)
        else:
            carry = lax.fori_loop(0, n_tiles, lambda kt, c: tile_step(b, kt, c), carry)
        for g in range(gb):
            fin_ref[g, 0, pl.ds(b, 1), :] = carry[4 * g][0:1, :]
            fin_ref[g, 1, pl.ds(b, 1), :] = carry[4 * g + 2][0:1, :]
        return 0

    lax.fori_loop(0, n_batch, batch_body, 0)

    for g in range(gb):
        for r0 in range(0, n_rows, row_blk):
            rows = slice(r0, r0 + row_blk)
            y_ref[g, rows, :] += jnp.dot(sq_ref[g, rows, :].astype(BF16), q_ref[g], preferred_element_type=F32)


def _ssm_core(u, mp, q, cst, dvec, s0, *, n_batch, n_chunks):
    n_rows = n_batch * n_chunks
    gb = SSM_GROUP_BLOCK
    g3 = lambda i: (i, 0, 0)
    g4 = lambda i: (i, 0, 0, 0)
    g5 = lambda i: (i, 0, 0, 0, 0)
    return pl.pallas_call(
        functools.partial(_ssm_kernel, n_batch=n_batch, n_chunks=n_chunks),
        grid=(SSM_GROUPS // gb,),
        in_specs=[
            pl.BlockSpec((gb, n_rows, SSM_ROW), g3),
            pl.BlockSpec((gb, SSM_ROW, SSM_ROW + SSM_W_COLS), g3),
            pl.BlockSpec((gb, 2 * LANES, SSM_ROW), g3),
            pl.BlockSpec((gb, 2, N_SCAN_CONST, SUBLANES, LANES), g5),
            pl.BlockSpec((gb, 1, SSM_ROW), g3),
            pl.BlockSpec((gb, 4, n_batch, LANES), g4),
        ],
        out_specs=[
            pl.BlockSpec((gb, n_rows, SSM_ROW), g3),
            pl.BlockSpec((gb, 2, n_batch, LANES), g4),
        ],
        out_shape=[
            jax.ShapeDtypeStruct((SSM_GROUPS, n_rows, SSM_ROW), F32),
            jax.ShapeDtypeStruct((SSM_GROUPS, 2, n_batch, LANES), F32),
        ],
        scratch_shapes=[pltpu.VMEM((gb, n_rows, SSM_W_COLS), F32), pltpu.VMEM((gb, n_rows, 2 * LANES), F32)],
        compiler_params=_cparams(("arbitrary",)),
        name="ssm_core",
    )(u, mp, q, cst, dvec, s0)


def _cmul(a, b):
    return a[0] * b[0] - a[1] * b[1], a[0] * b[1] + a[1] * b[0]


_CHUNK_IDX = np.arange(SSM_CHUNK)
_LAG = (_CHUNK_IDX[None, None, :] - _CHUNK_IDX[None, :, None] == _CHUNK_IDX[:, None, None],
        _CHUNK_IDX[None, :, None] - _CHUNK_IDX[None, None, :] == _CHUNK_IDX[:, None, None])


def _ssm_params(lp):
    t = SSM_CHUNK
    hi = dict(precision='highest')
    mats, pcols, qrows, csts = 0.0, [], [], []
    steps = jnp.arange(t + 1, dtype=F32)[:, None, None]
    tile_steps = np.arange(1, SUBLANES + 1, dtype=np.float32) * t
    for d in range(2):
        lam = (lp['ssm_lambda_re'][d].astype(F32), lp['ssm_lambda_im'][d].astype(F32))
        dt = jnp.exp(lp['ssm_log_dt'][d].astype(F32))[:, None]
        arg = (lam[0] * dt, lam[1] * dt)

        def powers(n):
            mag = jnp.exp(n * arg[0][None])
            return mag * jnp.cos(n * arg[1][None]), mag * jnp.sin(n * arg[1][None])

        pw = powers(steps)
        num = (pw[0][1] - 1.0, pw[1][1])
        den = lam[0] * lam[0] + lam[1] * lam[1]
        zoh = ((num[0] * lam[0] + num[1] * lam[1]) / den, (num[1] * lam[0] - num[0] * lam[1]) / den)
        b_mat = (lp['ssm_b_re'][d].astype(F32), lp['ssm_b_im'][d].astype(F32))
        bbar = _cmul((zoh[0][..., None], zoh[1][..., None]), b_mat)
        c_mat = (lp['ssm_c_re'][d].astype(F32), lp['ssm_c_im'][d].astype(F32))
        pb = _cmul((pw[0][:t, :, :, None], pw[1][:t, :, :, None]), (bbar[0][None], bbar[1][None]))
        kern = (jnp.einsum('gcp,ngpd->gndc', c_mat[0], pb[0], **hi)
                - jnp.einsum('gcp,ngpd->gndc', c_mat[1], pb[1], **hi))
        kern_ic = jnp.tile(kern, (1, 1, 1, t))
        lag_ic = np.repeat(_LAG[d].astype(np.float32), SSM_CH, axis=2)
        for n in range(t):
            mats = mats + kern_ic[:, n, None, :, :] * lag_ic[n][None, :, None, :]
        pm = [jnp.flip(x, 0) if d == 0 else x for x in pb]
        pm = [jnp.transpose(x, (1, 0, 3, 2)) for x in pm]
        pcols += [pm[0], pm[1], pm[1], pm[0]]
        qp = [x[1:] if d == 0 else jnp.flip(x[1:], 0) for x in pw]
        qp = [jnp.repeat(jnp.transpose(x, (1, 2, 0)), SSM_CH, axis=2) for x in qp]
        ct = [jnp.tile(jnp.transpose(x, (0, 2, 1)), (1, 1, t)) for x in c_mat]
        qm = _cmul(ct, qp)
        qrows += [qm[0], -qm[1]]
        n_shift = len(SCAN_SHIFTS)
        tile_pw = tile_steps if d == 0 else tile_steps[::-1]
        expo = np.concatenate([np.asarray(SCAN_SHIFTS, np.float32) * t, tile_pw])
        pr, pi = powers(jnp.asarray(expo, F32)[:, None, None])
        forms = jnp.stack([jnp.concatenate([pr, pr], -1), jnp.concatenate([-pi, pi], -1),
                           jnp.concatenate([pi, -pi], -1)], axis=1)
        per_shift = jnp.broadcast_to(forms[:n_shift].reshape(3 * n_shift, SSM_GROUPS, 1, LANES),
                                     (3 * n_shift, SSM_GROUPS, SUBLANES, LANES))
        per_row = jnp.transpose(forms[n_shift:], (1, 2, 0, 3))
        csts.append(jnp.transpose(jnp.concatenate([per_shift, per_row], axis=0), (1, 0, 2, 3)))
    g = SSM_GROUPS
    pcat = jnp.concatenate(pcols, axis=-1).reshape(g, SSM_ROW, SSM_W_COLS)
    mp = jnp.concatenate([mats.reshape(g, SSM_ROW, SSM_ROW), pcat], axis=-1).astype(BF16)
    q = jnp.concatenate(qrows, axis=1).astype(BF16)
    cst = jnp.stack(csts, axis=1)
    dvec = jnp.tile(lp['ssm_d'].astype(F32).reshape(SSM_GROUPS, 1, SSM_CH), (1, 1, SSM_CHUNK))
    return mp, q, cst, dvec


def _ssm_mixer_core(u, params, n_batch, seq_len, init_state):
    mp, q, cst, dvec = params
    n_chunks = seq_len // SSM_CHUNK
    if init_state is None:
        s0 = jnp.zeros((SSM_GROUPS, 4, n_batch, LANES), F32)
    else:
        st = jnp.transpose(init_state.astype(F32), (2, 1, 0, 3, 4))
        re, im = st[..., 0], st[..., 1]
        s0 = jnp.stack([jnp.concatenate([re[:, 0], im[:, 0]], -1), jnp.concatenate([im[:, 0], re[:, 0]], -1),
                        jnp.concatenate([re[:, 1], im[:, 1]], -1), jnp.concatenate([im[:, 1], re[:, 1]], -1)], axis=1)
    y, fin = _ssm_core(u, mp, q, cst, dvec, s0, n_batch=n_batch, n_chunks=n_chunks)
    fin = jnp.stack([fin[..., :SSM_STATE], fin[..., SSM_STATE:]], axis=-1)
    return y, jnp.transpose(fin, (2, 1, 0, 3, 4))


def _head_lane_mask(hh):
    lane = lax.broadcasted_iota(jnp.int32, (1, LANES), 1)
    return (lane >= hh * NA_HEAD_DIM) & (lane < (hh + 1) * NA_HEAD_DIM)


_NT_DIMS = (((1,), (1,)), ((), ()))


def _ctx_attn_kernel(q_ref, k_ref, v_ref, o_ref):
    for p in range(HEAD_PAIRS):
        qp, kp, vp = q_ref[p], k_ref[p], v_ref[p]
        acc = jnp.zeros((SEQ, LANES), F32)
        for hh in range(2):
            lm = _head_lane_mask(hh)
            qm = jnp.where(lm, qp, jnp.zeros_like(qp))
            s = lax.dot_general(qm, kp, _NT_DIMS, preferred_element_type=F32)
            e = jnp.exp(s - jnp.max(s, axis=-1, keepdims=True))
            o = jnp.dot(e.astype(BF16), vp, preferred_element_type=F32)
            acc = jnp.where(lm, o / jnp.sum(e, axis=-1, keepdims=True), acc)
        o_ref[:, p * LANES:(p + 1) * LANES] = acc.astype(BF16)


def _ctx_attn(q, k, v, n_batch):
    spec = pl.BlockSpec((HEAD_PAIRS, SEQ, LANES), lambda b: (0, b, 0))
    return pl.pallas_call(
        _ctx_attn_kernel,
        grid=(n_batch,),
        in_specs=[spec, spec, spec],
        out_specs=pl.BlockSpec((SEQ, NA_WIDTH), lambda b: (b, 0)),
        out_shape=jax.ShapeDtypeStruct((n_batch * SEQ, NA_WIDTH), BF16),
        compiler_params=_cparams(("arbitrary",)),
        name="ctx_attn",
    )(q, k, v)


def _na_kernel(q_ref, k_ref, v_ref, kc_ref, vc_ref, tb_ref, rm_ref, o_ref, qb_ref, os_ref):
    rt = pl.program_id(1)
    n_rt = pl.num_programs(1)
    r0 = rt * NA_Q_ROWS
    ttype = jnp.where(rt == 0, 0, jnp.where(rt == n_rt - 1, 2, 1))
    rmask = rm_ref[ttype]
    blk_q = NA_Q_ROWS * NA_COLS

    def pair_body(p, carry):
        for n in range(NA_COL_BLOCKS):
            for rl in range(NA_Q_ROWS):
                qb_ref[n * blk_q + rl * NA_COLS:n * blk_q + (rl + 1) * NA_COLS, :] = (
                    q_ref[p, rl * GRID_W + n * NA_COLS:rl * GRID_W + (n + 1) * NA_COLS, :])
        kcp, vcp = kc_ref[0, p], vc_ref[0, p]
        qall = qb_ref[...]
        sc_all = []
        for hh in range(2):
            qm = jnp.where(_head_lane_mask(hh), qall, jnp.zeros_like(qall))
            sc_all.append(lax.dot_general(qm, kcp, _NT_DIMS, preferred_element_type=F32))
        for n in range(NA_COL_BLOCKS):
            kparts, vparts = [], []
            for i in range(NA_K_ROWS):
                kr = jnp.clip(r0 - (NA_K_ROWS - NA_Q_ROWS) // 2 + i, 0, GRID_W - 1)
                start = pl.multiple_of(kr * GRID_W + NA_KC0[n], 8)
                kparts.append(k_ref[p, pl.ds(start, NA_KCOLS), :])
                vparts.append(v_ref[p, pl.ds(start, NA_KCOLS), :])
            kblk = jnp.concatenate(kparts, axis=0).astype(BF16)
            vblk = jnp.concatenate(vparts, axis=0).astype(BF16)
            qblk = qall[n * blk_q:(n + 1) * blk_q, :]
            oblk = jnp.zeros((blk_q, LANES), F32)
            for hh in range(2):
                lm = _head_lane_mask(hh)
                qm = jnp.where(lm, qblk, jnp.zeros_like(qblk))
                s = lax.dot_general(qm, kblk, _NT_DIMS, preferred_element_type=F32)
                s = s + tb_ref[2 * p + hh, n] + rmask
                sc = sc_all[hh][n * blk_q:(n + 1) * blk_q, :]
                m = jnp.maximum(jnp.max(s, axis=-1, keepdims=True), jnp.max(sc, axis=-1, keepdims=True))
                e = jnp.exp(s - m)
                ec = jnp.exp(sc - m)
                den = jnp.sum(e, axis=-1, keepdims=True) + jnp.sum(ec, axis=-1, keepdims=True)
                o = (jnp.dot(e.astype(BF16), vblk, preferred_element_type=F32)
                     + jnp.dot(ec.astype(BF16), vcp, preferred_element_type=F32))
                oblk = jnp.where(lm, o / den, oblk)
            for rl in range(NA_Q_ROWS):
                os_ref[p, rl * GRID_W + n * NA_COLS:rl * GRID_W + (n + 1) * NA_COLS, :] = (
                    oblk[rl * NA_COLS:(rl + 1) * NA_COLS, :])
        return carry

    lax.fori_loop(0, HEAD_PAIRS, pair_body, 0)
    for p in range(HEAD_PAIRS):
        o_ref[:, p * LANES:(p + 1) * LANES] = os_ref[p].astype(BF16)


def _na_attn(q, k, v, kc, vc, tb, rm, n_batch):
    tile_tok = NA_Q_ROWS * GRID_W
    n_rt = DEC_SEQ // tile_tok
    return pl.pallas_call(
        _na_kernel,
        grid=(n_batch, n_rt),
        in_specs=[
            pl.BlockSpec((HEAD_PAIRS, tile_tok, LANES), lambda b, r: (0, b * n_rt + r, 0)),
            pl.BlockSpec((HEAD_PAIRS, DEC_SEQ, LANES), lambda b, r: (0, b, 0), pipeline_mode=pl.Buffered(1)),
            pl.BlockSpec((HEAD_PAIRS, DEC_SEQ, LANES), lambda b, r: (0, b, 0), pipeline_mode=pl.Buffered(1)),
            pl.BlockSpec((1, HEAD_PAIRS, PAST_LEN, LANES), lambda b, r: (b, 0, 0, 0)),
            pl.BlockSpec((1, HEAD_PAIRS, PAST_LEN, LANES), lambda b, r: (b, 0, 0, 0)),
            _const_spec(tb.shape),
            _const_spec(rm.shape),
        ],
        out_specs=pl.BlockSpec((tile_tok, NA_WIDTH), lambda b, r: (b * n_rt + r, 0)),
        out_shape=jax.ShapeDtypeStruct((n_batch * DEC_SEQ, NA_WIDTH), BF16),
        scratch_shapes=[pltpu.VMEM((tile_tok, LANES), BF16), pltpu.VMEM((HEAD_PAIRS, tile_tok, LANES), F32)],
        compiler_params=_cparams(("arbitrary", "arbitrary")),
        name="na_attn",
    )(q, k, v, kc, vc, tb, rm)


def _na_tables(rpb_l):
    n = np.arange(NA_COL_BLOCKS)
    cc = np.arange(NA_COLS)
    kk = np.arange(NA_KCOLS)
    rl = np.arange(NA_Q_ROWS)
    ki = np.arange(NA_K_ROWS)
    n_dr, n_dc = 2 * NA_MAX_ROWS - 1, 2 * NA_COLS - 1
    c = n[:, None] * NA_COLS + cc[None, :]
    kcol = np.asarray(NA_KC0)[:, None] + kk[None, :]
    dc = np.clip(kcol[:, None, :] - c[:, :, None], -(NA_COLS - 1), NA_COLS - 1) + (NA_COLS - 1)
    cs = np.clip(c - NA_COLS // 2, 0, GRID_W - NA_COLS)
    col_ok = (kcol[:, None, :] >= cs[:, :, None]) & (kcol[:, None, :] < cs[:, :, None] + NA_COLS)
    half = (NA_K_ROWS - NA_Q_ROWS) // 2
    dr = np.clip(ki[None, :] - half - rl[:, None] + (NA_MAX_ROWS - 1), 0, n_dr - 1)
    oh_dc = jnp.asarray(dc[None] == np.arange(n_dc)[:, None, None, None], F32)
    cols = jnp.einsum('hab,bnck->hncak', rpb_l.astype(F32), oh_dc, precision='highest')
    cols = jnp.where(jnp.asarray(col_ok)[None, :, :, None, :], cols, NEG_INF)
    off0 = (NA_MAX_ROWS - 1) - half
    pad_lo = (NA_Q_ROWS - 1) - off0
    pad_hi = off0 + (NA_K_ROWS - 1) - (n_dr - 1)
    colsp = jnp.concatenate([cols[:, :, :, :1]] * pad_lo + [cols] + [cols[:, :, :, -1:]] * pad_hi, axis=3)
    starts = [pad_lo + off0 - r for r in range(NA_Q_ROWS)]
    assert np.array_equal(dr, np.clip(ki[None] + np.asarray(starts)[:, None] - pad_lo, 0, n_dr - 1))
    tb = jnp.stack([colsp[:, :, :, s0:s0 + NA_K_ROWS] for s0 in starts], axis=2)
    tb = tb.reshape(NA_HEADS, NA_COL_BLOCKS, NA_Q_ROWS * NA_COLS, NA_K_ROWS * NA_KCOLS)
    rows = GRID_W
    masks = []
    for r0 in (0, NA_Q_ROWS, rows - NA_Q_ROWS):
        r = r0 + rl
        rs = np.clip(r - NA_MAX_ROWS // 2, 0, rows - NA_MAX_ROWS)
        kr = r0 - half + ki
        ok = (kr[None, :] >= rs[:, None]) & (kr[None, :] < rs[:, None] + NA_MAX_ROWS)
        full = np.broadcast_to(ok[:, None, :, None], (NA_Q_ROWS, NA_COLS, NA_K_ROWS, NA_KCOLS))
        masks.append(np.where(full, 0.0, NEG_INF).reshape(NA_Q_ROWS * NA_COLS, NA_K_ROWS * NA_KCOLS))
    rm = jnp.asarray(np.stack(masks), F32)
    return tb, rm


def _outproj_kernel(x_ref, mod_ref, y_ref, na_ref, gm_ref, gw_ref, gb_ref, w_ref, o_ref, yr_ref, ys_ref, cat_ref):
    gate = mod_ref[0, 5:6, :]
    n_rows = MIX_TILE // SSM_CHUNK
    for g in range(SSM_GROUPS):
        yr_ref[g] = _roll_lanes(y_ref[g], SSM_CH * g)
    for t in range(SSM_CHUNK):
        yt = _merge_segments(lambda src, lt: yr_ref[src, :, lt * LANES:(lt + 1) * LANES], t)
        yt = _roll_lanes(yt, -SSM_CH * t)
        for j in range(SSM_WIDTH // LANES):
            ys_ref[j, pl.ds(t, n_rows, stride=SSM_CHUNK), :] = yt[:, j * LANES:(j + 1) * LANES]
    y = _gelu(jnp.concatenate([ys_ref[j] for j in range(SSM_WIDTH // LANES)], axis=-1))
    z = jnp.dot(y.astype(BF16), gw_ref[...], preferred_element_type=F32) + gb_ref[...]
    cat_ref[:, 0:SSM_WIDTH] = (y * _sigmoid(z)).astype(BF16)
    cat_ref[:, SSM_WIDTH:SSM_WIDTH + NA_WIDTH] = na_ref[...]
    cat_ref[:, SSM_WIDTH + NA_WIDTH:] = gm_ref[...]
    mix = jnp.dot(cat_ref[...], w_ref[...], preferred_element_type=F32)
    o_ref[...] = x_ref[...] + gate * mix


def _outproj(x, mod_l, y, na, gm, gw, gb, w, *, seg0, seg_tokens):
    n_tok = x.shape[0]
    n_rows = MIX_TILE // SSM_CHUNK
    tok = lambda width: pl.BlockSpec((MIX_TILE, width), lambda i: (i, 0))
    return pl.pallas_call(
        _outproj_kernel,
        grid=(n_tok // MIX_TILE,),
        in_specs=[
            tok(D_MODEL),
            pl.BlockSpec((1, N_MOD, D_MODEL), _seg_map(seg0, seg_tokens, MIX_TILE)),
            pl.BlockSpec((SSM_GROUPS, n_rows, SSM_ROW), lambda i: (0, i, 0)),
            tok(NA_WIDTH), tok(GM_WIDTH),
            _const_spec(gw.shape),
            _const_spec(gb.shape),
            _const_spec(w.shape),
        ],
        out_specs=tok(D_MODEL),
        out_shape=jax.ShapeDtypeStruct((n_tok, D_MODEL), F32),
        scratch_shapes=[pltpu.VMEM((SSM_GROUPS, n_rows, SSM_ROW), F32),
                        pltpu.VMEM((SSM_WIDTH // LANES, MIX_TILE, LANES), F32),
                        pltpu.VMEM((MIX_TILE, D_MODEL), BF16)],
        compiler_params=_cparams(("arbitrary",)),
        name="outproj",
    )(x, mod_l, y, na, gm, gw, gb, w)


def _layer_weights(l, p):
    def ffn_in(w):
        gate = w[:, :D_FF].reshape(D_MODEL, N_FF_CHUNKS, FF_CHUNK)
        up = w[:, D_FF:].reshape(D_MODEL, N_FF_CHUNKS, FF_CHUNK)
        return jnp.concatenate([gate, up], axis=-1).reshape(D_MODEL, 2 * D_FF).astype(BF16)

    heads = np.arange(NA_WIDTH) // NA_HEAD_DIM
    hsum = jnp.asarray((heads[:, None] == heads[None, :]) / NA_HEAD_DIM, BF16)
    gmb = jnp.repeat(jnp.transpose(p['gm_bs'][l]).astype(F32), GM_WIDTH // GM_GROUPS, axis=1)
    lp = {k: p[k][l] for k in ('ssm_lambda_re', 'ssm_lambda_im', 'ssm_log_dt', 'ssm_b_re', 'ssm_b_im',
                               'ssm_c_re', 'ssm_c_im', 'ssm_d')}
    tb, rm = _na_tables(p['na_rpb'][l])
    return dict(
        g1=p['norm_ffn1'][l].reshape(1, D_MODEL), g2=p['norm_mix'][l].reshape(1, D_MODEL),
        g3=p['norm_ffn2'][l].reshape(1, D_MODEL),
        f1_in=ffn_in(p['ffn1_w_in'][l]), f1_out=p['ffn1_w_out'][l].astype(BF16),
        f2_in=ffn_in(p['ffn2_w_in'][l]), f2_out=p['ffn2_w_out'][l].astype(BF16),
        w_in=p['w_in'][l].astype(BF16), w_out=p['w_out'][l].astype(BF16),
        hsum=hsum,
        qg=jnp.tile(p['na_q_norm'][l].astype(F32), NA_HEADS).reshape(1, NA_WIDTH),
        kg=jnp.tile(p['na_k_norm'][l].astype(F32), NA_HEADS).reshape(1, NA_WIDTH),
        ws=p['gm_ws'][l].astype(BF16), gmb=gmb,
        ssm=_ssm_params(lp),
        glu_w=p['ssm_glu_w'][l].astype(BF16), glu_b=p['ssm_glu_b'][l].astype(F32).reshape(1, SSM_WIDTH),
        tb=tb, rm=rm,
    )


def _pairs_major(t):
    b, s = t.shape[0], t.shape[1]
    return jnp.transpose(t.reshape(b, s, HEAD_PAIRS, LANES), (0, 2, 1, 3)).astype(BF16)


def _trunk_layer(x, mod_l, lw, *, n_batch, seq_len, layer=None, caches=None, ctx_kv=None, ssm_init=None):
    is_ctx = layer is not None
    seg = dict(seg0=0 if is_ctx else 1, seg_tokens=x.shape[0] if is_ctx else seq_len)
    x = _ffn(x, mod_l, lw['g1'], lw['f1_in'], lw['f1_out'], mod_base=0, **seg)
    outs = _inproj(x, mod_l, lw['g2'], lw['w_in'], lw['hsum'], lw['qg'], lw['kg'], lw['ws'], lw['gmb'],
                   layer=layer, caches=caches, **seg)
    u, q, k, v, gm = outs[:5]
    y, fin = _ssm_mixer_core(u, lw['ssm'], n_batch, seq_len, ssm_init)
    if is_ctx:
        na = _ctx_attn(q, k, v, n_batch)
    else:
        na = _na_attn(q, k, v, ctx_kv[0], ctx_kv[1], lw['tb'], lw['rm'], n_batch)
    x = _outproj(x, mod_l, y, na, gm, lw['glu_w'], lw['glu_b'], lw['w_out'], **seg)
    x = _ffn(x, mod_l, lw['g3'], lw['f2_in'], lw['f2_out'], mod_base=6, **seg)
    return x, (tuple(outs[5:]), fin)


def kernel(x_prompt, x_sample, c, cache_k, cache_v, state_ssm, c_ctx, w_ada, b_ada, norm_ffn1, ffn1_w_in, ffn1_w_out, norm_mix, w_in, w_out, ssm_lambda_re, ssm_lambda_im, ssm_log_dt, ssm_b_re, ssm_b_im, ssm_c_re, ssm_c_im, ssm_d, ssm_glu_w, ssm_glu_b, na_q_norm, na_k_norm, na_rpb, gm_ws, gm_bs, norm_ffn2, ffn2_w_in, ffn2_w_out):
    p = dict(norm_ffn1=norm_ffn1, ffn1_w_in=ffn1_w_in, ffn1_w_out=ffn1_w_out, norm_mix=norm_mix, w_in=w_in,
             w_out=w_out, ssm_lambda_re=ssm_lambda_re, ssm_lambda_im=ssm_lambda_im, ssm_log_dt=ssm_log_dt,
             ssm_b_re=ssm_b_re, ssm_b_im=ssm_b_im, ssm_c_re=ssm_c_re, ssm_c_im=ssm_c_im, ssm_d=ssm_d,
             ssm_glu_w=ssm_glu_w, ssm_glu_b=ssm_glu_b, na_q_norm=na_q_norm, na_k_norm=na_k_norm, na_rpb=na_rpb,
             gm_ws=gm_ws, gm_bs=gm_bs, norm_ffn2=norm_ffn2, ffn2_w_in=ffn2_w_in, ffn2_w_out=ffn2_w_out)
    batch, dec_batch = x_prompt.shape[0], x_sample.shape[0]
    cond8 = jnp.zeros((8, D_MODEL), F32).at[0].set(c_ctx).at[1:1 + dec_batch].set(c)
    mod = _adaln(cond8, w_ada, b_ada).reshape(DEPTH, 8, N_MOD, D_MODEL)

    xp = x_prompt.reshape(batch * SEQ, D_MODEL)
    xs = x_sample.reshape(dec_batch * DEC_SEQ, D_MODEL)
    caches, new_s = None, []
    for l in range(DEPTH):
        lw = _layer_weights(l, p)
        xp, (caches, s_c) = _trunk_layer(xp, mod[l], lw, n_batch=batch, seq_len=SEQ, layer=l, caches=caches)
        new_s.append(s_c)
        ctx_kv = (_pairs_major(cache_k[:, l]), _pairs_major(cache_v[:, l]))
        xs, _ = _trunk_layer(xs, mod[l], lw, n_batch=dec_batch, seq_len=DEC_SEQ,
                             ctx_kv=ctx_kv, ssm_init=state_ssm[:, l])
    cache_shape = (batch, DEPTH, SEQ, NA_HEADS, NA_HEAD_DIM)
    return (xp.reshape(batch, SEQ, D_MODEL), xs.reshape(dec_batch, DEC_SEQ, D_MODEL),
            caches[0].reshape(cache_shape), caches[1].reshape(cache_shape), jnp.stack(new_s, axis=1))
```

```python
import functools

import numpy as np
import jax
import jax.numpy as jnp
from jax import lax
from jax.experimental import pallas as pl
from jax.experimental.pallas import tpu as pltpu

D_MODEL = 1024
DEPTH = 2
SEQ = 256
DEC_SEQ = 4096
PAST_LEN = 256
GRID_W = 64
SSM_WIDTH = 256
SSM_CH = 16
SSM_GROUPS = 16
SSM_STATE = 64
NA_WIDTH = 512
NA_HEAD_DIM = 64
NA_HEADS = 8
NA_MAX_ROWS = 8
NA_COLS = 16
GM_WIDTH = 256
GM_GROUPS = 4
GM_CHUNK = 128
D_FF = 2816
N_MOD = 9
RMS_EPS = 1e-6
LN_EPS = 1e-5
NEG_INF = -1e30

F32 = jnp.float32
BF16 = jnp.bfloat16

LANES = 128
SUBLANES = 8
MXU_DIM = 256
VMEM_LIMIT_BYTES = 56 * 1024 * 1024

FFN_TILE = 1024
MIX_TILE = 1024
FF_CHUNK = MXU_DIM
N_FF_CHUNKS = D_FF // FF_CHUNK
SSM_CHUNK = 16
SSM_ROW = SSM_CHUNK * SSM_CH
SSM_GROUP_BLOCK = 4
SSM_W_COLS = 4 * LANES
SCAN_SHIFTS = (1, 2, 4)
N_SCAN_CONST = 3 * len(SCAN_SHIFTS) + 3
HEAD_PAIRS = NA_HEADS // 2
NA_Q_ROWS = 8
NA_K_ROWS = 16
NA_KCOLS = 2 * NA_COLS
NA_COL_BLOCKS = GRID_W // NA_COLS
NA_KC0 = (0, 8, 24, 32)
SEGS_PER_TILE = LANES // SSM_CH
NA_OFF_PAD = (NA_Q_ROWS - 1) - ((NA_MAX_ROWS - 1) - (NA_K_ROWS - NA_Q_ROWS) // 2)
NA_OFF_START = NA_Q_ROWS - 1
NA_OFF_ROWS = NA_OFF_START + NA_K_ROWS


def _silu(x):
    return x * (1.0 / (1.0 + jnp.exp(-x)))


def _sigmoid(x):
    return 1.0 / (1.0 + jnp.exp(-x))


def _gelu(x):
    return 0.5 * x * (1.0 + jnp.tanh(0.7978845608028654 * (x + 0.044715 * (x * x * x))))


def _cparams(sem):
    return pltpu.CompilerParams(dimension_semantics=sem, vmem_limit_bytes=VMEM_LIMIT_BYTES)


def _const_spec(shape):
    nd = len(shape)
    return pl.BlockSpec(shape, lambda *_: (0,) * nd, pipeline_mode=pl.Buffered(1))


def _ada_kernel(c_ref, w_ref, b_ref, o_ref):
    s = _silu(c_ref[...]).astype(BF16)
    w = w_ref[0].astype(BF16)
    o_ref[0] = jnp.dot(s, w, preferred_element_type=F32) + b_ref[0]


def _adaln(cond8, w_ada, b_ada):
    tn = D_MODEL
    ncol = N_MOD * D_MODEL
    return pl.pallas_call(
        _ada_kernel,
        grid=(DEPTH, ncol // tn),
        in_specs=[
            pl.BlockSpec((8, D_MODEL), lambda l, j: (0, 0)),
            pl.BlockSpec((1, D_MODEL, tn), lambda l, j: (l, 0, j)),
            pl.BlockSpec((1, 1, tn), lambda l, j: (l, 0, j)),
        ],
        out_specs=pl.BlockSpec((1, 8, tn), lambda l, j: (l, 0, j)),
        out_shape=jax.ShapeDtypeStruct((DEPTH, 8, ncol), F32),
        compiler_params=_cparams(("arbitrary", "arbitrary")),
        name="adaln",
    )(cond8, w_ada, b_ada.reshape(DEPTH, 1, ncol))


def _normed(x, g_row, shift, scale):
    ms = jnp.mean(x * x, axis=-1, keepdims=True)
    h = x * lax.rsqrt(ms + RMS_EPS) * g_row
    return h * (1.0 + scale) + shift


def _seg_map(seg0, seg_tokens, tile):
    tiles_per_seg = seg_tokens // tile
    return lambda i: (seg0 + i // tiles_per_seg, 0, 0)


def _ffn_kernel(x_ref, mod_ref, g_ref, win_ref, wout_ref, o_ref, h_ref, act_ref, *, mod_base):
    x = x_ref[...]
    shift = mod_ref[0, mod_base:mod_base + 1, :]
    scale = mod_ref[0, mod_base + 1:mod_base + 2, :]
    gate = mod_ref[0, mod_base + 2:mod_base + 3, :]
    h_ref[...] = _normed(x, g_ref[...], shift, scale).astype(BF16)
    for j in range(N_FF_CHUNKS):
        cols = slice(j * FF_CHUNK, (j + 1) * FF_CHUNK)
        gate_j = jnp.dot(h_ref[...], win_ref[:, cols], preferred_element_type=F32)
        up_j = jnp.dot(h_ref[...], win_ref[:, D_FF + j * FF_CHUNK:D_FF + (j + 1) * FF_CHUNK], preferred_element_type=F32)
        act_ref[:, cols] = (_silu(gate_j) * up_j).astype(BF16)
    for n in range(D_MODEL // MXU_DIM):
        cols = slice(n * MXU_DIM, (n + 1) * MXU_DIM)
        y = jnp.dot(act_ref[...], wout_ref[:, cols], preferred_element_type=F32)
        o_ref[:, cols] = x_ref[:, cols] + 0.5 * gate[:, cols] * y


def _ffn(x, mod_l, g, w_in3, w_out, *, mod_base, seg0, seg_tokens):
    n_tok = x.shape[0]
    return pl.pallas_call(
        functools.partial(_ffn_kernel, mod_base=mod_base),
        grid=(n_tok // FFN_TILE,),
        in_specs=[
            pl.BlockSpec((FFN_TILE, D_MODEL), lambda i: (i, 0)),
            pl.BlockSpec((1, N_MOD, D_MODEL), _seg_map(seg0, seg_tokens, FFN_TILE)),
            _const_spec((1, D_MODEL)),
            _const_spec((D_MODEL, 2 * D_FF)),
            _const_spec((D_FF, D_MODEL)),
        ],
        out_specs=pl.BlockSpec((FFN_TILE, D_MODEL), lambda i: (i, 0)),
        out_shape=jax.ShapeDtypeStruct((n_tok, D_MODEL), F32),
        scratch_shapes=[pltpu.VMEM((FFN_TILE, D_MODEL), BF16), pltpu.VMEM((FFN_TILE, D_FF), BF16)],
        compiler_params=_cparams(("arbitrary",)),
        name="ffn",
    )(x, mod_l, g, w_in3, w_out)


def _segment_ids():
    return lax.broadcasted_iota(jnp.int32, (1, LANES), 1) // SSM_CH


def _merge_segments(load_piece, rot):
    seg = _segment_ids()
    tiles = []
    for lt in range(SSM_ROW // LANES):
        acc = None
        for jj in range(SEGS_PER_TILE):
            src = (lt * SEGS_PER_TILE + jj - rot) % SSM_CHUNK
            piece = load_piece(src, lt)
            acc = piece if acc is None else jnp.where(seg == jj, piece, acc)
        tiles.append(acc)
    return jnp.concatenate(tiles, axis=-1)


def _roll_lanes(x, shift):
    shift %= x.shape[-1]
    return pltpu.roll(x, shift, axis=x.ndim - 1) if shift else x


def _inproj_kernel(*refs, n_in, n_out):
    x_ref, mod_ref, g_ref, w_ref, hsum_ref, qg_ref, kg_ref, ws_ref, gmb_ref = refs[:9]
    u_ref, q_ref, k_ref, v_ref, gm_ref, *tok_refs = refs[n_in:n_in + n_out]
    h_ref, xs_ref, z_ref = refs[n_in + n_out:]
    shift = mod_ref[0, 3:4, :]
    scale = mod_ref[0, 4:5, :]
    h_ref[...] = _normed(x_ref[...], g_ref[...], shift, scale).astype(BF16)
    n_rows = MIX_TILE // SSM_CHUNK

    def proj(c0, width):
        return jnp.dot(h_ref[...], w_ref[:, c0:c0 + width], preferred_element_type=F32)

    xs = proj(0, SSM_WIDTH)
    for j in range(SSM_WIDTH // LANES):
        xs_ref[j] = xs[:, j * LANES:(j + 1) * LANES]
    for t in range(SSM_CHUNK):
        zt = jnp.concatenate([xs_ref[j, pl.ds(t, n_rows, stride=SSM_CHUNK), :] for j in range(SSM_WIDTH // LANES)],
                             axis=-1)
        z_ref[t] = _roll_lanes(zt, SSM_CH * t)
    for g in range(SSM_GROUPS):
        ug = _merge_segments(lambda src, lt: z_ref[src, :, lt * LANES:(lt + 1) * LANES], g)
        u_ref[g] = _roll_lanes(ug, -SSM_CH * g)

    def head_norm(t, gain_row):
        ms = jnp.dot((t * t).astype(BF16), hsum_ref[...], preferred_element_type=F32)
        return t * lax.rsqrt(ms + RMS_EPS) * gain_row

    c0 = SSM_WIDTH
    q = head_norm(proj(c0, NA_WIDTH), qg_ref[...]) * (NA_HEAD_DIM ** -0.5)
    k = head_norm(proj(c0 + NA_WIDTH, NA_WIDTH), kg_ref[...])
    v = proj(c0 + 2 * NA_WIDTH, NA_WIDTH)
    for p in range(HEAD_PAIRS):
        lanes = slice(p * LANES, (p + 1) * LANES)
        q_ref[p] = q[:, lanes].astype(q_ref.dtype)
        k_ref[p] = k[:, lanes].astype(k_ref.dtype)
        v_ref[p] = v[:, lanes].astype(v_ref.dtype)
    if tok_refs:
        for slot in range(tok_refs[0].shape[1]):
            tok_refs[0][:, slot] = k.reshape(MIX_TILE // SEQ, SEQ, NA_WIDTH)
            tok_refs[1][:, slot] = v.reshape(MIX_TILE // SEQ, SEQ, NA_WIDTH)

    a = _gelu(proj(c0 + 3 * NA_WIDTH, 2 * GM_WIDTH))
    u = a[:, :GM_WIDTH]
    vv = a[:, GM_WIDTH:]
    mu = jnp.mean(vv, axis=-1, keepdims=True)
    vc = vv - mu
    var = jnp.mean(vc * vc, axis=-1, keepdims=True)
    vln = (vc * lax.rsqrt(var + LN_EPS)).astype(BF16)
    lane = lax.broadcasted_iota(jnp.int32, (1, GM_WIDTH), 1)
    gw = GM_WIDTH // GM_GROUPS
    for ci in range(MIX_TILE // GM_CHUNK):
        rows = slice(ci * GM_CHUNK, (ci + 1) * GM_CHUNK)
        vch = vln[rows, :]
        sp = gmb_ref[...]
        for gi in range(GM_GROUPS):
            t = jnp.dot(ws_ref[gi], vch, preferred_element_type=F32)
            sp = sp + jnp.where((lane >= gi * gw) & (lane < (gi + 1) * gw), t, 0.0)
        gm_ref[rows, :] = (u[rows, :] * sp).astype(BF16)


def _inproj(x, mod_l, g, w, hsum, qg, kg, ws, gmb, *, seg0, seg_tokens, layer=None, caches=None):
    n_tok = x.shape[0]
    is_ctx = layer is not None
    kv_dtype = BF16 if is_ctx else F32
    n_rows = MIX_TILE // SSM_CHUNK
    pair_spec = pl.BlockSpec((HEAD_PAIRS, MIX_TILE, LANES), lambda i: (0, i, 0))
    out_specs = [
        pl.BlockSpec((SSM_GROUPS, n_rows, SSM_ROW), lambda i: (0, i, 0)),
        pair_spec, pair_spec, pair_spec,
        pl.BlockSpec((MIX_TILE, GM_WIDTH), lambda i: (i, 0)),
    ]
    out_shape = [
        jax.ShapeDtypeStruct((SSM_GROUPS, n_tok // SSM_CHUNK, SSM_ROW), F32),
        jax.ShapeDtypeStruct((HEAD_PAIRS, n_tok, LANES), BF16),
        jax.ShapeDtypeStruct((HEAD_PAIRS, n_tok, LANES), kv_dtype),
        jax.ShapeDtypeStruct((HEAD_PAIRS, n_tok, LANES), kv_dtype),
        jax.ShapeDtypeStruct((n_tok, GM_WIDTH), BF16),
    ]
    args = [x, mod_l, g, w, hsum, qg, kg, ws, gmb]
    in_specs = [
        pl.BlockSpec((MIX_TILE, D_MODEL), lambda i: (i, 0)),
        pl.BlockSpec((1, N_MOD, D_MODEL), _seg_map(seg0, seg_tokens, MIX_TILE)),
        _const_spec((1, D_MODEL)),
        _const_spec(w.shape),
        _const_spec((NA_WIDTH, NA_WIDTH)),
        _const_spec((1, NA_WIDTH)),
        _const_spec((1, NA_WIDTH)),
        _const_spec((GM_GROUPS, GM_CHUNK, GM_CHUNK)),
        _const_spec((GM_CHUNK, GM_WIDTH)),
    ]
    aliases = {}
    if is_ctx:
        bt = MIX_TILE // SEQ
        cache_shape = jax.ShapeDtypeStruct((n_tok // SEQ, DEPTH, SEQ, NA_WIDTH), F32)
        if caches is None:
            out_specs += [pl.BlockSpec((bt, DEPTH, SEQ, NA_WIDTH), lambda i: (i, 0, 0, 0))] * 2
        else:
            out_specs += [pl.BlockSpec((bt, 1, SEQ, NA_WIDTH), lambda i: (i, layer, 0, 0))] * 2
        out_shape += [cache_shape] * 2
        if caches is not None:
            aliases = {len(args): len(out_shape) - 2, len(args) + 1: len(out_shape) - 1}
            args += list(caches)
            in_specs += [pl.BlockSpec(memory_space=pl.ANY)] * 2
    return pl.pallas_call(
        functools.partial(_inproj_kernel, n_in=len(args), n_out=len(out_shape)),
        grid=(n_tok // MIX_TILE,),
        in_specs=in_specs,
        out_specs=out_specs,
        out_shape=out_shape,
        input_output_aliases=aliases,
        scratch_shapes=[pltpu.VMEM((MIX_TILE, D_MODEL), BF16), pltpu.VMEM((SSM_WIDTH // LANES, MIX_TILE, LANES), F32),
                        pltpu.VMEM((SSM_CHUNK, n_rows, SSM_ROW), F32)],
        compiler_params=_cparams(("arbitrary",)),
        name="inproj_ctx" if is_ctx else "inproj_lat",
    )(*args)


def _shift_rows(x, d, row, down):
    if down:
        return jnp.where(row >= d, pltpu.roll(x, d, axis=0), 0.0)
    return jnp.where(row < SUBLANES - d, pltpu.roll(x, SUBLANES - d, axis=0), 0.0)


def _scan_tile(w, wp, c, cp, cst, row, down):
    x, xp = w, wp
    for si, d in enumerate(SCAN_SHIFTS):
        a, b, bp = cst[3 * si], cst[3 * si + 1], cst[3 * si + 2]
        xs, xps = _shift_rows(x, d, row, down), _shift_rows(xp, d, row, down)
        x, xp = x + a * xs + b * xps, xp + a * xps + bp * xs
    a, b, bp = cst[N_SCAN_CONST - 3], cst[N_SCAN_CONST - 2], cst[N_SCAN_CONST - 1]
    s = x + a * c + b * cp
    sp = xp + a * cp + bp * c
    edge = 0 if down else SUBLANES - 1
    last = SUBLANES - 1 if down else 0
    seen = jnp.where(row == edge, c, _shift_rows(s, 1, row, down))
    c_new = jnp.broadcast_to(s[last:last + 1, :], s.shape)
    cp_new = jnp.broadcast_to(sp[last:last + 1, :], sp.shape)
    return seen, c_new, cp_new


def _ssm_kernel(u_ref, mp_ref, q_ref, cst_ref, dvec_ref, s0_ref, y_ref, fin_ref, w_ref, sq_ref, *, n_batch, n_chunks):
    gb = u_ref.shape[0]
    n_rows = n_batch * n_chunks
    row_blk = min(n_rows, 512)
    for g in range(gb):
        for r0 in range(0, n_rows, row_blk):
            rows = slice(r0, r0 + row_blk)
            ug = u_ref[g, rows, :]
            wy = jnp.dot(ug.astype(BF16), mp_ref[g], preferred_element_type=F32)
            y_ref[g, rows, :] = wy[:, :SSM_ROW] + ug * dvec_ref[g]
            w_ref[g, rows, :] = wy[:, SSM_ROW:]

    n_tiles = n_chunks // SUBLANES
    row = lax.broadcasted_iota(jnp.int32, (SUBLANES, LANES), 0)

    def tile_step(b, kt, carry):
        out = []
        for g in range(gb):
            cf, cfp, cb, cbp = carry[4 * g:4 * g + 4]
            rf = pl.multiple_of(b * n_chunks + kt * SUBLANES, SUBLANES)
            rb = pl.multiple_of(b * n_chunks + (n_tiles - 1 - kt) * SUBLANES, SUBLANES)
            wf = w_ref[g, pl.ds(rf, SUBLANES), :]
            wb = w_ref[g, pl.ds(rb, SUBLANES), :]
            cst_f = [cst_ref[g, 0, i] for i in range(N_SCAN_CONST)]
            cst_b = [cst_ref[g, 1, i] for i in range(N_SCAN_CONST)]
            seen_f, cf, cfp = _scan_tile(wf[:, 0:LANES], wf[:, LANES:2 * LANES], cf, cfp, cst_f, row, True)
            seen_b, cb, cbp = _scan_tile(wb[:, 2 * LANES:3 * LANES], wb[:, 3 * LANES:], cb, cbp, cst_b, row, False)
            sq_ref[g, pl.ds(rf, SUBLANES), 0:LANES] = seen_f
            sq_ref[g, pl.ds(rb, SUBLANES), LANES:2 * LANES] = seen_b
            out += [cf, cfp, cb, cbp]
        return tuple(out)

    def batch_body(b, _):
        carry = []
        for g in range(gb):
            for i in range(4):
                carry.append(jnp.broadcast_to(s0_ref[g, i, pl.ds(b, 1), :], (SUBLANES, LANES)))
        carry = tuple(carry)
        if n_tiles <= 2:
            for kt in range(n_tiles):
                carry = tile_step(b, kt, carry)
        else:
            carry = lax.fori_loop(0, n_tiles, lambda kt, c: tile_step(b, kt, c), carry)
        for g in range(gb):
            fin_ref[g, 0, pl.ds(b, 1), :] = carry[4 * g][0:1, :]
            fin_ref[g, 1, pl.ds(b, 1), :] = carry[4 * g + 2][0:1, :]
        return 0

    lax.fori_loop(0, n_batch, batch_body, 0)

    for g in range(gb):
        for r0 in range(0, n_rows, row_blk):
            rows = slice(r0, r0 + row_blk)
            y_ref[g, rows, :] += jnp.dot(sq_ref[g, rows, :].astype(BF16), q_ref[g], preferred_element_type=F32)


def _ssm_core(u, mp, q, cst, dvec, s0, *, n_batch, n_chunks):
    n_rows = n_batch * n_chunks
    gb = SSM_GROUP_BLOCK
    g3 = lambda i: (i, 0, 0)
    g4 = lambda i: (i, 0, 0, 0)
    g5 = lambda i: (i, 0, 0, 0, 0)
    return pl.pallas_call(
        functools.partial(_ssm_kernel, n_batch=n_batch, n_chunks=n_chunks),
        grid=(SSM_GROUPS // gb,),
        in_specs=[
            pl.BlockSpec((gb, n_rows, SSM_ROW), g3),
            pl.BlockSpec((gb, SSM_ROW, SSM_ROW + SSM_W_COLS), g3),
            pl.BlockSpec((gb, 2 * LANES, SSM_ROW), g3),
            pl.BlockSpec((gb, 2, N_SCAN_CONST, SUBLANES, LANES), g5),
            pl.BlockSpec((gb, 1, SSM_ROW), g3),
            pl.BlockSpec((gb, 4, n_batch, LANES), g4),
        ],
        out_specs=[
            pl.BlockSpec((gb, n_rows, SSM_ROW), g3),
            pl.BlockSpec((gb, 2, n_batch, LANES), g4),
        ],
        out_shape=[
            jax.ShapeDtypeStruct((SSM_GROUPS, n_rows, SSM_ROW), F32),
            jax.ShapeDtypeStruct((SSM_GROUPS, 2, n_batch, LANES), F32),
        ],
        scratch_shapes=[pltpu.VMEM((gb, n_rows, SSM_W_COLS), F32), pltpu.VMEM((gb, n_rows, 2 * LANES), F32)],
        compiler_params=_cparams(("arbitrary",)),
        name="ssm_core",
    )(u, mp, q, cst, dvec, s0)


def _cmul(a, b):
    return a[0] * b[0] - a[1] * b[1], a[0] * b[1] + a[1] * b[0]


_CHUNK_IDX = np.arange(SSM_CHUNK)


def _ssm_params(lp):
    t = SSM_CHUNK
    hi = dict(precision='highest')
    mats, pcols, qrows, csts = 0.0, [], [], []
    steps = jnp.arange(t + 1, dtype=F32)[:, None, None]
    tile_steps = np.arange(1, SUBLANES + 1, dtype=np.float32) * t
    for d in range(2):
        lam = (lp['ssm_lambda_re'][d].astype(F32), lp['ssm_lambda_im'][d].astype(F32))
        dt = jnp.exp(lp['ssm_log_dt'][d].astype(F32))[:, None]
        arg = (lam[0] * dt, lam[1] * dt)

        def powers(n):
            mag = jnp.exp(n * arg[0][None])
            return mag * jnp.cos(n * arg[1][None]), mag * jnp.sin(n * arg[1][None])

        pw = powers(steps)
        num = (pw[0][1] - 1.0, pw[1][1])
        den = lam[0] * lam[0] + lam[1] * lam[1]
        zoh = ((num[0] * lam[0] + num[1] * lam[1]) / den, (num[1] * lam[0] - num[0] * lam[1]) / den)
        b_mat = (lp['ssm_b_re'][d].astype(F32), lp['ssm_b_im'][d].astype(F32))
        bbar = _cmul((zoh[0][..., None], zoh[1][..., None]), b_mat)
        c_mat = (lp['ssm_c_re'][d].astype(F32), lp['ssm_c_im'][d].astype(F32))
        pb = _cmul((pw[0][:t, :, :, None], pw[1][:t, :, :, None]), (bbar[0][None], bbar[1][None]))
        ct = [jnp.tile(jnp.transpose(x, (0, 2, 1)), (1, 1, t)) for x in c_mat]
        bt = [jnp.tile(x, (1, 1, t)) for x in bbar]
        sgn = 1.0 if d == 0 else -1.0
        expo = jnp.asarray(np.repeat(_CHUNK_IDX.astype(np.float32) - (t - 1) / 2, SSM_CH))[:, None, None]
        xm = _cmul([jnp.transpose(x, (1, 2, 0)) for x in powers(-sgn * expo)], bt)
        ym = _cmul([jnp.transpose(x, (1, 2, 0)) for x in powers(sgn * expo)], ct)
        resp = jnp.einsum('gpx,gpy->gxy', jnp.concatenate([xm[0], -xm[1]], axis=1),
                          jnp.concatenate([ym[0], ym[1]], axis=1), **hi)
        steps_x = np.repeat(_CHUNK_IDX, SSM_CH)
        causal = (steps_x[None, :] >= steps_x[:, None]) if d == 0 else (steps_x[:, None] >= steps_x[None, :])
        mats = mats + jnp.where(jnp.asarray(causal)[None], resp, 0.0)
        pm = [jnp.flip(x, 0) if d == 0 else x for x in pb]
        pm = [jnp.transpose(x, (1, 0, 3, 2)) for x in pm]
        pcols += [pm[0], pm[1], pm[1], pm[0]]
        qp = [x[1:] if d == 0 else jnp.flip(x[1:], 0) for x in pw]
        qp = [jnp.repeat(jnp.transpose(x, (1, 2, 0)), SSM_CH, axis=2) for x in qp]
        qm = _cmul(ct, qp)
        qrows += [qm[0], -qm[1]]
        n_shift = len(SCAN_SHIFTS)
        tile_pw = tile_steps if d == 0 else tile_steps[::-1]
        expo = np.concatenate([np.asarray(SCAN_SHIFTS, np.float32) * t, tile_pw])
        pr, pi = powers(jnp.asarray(expo, F32)[:, None, None])
        forms = jnp.stack([jnp.concatenate([pr, pr], -1), jnp.concatenate([-pi, pi], -1),
                           jnp.concatenate([pi, -pi], -1)], axis=1)
        per_shift = jnp.broadcast_to(forms[:n_shift].reshape(3 * n_shift, SSM_GROUPS, 1, LANES),
                                     (3 * n_shift, SSM_GROUPS, SUBLANES, LANES))
        per_row = jnp.transpose(forms[n_shift:], (1, 2, 0, 3))
        csts.append(jnp.transpose(jnp.concatenate([per_shift, per_row], axis=0), (1, 0, 2, 3)))
    g = SSM_GROUPS
    pcat = jnp.concatenate(pcols, axis=-1).reshape(g, SSM_ROW, SSM_W_COLS)
    mp = jnp.concatenate([mats.reshape(g, SSM_ROW, SSM_ROW), pcat], axis=-1).astype(BF16)
    q = jnp.concatenate(qrows, axis=1).astype(BF16)
    cst = jnp.stack(csts, axis=1)
    dvec = jnp.tile(lp['ssm_d'].astype(F32).reshape(SSM_GROUPS, 1, SSM_CH), (1, 1, SSM_CHUNK))
    return mp, q, cst, dvec


def _ssm_mixer_core(u, params, n_batch, seq_len, init_state):
    mp, q, cst, dvec = params
    n_chunks = seq_len // SSM_CHUNK
    if init_state is None:
        s0 = jnp.zeros((SSM_GROUPS, 4, n_batch, LANES), F32)
    else:
        st = jnp.transpose(init_state.astype(F32), (2, 1, 0, 3, 4))
        re, im = st[..., 0], st[..., 1]
        s0 = jnp.stack([jnp.concatenate([re[:, 0], im[:, 0]], -1), jnp.concatenate([im[:, 0], re[:, 0]], -1),
                        jnp.concatenate([re[:, 1], im[:, 1]], -1), jnp.concatenate([im[:, 1], re[:, 1]], -1)], axis=1)
    y, fin = _ssm_core(u, mp, q, cst, dvec, s0, n_batch=n_batch, n_chunks=n_chunks)
    fin = jnp.stack([fin[..., :SSM_STATE], fin[..., SSM_STATE:]], axis=-1)
    return y, jnp.transpose(fin, (2, 1, 0, 3, 4))


def _head_lane_mask(hh):
    lane = lax.broadcasted_iota(jnp.int32, (1, LANES), 1)
    return (lane >= hh * NA_HEAD_DIM) & (lane < (hh + 1) * NA_HEAD_DIM)


_NT_DIMS = (((1,), (1,)), ((), ()))


def _ctx_attn_kernel(q_ref, k_ref, v_ref, o_ref):
    for p in range(HEAD_PAIRS):
        qp, kp, vp = q_ref[p], k_ref[p], v_ref[p]
        acc = jnp.zeros((SEQ, LANES), F32)
        for hh in range(2):
            lm = _head_lane_mask(hh)
            qm = jnp.where(lm, qp, jnp.zeros_like(qp))
            s = lax.dot_general(qm, kp, _NT_DIMS, preferred_element_type=F32)
            e = jnp.exp(s - jnp.max(s, axis=-1, keepdims=True))
            o = jnp.dot(e.astype(BF16), vp, preferred_element_type=F32)
            acc = jnp.where(lm, o / jnp.sum(e, axis=-1, keepdims=True), acc)
        o_ref[:, p * LANES:(p + 1) * LANES] = acc.astype(BF16)


def _ctx_attn(q, k, v, n_batch):
    spec = pl.BlockSpec((HEAD_PAIRS, SEQ, LANES), lambda b: (0, b, 0))
    return pl.pallas_call(
        _ctx_attn_kernel,
        grid=(n_batch,),
        in_specs=[spec, spec, spec],
        out_specs=pl.BlockSpec((SEQ, NA_WIDTH), lambda b: (b, 0)),
        out_shape=jax.ShapeDtypeStruct((n_batch * SEQ, NA_WIDTH), BF16),
        compiler_params=_cparams(("arbitrary",)),
        name="ctx_attn",
    )(q, k, v)


def _na_kernel(q_ref, k_ref, v_ref, kc_ref, vc_ref, t2_ref, rm_ref, o_ref, qb_ref, os_ref, tb_ref):
    rt = pl.program_id(1)
    n_rt = pl.num_programs(1)

    @pl.when((pl.program_id(0) == 0) & (rt == 0))
    def _():
        for h in range(NA_HEADS):
            for n in range(NA_COL_BLOCKS):
                for rl in range(NA_Q_ROWS):
                    lane0 = (NA_OFF_START - rl) * NA_KCOLS
                    tb_ref[h, n, rl * NA_COLS:(rl + 1) * NA_COLS, :] = (
                        t2_ref[h, n, :, lane0:lane0 + NA_K_ROWS * NA_KCOLS])

    r0 = rt * NA_Q_ROWS
    ttype = jnp.where(rt == 0, 0, jnp.where(rt == n_rt - 1, 2, 1))
    rmask = rm_ref[ttype]
    blk_q = NA_Q_ROWS * NA_COLS

    def pair_body(p, carry):
        lm0 = _head_lane_mask(0)
        for n in range(NA_COL_BLOCKS):
            for rl in range(NA_Q_ROWS):
                qrow = q_ref[p, rl * GRID_W + n * NA_COLS:rl * GRID_W + (n + 1) * NA_COLS, :]
                zero = jnp.zeros_like(qrow)
                base = 2 * n * blk_q + rl * NA_COLS
                qb_ref[base:base + NA_COLS, :] = jnp.where(lm0, qrow, zero)
                qb_ref[base + blk_q:base + blk_q + NA_COLS, :] = jnp.where(lm0, zero, qrow)
        kcp, vcp = kc_ref[0, p], vc_ref[0, p]
        sc_all = lax.dot_general(qb_ref[...], kcp, _NT_DIMS, preferred_element_type=F32)
        for n in range(NA_COL_BLOCKS):
            kparts, vparts = [], []
            for i in range(NA_K_ROWS):
                kr = jnp.clip(r0 - (NA_K_ROWS - NA_Q_ROWS) // 2 + i, 0, GRID_W - 1)
                start = pl.multiple_of(kr * GRID_W + NA_KC0[n], 8)
                kparts.append(k_ref[p, pl.ds(start, NA_KCOLS), :])
                vparts.append(v_ref[p, pl.ds(start, NA_KCOLS), :])
            kblk = jnp.concatenate(kparts, axis=0).astype(BF16)
            vblk = jnp.concatenate(vparts, axis=0).astype(BF16)
            rows2 = slice(2 * n * blk_q, 2 * (n + 1) * blk_q)
            s = lax.dot_general(qb_ref[rows2, :], kblk, _NT_DIMS, preferred_element_type=F32)
            s = s + jnp.concatenate([tb_ref[2 * p, n] + rmask, tb_ref[2 * p + 1, n] + rmask], axis=0)
            sc = sc_all[rows2, :]
            m = jnp.maximum(jnp.max(s, axis=-1, keepdims=True), jnp.max(sc, axis=-1, keepdims=True))
            e = jnp.exp(s - m)
            ec = jnp.exp(sc - m)
            den = jnp.sum(e, axis=-1, keepdims=True) + jnp.sum(ec, axis=-1, keepdims=True)
            o = (jnp.dot(e.astype(BF16), vblk, preferred_element_type=F32)
                 + jnp.dot(ec.astype(BF16), vcp, preferred_element_type=F32)) / den
            oblk = jnp.where(lm0, o[:blk_q, :], o[blk_q:, :])
            for rl in range(NA_Q_ROWS):
                os_ref[p, rl * GRID_W + n * NA_COLS:rl * GRID_W + (n + 1) * NA_COLS, :] = (
                    oblk[rl * NA_COLS:(rl + 1) * NA_COLS, :])
        return carry

    lax.fori_loop(0, HEAD_PAIRS, pair_body, 0)
    for p in range(HEAD_PAIRS):
        o_ref[:, p * LANES:(p + 1) * LANES] = os_ref[p].astype(BF16)


def _na_attn(q, k, v, kc, vc, tb, rm, n_batch):
    tile_tok = NA_Q_ROWS * GRID_W
    n_rt = DEC_SEQ // tile_tok
    return pl.pallas_call(
        _na_kernel,
        grid=(n_batch, n_rt),
        in_specs=[
            pl.BlockSpec((HEAD_PAIRS, tile_tok, LANES), lambda b, r: (0, b * n_rt + r, 0)),
            pl.BlockSpec((HEAD_PAIRS, DEC_SEQ, LANES), lambda b, r: (0, b, 0), pipeline_mode=pl.Buffered(1)),
            pl.BlockSpec((HEAD_PAIRS, DEC_SEQ, LANES), lambda b, r: (0, b, 0), pipeline_mode=pl.Buffered(1)),
            pl.BlockSpec((1, HEAD_PAIRS, PAST_LEN, LANES), lambda b, r: (b, 0, 0, 0)),
            pl.BlockSpec((1, HEAD_PAIRS, PAST_LEN, LANES), lambda b, r: (b, 0, 0, 0)),
            _const_spec(tb.shape),
            _const_spec(rm.shape),
        ],
        out_specs=pl.BlockSpec((tile_tok, NA_WIDTH), lambda b, r: (b * n_rt + r, 0)),
        out_shape=jax.ShapeDtypeStruct((n_batch * DEC_SEQ, NA_WIDTH), BF16),
        scratch_shapes=[pltpu.VMEM((2 * tile_tok, LANES), BF16), pltpu.VMEM((HEAD_PAIRS, tile_tok, LANES), F32),
                        pltpu.VMEM((NA_HEADS, NA_COL_BLOCKS, NA_Q_ROWS * NA_COLS, NA_K_ROWS * NA_KCOLS), F32)],
        compiler_params=_cparams(("arbitrary", "arbitrary")),
        name="na_attn",
    )(q, k, v, kc, vc, tb, rm)


def _na_tables(rpb_l):
    n = np.arange(NA_COL_BLOCKS)
    cc = np.arange(NA_COLS)
    kk = np.arange(NA_KCOLS)
    rl = np.arange(NA_Q_ROWS)
    ki = np.arange(NA_K_ROWS)
    n_dr, n_dc = 2 * NA_MAX_ROWS - 1, 2 * NA_COLS - 1
    c = n[:, None] * NA_COLS + cc[None, :]
    kcol = np.asarray(NA_KC0)[:, None] + kk[None, :]
    dc = np.clip(kcol[:, None, :] - c[:, :, None], -(NA_COLS - 1), NA_COLS - 1) + (NA_COLS - 1)
    cs = np.clip(c - NA_COLS // 2, 0, GRID_W - NA_COLS)
    col_ok = (kcol[:, None, :] >= cs[:, :, None]) & (kcol[:, None, :] < cs[:, :, None] + NA_COLS)
    half = (NA_K_ROWS - NA_Q_ROWS) // 2
    dr = np.clip(ki[None, :] - half - rl[:, None] + (NA_MAX_ROWS - 1), 0, n_dr - 1)
    oh_dc = jnp.asarray(dc[None] == np.arange(n_dc)[:, None, None, None], F32)
    cols = jnp.einsum('hab,bnck->hncak', rpb_l.astype(F32), oh_dc, precision='highest')
    cols = jnp.where(jnp.asarray(col_ok)[None, :, :, None, :], cols, NEG_INF)
    pad_hi = NA_OFF_ROWS - NA_OFF_PAD - n_dr
    colsp = jnp.concatenate([cols[:, :, :, :1]] * NA_OFF_PAD + [cols] + [cols[:, :, :, -1:]] * pad_hi, axis=3)
    assert np.array_equal(dr, np.clip(ki[None] + (NA_OFF_START - rl)[:, None] - NA_OFF_PAD, 0, n_dr - 1))
    tb = colsp.reshape(NA_HEADS, NA_COL_BLOCKS, NA_COLS, NA_OFF_ROWS * NA_KCOLS)
    rows = GRID_W
    masks = []
    for r0 in (0, NA_Q_ROWS, rows - NA_Q_ROWS):
        r = r0 + rl
        rs = np.clip(r - NA_MAX_ROWS // 2, 0, rows - NA_MAX_ROWS)
        kr = r0 - half + ki
        ok = (kr[None, :] >= rs[:, None]) & (kr[None, :] < rs[:, None] + NA_MAX_ROWS)
        full = np.broadcast_to(ok[:, None, :, None], (NA_Q_ROWS, NA_COLS, NA_K_ROWS, NA_KCOLS))
        masks.append(np.where(full, 0.0, NEG_INF).reshape(NA_Q_ROWS * NA_COLS, NA_K_ROWS * NA_KCOLS))
    rm = jnp.asarray(np.stack(masks), F32)
    return tb, rm


def _outproj_kernel(x_ref, mod_ref, y_ref, na_ref, gm_ref, gw_ref, gb_ref, w_ref, o_ref, yr_ref, ys_ref, cat_ref):
    gate = mod_ref[0, 5:6, :]
    n_rows = MIX_TILE // SSM_CHUNK
    for g in range(SSM_GROUPS):
        yr_ref[g] = _roll_lanes(y_ref[g], SSM_CH * g)
    for t in range(SSM_CHUNK):
        yt = _merge_segments(lambda src, lt: yr_ref[src, :, lt * LANES:(lt + 1) * LANES], t)
        yt = _roll_lanes(yt, -SSM_CH * t)
        for j in range(SSM_WIDTH // LANES):
            ys_ref[j, pl.ds(t, n_rows, stride=SSM_CHUNK), :] = yt[:, j * LANES:(j + 1) * LANES]
    y = _gelu(jnp.concatenate([ys_ref[j] for j in range(SSM_WIDTH // LANES)], axis=-1))
    z = jnp.dot(y.astype(BF16), gw_ref[...], preferred_element_type=F32) + gb_ref[...]
    cat_ref[:, 0:SSM_WIDTH] = (y * _sigmoid(z)).astype(BF16)
    cat_ref[:, SSM_WIDTH:SSM_WIDTH + NA_WIDTH] = na_ref[...]
    cat_ref[:, SSM_WIDTH + NA_WIDTH:] = gm_ref[...]
    mix = jnp.dot(cat_ref[...], w_ref[...], preferred_element_type=F32)
    o_ref[...] = x_ref[...] + gate * mix


def _outproj(x, mod_l, y, na, gm, gw, gb, w, *, seg0, seg_tokens):
    n_tok = x.shape[0]
    n_rows = MIX_TILE // SSM_CHUNK
    tok = lambda width: pl.BlockSpec((MIX_TILE, width), lambda i: (i, 0))
    return pl.pallas_call(
        _outproj_kernel,
        grid=(n_tok // MIX_TILE,),
        in_specs=[
            tok(D_MODEL),
            pl.BlockSpec((1, N_MOD, D_MODEL), _seg_map(seg0, seg_tokens, MIX_TILE)),
            pl.BlockSpec((SSM_GROUPS, n_rows, SSM_ROW), lambda i: (0, i, 0)),
            tok(NA_WIDTH), tok(GM_WIDTH),
            _const_spec(gw.shape),
            _const_spec(gb.shape),
            _const_spec(w.shape),
        ],
        out_specs=tok(D_MODEL),
        out_shape=jax.ShapeDtypeStruct((n_tok, D_MODEL), F32),
        scratch_shapes=[pltpu.VMEM((SSM_GROUPS, n_rows, SSM_ROW), F32),
                        pltpu.VMEM((SSM_WIDTH // LANES, MIX_TILE, LANES), F32),
                        pltpu.VMEM((MIX_TILE, D_MODEL), BF16)],
        compiler_params=_cparams(("arbitrary",)),
        name="outproj",
    )(x, mod_l, y, na, gm, gw, gb, w)


def _layer_weights(l, p):
    heads = np.arange(NA_WIDTH) // NA_HEAD_DIM
    hsum = jnp.asarray((heads[:, None] == heads[None, :]) / NA_HEAD_DIM, BF16)
    gmb = jnp.repeat(jnp.transpose(p['gm_bs'][l]).astype(F32), GM_WIDTH // GM_GROUPS, axis=1)
    lp = {k: p[k][l] for k in ('ssm_lambda_re', 'ssm_lambda_im', 'ssm_log_dt', 'ssm_b_re', 'ssm_b_im',
                               'ssm_c_re', 'ssm_c_im', 'ssm_d')}
    tb, rm = _na_tables(p['na_rpb'][l])
    return dict(
        g1=p['norm_ffn1'][l].reshape(1, D_MODEL), g2=p['norm_mix'][l].reshape(1, D_MODEL),
        g3=p['norm_ffn2'][l].reshape(1, D_MODEL),
        f1_in=p['ffn1_w_in'][l].astype(BF16), f1_out=p['ffn1_w_out'][l].astype(BF16),
        f2_in=p['ffn2_w_in'][l].astype(BF16), f2_out=p['ffn2_w_out'][l].astype(BF16),
        w_in=p['w_in'][l].astype(BF16), w_out=p['w_out'][l].astype(BF16),
        hsum=hsum,
        qg=jnp.tile(p['na_q_norm'][l].astype(F32), NA_HEADS).reshape(1, NA_WIDTH),
        kg=jnp.tile(p['na_k_norm'][l].astype(F32), NA_HEADS).reshape(1, NA_WIDTH),
        ws=p['gm_ws'][l].astype(BF16), gmb=gmb,
        ssm=_ssm_params(lp),
        glu_w=p['ssm_glu_w'][l].astype(BF16), glu_b=p['ssm_glu_b'][l].astype(F32).reshape(1, SSM_WIDTH),
        tb=tb, rm=rm,
    )


def _pairs_major(t):
    b, s = t.shape[0], t.shape[1]
    return jnp.transpose(t.reshape(b, s, HEAD_PAIRS, LANES), (0, 2, 1, 3)).astype(BF16)


def _trunk_layer(x, mod_l, lw, *, n_batch, seq_len, layer=None, caches=None, ctx_kv=None, ssm_init=None):
    is_ctx = layer is not None
    seg = dict(seg0=0 if is_ctx else 1, seg_tokens=x.shape[0] if is_ctx else seq_len)
    x = _ffn(x, mod_l, lw['g1'], lw['f1_in'], lw['f1_out'], mod_base=0, **seg)
    outs = _inproj(x, mod_l, lw['g2'], lw['w_in'], lw['hsum'], lw['qg'], lw['kg'], lw['ws'], lw['gmb'],
                   layer=layer, caches=caches, **seg)
    u, q, k, v, gm = outs[:5]
    y, fin = _ssm_mixer_core(u, lw['ssm'], n_batch, seq_len, ssm_init)
    if is_ctx:
        na = _ctx_attn(q, k, v, n_batch)
    else:
        na = _na_attn(q, k, v, ctx_kv[0], ctx_kv[1], lw['tb'], lw['rm'], n_batch)
    x = _outproj(x, mod_l, y, na, gm, lw['glu_w'], lw['glu_b'], lw['w_out'], **seg)
    x = _ffn(x, mod_l, lw['g3'], lw['f2_in'], lw['f2_out'], mod_base=6, **seg)
    return x, (tuple(outs[5:]), fin)


def kernel(x_prompt, x_sample, c, cache_k, cache_v, state_ssm, c_ctx, w_ada, b_ada, norm_ffn1, ffn1_w_in, ffn1_w_out, norm_mix, w_in, w_out, ssm_lambda_re, ssm_lambda_im, ssm_log_dt, ssm_b_re, ssm_b_im, ssm_c_re, ssm_c_im, ssm_d, ssm_glu_w, ssm_glu_b, na_q_norm, na_k_norm, na_rpb, gm_ws, gm_bs, norm_ffn2, ffn2_w_in, ffn2_w_out):
    p = dict(norm_ffn1=norm_ffn1, ffn1_w_in=ffn1_w_in, ffn1_w_out=ffn1_w_out, norm_mix=norm_mix, w_in=w_in,
             w_out=w_out, ssm_lambda_re=ssm_lambda_re, ssm_lambda_im=ssm_lambda_im, ssm_log_dt=ssm_log_dt,
             ssm_b_re=ssm_b_re, ssm_b_im=ssm_b_im, ssm_c_re=ssm_c_re, ssm_c_im=ssm_c_im, ssm_d=ssm_d,
             ssm_glu_w=ssm_glu_w, ssm_glu_b=ssm_glu_b, na_q_norm=na_q_norm, na_k_norm=na_k_norm, na_rpb=na_rpb,
             gm_ws=gm_ws, gm_bs=gm_bs, norm_ffn2=norm_ffn2, ffn2_w_in=ffn2_w_in, ffn2_w_out=ffn2_w_out)
    batch, dec_batch = x_prompt.shape[0], x_sample.shape[0]
    cond8 = jnp.zeros((8, D_MODEL), F32).at[0].set(c_ctx).at[1:1 + dec_batch].set(c)
    mod = _adaln(cond8, w_ada, b_ada).reshape(DEPTH, 8, N_MOD, D_MODEL)

    xp = x_prompt.reshape(batch * SEQ, D_MODEL)
    xs = x_sample.reshape(dec_batch * DEC_SEQ, D_MODEL)
    caches, new_s = None, []
    for l in range(DEPTH):
        lw = _layer_weights(l, p)
        xp, (caches, s_c) = _trunk_layer(xp, mod[l], lw, n_batch=batch, seq_len=SEQ, layer=l, caches=caches)
        new_s.append(s_c)
        ctx_kv = (_pairs_major(cache_k[:, l]), _pairs_major(cache_v[:, l]))
        xs, _ = _trunk_layer(xs, mod[l], lw, n_batch=dec_batch, seq_len=DEC_SEQ,
                             ctx_kv=ctx_kv, ssm_init=state_ssm[:, l])
    cache_shape = (batch, DEPTH, SEQ, NA_HEADS, NA_HEAD_DIM)
    return (xp.reshape(batch, SEQ, D_MODEL), xs.reshape(dec_batch, DEC_SEQ, D_MODEL),
            caches[0].reshape(cache_shape), caches[1].reshape(cache_shape), jnp.stack(new_s, axis=1))
```

```python
import functools

import numpy as np
import jax
import jax.numpy as jnp
from jax import lax
from jax.experimental import pallas as pl
from jax.experimental.pallas import tpu as pltpu

D_MODEL = 1024
DEPTH = 2
SEQ = 256
DEC_SEQ = 4096
PAST_LEN = 256
GRID_W = 64
SSM_WIDTH = 256
SSM_CH = 16
SSM_GROUPS = 16
SSM_STATE = 64
NA_WIDTH = 512
NA_HEAD_DIM = 64
NA_HEADS = 8
NA_MAX_ROWS = 8
NA_COLS = 16
GM_WIDTH = 256
GM_GROUPS = 4
GM_CHUNK = 128
D_FF = 2816
N_MOD = 9
RMS_EPS = 1e-6
LN_EPS = 1e-5
NEG_INF = -1e30

F32 = jnp.float32
BF16 = jnp.bfloat16

LANES = 128
SUBLANES = 8
MXU_DIM = 256
VMEM_LIMIT_BYTES = 56 * 1024 * 1024

FFN_TILE = 1024
MIX_TILE = 1024
FF_CHUNK = MXU_DIM
N_FF_CHUNKS = D_FF // FF_CHUNK
SSM_CHUNK = 16
SSM_ROW = SSM_CHUNK * SSM_CH
SSM_GROUP_BLOCK = 4
SSM_W_COLS = 4 * LANES
SCAN_SHIFTS = (1, 2, 4)
N_SCAN_CONST = 3 * len(SCAN_SHIFTS) + 3
HEAD_PAIRS = NA_HEADS // 2
NA_Q_ROWS = 8
NA_K_ROWS = 16
NA_KCOLS = 2 * NA_COLS
NA_COL_BLOCKS = GRID_W // NA_COLS
NA_KC0 = (0, 8, 24, 32)
SEGS_PER_TILE = LANES // SSM_CH
NA_OFF_PAD = (NA_Q_ROWS - 1) - ((NA_MAX_ROWS - 1) - (NA_K_ROWS - NA_Q_ROWS) // 2)
NA_OFF_START = NA_Q_ROWS - 1
NA_OFF_ROWS = NA_OFF_START + NA_K_ROWS


def _silu(x):
    return x * (1.0 / (1.0 + jnp.exp(-x)))


def _sigmoid(x):
    return 1.0 / (1.0 + jnp.exp(-x))


def _gelu(x):
    return 0.5 * x * (1.0 + jnp.tanh(0.7978845608028654 * (x + 0.044715 * (x * x * x))))


def _cparams(sem):
    return pltpu.CompilerParams(dimension_semantics=sem, vmem_limit_bytes=VMEM_LIMIT_BYTES)


def _const_spec(shape):
    nd = len(shape)
    return pl.BlockSpec(shape, lambda *_: (0,) * nd, pipeline_mode=pl.Buffered(1))


def _layer_spec(stacked, l):
    nd = stacked.ndim
    return pl.BlockSpec((None,) + stacked.shape[1:], lambda *_: (l,) + (0,) * (nd - 1), pipeline_mode=pl.Buffered(1))


def _ada_kernel(c_ref, w_ref, b_ref, o_ref):
    s = _silu(c_ref[...]).astype(BF16)
    w = w_ref[0].astype(BF16)
    o_ref[0] = jnp.dot(s, w, preferred_element_type=F32) + b_ref[0]


def _adaln(cond8, w_ada, b_ada):
    tn = D_MODEL
    ncol = N_MOD * D_MODEL
    return pl.pallas_call(
        _ada_kernel,
        grid=(DEPTH, ncol // tn),
        in_specs=[
            pl.BlockSpec((8, D_MODEL), lambda l, j: (0, 0)),
            pl.BlockSpec((1, D_MODEL, tn), lambda l, j: (l, 0, j)),
            pl.BlockSpec((1, 1, tn), lambda l, j: (l, 0, j)),
        ],
        out_specs=pl.BlockSpec((1, 8, tn), lambda l, j: (l, 0, j)),
        out_shape=jax.ShapeDtypeStruct((DEPTH, 8, ncol), F32),
        compiler_params=_cparams(("arbitrary", "arbitrary")),
        name="adaln",
    )(cond8, w_ada, b_ada.reshape(DEPTH, 1, ncol))


def _normed(x, g_row, shift, scale):
    ms = jnp.mean(x * x, axis=-1, keepdims=True)
    h = x * lax.rsqrt(ms + RMS_EPS) * g_row
    return h * (1.0 + scale) + shift


def _mod_spec(l, seg0, seg_tokens, tile):
    tiles_per_seg = seg_tokens // tile
    return pl.BlockSpec((None, 1, N_MOD, D_MODEL), lambda i: (l, seg0 + i // tiles_per_seg, 0, 0))


def _ffn_kernel(x_ref, mod_ref, g_ref, win_ref, wout_ref, o_ref, h_ref, act_ref, *, mod_base):
    x = x_ref[...]
    shift = mod_ref[0, mod_base:mod_base + 1, :]
    scale = mod_ref[0, mod_base + 1:mod_base + 2, :]
    gate = mod_ref[0, mod_base + 2:mod_base + 3, :]
    h_ref[...] = _normed(x, g_ref[...], shift, scale).astype(BF16)
    for j in range(N_FF_CHUNKS):
        cols = slice(j * FF_CHUNK, (j + 1) * FF_CHUNK)
        gate_j = jnp.dot(h_ref[...], win_ref[:, cols], preferred_element_type=F32)
        up_j = jnp.dot(h_ref[...], win_ref[:, D_FF + j * FF_CHUNK:D_FF + (j + 1) * FF_CHUNK], preferred_element_type=F32)
        act_ref[:, cols] = (_silu(gate_j) * up_j).astype(BF16)
    for n in range(D_MODEL // MXU_DIM):
        cols = slice(n * MXU_DIM, (n + 1) * MXU_DIM)
        y = jnp.dot(act_ref[...], wout_ref[:, cols], preferred_element_type=F32)
        o_ref[:, cols] = x_ref[:, cols] + 0.5 * gate[:, cols] * y


def _ffn(x, mod, g, w_in, w_out, *, l, mod_base, seg0, seg_tokens):
    n_tok = x.shape[0]
    return pl.pallas_call(
        functools.partial(_ffn_kernel, mod_base=mod_base),
        grid=(n_tok // FFN_TILE,),
        in_specs=[
            pl.BlockSpec((FFN_TILE, D_MODEL), lambda i: (i, 0)),
            _mod_spec(l, seg0, seg_tokens, FFN_TILE),
            _layer_spec(g, l), _layer_spec(w_in, l), _layer_spec(w_out, l),
        ],
        out_specs=pl.BlockSpec((FFN_TILE, D_MODEL), lambda i: (i, 0)),
        out_shape=jax.ShapeDtypeStruct((n_tok, D_MODEL), F32),
        scratch_shapes=[pltpu.VMEM((FFN_TILE, D_MODEL), BF16), pltpu.VMEM((FFN_TILE, D_FF), BF16)],
        compiler_params=_cparams(("arbitrary",)),
        name="ffn",
    )(x, mod, g, w_in, w_out)


def _segment_ids():
    return lax.broadcasted_iota(jnp.int32, (1, LANES), 1) // SSM_CH


def _merge_segments(load_piece, rot):
    seg = _segment_ids()
    tiles = []
    for lt in range(SSM_ROW // LANES):
        acc = None
        for jj in range(SEGS_PER_TILE):
            src = (lt * SEGS_PER_TILE + jj - rot) % SSM_CHUNK
            piece = load_piece(src, lt)
            acc = piece if acc is None else jnp.where(seg == jj, piece, acc)
        tiles.append(acc)
    return jnp.concatenate(tiles, axis=-1)


def _roll_lanes(x, shift):
    shift %= x.shape[-1]
    return pltpu.roll(x, shift, axis=x.ndim - 1) if shift else x


def _inproj_kernel(*refs, n_in, n_out):
    x_ref, mod_ref, g_ref, w_ref, hsum_ref, qg_ref, kg_ref, ws_ref, gmb_ref = refs[:9]
    u_ref, q_ref, k_ref, v_ref, gm_ref, *tok_refs = refs[n_in:n_in + n_out]
    h_ref, xs_ref, z_ref = refs[n_in + n_out:]
    shift = mod_ref[0, 3:4, :]
    scale = mod_ref[0, 4:5, :]
    h_ref[...] = _normed(x_ref[...], g_ref[...], shift, scale).astype(BF16)
    n_rows = MIX_TILE // SSM_CHUNK

    def proj(c0, width):
        return jnp.dot(h_ref[...], w_ref[:, c0:c0 + width], preferred_element_type=F32)

    xs = proj(0, SSM_WIDTH)
    for j in range(SSM_WIDTH // LANES):
        xs_ref[j] = xs[:, j * LANES:(j + 1) * LANES]
    for t in range(SSM_CHUNK):
        zt = jnp.concatenate([xs_ref[j, pl.ds(t, n_rows, stride=SSM_CHUNK), :] for j in range(SSM_WIDTH // LANES)],
                             axis=-1)
        z_ref[t] = _roll_lanes(zt, SSM_CH * t)
    for g in range(SSM_GROUPS):
        ug = _merge_segments(lambda src, lt: z_ref[src, :, lt * LANES:(lt + 1) * LANES], g)
        u_ref[g] = _roll_lanes(ug, -SSM_CH * g)

    def head_norm(t, gain_row):
        ms = jnp.dot((t * t).astype(BF16), hsum_ref[...], preferred_element_type=F32)
        return t * lax.rsqrt(ms + RMS_EPS) * gain_row

    c0 = SSM_WIDTH
    q = head_norm(proj(c0, NA_WIDTH), qg_ref[...]) * (NA_HEAD_DIM ** -0.5)
    k = head_norm(proj(c0 + NA_WIDTH, NA_WIDTH), kg_ref[...])
    v = proj(c0 + 2 * NA_WIDTH, NA_WIDTH)
    for p in range(HEAD_PAIRS):
        lanes = slice(p * LANES, (p + 1) * LANES)
        q_ref[p] = q[:, lanes].astype(q_ref.dtype)
        k_ref[p] = k[:, lanes].astype(k_ref.dtype)
        v_ref[p] = v[:, lanes].astype(v_ref.dtype)
    if tok_refs:
        for slot in range(tok_refs[0].shape[1]):
            tok_refs[0][:, slot] = k.reshape(MIX_TILE // SEQ, SEQ, NA_WIDTH)
            tok_refs[1][:, slot] = v.reshape(MIX_TILE // SEQ, SEQ, NA_WIDTH)

    a = _gelu(proj(c0 + 3 * NA_WIDTH, 2 * GM_WIDTH))
    u = a[:, :GM_WIDTH]
    vv = a[:, GM_WIDTH:]
    mu = jnp.mean(vv, axis=-1, keepdims=True)
    vc = vv - mu
    var = jnp.mean(vc * vc, axis=-1, keepdims=True)
    vln = (vc * lax.rsqrt(var + LN_EPS)).astype(BF16)
    lane = lax.broadcasted_iota(jnp.int32, (1, GM_WIDTH), 1)
    gw = GM_WIDTH // GM_GROUPS
    for ci in range(MIX_TILE // GM_CHUNK):
        rows = slice(ci * GM_CHUNK, (ci + 1) * GM_CHUNK)
        vch = vln[rows, :]
        sp = gmb_ref[...]
        for gi in range(GM_GROUPS):
            t = jnp.dot(ws_ref[gi], vch, preferred_element_type=F32)
            sp = sp + jnp.where((lane >= gi * gw) & (lane < (gi + 1) * gw), t, 0.0)
        gm_ref[rows, :] = (u[rows, :] * sp).astype(BF16)


def _inproj(x, mod, g, w, hsum, qg, kg, ws, gmb, *, l, seg0, seg_tokens, layer=None, caches=None):
    n_tok = x.shape[0]
    is_ctx = layer is not None
    kv_dtype = BF16 if is_ctx else F32
    n_rows = MIX_TILE // SSM_CHUNK
    pair_spec = pl.BlockSpec((HEAD_PAIRS, MIX_TILE, LANES), lambda i: (0, i, 0))
    out_specs = [
        pl.BlockSpec((SSM_GROUPS, n_rows, SSM_ROW), lambda i: (0, i, 0)),
        pair_spec, pair_spec, pair_spec,
        pl.BlockSpec((MIX_TILE, GM_WIDTH), lambda i: (i, 0)),
    ]
    out_shape = [
        jax.ShapeDtypeStruct((SSM_GROUPS, n_tok // SSM_CHUNK, SSM_ROW), F32),
        jax.ShapeDtypeStruct((HEAD_PAIRS, n_tok, LANES), BF16),
        jax.ShapeDtypeStruct((HEAD_PAIRS, n_tok, LANES), kv_dtype),
        jax.ShapeDtypeStruct((HEAD_PAIRS, n_tok, LANES), kv_dtype),
        jax.ShapeDtypeStruct((n_tok, GM_WIDTH), BF16),
    ]
    args = [x, mod, g, w, hsum, qg, kg, ws, gmb]
    in_specs = [
        pl.BlockSpec((MIX_TILE, D_MODEL), lambda i: (i, 0)),
        _mod_spec(l, seg0, seg_tokens, MIX_TILE),
        _layer_spec(g, l), _layer_spec(w, l),
        _const_spec((NA_WIDTH, NA_WIDTH)),
        _layer_spec(qg, l), _layer_spec(kg, l), _layer_spec(ws, l), _layer_spec(gmb, l),
    ]
    aliases = {}
    if is_ctx:
        bt = MIX_TILE // SEQ
        cache_shape = jax.ShapeDtypeStruct((n_tok // SEQ, DEPTH, SEQ, NA_WIDTH), F32)
        if caches is None:
            out_specs += [pl.BlockSpec((bt, DEPTH, SEQ, NA_WIDTH), lambda i: (i, 0, 0, 0))] * 2
        else:
            out_specs += [pl.BlockSpec((bt, 1, SEQ, NA_WIDTH), lambda i: (i, layer, 0, 0))] * 2
        out_shape += [cache_shape] * 2
        if caches is not None:
            aliases = {len(args): len(out_shape) - 2, len(args) + 1: len(out_shape) - 1}
            args += list(caches)
            in_specs += [pl.BlockSpec(memory_space=pl.ANY)] * 2
    return pl.pallas_call(
        functools.partial(_inproj_kernel, n_in=len(args), n_out=len(out_shape)),
        grid=(n_tok // MIX_TILE,),
        in_specs=in_specs,
        out_specs=out_specs,
        out_shape=out_shape,
        input_output_aliases=aliases,
        scratch_shapes=[pltpu.VMEM((MIX_TILE, D_MODEL), BF16), pltpu.VMEM((SSM_WIDTH // LANES, MIX_TILE, LANES), F32),
                        pltpu.VMEM((SSM_CHUNK, n_rows, SSM_ROW), F32)],
        compiler_params=_cparams(("arbitrary",)),
        name="inproj_ctx" if is_ctx else "inproj_lat",
    )(*args)


def _shift_rows(x, d, row, down):
    if down:
        return jnp.where(row >= d, pltpu.roll(x, d, axis=0), 0.0)
    return jnp.where(row < SUBLANES - d, pltpu.roll(x, SUBLANES - d, axis=0), 0.0)


def _scan_tile(w, wp, c, cp, cst, row, down):
    x, xp = w, wp
    for si, d in enumerate(SCAN_SHIFTS):
        a, b, bp = cst[3 * si], cst[3 * si + 1], cst[3 * si + 2]
        xs, xps = _shift_rows(x, d, row, down), _shift_rows(xp, d, row, down)
        x, xp = x + a * xs + b * xps, xp + a * xps + bp * xs
    a, b, bp = cst[N_SCAN_CONST - 3], cst[N_SCAN_CONST - 2], cst[N_SCAN_CONST - 1]
    s = x + a * c + b * cp
    sp = xp + a * cp + bp * c
    edge = 0 if down else SUBLANES - 1
    last = SUBLANES - 1 if down else 0
    seen = jnp.where(row == edge, c, _shift_rows(s, 1, row, down))
    c_new = jnp.broadcast_to(s[last:last + 1, :], s.shape)
    cp_new = jnp.broadcast_to(sp[last:last + 1, :], sp.shape)
    return seen, c_new, cp_new


def _ssm_kernel(u_ref, mp_ref, q_ref, cst_ref, dvec_ref, s0_ref, y_ref, fin_ref, w_ref, sq_ref, *, n_batch, n_chunks):
    gb = u_ref.shape[0]
    n_rows = n_batch * n_chunks
    row_blk = min(n_rows, 512)
    for g in range(gb):
        for r0 in range(0, n_rows, row_blk):
            rows = slice(r0, r0 + row_blk)
            ug = u_ref[g, rows, :]
            wy = jnp.dot(ug.astype(BF16), mp_ref[g], preferred_element_type=F32)
            y_ref[g, rows, :] = wy[:, :SSM_ROW] + ug * dvec_ref[g]
            w_ref[g, rows, :] = wy[:, SSM_ROW:]

    n_tiles = n_chunks // SUBLANES
    row = lax.broadcasted_iota(jnp.int32, (SUBLANES, LANES), 0)

    def tile_step(b, kt, carry):
        out = []
        for g in range(gb):
            cf, cfp, cb, cbp = carry[4 * g:4 * g + 4]
            rf = pl.multiple_of(b * n_chunks + kt * SUBLANES, SUBLANES)
            rb = pl.multiple_of(b * n_chunks + (n_tiles - 1 - kt) * SUBLANES, SUBLANES)
            wf = w_ref[g, pl.ds(rf, SUBLANES), :]
            wb = w_ref[g, pl.ds(rb, SUBLANES), :]
            cst_f = [cst_ref[g, 0, i] for i in range(N_SCAN_CONST)]
            cst_b = [cst_ref[g, 1, i] for i in range(N_SCAN_CONST)]
            seen_f, cf, cfp = _scan_tile(wf[:, 0:LANES], wf[:, LANES:2 * LANES], cf, cfp, cst_f, row, True)
            seen_b, cb, cbp = _scan_tile(wb[:, 2 * LANES:3 * LANES], wb[:, 3 * LANES:], cb, cbp, cst_b, row, False)
            sq_ref[g, pl.ds(rf, SUBLANES), 0:LANES] = seen_f
            sq_ref[g, pl.ds(rb, SUBLANES), LANES:2 * LANES] = seen_b
            out += [cf, cfp, cb, cbp]
        return tuple(out)

    def batch_body(b, _):
        carry = []
        for g in range(gb):
            for i in range(4):
                carry.append(jnp.broadcast_to(s0_ref[g, i, pl.ds(b, 1), :], (SUBLANES, LANES)))
        carry = tuple(carry)
        if n_tiles <= 2:
            for kt in range(n_tiles):
                carry = tile_step(b, kt, carry)
        else:
            carry = lax.fori_loop(0, n_tiles, lambda kt, c: tile_step(b, kt, c), carry)
        for g in range(gb):
            fin_ref[g, 0, pl.ds(b, 1), :] = carry[4 * g][0:1, :]
            fin_ref[g, 1, pl.ds(b, 1), :] = carry[4 * g + 2][0:1, :]
        return 0

    lax.fori_loop(0, n_batch, batch_body, 0)

    for g in range(gb):
        for r0 in range(0, n_rows, row_blk):
            rows = slice(r0, r0 + row_blk)
            y_ref[g, rows, :] += jnp.dot(sq_ref[g, rows, :].astype(BF16), q_ref[g], preferred_element_type=F32)


def _ssm_core(u, mp, q, cst, dvec, s0, *, l, n_batch, n_chunks):
    n_rows = n_batch * n_chunks
    gb = SSM_GROUP_BLOCK
    g3 = lambda i: (i, 0, 0)
    g4 = lambda i: (i, 0, 0, 0)
    return pl.pallas_call(
        functools.partial(_ssm_kernel, n_batch=n_batch, n_chunks=n_chunks),
        grid=(SSM_GROUPS // gb,),
        in_specs=[
            pl.BlockSpec((gb, n_rows, SSM_ROW), g3),
            pl.BlockSpec((None, gb, SSM_ROW, SSM_ROW + SSM_W_COLS), lambda i: (l, i, 0, 0)),
            pl.BlockSpec((None, gb, 2 * LANES, SSM_ROW), lambda i: (l, i, 0, 0)),
            pl.BlockSpec((None, gb, 2, N_SCAN_CONST, SUBLANES, LANES), lambda i: (l, i, 0, 0, 0, 0)),
            pl.BlockSpec((None, gb, 1, SSM_ROW), lambda i: (l, i, 0, 0)),
            pl.BlockSpec((gb, 4, n_batch, LANES), g4),
        ],
        out_specs=[
            pl.BlockSpec((gb, n_rows, SSM_ROW), g3),
            pl.BlockSpec((gb, 2, n_batch, LANES), g4),
        ],
        out_shape=[
            jax.ShapeDtypeStruct((SSM_GROUPS, n_rows, SSM_ROW), F32),
            jax.ShapeDtypeStruct((SSM_GROUPS, 2, n_batch, LANES), F32),
        ],
        scratch_shapes=[pltpu.VMEM((gb, n_rows, SSM_W_COLS), F32), pltpu.VMEM((gb, n_rows, 2 * LANES), F32)],
        compiler_params=_cparams(("arbitrary",)),
        name="ssm_core",
    )(u, mp, q, cst, dvec, s0)


def _cmul(a, b):
    return a[0] * b[0] - a[1] * b[1], a[0] * b[1] + a[1] * b[0]


_CHUNK_IDX = np.arange(SSM_CHUNK)


def _ssm_exponents():
    t = SSM_CHUNK
    idx = _CHUNK_IDX.astype(np.float32)
    both = lambda a: np.stack([a, a], axis=1)
    lane_step = np.repeat(idx - (t - 1) / 2, SSM_CH)
    tile_rows = np.arange(1, SUBLANES + 1, dtype=np.float32) * t
    tables = dict(
        one=both(np.ones(1, np.float32)),
        incr=np.stack([t - 1 - idx, idx], axis=1),
        seen=np.stack([idx + 1, t - idx], axis=1),
        resp_in=np.stack([-lane_step, lane_step], axis=1),
        resp_out=np.stack([lane_step, -lane_step], axis=1),
        shift=both(np.asarray(SCAN_SHIFTS, np.float32) * t),
        tile=np.stack([tile_rows, tile_rows[::-1]], axis=1),
    )
    spans, start = {}, 0
    for name, tab in tables.items():
        spans[name] = slice(start, start + len(tab))
        start += len(tab)
    return np.concatenate(list(tables.values()), axis=0), spans


def _ssm_params(p):
    t = SSM_CHUNK
    n_l = p['ssm_lambda_re'].shape[0]
    lam = (p['ssm_lambda_re'].astype(F32), p['ssm_lambda_im'].astype(F32))
    dt = jnp.exp(p['ssm_log_dt'].astype(F32))[..., None]
    arg = (lam[0] * dt, lam[1] * dt)
    expo, span = _ssm_exponents()
    n = jnp.asarray(expo)[:, None, :, None, None]
    mag = jnp.exp(n * arg[0][None])
    pw = (mag * jnp.cos(n * arg[1][None]), mag * jnp.sin(n * arg[1][None]))
    take = lambda name: [x[span[name]] for x in pw]
    lanes_last = lambda xs: [jnp.moveaxis(x, 0, -1) for x in xs]

    lbar = [x[0] for x in take('one')]
    num = (lbar[0] - 1.0, lbar[1])
    den = lam[0] * lam[0] + lam[1] * lam[1]
    zoh = ((num[0] * lam[0] + num[1] * lam[1]) / den, (num[1] * lam[0] - num[0] * lam[1]) / den)
    b_mat = (p['ssm_b_re'].astype(F32), p['ssm_b_im'].astype(F32))
    bbar = _cmul((zoh[0][..., None], zoh[1][..., None]), b_mat)
    c_mat = [jnp.swapaxes(p[k].astype(F32), -1, -2) for k in ('ssm_c_re', 'ssm_c_im')]
    ct = [jnp.tile(x, (1, 1, 1, 1, t)) for x in c_mat]
    bt = [jnp.tile(x, (1, 1, 1, 1, t)) for x in bbar]

    xm = _cmul(lanes_last(take('resp_in')), bt)
    ym = _cmul(lanes_last(take('resp_out')), ct)
    resp = jnp.einsum('ldgpx,ldgpy->ldgxy', jnp.concatenate([xm[0], -xm[1]], axis=3),
                      jnp.concatenate([ym[0], ym[1]], axis=3), precision='highest')
    steps_x = np.repeat(_CHUNK_IDX, SSM_CH)
    causal = np.stack([steps_x[None, :] >= steps_x[:, None], steps_x[:, None] >= steps_x[None, :]])
    mats = jnp.sum(jnp.where(jnp.asarray(causal)[None, :, None], resp, 0.0), axis=1)

    incr = _cmul([x[..., None] for x in take('incr')], [x[None] for x in bbar])
    incr = [jnp.transpose(x, (1, 2, 3, 0, 5, 4)) for x in incr]
    pcat = jnp.concatenate([incr[0], incr[1], incr[1], incr[0]], axis=-1)
    pcat = jnp.concatenate([pcat[:, 0], pcat[:, 1]], axis=-1).reshape(n_l, SSM_GROUPS, SSM_ROW, SSM_W_COLS)
    mp = jnp.concatenate([mats, pcat], axis=-1).astype(BF16)

    seen = [jnp.repeat(x, SSM_CH, axis=-1) for x in lanes_last(take('seen'))]
    qm = _cmul(ct, seen)
    q = jnp.concatenate([qm[0][:, 0], -qm[1][:, 0], qm[0][:, 1], -qm[1][:, 1]], axis=2).astype(BF16)

    def forms(name, rows):
        pr, pi = [jnp.moveaxis(x, 0, 3) for x in take(name)]
        f = jnp.stack([jnp.concatenate([pr, pr], -1), jnp.concatenate([-pi, pi], -1),
                       jnp.concatenate([pi, -pi], -1)], axis=4)
        return f if rows else jnp.broadcast_to(
            f.reshape(n_l, 2, SSM_GROUPS, 3 * len(SCAN_SHIFTS), 1, LANES),
            (n_l, 2, SSM_GROUPS, 3 * len(SCAN_SHIFTS), SUBLANES, LANES))
    per_row = jnp.swapaxes(forms('tile', True), 3, 4)
    cst = jnp.swapaxes(jnp.concatenate([forms('shift', False), per_row], axis=3), 1, 2)
    dvec = jnp.tile(p['ssm_d'].astype(F32).reshape(n_l, SSM_GROUPS, 1, SSM_CH), (1, 1, 1, SSM_CHUNK))
    return mp, q, cst, dvec


def _ssm_states_in(state_ssm):
    st = jnp.transpose(state_ssm.astype(F32), (1, 3, 2, 0, 4, 5))
    re, im = st[..., 0], st[..., 1]
    ri, ir = jnp.concatenate([re, im], -1), jnp.concatenate([im, re], -1)
    return jnp.stack([ri[:, :, 0], ir[:, :, 0], ri[:, :, 1], ir[:, :, 1]], axis=2)


def _ssm_states_out(fins):
    fin = jnp.stack(fins, axis=0)
    fin = jnp.stack([fin[..., :SSM_STATE], fin[..., SSM_STATE:]], axis=-1)
    return jnp.transpose(fin, (3, 0, 2, 1, 4, 5))


def _head_lane_mask(hh):
    lane = lax.broadcasted_iota(jnp.int32, (1, LANES), 1)
    return (lane >= hh * NA_HEAD_DIM) & (lane < (hh + 1) * NA_HEAD_DIM)


_NT_DIMS = (((1,), (1,)), ((), ()))


def _ctx_attn_kernel(q_ref, k_ref, v_ref, o_ref):
    for p in range(HEAD_PAIRS):
        qp, kp, vp = q_ref[p], k_ref[p], v_ref[p]
        lm0 = _head_lane_mask(0)
        zero = jnp.zeros_like(qp)
        q2 = jnp.concatenate([jnp.where(lm0, qp, zero), jnp.where(lm0, zero, qp)], axis=0)
        s = lax.dot_general(q2, kp, _NT_DIMS, preferred_element_type=F32)
        e = jnp.exp(s - jnp.max(s, axis=-1, keepdims=True))
        o = jnp.dot(e.astype(BF16), vp, preferred_element_type=F32) / jnp.sum(e, axis=-1, keepdims=True)
        o_ref[:, p * LANES:(p + 1) * LANES] = jnp.where(lm0, o[:SEQ, :], o[SEQ:, :]).astype(BF16)


def _ctx_attn(q, k, v, n_batch):
    spec = pl.BlockSpec((HEAD_PAIRS, SEQ, LANES), lambda b: (0, b, 0))
    return pl.pallas_call(
        _ctx_attn_kernel,
        grid=(n_batch,),
        in_specs=[spec, spec, spec],
        out_specs=pl.BlockSpec((SEQ, NA_WIDTH), lambda b: (b, 0)),
        out_shape=jax.ShapeDtypeStruct((n_batch * SEQ, NA_WIDTH), BF16),
        compiler_params=_cparams(("arbitrary",)),
        name="ctx_attn",
    )(q, k, v)


def _na_kernel(q_ref, k_ref, v_ref, kc_ref, vc_ref, t2_ref, rm_ref, o_ref, qb_ref, os_ref, tb_ref):
    rt = pl.program_id(1)
    n_rt = pl.num_programs(1)

    @pl.when((pl.program_id(0) == 0) & (rt == 0))
    def _():
        for h in range(NA_HEADS):
            for n in range(NA_COL_BLOCKS):
                for rl in range(NA_Q_ROWS):
                    lane0 = (NA_OFF_START - rl) * NA_KCOLS
                    tb_ref[h, n, rl * NA_COLS:(rl + 1) * NA_COLS, :] = (
                        t2_ref[h, n, :, lane0:lane0 + NA_K_ROWS * NA_KCOLS])

    r0 = rt * NA_Q_ROWS
    ttype = jnp.where(rt == 0, 0, jnp.where(rt == n_rt - 1, 2, 1))
    rmask = rm_ref[ttype]
    blk_q = NA_Q_ROWS * NA_COLS

    def pair_body(p, carry):
        lm0 = _head_lane_mask(0)
        for n in range(NA_COL_BLOCKS):
            for rl in range(NA_Q_ROWS):
                qrow = q_ref[p, rl * GRID_W + n * NA_COLS:rl * GRID_W + (n + 1) * NA_COLS, :]
                zero = jnp.zeros_like(qrow)
                base = 2 * n * blk_q + rl * NA_COLS
                qb_ref[base:base + NA_COLS, :] = jnp.where(lm0, qrow, zero)
                qb_ref[base + blk_q:base + blk_q + NA_COLS, :] = jnp.where(lm0, zero, qrow)
        kcp, vcp = kc_ref[0, p], vc_ref[0, p]
        sc_all = lax.dot_general(qb_ref[...], kcp, _NT_DIMS, preferred_element_type=F32)
        for n in range(NA_COL_BLOCKS):
            kparts, vparts = [], []
            for i in range(NA_K_ROWS):
                kr = jnp.clip(r0 - (NA_K_ROWS - NA_Q_ROWS) // 2 + i, 0, GRID_W - 1)
                start = pl.multiple_of(kr * GRID_W + NA_KC0[n], 8)
                kparts.append(k_ref[p, pl.ds(start, NA_KCOLS), :])
                vparts.append(v_ref[p, pl.ds(start, NA_KCOLS), :])
            kblk = jnp.concatenate(kparts, axis=0).astype(BF16)
            vblk = jnp.concatenate(vparts, axis=0).astype(BF16)
            rows2 = slice(2 * n * blk_q, 2 * (n + 1) * blk_q)
            s = lax.dot_general(qb_ref[rows2, :], kblk, _NT_DIMS, preferred_element_type=F32)
            s = s + jnp.concatenate([tb_ref[2 * p, n] + rmask, tb_ref[2 * p + 1, n] + rmask], axis=0)
            sc = sc_all[rows2, :]
            m = jnp.maximum(jnp.max(s, axis=-1, keepdims=True), jnp.max(sc, axis=-1, keepdims=True))
            e = jnp.exp(s - m)
            ec = jnp.exp(sc - m)
            den = jnp.sum(e, axis=-1, keepdims=True) + jnp.sum(ec, axis=-1, keepdims=True)
            o = (jnp.dot(e.astype(BF16), vblk, preferred_element_type=F32)
                 + jnp.dot(ec.astype(BF16), vcp, preferred_element_type=F32)) / den
            oblk = jnp.where(lm0, o[:blk_q, :], o[blk_q:, :])
            for rl in range(NA_Q_ROWS):
                os_ref[p, rl * GRID_W + n * NA_COLS:rl * GRID_W + (n + 1) * NA_COLS, :] = (
                    oblk[rl * NA_COLS:(rl + 1) * NA_COLS, :])
        return carry

    lax.fori_loop(0, HEAD_PAIRS, pair_body, 0)
    for p in range(HEAD_PAIRS):
        o_ref[:, p * LANES:(p + 1) * LANES] = os_ref[p].astype(BF16)


def _na_attn(q, k, v, kc, vc, tb, rm, n_batch, *, l):
    tile_tok = NA_Q_ROWS * GRID_W
    n_rt = DEC_SEQ // tile_tok
    return pl.pallas_call(
        _na_kernel,
        grid=(n_batch, n_rt),
        in_specs=[
            pl.BlockSpec((HEAD_PAIRS, tile_tok, LANES), lambda b, r: (0, b * n_rt + r, 0)),
            pl.BlockSpec((HEAD_PAIRS, DEC_SEQ, LANES), lambda b, r: (0, b, 0), pipeline_mode=pl.Buffered(1)),
            pl.BlockSpec((HEAD_PAIRS, DEC_SEQ, LANES), lambda b, r: (0, b, 0), pipeline_mode=pl.Buffered(1)),
            pl.BlockSpec((1, None, HEAD_PAIRS, PAST_LEN, LANES), lambda b, r: (b, l, 0, 0, 0)),
            pl.BlockSpec((1, None, HEAD_PAIRS, PAST_LEN, LANES), lambda b, r: (b, l, 0, 0, 0)),
            _layer_spec(tb, l),
            _const_spec(rm.shape),
        ],
        out_specs=pl.BlockSpec((tile_tok, NA_WIDTH), lambda b, r: (b * n_rt + r, 0)),
        out_shape=jax.ShapeDtypeStruct((n_batch * DEC_SEQ, NA_WIDTH), BF16),
        scratch_shapes=[pltpu.VMEM((2 * tile_tok, LANES), BF16), pltpu.VMEM((HEAD_PAIRS, tile_tok, LANES), F32),
                        pltpu.VMEM((NA_HEADS, NA_COL_BLOCKS, NA_Q_ROWS * NA_COLS, NA_K_ROWS * NA_KCOLS), F32)],
        compiler_params=_cparams(("arbitrary", "arbitrary")),
        name="na_attn",
    )(q, k, v, kc, vc, tb, rm)


def _na_tables(rpb):
    n = np.arange(NA_COL_BLOCKS)
    cc = np.arange(NA_COLS)
    kk = np.arange(NA_KCOLS)
    rl = np.arange(NA_Q_ROWS)
    ki = np.arange(NA_K_ROWS)
    n_dr, n_dc = 2 * NA_MAX_ROWS - 1, 2 * NA_COLS - 1
    c = n[:, None] * NA_COLS + cc[None, :]
    kcol = np.asarray(NA_KC0)[:, None] + kk[None, :]
    dc = np.clip(kcol[:, None, :] - c[:, :, None], -(NA_COLS - 1), NA_COLS - 1) + (NA_COLS - 1)
    cs = np.clip(c - NA_COLS // 2, 0, GRID_W - NA_COLS)
    col_ok = (kcol[:, None, :] >= cs[:, :, None]) & (kcol[:, None, :] < cs[:, :, None] + NA_COLS)
    half = (NA_K_ROWS - NA_Q_ROWS) // 2
    dr = np.clip(ki[None, :] - half - rl[:, None] + (NA_MAX_ROWS - 1), 0, n_dr - 1)
    oh_dc = jnp.asarray(dc[None] == np.arange(n_dc)[:, None, None, None], F32)
    cols = jnp.einsum('lhab,bnck->lhncak', rpb.astype(F32), oh_dc, precision='highest')
    cols = jnp.where(jnp.asarray(col_ok)[None, None, :, :, None, :], cols, NEG_INF)
    pad_hi = NA_OFF_ROWS - NA_OFF_PAD - n_dr
    colsp = jnp.concatenate([cols[..., :1, :]] * NA_OFF_PAD + [cols] + [cols[..., -1:, :]] * pad_hi, axis=4)
    assert np.array_equal(dr, np.clip(ki[None] + (NA_OFF_START - rl)[:, None] - NA_OFF_PAD, 0, n_dr - 1))
    tb = colsp.reshape(-1, NA_HEADS, NA_COL_BLOCKS, NA_COLS, NA_OFF_ROWS * NA_KCOLS)
    rows = GRID_W
    masks = []
    for r0 in (0, NA_Q_ROWS, rows - NA_Q_ROWS):
        r = r0 + rl
        rs = np.clip(r - NA_MAX_ROWS // 2, 0, rows - NA_MAX_ROWS)
        kr = r0 - half + ki
        ok = (kr[None, :] >= rs[:, None]) & (kr[None, :] < rs[:, None] + NA_MAX_ROWS)
        full = np.broadcast_to(ok[:, None, :, None], (NA_Q_ROWS, NA_COLS, NA_K_ROWS, NA_KCOLS))
        masks.append(np.where(full, 0.0, NEG_INF).reshape(NA_Q_ROWS * NA_COLS, NA_K_ROWS * NA_KCOLS))
    rm = jnp.asarray(np.stack(masks), F32)
    return tb, rm


def _outproj_kernel(x_ref, mod_ref, y_ref, na_ref, gm_ref, gw_ref, gb_ref, w_ref, o_ref, yr_ref, ys_ref, cat_ref):
    gate = mod_ref[0, 5:6, :]
    n_rows = MIX_TILE // SSM_CHUNK
    for g in range(SSM_GROUPS):
        yr_ref[g] = _roll_lanes(y_ref[g], SSM_CH * g)
    for t in range(SSM_CHUNK):
        yt = _merge_segments(lambda src, lt: yr_ref[src, :, lt * LANES:(lt + 1) * LANES], t)
        yt = _roll_lanes(yt, -SSM_CH * t)
        for j in range(SSM_WIDTH // LANES):
            ys_ref[j, pl.ds(t, n_rows, stride=SSM_CHUNK), :] = yt[:, j * LANES:(j + 1) * LANES]
    y = _gelu(jnp.concatenate([ys_ref[j] for j in range(SSM_WIDTH // LANES)], axis=-1))
    z = jnp.dot(y.astype(BF16), gw_ref[...], preferred_element_type=F32) + gb_ref[...]
    cat_ref[:, 0:SSM_WIDTH] = (y * _sigmoid(z)).astype(BF16)
    cat_ref[:, SSM_WIDTH:SSM_WIDTH + NA_WIDTH] = na_ref[...]
    cat_ref[:, SSM_WIDTH + NA_WIDTH:] = gm_ref[...]
    mix = jnp.dot(cat_ref[...], w_ref[...], preferred_element_type=F32)
    o_ref[...] = x_ref[...] + gate * mix


def _outproj(x, mod, y, na, gm, gw, gb, w, *, l, seg0, seg_tokens):
    n_tok = x.shape[0]
    n_rows = MIX_TILE // SSM_CHUNK
    tok = lambda width: pl.BlockSpec((MIX_TILE, width), lambda i: (i, 0))
    return pl.pallas_call(
        _outproj_kernel,
        grid=(n_tok // MIX_TILE,),
        in_specs=[
            tok(D_MODEL),
            _mod_spec(l, seg0, seg_tokens, MIX_TILE),
            pl.BlockSpec((SSM_GROUPS, n_rows, SSM_ROW), lambda i: (0, i, 0)),
            tok(NA_WIDTH), tok(GM_WIDTH),
            _layer_spec(gw, l), _layer_spec(gb, l), _layer_spec(w, l),
        ],
        out_specs=tok(D_MODEL),
        out_shape=jax.ShapeDtypeStruct((n_tok, D_MODEL), F32),
        scratch_shapes=[pltpu.VMEM((SSM_GROUPS, n_rows, SSM_ROW), F32),
                        pltpu.VMEM((SSM_WIDTH // LANES, MIX_TILE, LANES), F32),
                        pltpu.VMEM((MIX_TILE, D_MODEL), BF16)],
        compiler_params=_cparams(("arbitrary",)),
        name="outproj",
    )(x, mod, y, na, gm, gw, gb, w)


def _prep(p):
    heads = np.arange(NA_WIDTH) // NA_HEAD_DIM
    hsum = jnp.asarray((heads[:, None] == heads[None, :]) / NA_HEAD_DIM, BF16)
    gmb = jnp.repeat(jnp.swapaxes(p['gm_bs'], 1, 2).astype(F32), GM_WIDTH // GM_GROUPS, axis=2)
    tb, rm = _na_tables(p['na_rpb'])
    row = lambda a: a.astype(F32).reshape(DEPTH, 1, -1)
    return dict(
        g1=row(p['norm_ffn1']), g2=row(p['norm_mix']), g3=row(p['norm_ffn2']),
        f1_in=p['ffn1_w_in'].astype(BF16), f1_out=p['ffn1_w_out'].astype(BF16),
        f2_in=p['ffn2_w_in'].astype(BF16), f2_out=p['ffn2_w_out'].astype(BF16),
        w_in=p['w_in'].astype(BF16), w_out=p['w_out'].astype(BF16),
        hsum=hsum,
        qg=row(jnp.tile(p['na_q_norm'], (1, NA_HEADS))), kg=row(jnp.tile(p['na_k_norm'], (1, NA_HEADS))),
        ws=p['gm_ws'].astype(BF16), gmb=gmb,
        ssm=_ssm_params(p),
        glu_w=p['ssm_glu_w'].astype(BF16), glu_b=row(p['ssm_glu_b']),
        tb=tb, rm=rm,
    )


def _pairs_major(t):
    b, n_l, s = t.shape[:3]
    return jnp.transpose(t.reshape(b, n_l, s, HEAD_PAIRS, LANES), (0, 1, 3, 2, 4)).astype(BF16)


def _trunk_layer(x, mod, w, l, *, n_batch, seq_len, caches=None, is_ctx=False, ctx_kv=None, ssm_init=None):
    seg = dict(l=l, seg0=0 if is_ctx else 1, seg_tokens=x.shape[0] if is_ctx else seq_len)
    x = _ffn(x, mod, w['g1'], w['f1_in'], w['f1_out'], mod_base=0, **seg)
    outs = _inproj(x, mod, w['g2'], w['w_in'], w['hsum'], w['qg'], w['kg'], w['ws'], w['gmb'],
                   layer=l if is_ctx else None, caches=caches, **seg)
    u, q, k, v, gm = outs[:5]
    s0 = jnp.zeros((SSM_GROUPS, 4, n_batch, LANES), F32) if is_ctx else ssm_init[l]
    y, fin = _ssm_core(u, *w['ssm'], s0, l=l, n_batch=n_batch, n_chunks=seq_len // SSM_CHUNK)
    if is_ctx:
        na = _ctx_attn(q, k, v, n_batch)
    else:
        na = _na_attn(q, k, v, ctx_kv[0], ctx_kv[1], w['tb'], w['rm'], n_batch, l=l)
    x = _outproj(x, mod, y, na, gm, w['glu_w'], w['glu_b'], w['w_out'], **seg)
    x = _ffn(x, mod, w['g3'], w['f2_in'], w['f2_out'], mod_base=6, **seg)
    return x, (tuple(outs[5:]), fin)


def kernel(x_prompt, x_sample, c, cache_k, cache_v, state_ssm, c_ctx, w_ada, b_ada, norm_ffn1, ffn1_w_in, ffn1_w_out, norm_mix, w_in, w_out, ssm_lambda_re, ssm_lambda_im, ssm_log_dt, ssm_b_re, ssm_b_im, ssm_c_re, ssm_c_im, ssm_d, ssm_glu_w, ssm_glu_b, na_q_norm, na_k_norm, na_rpb, gm_ws, gm_bs, norm_ffn2, ffn2_w_in, ffn2_w_out):
    p = dict(norm_ffn1=norm_ffn1, ffn1_w_in=ffn1_w_in, ffn1_w_out=ffn1_w_out, norm_mix=norm_mix, w_in=w_in,
             w_out=w_out, ssm_lambda_re=ssm_lambda_re, ssm_lambda_im=ssm_lambda_im, ssm_log_dt=ssm_log_dt,
             ssm_b_re=ssm_b_re, ssm_b_im=ssm_b_im, ssm_c_re=ssm_c_re, ssm_c_im=ssm_c_im, ssm_d=ssm_d,
             ssm_glu_w=ssm_glu_w, ssm_glu_b=ssm_glu_b, na_q_norm=na_q_norm, na_k_norm=na_k_norm, na_rpb=na_rpb,
             gm_ws=gm_ws, gm_bs=gm_bs, norm_ffn2=norm_ffn2, ffn2_w_in=ffn2_w_in, ffn2_w_out=ffn2_w_out)
    batch, dec_batch = x_prompt.shape[0], x_sample.shape[0]
    cond8 = jnp.zeros((8, D_MODEL), F32).at[0].set(c_ctx).at[1:1 + dec_batch].set(c)
    mod = _adaln(cond8, w_ada, b_ada).reshape(DEPTH, 8, N_MOD, D_MODEL)

    xp = x_prompt.reshape(batch * SEQ, D_MODEL)
    xs = x_sample.reshape(dec_batch * DEC_SEQ, D_MODEL)
    w = _prep(p)
    ctx_kv = (_pairs_major(cache_k), _pairs_major(cache_v))
    ssm_init = _ssm_states_in(state_ssm)
    caches, fins = None, []
    for l in range(DEPTH):
        xp, (caches, fin) = _trunk_layer(xp, mod, w, l, n_batch=batch, seq_len=SEQ, caches=caches, is_ctx=True)
        fins.append(fin)
        xs, _ = _trunk_layer(xs, mod, w, l, n_batch=dec_batch, seq_len=DEC_SEQ, ctx_kv=ctx_kv, ssm_init=ssm_init)
    cache_shape = (batch, DEPTH, SEQ, NA_HEADS, NA_HEAD_DIM)
    return (xp.reshape(batch, SEQ, D_MODEL), xs.reshape(dec_batch, DEC_SEQ, D_MODEL),
            caches[0].reshape(cache_shape), caches[1].reshape(cache_shape), _ssm_states_out(fins))
```

```python
import functools

import numpy as np
import jax
import jax.numpy as jnp
from jax import lax
from jax.experimental import pallas as pl
from jax.experimental.pallas import tpu as pltpu

D_MODEL = 1024
DEPTH = 2
SEQ = 256
DEC_SEQ = 4096
PAST_LEN = 256
GRID_W = 64
SSM_WIDTH = 256
SSM_CH = 16
SSM_GROUPS = 16
SSM_STATE = 64
NA_WIDTH = 512
NA_HEAD_DIM = 64
NA_HEADS = 8
NA_MAX_ROWS = 8
NA_COLS = 16
GM_WIDTH = 256
GM_GROUPS = 4
GM_CHUNK = 128
D_FF = 2816
N_MOD = 9
RMS_EPS = 1e-6
LN_EPS = 1e-5
NEG_INF = -1e30

F32 = jnp.float32
BF16 = jnp.bfloat16

LANES = 128
SUBLANES = 8
MXU_DIM = 256
VMEM_LIMIT_BYTES = 56 * 1024 * 1024

FFN_TILE = 1024
MIX_TILE = 1024
FF_CHUNK = MXU_DIM
N_FF_CHUNKS = D_FF // FF_CHUNK
SSM_CHUNK = 16
SSM_ROW = SSM_CHUNK * SSM_CH
SSM_GROUP_BLOCK = 4
SSM_W_COLS = 4 * LANES
SCAN_SHIFTS = (1, 2, 4)
N_SCAN_CONST = 3 * len(SCAN_SHIFTS) + 3
HEAD_PAIRS = NA_HEADS // 2
NA_Q_ROWS = 8
NA_K_ROWS = 16
NA_KCOLS = 2 * NA_COLS
NA_COL_BLOCKS = GRID_W // NA_COLS
NA_KC0 = (0, 8, 24, 32)
SEGS_PER_TILE = LANES // SSM_CH
NA_OFF_PAD = (NA_Q_ROWS - 1) - ((NA_MAX_ROWS - 1) - (NA_K_ROWS - NA_Q_ROWS) // 2)
NA_OFF_START = NA_Q_ROWS - 1
NA_OFF_ROWS = NA_OFF_START + NA_K_ROWS


def _silu(x):
    return x * (1.0 / (1.0 + jnp.exp(-x)))


def _sigmoid(x):
    return 1.0 / (1.0 + jnp.exp(-x))


def _gelu(x):
    return 0.5 * x * (1.0 + jnp.tanh(0.7978845608028654 * (x + 0.044715 * (x * x * x))))


def _cparams(sem):
    return pltpu.CompilerParams(dimension_semantics=sem, vmem_limit_bytes=VMEM_LIMIT_BYTES)


def _const_spec(shape):
    nd = len(shape)
    return pl.BlockSpec(shape, lambda *_: (0,) * nd, pipeline_mode=pl.Buffered(1))


def _layer_spec(stacked, l):
    nd = stacked.ndim
    return pl.BlockSpec((None,) + stacked.shape[1:], lambda *_: (l,) + (0,) * (nd - 1), pipeline_mode=pl.Buffered(1))


def _ada_kernel(c_ref, w_ref, b_ref, o_ref):
    s = _silu(c_ref[...]).astype(BF16)
    w = w_ref[0].astype(BF16)
    o_ref[0] = jnp.dot(s, w, preferred_element_type=F32) + b_ref[0]


def _adaln(cond8, w_ada, b_ada):
    tn = D_MODEL
    ncol = N_MOD * D_MODEL
    return pl.pallas_call(
        _ada_kernel,
        grid=(DEPTH, ncol // tn),
        in_specs=[
            pl.BlockSpec((8, D_MODEL), lambda l, j: (0, 0)),
            pl.BlockSpec((1, D_MODEL, tn), lambda l, j: (l, 0, j)),
            pl.BlockSpec((1, 1, tn), lambda l, j: (l, 0, j)),
        ],
        out_specs=pl.BlockSpec((1, 8, tn), lambda l, j: (l, 0, j)),
        out_shape=jax.ShapeDtypeStruct((DEPTH, 8, ncol), F32),
        compiler_params=_cparams(("arbitrary", "arbitrary")),
        name="adaln",
    )(cond8, w_ada, b_ada.reshape(DEPTH, 1, ncol))


def _normed(x, g_row, shift, scale):
    ms = jnp.mean(x * x, axis=-1, keepdims=True)
    h = x * lax.rsqrt(ms + RMS_EPS) * g_row
    return h * (1.0 + scale) + shift


def _mod_spec(l, seg0, seg_tokens, tile):
    tiles_per_seg = seg_tokens // tile
    return pl.BlockSpec((None, 1, N_MOD, D_MODEL), lambda i: (l, seg0 + i // tiles_per_seg, 0, 0))


def _ffn_kernel(x_ref, mod_ref, g_ref, win_ref, wout_ref, o_ref, h_ref, act_ref, *, mod_base):
    x = x_ref[...]
    shift = mod_ref[0, mod_base:mod_base + 1, :]
    scale = mod_ref[0, mod_base + 1:mod_base + 2, :]
    gate = mod_ref[0, mod_base + 2:mod_base + 3, :]
    h_ref[...] = _normed(x, g_ref[...], shift, scale).astype(BF16)
    for j in range(N_FF_CHUNKS):
        cols = slice(j * FF_CHUNK, (j + 1) * FF_CHUNK)
        gate_j = jnp.dot(h_ref[...], win_ref[:, cols], preferred_element_type=F32)
        up_j = jnp.dot(h_ref[...], win_ref[:, D_FF + j * FF_CHUNK:D_FF + (j + 1) * FF_CHUNK], preferred_element_type=F32)
        act_ref[:, cols] = (_silu(gate_j) * up_j).astype(BF16)
    for n in range(D_MODEL // MXU_DIM):
        cols = slice(n * MXU_DIM, (n + 1) * MXU_DIM)
        y = jnp.dot(act_ref[...], wout_ref[:, cols], preferred_element_type=F32)
        o_ref[:, cols] = x_ref[:, cols] + 0.5 * gate[:, cols] * y


def _ffn(x, mod, g, w_in, w_out, *, l, mod_base, seg0, seg_tokens):
    n_tok = x.shape[0]
    return pl.pallas_call(
        functools.partial(_ffn_kernel, mod_base=mod_base),
        grid=(n_tok // FFN_TILE,),
        in_specs=[
            pl.BlockSpec((FFN_TILE, D_MODEL), lambda i: (i, 0)),
            _mod_spec(l, seg0, seg_tokens, FFN_TILE),
            _layer_spec(g, l), _layer_spec(w_in, l), _layer_spec(w_out, l),
        ],
        out_specs=pl.BlockSpec((FFN_TILE, D_MODEL), lambda i: (i, 0)),
        out_shape=jax.ShapeDtypeStruct((n_tok, D_MODEL), F32),
        scratch_shapes=[pltpu.VMEM((FFN_TILE, D_MODEL), BF16), pltpu.VMEM((FFN_TILE, D_FF), BF16)],
        compiler_params=_cparams(("arbitrary",)),
        name="ffn",
    )(x, mod, g, w_in, w_out)


def _segment_ids():
    return lax.broadcasted_iota(jnp.int32, (1, LANES), 1) // SSM_CH


def _merge_segments(load_piece, rot):
    seg = _segment_ids()
    tiles = []
    for lt in range(SSM_ROW // LANES):
        acc = None
        for jj in range(SEGS_PER_TILE):
            src = (lt * SEGS_PER_TILE + jj - rot) % SSM_CHUNK
            piece = load_piece(src, lt)
            acc = piece if acc is None else jnp.where(seg == jj, piece, acc)
        tiles.append(acc)
    return jnp.concatenate(tiles, axis=-1)


def _roll_lanes(x, shift):
    shift %= x.shape[-1]
    return pltpu.roll(x, shift, axis=x.ndim - 1) if shift else x


def _inproj_kernel(*refs, n_in, n_out):
    x_ref, mod_ref, g_ref, w_ref, hsum_ref, qg_ref, kg_ref, ws_ref, gmb_ref = refs[:9]
    u_ref, q_ref, k_ref, v_ref, gm_ref, *tok_refs = refs[n_in:n_in + n_out]
    h_ref, xs_ref, z_ref = refs[n_in + n_out:]
    shift = mod_ref[0, 3:4, :]
    scale = mod_ref[0, 4:5, :]
    h_ref[...] = _normed(x_ref[...], g_ref[...], shift, scale).astype(BF16)
    n_rows = MIX_TILE // SSM_CHUNK

    def proj(c0, width):
        return jnp.dot(h_ref[...], w_ref[:, c0:c0 + width], preferred_element_type=F32)

    xs = proj(0, SSM_WIDTH)
    for j in range(SSM_WIDTH // LANES):
        xs_ref[j] = xs[:, j * LANES:(j + 1) * LANES]
    for t in range(SSM_CHUNK):
        zt = jnp.concatenate([xs_ref[j, pl.ds(t, n_rows, stride=SSM_CHUNK), :] for j in range(SSM_WIDTH // LANES)],
                             axis=-1)
        z_ref[t] = _roll_lanes(zt, SSM_CH * t)
    for g in range(SSM_GROUPS):
        ug = _merge_segments(lambda src, lt: z_ref[src, :, lt * LANES:(lt + 1) * LANES], g)
        u_ref[g] = _roll_lanes(ug, -SSM_CH * g)

    def head_norm(t, gain_row):
        sq = (t * t).astype(BF16)
        ms = jnp.concatenate([jnp.dot(sq[:, c:c + MXU_DIM], hsum_ref[...], preferred_element_type=F32)
                              for c in range(0, NA_WIDTH, MXU_DIM)], axis=-1)
        return t * lax.rsqrt(ms + RMS_EPS) * gain_row

    c0 = SSM_WIDTH
    q = head_norm(proj(c0, NA_WIDTH), qg_ref[...]) * (NA_HEAD_DIM ** -0.5)
    k = head_norm(proj(c0 + NA_WIDTH, NA_WIDTH), kg_ref[...])
    v = proj(c0 + 2 * NA_WIDTH, NA_WIDTH)
    for p in range(HEAD_PAIRS):
        lanes = slice(p * LANES, (p + 1) * LANES)
        q_ref[p] = q[:, lanes].astype(q_ref.dtype)
        k_ref[p] = k[:, lanes].astype(k_ref.dtype)
        v_ref[p] = v[:, lanes].astype(v_ref.dtype)
    if tok_refs:
        for slot in range(tok_refs[0].shape[1]):
            tok_refs[0][:, slot] = k.reshape(MIX_TILE // SEQ, SEQ, NA_WIDTH)
            tok_refs[1][:, slot] = v.reshape(MIX_TILE // SEQ, SEQ, NA_WIDTH)

    a = _gelu(proj(c0 + 3 * NA_WIDTH, 2 * GM_WIDTH))
    u = a[:, :GM_WIDTH]
    vv = a[:, GM_WIDTH:]
    mu = jnp.mean(vv, axis=-1, keepdims=True)
    vc = vv - mu
    var = jnp.mean(vc * vc, axis=-1, keepdims=True)
    vln = (vc * lax.rsqrt(var + LN_EPS)).astype(BF16)
    lane = lax.broadcasted_iota(jnp.int32, (1, GM_WIDTH), 1)
    gw = GM_WIDTH // GM_GROUPS
    for ci in range(MIX_TILE // GM_CHUNK):
        rows = slice(ci * GM_CHUNK, (ci + 1) * GM_CHUNK)
        vch = vln[rows, :]
        sp = gmb_ref[...]
        for gi in range(GM_GROUPS):
            t = jnp.dot(ws_ref[gi], vch, preferred_element_type=F32)
            sp = sp + jnp.where((lane >= gi * gw) & (lane < (gi + 1) * gw), t, 0.0)
        gm_ref[rows, :] = (u[rows, :] * sp).astype(BF16)


def _inproj(x, mod, g, w, hsum, qg, kg, ws, gmb, *, l, seg0, seg_tokens, layer=None, caches=None):
    n_tok = x.shape[0]
    is_ctx = layer is not None
    kv_dtype = BF16 if is_ctx else F32
    n_rows = MIX_TILE // SSM_CHUNK
    pair_spec = pl.BlockSpec((HEAD_PAIRS, MIX_TILE, LANES), lambda i: (0, i, 0))
    out_specs = [
        pl.BlockSpec((SSM_GROUPS, n_rows, SSM_ROW), lambda i: (0, i, 0)),
        pair_spec, pair_spec, pair_spec,
        pl.BlockSpec((MIX_TILE, GM_WIDTH), lambda i: (i, 0)),
    ]
    out_shape = [
        jax.ShapeDtypeStruct((SSM_GROUPS, n_tok // SSM_CHUNK, SSM_ROW), F32),
        jax.ShapeDtypeStruct((HEAD_PAIRS, n_tok, LANES), BF16),
        jax.ShapeDtypeStruct((HEAD_PAIRS, n_tok, LANES), kv_dtype),
        jax.ShapeDtypeStruct((HEAD_PAIRS, n_tok, LANES), kv_dtype),
        jax.ShapeDtypeStruct((n_tok, GM_WIDTH), BF16),
    ]
    args = [x, mod, g, w, hsum, qg, kg, ws, gmb]
    in_specs = [
        pl.BlockSpec((MIX_TILE, D_MODEL), lambda i: (i, 0)),
        _mod_spec(l, seg0, seg_tokens, MIX_TILE),
        _layer_spec(g, l), _layer_spec(w, l),
        _const_spec((MXU_DIM, MXU_DIM)),
        _layer_spec(qg, l), _layer_spec(kg, l), _layer_spec(ws, l), _layer_spec(gmb, l),
    ]
    aliases = {}
    if is_ctx:
        bt = MIX_TILE // SEQ
        cache_shape = jax.ShapeDtypeStruct((n_tok // SEQ, DEPTH, SEQ, NA_WIDTH), F32)
        if caches is None:
            out_specs += [pl.BlockSpec((bt, DEPTH, SEQ, NA_WIDTH), lambda i: (i, 0, 0, 0))] * 2
        else:
            out_specs += [pl.BlockSpec((bt, 1, SEQ, NA_WIDTH), lambda i: (i, layer, 0, 0))] * 2
        out_shape += [cache_shape] * 2
        if caches is not None:
            aliases = {len(args): len(out_shape) - 2, len(args) + 1: len(out_shape) - 1}
            args += list(caches)
            in_specs += [pl.BlockSpec(memory_space=pl.ANY)] * 2
    return pl.pallas_call(
        functools.partial(_inproj_kernel, n_in=len(args), n_out=len(out_shape)),
        grid=(n_tok // MIX_TILE,),
        in_specs=in_specs,
        out_specs=out_specs,
        out_shape=out_shape,
        input_output_aliases=aliases,
        scratch_shapes=[pltpu.VMEM((MIX_TILE, D_MODEL), BF16), pltpu.VMEM((SSM_WIDTH // LANES, MIX_TILE, LANES), F32),
                        pltpu.VMEM((SSM_CHUNK, n_rows, SSM_ROW), F32)],
        compiler_params=_cparams(("arbitrary",)),
        name="inproj_ctx" if is_ctx else "inproj_lat",
    )(*args)


def _shift_rows(x, d, row, down):
    if down:
        return jnp.where(row >= d, pltpu.roll(x, d, axis=0), 0.0)
    return jnp.where(row < SUBLANES - d, pltpu.roll(x, SUBLANES - d, axis=0), 0.0)


def _scan_tile(w, wp, c, cp, cst, row, down):
    x, xp = w, wp
    for si, d in enumerate(SCAN_SHIFTS):
        a, b, bp = cst[3 * si], cst[3 * si + 1], cst[3 * si + 2]
        xs, xps = _shift_rows(x, d, row, down), _shift_rows(xp, d, row, down)
        x, xp = x + a * xs + b * xps, xp + a * xps + bp * xs
    a, b, bp = cst[N_SCAN_CONST - 3], cst[N_SCAN_CONST - 2], cst[N_SCAN_CONST - 1]
    s = x + a * c + b * cp
    sp = xp + a * cp + bp * c
    edge = 0 if down else SUBLANES - 1
    last = SUBLANES - 1 if down else 0
    seen = jnp.where(row == edge, c, _shift_rows(s, 1, row, down))
    c_new = jnp.broadcast_to(s[last:last + 1, :], s.shape)
    cp_new = jnp.broadcast_to(sp[last:last + 1, :], sp.shape)
    return seen, c_new, cp_new


def _ssm_kernel(u_ref, mp_ref, q_ref, cst_ref, dvec_ref, s0_ref, y_ref, fin_ref, w_ref, sq_ref, *, n_batch, n_chunks):
    gb = u_ref.shape[0]
    n_rows = n_batch * n_chunks
    row_blk = min(n_rows, 512)
    for g in range(gb):
        for r0 in range(0, n_rows, row_blk):
            rows = slice(r0, r0 + row_blk)
            ug = u_ref[g, rows, :]
            wy = jnp.dot(ug.astype(BF16), mp_ref[g], preferred_element_type=F32)
            y_ref[g, rows, :] = wy[:, :SSM_ROW] + ug * dvec_ref[g]
            w_ref[g, rows, :] = wy[:, SSM_ROW:]

    n_tiles = n_chunks // SUBLANES
    row = lax.broadcasted_iota(jnp.int32, (SUBLANES, LANES), 0)

    def tile_step(b, kt, carry):
        out = []
        for g in range(gb):
            cf, cfp, cb, cbp = carry[4 * g:4 * g + 4]
            rf = pl.multiple_of(b * n_chunks + kt * SUBLANES, SUBLANES)
            rb = pl.multiple_of(b * n_chunks + (n_tiles - 1 - kt) * SUBLANES, SUBLANES)
            wf = w_ref[g, pl.ds(rf, SUBLANES), :]
            wb = w_ref[g, pl.ds(rb, SUBLANES), :]
            cst_f = [cst_ref[g, 0, i] for i in range(N_SCAN_CONST)]
            cst_b = [cst_ref[g, 1, i] for i in range(N_SCAN_CONST)]
            seen_f, cf, cfp = _scan_tile(wf[:, 0:LANES], wf[:, LANES:2 * LANES], cf, cfp, cst_f, row, True)
            seen_b, cb, cbp = _scan_tile(wb[:, 2 * LANES:3 * LANES], wb[:, 3 * LANES:], cb, cbp, cst_b, row, False)
            sq_ref[g, pl.ds(rf, SUBLANES), 0:LANES] = seen_f
            sq_ref[g, pl.ds(rb, SUBLANES), LANES:2 * LANES] = seen_b
            out += [cf, cfp, cb, cbp]
        return tuple(out)

    def batch_body(b, _):
        carry = []
        for g in range(gb):
            for i in range(4):
                carry.append(jnp.broadcast_to(s0_ref[g, i, pl.ds(b, 1), :], (SUBLANES, LANES)))
        carry = tuple(carry)
        if n_tiles <= 2:
            for kt in range(n_tiles):
                carry = tile_step(b, kt, carry)
        else:
            carry = lax.fori_loop(0, n_tiles, lambda kt, c: tile_step(b, kt, c), carry)
        for g in range(gb):
            fin_ref[g, 0, pl.ds(b, 1), :] = carry[4 * g][0:1, :]
            fin_ref[g, 1, pl.ds(b, 1), :] = carry[4 * g + 2][0:1, :]
        return 0

    lax.fori_loop(0, n_batch, batch_body, 0)

    for g in range(gb):
        for r0 in range(0, n_rows, row_blk):
            rows = slice(r0, r0 + row_blk)
            y_ref[g, rows, :] += jnp.dot(sq_ref[g, rows, :].astype(BF16), q_ref[g], preferred_element_type=F32)


def _ssm_core(u, mp, q, cst, dvec, s0, *, l, n_batch, n_chunks):
    n_rows = n_batch * n_chunks
    gb = SSM_GROUP_BLOCK
    g3 = lambda i: (i, 0, 0)
    g4 = lambda i: (i, 0, 0, 0)
    return pl.pallas_call(
        functools.partial(_ssm_kernel, n_batch=n_batch, n_chunks=n_chunks),
        grid=(SSM_GROUPS // gb,),
        in_specs=[
            pl.BlockSpec((gb, n_rows, SSM_ROW), g3),
            pl.BlockSpec((None, gb, SSM_ROW, SSM_ROW + SSM_W_COLS), lambda i: (l, i, 0, 0)),
            pl.BlockSpec((None, gb, 2 * LANES, SSM_ROW), lambda i: (l, i, 0, 0)),
            pl.BlockSpec((None, gb, 2, N_SCAN_CONST, SUBLANES, LANES), lambda i: (l, i, 0, 0, 0, 0)),
            pl.BlockSpec((None, gb, 1, SSM_ROW), lambda i: (l, i, 0, 0)),
            pl.BlockSpec((gb, 4, n_batch, LANES), g4),
        ],
        out_specs=[
            pl.BlockSpec((gb, n_rows, SSM_ROW), g3),
            pl.BlockSpec((gb, 2, n_batch, LANES), g4),
        ],
        out_shape=[
            jax.ShapeDtypeStruct((SSM_GROUPS, n_rows, SSM_ROW), F32),
            jax.ShapeDtypeStruct((SSM_GROUPS, 2, n_batch, LANES), F32),
        ],
        scratch_shapes=[pltpu.VMEM((gb, n_rows, SSM_W_COLS), F32), pltpu.VMEM((gb, n_rows, 2 * LANES), F32)],
        compiler_params=_cparams(("arbitrary",)),
        name="ssm_core",
    )(u, mp, q, cst, dvec, s0)


def _cmul(a, b):
    return a[0] * b[0] - a[1] * b[1], a[0] * b[1] + a[1] * b[0]


_CHUNK_IDX = np.arange(SSM_CHUNK)


def _ssm_exponents():
    t = SSM_CHUNK
    idx = _CHUNK_IDX.astype(np.float32)
    both = lambda a: np.stack([a, a], axis=1)
    lane_step = idx - (t - 1) / 2
    tile_rows = np.arange(1, SUBLANES + 1, dtype=np.float32) * t
    tables = dict(
        one=both(np.ones(1, np.float32)),
        incr=np.stack([t - 1 - idx, idx], axis=1),
        seen=np.stack([idx + 1, t - idx], axis=1),
        resp_in=np.stack([-lane_step, lane_step], axis=1),
        resp_out=np.stack([lane_step, -lane_step], axis=1),
        shift=both(np.asarray(SCAN_SHIFTS, np.float32) * t),
        tile=np.stack([tile_rows, tile_rows[::-1]], axis=1),
    )
    spans, start = {}, 0
    for name, tab in tables.items():
        spans[name] = slice(start, start + len(tab))
        start += len(tab)
    return np.concatenate(list(tables.values()), axis=0), spans


def _ssm_params(p):
    t = SSM_CHUNK
    n_l = p['ssm_lambda_re'].shape[0]
    lam = (p['ssm_lambda_re'].astype(F32), p['ssm_lambda_im'].astype(F32))
    dt = jnp.exp(p['ssm_log_dt'].astype(F32))[..., None]
    arg = (lam[0] * dt, lam[1] * dt)
    expo, span = _ssm_exponents()
    n = jnp.asarray(expo)[:, None, :, None, None]
    mag = jnp.exp(n * arg[0][None])
    pw = (mag * jnp.cos(n * arg[1][None]), mag * jnp.sin(n * arg[1][None]))
    take = lambda name: [x[span[name]] for x in pw]
    lanes_last = lambda xs: [jnp.moveaxis(x, 0, -1) for x in xs]

    lbar = [x[0] for x in take('one')]
    num = (lbar[0] - 1.0, lbar[1])
    den = lam[0] * lam[0] + lam[1] * lam[1]
    zoh = ((num[0] * lam[0] + num[1] * lam[1]) / den, (num[1] * lam[0] - num[0] * lam[1]) / den)
    b_mat = (p['ssm_b_re'].astype(F32), p['ssm_b_im'].astype(F32))
    bbar = _cmul((zoh[0][..., None], zoh[1][..., None]), b_mat)
    c_mat = [jnp.swapaxes(p[k].astype(F32), -1, -2) for k in ('ssm_c_re', 'ssm_c_im')]
    ct = [jnp.tile(x, (1, 1, 1, 1, t)) for x in c_mat]
    bt = [jnp.tile(x, (1, 1, 1, 1, t)) for x in bbar]

    per_lane = lambda name: [jnp.repeat(x, SSM_CH, axis=-1) for x in lanes_last(take(name))]
    xm = _cmul(per_lane('resp_in'), bt)
    ym = _cmul(per_lane('resp_out'), ct)
    resp = jnp.einsum('ldgpx,ldgpy->ldgxy', jnp.concatenate([xm[0], -xm[1]], axis=3),
                      jnp.concatenate([ym[0], ym[1]], axis=3), precision='highest')
    steps_x = np.repeat(_CHUNK_IDX, SSM_CH)
    causal = np.stack([steps_x[None, :] >= steps_x[:, None], steps_x[:, None] >= steps_x[None, :]])
    mats = jnp.sum(jnp.where(jnp.asarray(causal)[None, :, None], resp, 0.0), axis=1)

    incr = _cmul([x[..., None] for x in take('incr')], [x[None] for x in bbar])
    incr = [jnp.transpose(x, (1, 2, 3, 0, 5, 4)) for x in incr]
    pcat = jnp.concatenate([incr[0], incr[1], incr[1], incr[0]], axis=-1)
    pcat = jnp.concatenate([pcat[:, 0], pcat[:, 1]], axis=-1).reshape(n_l, SSM_GROUPS, SSM_ROW, SSM_W_COLS)
    mp = jnp.concatenate([mats, pcat], axis=-1).astype(BF16)

    qm = _cmul(ct, per_lane('seen'))
    q = jnp.concatenate([qm[0][:, 0], -qm[1][:, 0], qm[0][:, 1], -qm[1][:, 1]], axis=2).astype(BF16)

    def forms(name, rows):
        pr, pi = [jnp.moveaxis(x, 0, 3) for x in take(name)]
        f = jnp.stack([jnp.concatenate([pr, pr], -1), jnp.concatenate([-pi, pi], -1),
                       jnp.concatenate([pi, -pi], -1)], axis=4)
        return f if rows else jnp.broadcast_to(
            f.reshape(n_l, 2, SSM_GROUPS, 3 * len(SCAN_SHIFTS), 1, LANES),
            (n_l, 2, SSM_GROUPS, 3 * len(SCAN_SHIFTS), SUBLANES, LANES))
    per_row = jnp.swapaxes(forms('tile', True), 3, 4)
    cst = jnp.swapaxes(jnp.concatenate([forms('shift', False), per_row], axis=3), 1, 2)
    dvec = jnp.tile(p['ssm_d'].astype(F32).reshape(n_l, SSM_GROUPS, 1, SSM_CH), (1, 1, 1, SSM_CHUNK))
    return mp, q, cst, dvec


def _ssm_states_in(state_ssm):
    st = jnp.transpose(state_ssm.astype(F32), (1, 3, 2, 0, 4, 5))
    re, im = st[..., 0], st[..., 1]
    ri, ir = jnp.concatenate([re, im], -1), jnp.concatenate([im, re], -1)
    return jnp.stack([ri[:, :, 0], ir[:, :, 0], ri[:, :, 1], ir[:, :, 1]], axis=2)


def _ssm_states_out(fins):
    fin = jnp.stack(fins, axis=0)
    fin = jnp.stack([fin[..., :SSM_STATE], fin[..., SSM_STATE:]], axis=-1)
    return jnp.transpose(fin, (3, 0, 2, 1, 4, 5))


def _head_lane_mask(hh):
    lane = lax.broadcasted_iota(jnp.int32, (1, LANES), 1)
    return (lane >= hh * NA_HEAD_DIM) & (lane < (hh + 1) * NA_HEAD_DIM)


_NT_DIMS = (((1,), (1,)), ((), ()))


def _ctx_attn_kernel(q_ref, k_ref, v_ref, o_ref):
    for p in range(HEAD_PAIRS):
        qp, kp, vp = q_ref[p], k_ref[p], v_ref[p]
        lm0 = _head_lane_mask(0)
        zero = jnp.zeros_like(qp)
        q2 = jnp.concatenate([jnp.where(lm0, qp, zero), jnp.where(lm0, zero, qp)], axis=0)
        s = lax.dot_general(q2, kp, _NT_DIMS, preferred_element_type=F32)
        e = jnp.exp(s - jnp.max(s, axis=-1, keepdims=True))
        o = jnp.dot(e.astype(BF16), vp, preferred_element_type=F32) / jnp.sum(e, axis=-1, keepdims=True)
        o_ref[:, p * LANES:(p + 1) * LANES] = jnp.where(lm0, o[:SEQ, :], o[SEQ:, :]).astype(BF16)


def _ctx_attn(q, k, v, n_batch):
    spec = pl.BlockSpec((HEAD_PAIRS, SEQ, LANES), lambda b: (0, b, 0))
    return pl.pallas_call(
        _ctx_attn_kernel,
        grid=(n_batch,),
        in_specs=[spec, spec, spec],
        out_specs=pl.BlockSpec((SEQ, NA_WIDTH), lambda b: (b, 0)),
        out_shape=jax.ShapeDtypeStruct((n_batch * SEQ, NA_WIDTH), BF16),
        compiler_params=_cparams(("arbitrary",)),
        name="ctx_attn",
    )(q, k, v)


def _na_kernel(q_ref, k_ref, v_ref, kc_ref, vc_ref, t2_ref, rm_ref, o_ref, qb_ref, os_ref, tb_ref):
    rt = pl.program_id(1)
    n_rt = pl.num_programs(1)

    @pl.when((pl.program_id(0) == 0) & (rt == 0))
    def _():
        for h in range(NA_HEADS):
            for n in range(NA_COL_BLOCKS):
                for rl in range(NA_Q_ROWS):
                    lane0 = (NA_OFF_START - rl) * NA_KCOLS
                    tb_ref[h, n, rl * NA_COLS:(rl + 1) * NA_COLS, :] = (
                        t2_ref[h, n, :, lane0:lane0 + NA_K_ROWS * NA_KCOLS])

    r0 = rt * NA_Q_ROWS
    ttype = jnp.where(rt == 0, 0, jnp.where(rt == n_rt - 1, 2, 1))
    rmask = rm_ref[ttype]
    blk_q = NA_Q_ROWS * NA_COLS

    def pair_body(p, carry):
        lm0 = _head_lane_mask(0)
        for n in range(NA_COL_BLOCKS):
            for rl in range(NA_Q_ROWS):
                qrow = q_ref[p, rl * GRID_W + n * NA_COLS:rl * GRID_W + (n + 1) * NA_COLS, :]
                zero = jnp.zeros_like(qrow)
                base = 2 * n * blk_q + rl * NA_COLS
                qb_ref[base:base + NA_COLS, :] = jnp.where(lm0, qrow, zero)
                qb_ref[base + blk_q:base + blk_q + NA_COLS, :] = jnp.where(lm0, zero, qrow)
        kcp, vcp = kc_ref[0, p], vc_ref[0, p]
        sc_all = lax.dot_general(qb_ref[...], kcp, _NT_DIMS, preferred_element_type=F32)
        for n in range(NA_COL_BLOCKS):
            kparts, vparts = [], []
            for i in range(NA_K_ROWS):
                kr = jnp.clip(r0 - (NA_K_ROWS - NA_Q_ROWS) // 2 + i, 0, GRID_W - 1)
                start = pl.multiple_of(kr * GRID_W + NA_KC0[n], 8)
                kparts.append(k_ref[p, pl.ds(start, NA_KCOLS), :])
                vparts.append(v_ref[p, pl.ds(start, NA_KCOLS), :])
            kblk = jnp.concatenate(kparts, axis=0).astype(BF16)
            vblk = jnp.concatenate(vparts, axis=0).astype(BF16)
            rows2 = slice(2 * n * blk_q, 2 * (n + 1) * blk_q)
            s = lax.dot_general(qb_ref[rows2, :], kblk, _NT_DIMS, preferred_element_type=F32)
            s = s + jnp.concatenate([tb_ref[2 * p, n] + rmask, tb_ref[2 * p + 1, n] + rmask], axis=0)
            sc = sc_all[rows2, :]
            m = jnp.maximum(jnp.max(s, axis=-1, keepdims=True), jnp.max(sc, axis=-1, keepdims=True))
            e = jnp.exp(s - m)
            ec = jnp.exp(sc - m)
            den = jnp.sum(e, axis=-1, keepdims=True) + jnp.sum(ec, axis=-1, keepdims=True)
            o = (jnp.dot(e.astype(BF16), vblk, preferred_element_type=F32)
                 + jnp.dot(ec.astype(BF16), vcp, preferred_element_type=F32)) / den
            oblk = jnp.where(lm0, o[:blk_q, :], o[blk_q:, :])
            for rl in range(NA_Q_ROWS):
                os_ref[p, rl * GRID_W + n * NA_COLS:rl * GRID_W + (n + 1) * NA_COLS, :] = (
                    oblk[rl * NA_COLS:(rl + 1) * NA_COLS, :])
        return carry

    lax.fori_loop(0, HEAD_PAIRS, pair_body, 0)
    for p in range(HEAD_PAIRS):
        o_ref[:, p * LANES:(p + 1) * LANES] = os_ref[p].astype(BF16)


def _na_attn(q, k, v, kc, vc, tb, rm, n_batch, *, l):
    tile_tok = NA_Q_ROWS * GRID_W
    n_rt = DEC_SEQ // tile_tok
    return pl.pallas_call(
        _na_kernel,
        grid=(n_batch, n_rt),
        in_specs=[
            pl.BlockSpec((HEAD_PAIRS, tile_tok, LANES), lambda b, r: (0, b * n_rt + r, 0)),
            pl.BlockSpec((HEAD_PAIRS, DEC_SEQ, LANES), lambda b, r: (0, b, 0), pipeline_mode=pl.Buffered(1)),
            pl.BlockSpec((HEAD_PAIRS, DEC_SEQ, LANES), lambda b, r: (0, b, 0), pipeline_mode=pl.Buffered(1)),
            pl.BlockSpec((1, None, HEAD_PAIRS, PAST_LEN, LANES), lambda b, r: (b, l, 0, 0, 0)),
            pl.BlockSpec((1, None, HEAD_PAIRS, PAST_LEN, LANES), lambda b, r: (b, l, 0, 0, 0)),
            _layer_spec(tb, l),
            _const_spec(rm.shape),
        ],
        out_specs=pl.BlockSpec((tile_tok, NA_WIDTH), lambda b, r: (b * n_rt + r, 0)),
        out_shape=jax.ShapeDtypeStruct((n_batch * DEC_SEQ, NA_WIDTH), BF16),
        scratch_shapes=[pltpu.VMEM((2 * tile_tok, LANES), BF16), pltpu.VMEM((HEAD_PAIRS, tile_tok, LANES), F32),
                        pltpu.VMEM((NA_HEADS, NA_COL_BLOCKS, NA_Q_ROWS * NA_COLS, NA_K_ROWS * NA_KCOLS), F32)],
        compiler_params=_cparams(("arbitrary", "arbitrary")),
        name="na_attn",
    )(q, k, v, kc, vc, tb, rm)


def _na_tables(rpb):
    n = np.arange(NA_COL_BLOCKS)
    cc = np.arange(NA_COLS)
    kk = np.arange(NA_KCOLS)
    rl = np.arange(NA_Q_ROWS)
    ki = np.arange(NA_K_ROWS)
    n_dr, n_dc = 2 * NA_MAX_ROWS - 1, 2 * NA_COLS - 1
    c = n[:, None] * NA_COLS + cc[None, :]
    kcol = np.asarray(NA_KC0)[:, None] + kk[None, :]
    dc = np.clip(kcol[:, None, :] - c[:, :, None], -(NA_COLS - 1), NA_COLS - 1) + (NA_COLS - 1)
    cs = np.clip(c - NA_COLS // 2, 0, GRID_W - NA_COLS)
    col_ok = (kcol[:, None, :] >= cs[:, :, None]) & (kcol[:, None, :] < cs[:, :, None] + NA_COLS)
    half = (NA_K_ROWS - NA_Q_ROWS) // 2
    dr = np.clip(ki[None, :] - half - rl[:, None] + (NA_MAX_ROWS - 1), 0, n_dr - 1)
    oh_dc = jnp.asarray(dc[None] == np.arange(n_dc)[:, None, None, None], F32)
    off = np.clip(np.arange(NA_OFF_ROWS) - NA_OFF_PAD, 0, n_dr - 1)
    assert np.array_equal(dr, off[ki[None] + (NA_OFF_START - rl)[:, None]])
    oh_off = jnp.asarray(off[None, :] == np.arange(n_dr)[:, None], F32)
    rows_p = jnp.einsum('lhab,aA->lhAb', rpb.astype(F32), oh_off, precision='highest')
    cols = jnp.einsum('lhAb,bnck->lhncAk', rows_p, oh_dc, precision='highest')
    cols = jnp.where(jnp.asarray(col_ok)[None, None, :, :, None, :], cols, NEG_INF)
    tb = cols.reshape(-1, NA_HEADS, NA_COL_BLOCKS, NA_COLS, NA_OFF_ROWS * NA_KCOLS)
    rows = GRID_W
    masks = []
    for r0 in (0, NA_Q_ROWS, rows - NA_Q_ROWS):
        r = r0 + rl
        rs = np.clip(r - NA_MAX_ROWS // 2, 0, rows - NA_MAX_ROWS)
        kr = r0 - half + ki
        ok = (kr[None, :] >= rs[:, None]) & (kr[None, :] < rs[:, None] + NA_MAX_ROWS)
        full = np.broadcast_to(ok[:, None, :, None], (NA_Q_ROWS, NA_COLS, NA_K_ROWS, NA_KCOLS))
        masks.append(np.where(full, 0.0, NEG_INF).reshape(NA_Q_ROWS * NA_COLS, NA_K_ROWS * NA_KCOLS))
    rm = jnp.asarray(np.stack(masks), F32)
    return tb, rm


def _outproj_kernel(x_ref, mod_ref, y_ref, na_ref, gm_ref, gw_ref, gb_ref, w_ref, o_ref, yr_ref, ys_ref, cat_ref):
    gate = mod_ref[0, 5:6, :]
    n_rows = MIX_TILE // SSM_CHUNK
    for g in range(SSM_GROUPS):
        yr_ref[g] = _roll_lanes(y_ref[g], SSM_CH * g)
    for t in range(SSM_CHUNK):
        yt = _merge_segments(lambda src, lt: yr_ref[src, :, lt * LANES:(lt + 1) * LANES], t)
        yt = _roll_lanes(yt, -SSM_CH * t)
        for j in range(SSM_WIDTH // LANES):
            ys_ref[j, pl.ds(t, n_rows, stride=SSM_CHUNK), :] = yt[:, j * LANES:(j + 1) * LANES]
    y = _gelu(jnp.concatenate([ys_ref[j] for j in range(SSM_WIDTH // LANES)], axis=-1))
    z = jnp.dot(y.astype(BF16), gw_ref[...], preferred_element_type=F32) + gb_ref[...]
    cat_ref[:, 0:SSM_WIDTH] = (y * _sigmoid(z)).astype(BF16)
    cat_ref[:, SSM_WIDTH:SSM_WIDTH + NA_WIDTH] = na_ref[...]
    cat_ref[:, SSM_WIDTH + NA_WIDTH:] = gm_ref[...]
    mix = jnp.dot(cat_ref[...], w_ref[...], preferred_element_type=F32)
    o_ref[...] = x_ref[...] + gate * mix


def _outproj(x, mod, y, na, gm, gw, gb, w, *, l, seg0, seg_tokens):
    n_tok = x.shape[0]
    n_rows = MIX_TILE // SSM_CHUNK
    tok = lambda width: pl.BlockSpec((MIX_TILE, width), lambda i: (i, 0))
    return pl.pallas_call(
        _outproj_kernel,
        grid=(n_tok // MIX_TILE,),
        in_specs=[
            tok(D_MODEL),
            _mod_spec(l, seg0, seg_tokens, MIX_TILE),
            pl.BlockSpec((SSM_GROUPS, n_rows, SSM_ROW), lambda i: (0, i, 0)),
            tok(NA_WIDTH), tok(GM_WIDTH),
            _layer_spec(gw, l), _layer_spec(gb, l), _layer_spec(w, l),
        ],
        out_specs=tok(D_MODEL),
        out_shape=jax.ShapeDtypeStruct((n_tok, D_MODEL), F32),
        scratch_shapes=[pltpu.VMEM((SSM_GROUPS, n_rows, SSM_ROW), F32),
                        pltpu.VMEM((SSM_WIDTH // LANES, MIX_TILE, LANES), F32),
                        pltpu.VMEM((MIX_TILE, D_MODEL), BF16)],
        compiler_params=_cparams(("arbitrary",)),
        name="outproj",
    )(x, mod, y, na, gm, gw, gb, w)


def _prep(p):
    heads = np.arange(MXU_DIM) // NA_HEAD_DIM
    hsum = jnp.asarray((heads[:, None] == heads[None, :]) / NA_HEAD_DIM, BF16)
    gmb = jnp.repeat(jnp.swapaxes(p['gm_bs'], 1, 2).astype(F32), GM_WIDTH // GM_GROUPS, axis=2)
    tb, rm = _na_tables(p['na_rpb'])
    row = lambda a: a.astype(F32).reshape(DEPTH, 1, -1)
    return dict(
        g1=row(p['norm_ffn1']), g2=row(p['norm_mix']), g3=row(p['norm_ffn2']),
        f1_in=p['ffn1_w_in'].astype(BF16), f1_out=p['ffn1_w_out'].astype(BF16),
        f2_in=p['ffn2_w_in'].astype(BF16), f2_out=p['ffn2_w_out'].astype(BF16),
        w_in=p['w_in'].astype(BF16), w_out=p['w_out'].astype(BF16),
        hsum=hsum,
        qg=row(jnp.tile(p['na_q_norm'], (1, NA_HEADS))), kg=row(jnp.tile(p['na_k_norm'], (1, NA_HEADS))),
        ws=p['gm_ws'].astype(BF16), gmb=gmb,
        ssm=_ssm_params(p),
        glu_w=p['ssm_glu_w'].astype(BF16), glu_b=row(p['ssm_glu_b']),
        tb=tb, rm=rm,
    )


def _pairs_major(t):
    b, n_l, s = t.shape[:3]
    return jnp.transpose(t.reshape(b, n_l, s, HEAD_PAIRS, LANES), (0, 1, 3, 2, 4)).astype(BF16)


def _trunk_layer(x, mod, w, l, *, n_batch, seq_len, caches=None, is_ctx=False, ctx_kv=None, ssm_init=None):
    seg = dict(l=l, seg0=0 if is_ctx else 1, seg_tokens=x.shape[0] if is_ctx else seq_len)
    x = _ffn(x, mod, w['g1'], w['f1_in'], w['f1_out'], mod_base=0, **seg)
    outs = _inproj(x, mod, w['g2'], w['w_in'], w['hsum'], w['qg'], w['kg'], w['ws'], w['gmb'],
                   layer=l if is_ctx else None, caches=caches, **seg)
    u, q, k, v, gm = outs[:5]
    s0 = jnp.zeros((SSM_GROUPS, 4, n_batch, LANES), F32) if is_ctx else ssm_init[l]
    y, fin = _ssm_core(u, *w['ssm'], s0, l=l, n_batch=n_batch, n_chunks=seq_len // SSM_CHUNK)
    if is_ctx:
        na = _ctx_attn(q, k, v, n_batch)
    else:
        na = _na_attn(q, k, v, ctx_kv[0], ctx_kv[1], w['tb'], w['rm'], n_batch, l=l)
    x = _outproj(x, mod, y, na, gm, w['glu_w'], w['glu_b'], w['w_out'], **seg)
    x = _ffn(x, mod, w['g3'], w['f2_in'], w['f2_out'], mod_base=6, **seg)
    return x, (tuple(outs[5:]), fin)


def kernel(x_prompt, x_sample, c, cache_k, cache_v, state_ssm, c_ctx, w_ada, b_ada, norm_ffn1, ffn1_w_in, ffn1_w_out, norm_mix, w_in, w_out, ssm_lambda_re, ssm_lambda_im, ssm_log_dt, ssm_b_re, ssm_b_im, ssm_c_re, ssm_c_im, ssm_d, ssm_glu_w, ssm_glu_b, na_q_norm, na_k_norm, na_rpb, gm_ws, gm_bs, norm_ffn2, ffn2_w_in, ffn2_w_out):
    p = dict(norm_ffn1=norm_ffn1, ffn1_w_in=ffn1_w_in, ffn1_w_out=ffn1_w_out, norm_mix=norm_mix, w_in=w_in,
             w_out=w_out, ssm_lambda_re=ssm_lambda_re, ssm_lambda_im=ssm_lambda_im, ssm_log_dt=ssm_log_dt,
             ssm_b_re=ssm_b_re, ssm_b_im=ssm_b_im, ssm_c_re=ssm_c_re, ssm_c_im=ssm_c_im, ssm_d=ssm_d,
             ssm_glu_w=ssm_glu_w, ssm_glu_b=ssm_glu_b, na_q_norm=na_q_norm, na_k_norm=na_k_norm, na_rpb=na_rpb,
             gm_ws=gm_ws, gm_bs=gm_bs, norm_ffn2=norm_ffn2, ffn2_w_in=ffn2_w_in, ffn2_w_out=ffn2_w_out)
    batch, dec_batch = x_prompt.shape[0], x_sample.shape[0]
    cond8 = jnp.zeros((8, D_MODEL), F32).at[0].set(c_ctx).at[1:1 + dec_batch].set(c)
    mod = _adaln(cond8, w_ada, b_ada).reshape(DEPTH, 8, N_MOD, D_MODEL)

    xp = x_prompt.reshape(batch * SEQ, D_MODEL)
    xs = x_sample.reshape(dec_batch * DEC_SEQ, D_MODEL)
    w = _prep(p)
    ctx_kv = (_pairs_major(cache_k), _pairs_major(cache_v))
    ssm_init = _ssm_states_in(state_ssm)
    caches, fins = None, []
    for l in range(DEPTH):
        xp, (caches, fin) = _trunk_layer(xp, mod, w, l, n_batch=batch, seq_len=SEQ, caches=caches, is_ctx=True)
        fins.append(fin)
        xs, _ = _trunk_layer(xs, mod, w, l, n_batch=dec_batch, seq_len=DEC_SEQ, ctx_kv=ctx_kv, ssm_init=ssm_init)
    cache_shape = (batch, DEPTH, SEQ, NA_HEADS, NA_HEAD_DIM)
    return (xp.reshape(batch, SEQ, D_MODEL), xs.reshape(dec_batch, DEC_SEQ, D_MODEL),
            caches[0].reshape(cache_shape), caches[1].reshape(cache_shape), _ssm_states_out(fins))
```

```python
import functools

import numpy as np
import jax
import jax.numpy as jnp
from jax import lax
from jax.experimental import pallas as pl
from jax.experimental.pallas import tpu as pltpu

D_MODEL = 1024
DEPTH = 2
SEQ = 256
DEC_SEQ = 4096
PAST_LEN = 256
GRID_W = 64
SSM_WIDTH = 256
SSM_CH = 16
SSM_GROUPS = 16
SSM_STATE = 64
NA_WIDTH = 512
NA_HEAD_DIM = 64
NA_HEADS = 8
NA_MAX_ROWS = 8
NA_COLS = 16
GM_WIDTH = 256
GM_GROUPS = 4
GM_CHUNK = 128
D_FF = 2816
N_MOD = 9
RMS_EPS = 1e-6
LN_EPS = 1e-5
NEG_INF = -1e30

F32 = jnp.float32
BF16 = jnp.bfloat16

LANES = 128
SUBLANES = 8
MXU_DIM = 256
VMEM_LIMIT_BYTES = 56 * 1024 * 1024

FFN_TILE = 1024
MIX_TILE = 1024
FF_CHUNK = MXU_DIM
N_FF_CHUNKS = D_FF // FF_CHUNK
SSM_CHUNK = 16
SSM_ROW = SSM_CHUNK * SSM_CH
SSM_GROUP_BLOCK = 4
SSM_W_COLS = 4 * LANES
SCAN_SHIFTS = (1, 2, 4)
N_SCAN_CONST = 3 * len(SCAN_SHIFTS) + 3
HEAD_PAIRS = NA_HEADS // 2
NA_Q_ROWS = 8
NA_K_ROWS = 16
NA_KCOLS = 2 * NA_COLS
NA_COL_BLOCKS = GRID_W // NA_COLS
NA_KC0 = (0, 8, 24, 32)
assert NA_KC0 == tuple((n + (n >> 1)) * (NA_COLS // 2) for n in range(GRID_W // NA_COLS))
SEGS_PER_TILE = LANES // SSM_CH
NA_OFF_PAD = (NA_Q_ROWS - 1) - ((NA_MAX_ROWS - 1) - (NA_K_ROWS - NA_Q_ROWS) // 2)
NA_OFF_START = NA_Q_ROWS - 1
NA_OFF_ROWS = NA_OFF_START + NA_K_ROWS


def _silu(x):
    return x * (1.0 / (1.0 + jnp.exp(-x)))


def _sigmoid(x):
    return 1.0 / (1.0 + jnp.exp(-x))


def _gelu(x):
    return 0.5 * x * (1.0 + jnp.tanh(0.7978845608028654 * (x + 0.044715 * (x * x * x))))


def _cparams(sem):
    return pltpu.CompilerParams(dimension_semantics=sem, vmem_limit_bytes=VMEM_LIMIT_BYTES)


def _const_spec(shape):
    nd = len(shape)
    return pl.BlockSpec(shape, lambda *_: (0,) * nd, pipeline_mode=pl.Buffered(1))


def _layer_spec(stacked, l):
    nd = stacked.ndim
    return pl.BlockSpec((None,) + stacked.shape[1:], lambda *_: (l,) + (0,) * (nd - 1), pipeline_mode=pl.Buffered(1))


def _ada_kernel(c_ref, w_ref, b_ref, o_ref):
    s = _silu(c_ref[...]).astype(BF16)
    w = w_ref[0].astype(BF16)
    o_ref[0] = jnp.dot(s, w, preferred_element_type=F32) + b_ref[0]


def _adaln(cond8, w_ada, b_ada):
    tn = D_MODEL
    ncol = N_MOD * D_MODEL
    return pl.pallas_call(
        _ada_kernel,
        grid=(DEPTH, ncol // tn),
        in_specs=[
            pl.BlockSpec((8, D_MODEL), lambda l, j: (0, 0)),
            pl.BlockSpec((1, D_MODEL, tn), lambda l, j: (l, 0, j)),
            pl.BlockSpec((1, 1, tn), lambda l, j: (l, 0, j)),
        ],
        out_specs=pl.BlockSpec((1, 8, tn), lambda l, j: (l, 0, j)),
        out_shape=jax.ShapeDtypeStruct((DEPTH, 8, ncol), F32),
        compiler_params=_cparams(("arbitrary", "arbitrary")),
        name="adaln",
    )(cond8, w_ada, b_ada.reshape(DEPTH, 1, ncol))


def _normed(x, g_row, shift, scale):
    ms = jnp.mean(x * x, axis=-1, keepdims=True)
    h = x * lax.rsqrt(ms + RMS_EPS) * g_row
    return h * (1.0 + scale) + shift


def _mod_spec(l, seg0, seg_tokens, tile):
    tiles_per_seg = seg_tokens // tile
    return pl.BlockSpec((None, 1, N_MOD, D_MODEL), lambda i: (l, seg0 + i // tiles_per_seg, 0, 0))


def _ffn_kernel(x_ref, mod_ref, g_ref, win_ref, wout_ref, o_ref, h_ref, act_ref, *, mod_base):
    x = x_ref[...]
    shift = mod_ref[0, mod_base:mod_base + 1, :]
    scale = mod_ref[0, mod_base + 1:mod_base + 2, :]
    gate = mod_ref[0, mod_base + 2:mod_base + 3, :]
    h_ref[...] = _normed(x, g_ref[...], shift, scale).astype(BF16)
    for j in range(N_FF_CHUNKS):
        cols = slice(j * FF_CHUNK, (j + 1) * FF_CHUNK)
        gate_j = jnp.dot(h_ref[...], win_ref[:, cols], preferred_element_type=F32)
        up_j = jnp.dot(h_ref[...], win_ref[:, D_FF + j * FF_CHUNK:D_FF + (j + 1) * FF_CHUNK], preferred_element_type=F32)
        act_ref[:, cols] = (_silu(gate_j) * up_j).astype(BF16)
    for n in range(D_MODEL // MXU_DIM):
        cols = slice(n * MXU_DIM, (n + 1) * MXU_DIM)
        y = jnp.dot(act_ref[...], wout_ref[:, cols], preferred_element_type=F32)
        o_ref[:, cols] = x_ref[:, cols] + 0.5 * gate[:, cols] * y


def _ffn(x, mod, g, w_in, w_out, *, l, mod_base, seg0, seg_tokens):
    n_tok = x.shape[0]
    return pl.pallas_call(
        functools.partial(_ffn_kernel, mod_base=mod_base),
        grid=(n_tok // FFN_TILE,),
        in_specs=[
            pl.BlockSpec((FFN_TILE, D_MODEL), lambda i: (i, 0)),
            _mod_spec(l, seg0, seg_tokens, FFN_TILE),
            _layer_spec(g, l), _layer_spec(w_in, l), _layer_spec(w_out, l),
        ],
        out_specs=pl.BlockSpec((FFN_TILE, D_MODEL), lambda i: (i, 0)),
        out_shape=jax.ShapeDtypeStruct((n_tok, D_MODEL), F32),
        scratch_shapes=[pltpu.VMEM((FFN_TILE, D_MODEL), BF16), pltpu.VMEM((FFN_TILE, D_FF), BF16)],
        compiler_params=_cparams(("arbitrary",)),
        name="ffn",
    )(x, mod, g, w_in, w_out)


def _segment_ids():
    return lax.broadcasted_iota(jnp.int32, (1, LANES), 1) // SSM_CH


def _merge_segments(load_piece, rot):
    seg = _segment_ids()
    tiles = []
    for lt in range(SSM_ROW // LANES):
        acc = None
        for jj in range(SEGS_PER_TILE):
            src = (lt * SEGS_PER_TILE + jj - rot) % SSM_CHUNK
            piece = load_piece(src, lt)
            acc = piece if acc is None else jnp.where(seg == jj, piece, acc)
        tiles.append(acc)
    return jnp.concatenate(tiles, axis=-1)


def _roll_lanes(x, shift):
    shift %= x.shape[-1]
    return pltpu.roll(x, shift, axis=x.ndim - 1) if shift else x


def _inproj_kernel(*refs, n_in, n_out):
    x_ref, mod_ref, g_ref, w_ref, hsum_ref, qg_ref, kg_ref, ws_ref, gmb_ref = refs[:9]
    u_ref, q_ref, k_ref, v_ref, gm_ref, *tok_refs = refs[n_in:n_in + n_out]
    h_ref, xs_ref, z_ref = refs[n_in + n_out:]
    shift = mod_ref[0, 3:4, :]
    scale = mod_ref[0, 4:5, :]
    h_ref[...] = _normed(x_ref[...], g_ref[...], shift, scale).astype(BF16)
    n_rows = MIX_TILE // SSM_CHUNK

    def proj(c0, width):
        return jnp.dot(h_ref[...], w_ref[:, c0:c0 + width], preferred_element_type=F32)

    xs = proj(0, SSM_WIDTH)
    for j in range(SSM_WIDTH // LANES):
        xs_ref[j] = xs[:, j * LANES:(j + 1) * LANES]
    for t in range(SSM_CHUNK):
        zt = jnp.concatenate([xs_ref[j, pl.ds(t, n_rows, stride=SSM_CHUNK), :] for j in range(SSM_WIDTH // LANES)],
                             axis=-1)
        z_ref[t] = _roll_lanes(zt, SSM_CH * t)
    for g in range(SSM_GROUPS):
        ug = _merge_segments(lambda src, lt: z_ref[src, :, lt * LANES:(lt + 1) * LANES], g)
        u_ref[g] = _roll_lanes(ug, -SSM_CH * g)

    def head_norm(t, gain_row):
        sq = (t * t).astype(BF16)
        ms = jnp.concatenate([jnp.dot(sq[:, c:c + MXU_DIM], hsum_ref[...], preferred_element_type=F32)
                              for c in range(0, NA_WIDTH, MXU_DIM)], axis=-1)
        return t * lax.rsqrt(ms + RMS_EPS) * gain_row

    c0 = SSM_WIDTH
    q = head_norm(proj(c0, NA_WIDTH), qg_ref[...]) * (NA_HEAD_DIM ** -0.5)
    k = head_norm(proj(c0 + NA_WIDTH, NA_WIDTH), kg_ref[...])
    v = proj(c0 + 2 * NA_WIDTH, NA_WIDTH)
    for p in range(HEAD_PAIRS):
        lanes = slice(p * LANES, (p + 1) * LANES)
        q_ref[p] = q[:, lanes].astype(q_ref.dtype)
        k_ref[p] = k[:, lanes].astype(k_ref.dtype)
        v_ref[p] = v[:, lanes].astype(v_ref.dtype)
    if tok_refs:
        for slot in range(tok_refs[0].shape[1]):
            tok_refs[0][:, slot] = k.reshape(MIX_TILE // SEQ, SEQ, NA_WIDTH)
            tok_refs[1][:, slot] = v.reshape(MIX_TILE // SEQ, SEQ, NA_WIDTH)

    a = _gelu(proj(c0 + 3 * NA_WIDTH, 2 * GM_WIDTH))
    u = a[:, :GM_WIDTH]
    vv = a[:, GM_WIDTH:]
    mu = jnp.mean(vv, axis=-1, keepdims=True)
    vc = vv - mu
    var = jnp.mean(vc * vc, axis=-1, keepdims=True)
    vln = (vc * lax.rsqrt(var + LN_EPS)).astype(BF16)
    lane = lax.broadcasted_iota(jnp.int32, (1, GM_WIDTH), 1)
    gw = GM_WIDTH // GM_GROUPS
    for ci in range(MIX_TILE // GM_CHUNK):
        rows = slice(ci * GM_CHUNK, (ci + 1) * GM_CHUNK)
        vch = vln[rows, :]
        sp = gmb_ref[...]
        for gi in range(GM_GROUPS):
            t = jnp.dot(ws_ref[gi], vch, preferred_element_type=F32)
            sp = sp + jnp.where((lane >= gi * gw) & (lane < (gi + 1) * gw), t, 0.0)
        gm_ref[rows, :] = (u[rows, :] * sp).astype(BF16)


def _inproj(x, mod, g, w, hsum, qg, kg, ws, gmb, *, l, seg0, seg_tokens, layer=None, caches=None):
    n_tok = x.shape[0]
    is_ctx = layer is not None
    kv_dtype = BF16 if is_ctx else F32
    n_rows = MIX_TILE // SSM_CHUNK
    pair_spec = pl.BlockSpec((HEAD_PAIRS, MIX_TILE, LANES), lambda i: (0, i, 0))
    out_specs = [
        pl.BlockSpec((SSM_GROUPS, n_rows, SSM_ROW), lambda i: (0, i, 0)),
        pair_spec, pair_spec, pair_spec,
        pl.BlockSpec((MIX_TILE, GM_WIDTH), lambda i: (i, 0)),
    ]
    out_shape = [
        jax.ShapeDtypeStruct((SSM_GROUPS, n_tok // SSM_CHUNK, SSM_ROW), F32),
        jax.ShapeDtypeStruct((HEAD_PAIRS, n_tok, LANES), BF16),
        jax.ShapeDtypeStruct((HEAD_PAIRS, n_tok, LANES), kv_dtype),
        jax.ShapeDtypeStruct((HEAD_PAIRS, n_tok, LANES), kv_dtype),
        jax.ShapeDtypeStruct((n_tok, GM_WIDTH), BF16),
    ]
    args = [x, mod, g, w, hsum, qg, kg, ws, gmb]
    in_specs = [
        pl.BlockSpec((MIX_TILE, D_MODEL), lambda i: (i, 0)),
        _mod_spec(l, seg0, seg_tokens, MIX_TILE),
        _layer_spec(g, l), _layer_spec(w, l),
        _const_spec((MXU_DIM, MXU_DIM)),
        _layer_spec(qg, l), _layer_spec(kg, l), _layer_spec(ws, l), _layer_spec(gmb, l),
    ]
    aliases = {}
    if is_ctx:
        bt = MIX_TILE // SEQ
        cache_shape = jax.ShapeDtypeStruct((n_tok // SEQ, DEPTH, SEQ, NA_WIDTH), F32)
        if caches is None:
            out_specs += [pl.BlockSpec((bt, DEPTH, SEQ, NA_WIDTH), lambda i: (i, 0, 0, 0))] * 2
        else:
            out_specs += [pl.BlockSpec((bt, 1, SEQ, NA_WIDTH), lambda i: (i, layer, 0, 0))] * 2
        out_shape += [cache_shape] * 2
        if caches is not None:
            aliases = {len(args): len(out_shape) - 2, len(args) + 1: len(out_shape) - 1}
            args += list(caches)
            in_specs += [pl.BlockSpec(memory_space=pl.ANY)] * 2
    return pl.pallas_call(
        functools.partial(_inproj_kernel, n_in=len(args), n_out=len(out_shape)),
        grid=(n_tok // MIX_TILE,),
        in_specs=in_specs,
        out_specs=out_specs,
        out_shape=out_shape,
        input_output_aliases=aliases,
        scratch_shapes=[pltpu.VMEM((MIX_TILE, D_MODEL), BF16), pltpu.VMEM((SSM_WIDTH // LANES, MIX_TILE, LANES), F32),
                        pltpu.VMEM((SSM_CHUNK, n_rows, SSM_ROW), F32)],
        compiler_params=_cparams(("arbitrary",)),
        name="inproj_ctx" if is_ctx else "inproj_lat",
    )(*args)


def _shift_rows(x, d, row, down):
    if down:
        return jnp.where(row >= d, pltpu.roll(x, d, axis=0), 0.0)
    return jnp.where(row < SUBLANES - d, pltpu.roll(x, SUBLANES - d, axis=0), 0.0)


def _scan_tile(w, wp, c, cp, cst, row, down):
    x, xp = w, wp
    for si, d in enumerate(SCAN_SHIFTS):
        a, b, bp = cst[3 * si], cst[3 * si + 1], cst[3 * si + 2]
        xs, xps = _shift_rows(x, d, row, down), _shift_rows(xp, d, row, down)
        x, xp = x + a * xs + b * xps, xp + a * xps + bp * xs
    a, b, bp = cst[N_SCAN_CONST - 3], cst[N_SCAN_CONST - 2], cst[N_SCAN_CONST - 1]
    s = x + a * c + b * cp
    sp = xp + a * cp + bp * c
    edge = 0 if down else SUBLANES - 1
    last = SUBLANES - 1 if down else 0
    seen = jnp.where(row == edge, c, _shift_rows(s, 1, row, down))
    c_new = jnp.broadcast_to(s[last:last + 1, :], s.shape)
    cp_new = jnp.broadcast_to(sp[last:last + 1, :], sp.shape)
    return seen, c_new, cp_new


def _ssm_kernel(u_ref, mp_ref, q_ref, cst_ref, dvec_ref, s0_ref, y_ref, fin_ref, w_ref, sq_ref, *, n_batch, n_chunks):
    gb = u_ref.shape[0]
    n_rows = n_batch * n_chunks
    row_blk = min(n_rows, 512)
    for g in range(gb):
        for r0 in range(0, n_rows, row_blk):
            rows = slice(r0, r0 + row_blk)
            ug = u_ref[g, rows, :]
            wy = jnp.dot(ug.astype(BF16), mp_ref[g], preferred_element_type=F32)
            y_ref[g, rows, :] = wy[:, :SSM_ROW] + ug * dvec_ref[g]
            w_ref[g, rows, :] = wy[:, SSM_ROW:]

    n_tiles = n_chunks // SUBLANES
    row = lax.broadcasted_iota(jnp.int32, (SUBLANES, LANES), 0)

    def tile_step(b, kt, carry):
        out = []
        for g in range(gb):
            cf, cfp, cb, cbp = carry[4 * g:4 * g + 4]
            rf = pl.multiple_of(b * n_chunks + kt * SUBLANES, SUBLANES)
            rb = pl.multiple_of(b * n_chunks + (n_tiles - 1 - kt) * SUBLANES, SUBLANES)
            wf = w_ref[g, pl.ds(rf, SUBLANES), :]
            wb = w_ref[g, pl.ds(rb, SUBLANES), :]
            cst_f = [cst_ref[g, 0, i] for i in range(N_SCAN_CONST)]
            cst_b = [cst_ref[g, 1, i] for i in range(N_SCAN_CONST)]
            seen_f, cf, cfp = _scan_tile(wf[:, 0:LANES], wf[:, LANES:2 * LANES], cf, cfp, cst_f, row, True)
            seen_b, cb, cbp = _scan_tile(wb[:, 2 * LANES:3 * LANES], wb[:, 3 * LANES:], cb, cbp, cst_b, row, False)
            sq_ref[g, pl.ds(rf, SUBLANES), 0:LANES] = seen_f
            sq_ref[g, pl.ds(rb, SUBLANES), LANES:2 * LANES] = seen_b
            out += [cf, cfp, cb, cbp]
        return tuple(out)

    def batch_body(b, _):
        carry = []
        for g in range(gb):
            for i in range(4):
                carry.append(jnp.broadcast_to(s0_ref[g, i, pl.ds(b, 1), :], (SUBLANES, LANES)))
        carry = tuple(carry)
        if n_tiles <= 2:
            for kt in range(n_tiles):
                carry = tile_step(b, kt, carry)
        else:
            carry = lax.fori_loop(0, n_tiles, lambda kt, c: tile_step(b, kt, c), carry)
        for g in range(gb):
            fin_ref[g, 0, pl.ds(b, 1), :] = carry[4 * g][0:1, :]
            fin_ref[g, 1, pl.ds(b, 1), :] = carry[4 * g + 2][0:1, :]
        return 0

    lax.fori_loop(0, n_batch, batch_body, 0)

    for g in range(gb):
        for r0 in range(0, n_rows, row_blk):
            rows = slice(r0, r0 + row_blk)
            y_ref[g, rows, :] += jnp.dot(sq_ref[g, rows, :].astype(BF16), q_ref[g], preferred_element_type=F32)


def _ssm_core(u, mp, q, cst, dvec, s0, *, l, n_batch, n_chunks):
    n_rows = n_batch * n_chunks
    gb = SSM_GROUP_BLOCK
    g3 = lambda i: (i, 0, 0)
    g4 = lambda i: (i, 0, 0, 0)
    return pl.pallas_call(
        functools.partial(_ssm_kernel, n_batch=n_batch, n_chunks=n_chunks),
        grid=(SSM_GROUPS // gb,),
        in_specs=[
            pl.BlockSpec((gb, n_rows, SSM_ROW), g3),
            pl.BlockSpec((None, gb, SSM_ROW, SSM_ROW + SSM_W_COLS), lambda i: (l, i, 0, 0)),
            pl.BlockSpec((None, gb, 2 * LANES, SSM_ROW), lambda i: (l, i, 0, 0)),
            pl.BlockSpec((None, gb, 2, N_SCAN_CONST, SUBLANES, LANES), lambda i: (l, i, 0, 0, 0, 0)),
            pl.BlockSpec((None, gb, 1, SSM_ROW), lambda i: (l, i, 0, 0)),
            pl.BlockSpec((gb, 4, n_batch, LANES), g4),
        ],
        out_specs=[
            pl.BlockSpec((gb, n_rows, SSM_ROW), g3),
            pl.BlockSpec((gb, 2, n_batch, LANES), g4),
        ],
        out_shape=[
            jax.ShapeDtypeStruct((SSM_GROUPS, n_rows, SSM_ROW), F32),
            jax.ShapeDtypeStruct((SSM_GROUPS, 2, n_batch, LANES), F32),
        ],
        scratch_shapes=[pltpu.VMEM((gb, n_rows, SSM_W_COLS), F32), pltpu.VMEM((gb, n_rows, 2 * LANES), F32)],
        compiler_params=_cparams(("arbitrary",)),
        name="ssm_core",
    )(u, mp, q, cst, dvec, s0)


def _cmul(a, b):
    return a[0] * b[0] - a[1] * b[1], a[0] * b[1] + a[1] * b[0]


_CHUNK_IDX = np.arange(SSM_CHUNK)


def _ssm_exponents():
    t = SSM_CHUNK
    idx = _CHUNK_IDX.astype(np.float32)
    both = lambda a: np.stack([a, a], axis=1)
    lane_step = idx - (t - 1) / 2
    tile_rows = np.arange(1, SUBLANES + 1, dtype=np.float32) * t
    tables = dict(
        one=both(np.ones(1, np.float32)),
        incr=np.stack([t - 1 - idx, idx], axis=1),
        seen=np.stack([idx + 1, t - idx], axis=1),
        resp_in=np.stack([-lane_step, lane_step], axis=1),
        resp_out=np.stack([lane_step, -lane_step], axis=1),
        shift=both(np.asarray(SCAN_SHIFTS, np.float32) * t),
        tile=np.stack([tile_rows, tile_rows[::-1]], axis=1),
    )
    spans, start = {}, 0
    for name, tab in tables.items():
        spans[name] = slice(start, start + len(tab))
        start += len(tab)
    return np.concatenate(list(tables.values()), axis=0), spans


def _ssm_params(p):
    t = SSM_CHUNK
    n_l = p['ssm_lambda_re'].shape[0]
    lam = (p['ssm_lambda_re'].astype(F32), p['ssm_lambda_im'].astype(F32))
    dt = jnp.exp(p['ssm_log_dt'].astype(F32))[..., None]
    arg = (lam[0] * dt, lam[1] * dt)
    expo, span = _ssm_exponents()
    n = jnp.asarray(expo)[:, None, :, None, None]
    mag = jnp.exp(n * arg[0][None])
    pw = (mag * jnp.cos(n * arg[1][None]), mag * jnp.sin(n * arg[1][None]))
    take = lambda name: [x[span[name]] for x in pw]
    lanes_last = lambda xs: [jnp.moveaxis(x, 0, -1) for x in xs]

    lbar = [x[0] for x in take('one')]
    num = (lbar[0] - 1.0, lbar[1])
    den = lam[0] * lam[0] + lam[1] * lam[1]
    zoh = ((num[0] * lam[0] + num[1] * lam[1]) / den, (num[1] * lam[0] - num[0] * lam[1]) / den)
    b_mat = (p['ssm_b_re'].astype(F32), p['ssm_b_im'].astype(F32))
    bbar = _cmul((zoh[0][..., None], zoh[1][..., None]), b_mat)
    c_mat = [jnp.swapaxes(p[k].astype(F32), -1, -2) for k in ('ssm_c_re', 'ssm_c_im')]
    ct = [jnp.tile(x, (1, 1, 1, 1, t)) for x in c_mat]
    bt = [jnp.tile(x, (1, 1, 1, 1, t)) for x in bbar]

    per_lane = lambda name: [jnp.repeat(x, SSM_CH, axis=-1) for x in lanes_last(take(name))]
    xm = _cmul(per_lane('resp_in'), bt)
    ym = _cmul(per_lane('resp_out'), ct)
    resp = jnp.einsum('ldgpx,ldgpy->ldgxy', jnp.concatenate([xm[0], -xm[1]], axis=3),
                      jnp.concatenate([ym[0], ym[1]], axis=3), precision='highest')
    steps_x = np.repeat(_CHUNK_IDX, SSM_CH)
    causal = np.stack([steps_x[None, :] >= steps_x[:, None], steps_x[:, None] >= steps_x[None, :]])
    mats = jnp.sum(jnp.where(jnp.asarray(causal)[None, :, None], resp, 0.0), axis=1)

    incr = _cmul([x[..., None] for x in take('incr')], [x[None] for x in bbar])
    incr = [jnp.transpose(x, (1, 2, 3, 0, 5, 4)) for x in incr]
    pcat = jnp.concatenate([incr[0], incr[1], incr[1], incr[0]], axis=-1)
    pcat = jnp.concatenate([pcat[:, 0], pcat[:, 1]], axis=-1).reshape(n_l, SSM_GROUPS, SSM_ROW, SSM_W_COLS)
    mp = jnp.concatenate([mats, pcat], axis=-1).astype(BF16)

    qm = _cmul(ct, per_lane('seen'))
    q = jnp.concatenate([qm[0][:, 0], -qm[1][:, 0], qm[0][:, 1], -qm[1][:, 1]], axis=2).astype(BF16)

    def forms(name, rows):
        pr, pi = [jnp.moveaxis(x, 0, 3) for x in take(name)]
        f = jnp.stack([jnp.concatenate([pr, pr], -1), jnp.concatenate([-pi, pi], -1),
                       jnp.concatenate([pi, -pi], -1)], axis=4)
        return f if rows else jnp.broadcast_to(
            f.reshape(n_l, 2, SSM_GROUPS, 3 * len(SCAN_SHIFTS), 1, LANES),
            (n_l, 2, SSM_GROUPS, 3 * len(SCAN_SHIFTS), SUBLANES, LANES))
    per_row = jnp.swapaxes(forms('tile', True), 3, 4)
    cst = jnp.swapaxes(jnp.concatenate([forms('shift', False), per_row], axis=3), 1, 2)
    dvec = jnp.tile(p['ssm_d'].astype(F32).reshape(n_l, SSM_GROUPS, 1, SSM_CH), (1, 1, 1, SSM_CHUNK))
    return mp, q, cst, dvec


def _ssm_states_in(state_ssm):
    st = jnp.transpose(state_ssm.astype(F32), (1, 3, 2, 0, 4, 5))
    re, im = st[..., 0], st[..., 1]
    ri, ir = jnp.concatenate([re, im], -1), jnp.concatenate([im, re], -1)
    return jnp.stack([ri[:, :, 0], ir[:, :, 0], ri[:, :, 1], ir[:, :, 1]], axis=2)


def _ssm_states_out(fins):
    fin = jnp.stack(fins, axis=0)
    fin = jnp.stack([fin[..., :SSM_STATE], fin[..., SSM_STATE:]], axis=-1)
    return jnp.transpose(fin, (3, 0, 2, 1, 4, 5))


def _head_lane_mask(hh):
    lane = lax.broadcasted_iota(jnp.int32, (1, LANES), 1)
    return (lane >= hh * NA_HEAD_DIM) & (lane < (hh + 1) * NA_HEAD_DIM)


_NT_DIMS = (((1,), (1,)), ((), ()))


def _ctx_attn_kernel(q_ref, k_ref, v_ref, o_ref):
    for p in range(HEAD_PAIRS):
        qp, kp, vp = q_ref[p], k_ref[p], v_ref[p]
        lm0 = _head_lane_mask(0)
        zero = jnp.zeros_like(qp)
        q2 = jnp.concatenate([jnp.where(lm0, qp, zero), jnp.where(lm0, zero, qp)], axis=0)
        s = lax.dot_general(q2, kp, _NT_DIMS, preferred_element_type=F32)
        e = jnp.exp(s - jnp.max(s, axis=-1, keepdims=True))
        o = jnp.dot(e.astype(BF16), vp, preferred_element_type=F32) / jnp.sum(e, axis=-1, keepdims=True)
        o_ref[:, p * LANES:(p + 1) * LANES] = jnp.where(lm0, o[:SEQ, :], o[SEQ:, :]).astype(BF16)


def _ctx_attn(q, k, v, n_batch):
    spec = pl.BlockSpec((HEAD_PAIRS, SEQ, LANES), lambda b: (0, b, 0))
    return pl.pallas_call(
        _ctx_attn_kernel,
        grid=(n_batch,),
        in_specs=[spec, spec, spec],
        out_specs=pl.BlockSpec((SEQ, NA_WIDTH), lambda b: (b, 0)),
        out_shape=jax.ShapeDtypeStruct((n_batch * SEQ, NA_WIDTH), BF16),
        compiler_params=_cparams(("arbitrary",)),
        name="ctx_attn",
    )(q, k, v)


def _na_kernel(q_ref, k_ref, v_ref, kc_ref, vc_ref, t2_ref, rm_ref, o_ref,
               qb_ref, sc_ref, s_ref, e_ref, ec_ref, vb_ref, den_ref, os_ref, tb_ref):
    rt = pl.program_id(1)
    n_rt = pl.num_programs(1)

    @pl.when((pl.program_id(0) == 0) & (rt == 0))
    def _():
        for h in range(NA_HEADS):
            for n in range(NA_COL_BLOCKS):
                for rl in range(NA_Q_ROWS):
                    lane0 = (NA_OFF_START - rl) * NA_KCOLS
                    tb_ref[h, n, rl * NA_COLS:(rl + 1) * NA_COLS, :] = (
                        t2_ref[h, n, :, lane0:lane0 + NA_K_ROWS * NA_KCOLS])

    r0 = rt * NA_Q_ROWS
    ttype = jnp.where(rt == 0, 0, jnp.where(rt == n_rt - 1, 2, 1))
    blk_q = NA_Q_ROWS * NA_COLS
    lm0 = _head_lane_mask(0)

    for p in range(HEAD_PAIRS):
        for n in range(NA_COL_BLOCKS):
            for rl in range(NA_Q_ROWS):
                qrow = q_ref[p, rl * GRID_W + n * NA_COLS:rl * GRID_W + (n + 1) * NA_COLS, :]
                zero = jnp.zeros_like(qrow)
                base = 2 * n * blk_q + rl * NA_COLS
                qb_ref[p, base:base + NA_COLS, :] = jnp.where(lm0, qrow, zero)
                qb_ref[p, base + blk_q:base + blk_q + NA_COLS, :] = jnp.where(lm0, zero, qrow)
        sc_ref[p] = lax.dot_general(qb_ref[p], kc_ref[0, p], _NT_DIMS, preferred_element_type=F32)

    def block_of(i):
        return i >> 2, i & (NA_COL_BLOCKS - 1)

    def rows_of(n):
        start = n * (2 * blk_q)
        return pl.ds(start if isinstance(start, int) else pl.multiple_of(start, 2 * blk_q), 2 * blk_q)

    def window_scores(i):
        p, n = block_of(i)
        kc0 = (n + (n >> 1)) * (NA_COLS // 2)
        kparts, vparts = [], []
        for j in range(NA_K_ROWS):
            kr = jnp.clip(r0 - (NA_K_ROWS - NA_Q_ROWS) // 2 + j, 0, GRID_W - 1)
            start = pl.multiple_of(kr * GRID_W + kc0, 8)
            kparts.append(k_ref[p, pl.ds(start, NA_KCOLS), :])
            vparts.append(v_ref[p, pl.ds(start, NA_KCOLS), :])
        kblk = jnp.concatenate(kparts, axis=0).astype(BF16)
        vb_ref[i & 3] = jnp.concatenate(vparts, axis=0).astype(BF16)
        s_ref[i & 1] = lax.dot_general(qb_ref[p, rows_of(n), :], kblk, _NT_DIMS, preferred_element_type=F32)

    def softmax(i):
        p, n = block_of(i)
        rmask = rm_ref[ttype]
        bias = jnp.concatenate([tb_ref[2 * p, n] + rmask, tb_ref[2 * p + 1, n] + rmask], axis=0)
        s = s_ref[i & 1] + bias
        sc = sc_ref[p, rows_of(n), :]
        m = jnp.maximum(jnp.max(s, axis=-1, keepdims=True), jnp.max(sc, axis=-1, keepdims=True))
        e = jnp.exp(s - m)
        ec = jnp.exp(sc - m)
        den = jnp.sum(e, axis=-1, keepdims=True) + jnp.sum(ec, axis=-1, keepdims=True)
        e_ref[i & 1] = e.astype(BF16)
        ec_ref[i & 1] = ec.astype(BF16)
        den_ref[i & 1] = jnp.broadcast_to(den, (2 * blk_q, LANES))

    def value_products(i):
        p, n = block_of(i)
        o = (jnp.dot(e_ref[i & 1], vb_ref[i & 3], preferred_element_type=F32)
             + jnp.dot(ec_ref[i & 1], vc_ref[0, p], preferred_element_type=F32)) / den_ref[i & 1]
        oblk = jnp.where(lm0, o[:blk_q, :], o[blk_q:, :])
        for rl in range(NA_Q_ROWS):
            start = rl * GRID_W + n * NA_COLS
            start = start if isinstance(start, int) else pl.multiple_of(start, NA_COLS)
            os_ref[p, pl.ds(start, NA_COLS), :] = oblk[rl * NA_COLS:(rl + 1) * NA_COLS, :]

    n_blocks = HEAD_PAIRS * NA_COL_BLOCKS
    window_scores(0)
    window_scores(1)
    softmax(0)

    def steady(i, carry):
        value_products(i - 2)
        softmax(i - 1)
        window_scores(i)
        return carry

    lax.fori_loop(2, n_blocks, steady, 0)
    value_products(n_blocks - 2)
    softmax(n_blocks - 1)
    value_products(n_blocks - 1)
    for p in range(HEAD_PAIRS):
        o_ref[:, p * LANES:(p + 1) * LANES] = os_ref[p].astype(BF16)


def _na_attn(q, k, v, kc, vc, tb, rm, n_batch, *, l):
    tile_tok = NA_Q_ROWS * GRID_W
    n_rt = DEC_SEQ // tile_tok
    blk_q, blk_k = NA_Q_ROWS * NA_COLS, NA_K_ROWS * NA_KCOLS
    return pl.pallas_call(
        _na_kernel,
        grid=(n_batch, n_rt),
        in_specs=[
            pl.BlockSpec((HEAD_PAIRS, tile_tok, LANES), lambda b, r: (0, b * n_rt + r, 0)),
            pl.BlockSpec((HEAD_PAIRS, DEC_SEQ, LANES), lambda b, r: (0, b, 0), pipeline_mode=pl.Buffered(1)),
            pl.BlockSpec((HEAD_PAIRS, DEC_SEQ, LANES), lambda b, r: (0, b, 0), pipeline_mode=pl.Buffered(1)),
            pl.BlockSpec((1, None, HEAD_PAIRS, PAST_LEN, LANES), lambda b, r: (b, l, 0, 0, 0)),
            pl.BlockSpec((1, None, HEAD_PAIRS, PAST_LEN, LANES), lambda b, r: (b, l, 0, 0, 0)),
            _layer_spec(tb, l),
            _const_spec(rm.shape),
        ],
        out_specs=pl.BlockSpec((tile_tok, NA_WIDTH), lambda b, r: (b * n_rt + r, 0)),
        out_shape=jax.ShapeDtypeStruct((n_batch * DEC_SEQ, NA_WIDTH), BF16),
        scratch_shapes=[
            pltpu.VMEM((HEAD_PAIRS, 2 * tile_tok, LANES), BF16),
            pltpu.VMEM((HEAD_PAIRS, 2 * tile_tok, PAST_LEN), F32),
            pltpu.VMEM((2, 2 * blk_q, blk_k), F32),
            pltpu.VMEM((2, 2 * blk_q, blk_k), BF16),
            pltpu.VMEM((2, 2 * blk_q, PAST_LEN), BF16),
            pltpu.VMEM((4, blk_k, LANES), BF16),
            pltpu.VMEM((2, 2 * blk_q, LANES), F32),
            pltpu.VMEM((HEAD_PAIRS, tile_tok, LANES), F32),
            pltpu.VMEM((NA_HEADS, NA_COL_BLOCKS, blk_q, blk_k), F32)],
        compiler_params=_cparams(("arbitrary", "arbitrary")),
        name="na_attn",
    )(q, k, v, kc, vc, tb, rm)


def _na_tables(rpb):
    n = np.arange(NA_COL_BLOCKS)
    cc = np.arange(NA_COLS)
    kk = np.arange(NA_KCOLS)
    rl = np.arange(NA_Q_ROWS)
    ki = np.arange(NA_K_ROWS)
    n_dr, n_dc = 2 * NA_MAX_ROWS - 1, 2 * NA_COLS - 1
    c = n[:, None] * NA_COLS + cc[None, :]
    kcol = np.asarray(NA_KC0)[:, None] + kk[None, :]
    dc = np.clip(kcol[:, None, :] - c[:, :, None], -(NA_COLS - 1), NA_COLS - 1) + (NA_COLS - 1)
    cs = np.clip(c - NA_COLS // 2, 0, GRID_W - NA_COLS)
    col_ok = (kcol[:, None, :] >= cs[:, :, None]) & (kcol[:, None, :] < cs[:, :, None] + NA_COLS)
    half = (NA_K_ROWS - NA_Q_ROWS) // 2
    dr = np.clip(ki[None, :] - half - rl[:, None] + (NA_MAX_ROWS - 1), 0, n_dr - 1)
    oh_dc = jnp.asarray(dc[None] == np.arange(n_dc)[:, None, None, None], F32)
    off = np.clip(np.arange(NA_OFF_ROWS) - NA_OFF_PAD, 0, n_dr - 1)
    assert np.array_equal(dr, off[ki[None] + (NA_OFF_START - rl)[:, None]])
    oh_off = jnp.asarray(off[None, :] == np.arange(n_dr)[:, None], F32)
    rows_p = jnp.einsum('lhab,aA->lhAb', rpb.astype(F32), oh_off, precision='highest')
    cols = jnp.einsum('lhAb,bnck->lhncAk', rows_p, oh_dc, precision='highest')
    cols = jnp.where(jnp.asarray(col_ok)[None, None, :, :, None, :], cols, NEG_INF)
    tb = cols.reshape(-1, NA_HEADS, NA_COL_BLOCKS, NA_COLS, NA_OFF_ROWS * NA_KCOLS)
    rows = GRID_W
    masks = []
    for r0 in (0, NA_Q_ROWS, rows - NA_Q_ROWS):
        r = r0 + rl
        rs = np.clip(r - NA_MAX_ROWS // 2, 0, rows - NA_MAX_ROWS)
        kr = r0 - half + ki
        ok = (kr[None, :] >= rs[:, None]) & (kr[None, :] < rs[:, None] + NA_MAX_ROWS)
        full = np.broadcast_to(ok[:, None, :, None], (NA_Q_ROWS, NA_COLS, NA_K_ROWS, NA_KCOLS))
        masks.append(np.where(full, 0.0, NEG_INF).reshape(NA_Q_ROWS * NA_COLS, NA_K_ROWS * NA_KCOLS))
    rm = jnp.asarray(np.stack(masks), F32)
    return tb, rm


def _outproj_kernel(x_ref, mod_ref, y_ref, na_ref, gm_ref, gw_ref, gb_ref, w_ref, o_ref, yr_ref, ys_ref, cat_ref):
    gate = mod_ref[0, 5:6, :]
    n_rows = MIX_TILE // SSM_CHUNK
    for g in range(SSM_GROUPS):
        yr_ref[g] = _roll_lanes(y_ref[g], SSM_CH * g)
    for t in range(SSM_CHUNK):
        yt = _merge_segments(lambda src, lt: yr_ref[src, :, lt * LANES:(lt + 1) * LANES], t)
        yt = _roll_lanes(yt, -SSM_CH * t)
        for j in range(SSM_WIDTH // LANES):
            ys_ref[j, pl.ds(t, n_rows, stride=SSM_CHUNK), :] = yt[:, j * LANES:(j + 1) * LANES]
    y = _gelu(jnp.concatenate([ys_ref[j] for j in range(SSM_WIDTH // LANES)], axis=-1))
    z = jnp.dot(y.astype(BF16), gw_ref[...], preferred_element_type=F32) + gb_ref[...]
    cat_ref[:, 0:SSM_WIDTH] = (y * _sigmoid(z)).astype(BF16)
    cat_ref[:, SSM_WIDTH:SSM_WIDTH + NA_WIDTH] = na_ref[...]
    cat_ref[:, SSM_WIDTH + NA_WIDTH:] = gm_ref[...]
    mix = jnp.dot(cat_ref[...], w_ref[...], preferred_element_type=F32)
    o_ref[...] = x_ref[...] + gate * mix


def _outproj(x, mod, y, na, gm, gw, gb, w, *, l, seg0, seg_tokens):
    n_tok = x.shape[0]
    n_rows = MIX_TILE // SSM_CHUNK
    tok = lambda width: pl.BlockSpec((MIX_TILE, width), lambda i: (i, 0))
    return pl.pallas_call(
        _outproj_kernel,
        grid=(n_tok // MIX_TILE,),
        in_specs=[
            tok(D_MODEL),
            _mod_spec(l, seg0, seg_tokens, MIX_TILE),
            pl.BlockSpec((SSM_GROUPS, n_rows, SSM_ROW), lambda i: (0, i, 0)),
            tok(NA_WIDTH), tok(GM_WIDTH),
            _layer_spec(gw, l), _layer_spec(gb, l), _layer_spec(w, l),
        ],
        out_specs=tok(D_MODEL),
        out_shape=jax.ShapeDtypeStruct((n_tok, D_MODEL), F32),
        scratch_shapes=[pltpu.VMEM((SSM_GROUPS, n_rows, SSM_ROW), F32),
                        pltpu.VMEM((SSM_WIDTH // LANES, MIX_TILE, LANES), F32),
                        pltpu.VMEM((MIX_TILE, D_MODEL), BF16)],
        compiler_params=_cparams(("arbitrary",)),
        name="outproj",
    )(x, mod, y, na, gm, gw, gb, w)


def _prep(p):
    heads = np.arange(MXU_DIM) // NA_HEAD_DIM
    hsum = jnp.asarray((heads[:, None] == heads[None, :]) / NA_HEAD_DIM, BF16)
    gmb = jnp.repeat(jnp.swapaxes(p['gm_bs'], 1, 2).astype(F32), GM_WIDTH // GM_GROUPS, axis=2)
    tb, rm = _na_tables(p['na_rpb'])
    row = lambda a: a.astype(F32).reshape(DEPTH, 1, -1)
    return dict(
        g1=row(p['norm_ffn1']), g2=row(p['norm_mix']), g3=row(p['norm_ffn2']),
        f1_in=p['ffn1_w_in'].astype(BF16), f1_out=p['ffn1_w_out'].astype(BF16),
        f2_in=p['ffn2_w_in'].astype(BF16), f2_out=p['ffn2_w_out'].astype(BF16),
        w_in=p['w_in'].astype(BF16), w_out=p['w_out'].astype(BF16),
        hsum=hsum,
        qg=row(jnp.tile(p['na_q_norm'], (1, NA_HEADS))), kg=row(jnp.tile(p['na_k_norm'], (1, NA_HEADS))),
        ws=p['gm_ws'].astype(BF16), gmb=gmb,
        ssm=_ssm_params(p),
        glu_w=p['ssm_glu_w'].astype(BF16), glu_b=row(p['ssm_glu_b']),
        tb=tb, rm=rm,
    )


def _pairs_major(t):
    b, n_l, s = t.shape[:3]
    return jnp.transpose(t.reshape(b, n_l, s, HEAD_PAIRS, LANES), (0, 1, 3, 2, 4)).astype(BF16)


def _trunk_layer(x, mod, w, l, *, n_batch, seq_len, caches=None, is_ctx=False, ctx_kv=None, ssm_init=None):
    seg = dict(l=l, seg0=0 if is_ctx else 1, seg_tokens=x.shape[0] if is_ctx else seq_len)
    x = _ffn(x, mod, w['g1'], w['f1_in'], w['f1_out'], mod_base=0, **seg)
    outs = _inproj(x, mod, w['g2'], w['w_in'], w['hsum'], w['qg'], w['kg'], w['ws'], w['gmb'],
                   layer=l if is_ctx else None, caches=caches, **seg)
    u, q, k, v, gm = outs[:5]
    s0 = jnp.zeros((SSM_GROUPS, 4, n_batch, LANES), F32) if is_ctx else ssm_init[l]
    y, fin = _ssm_core(u, *w['ssm'], s0, l=l, n_batch=n_batch, n_chunks=seq_len // SSM_CHUNK)
    if is_ctx:
        na = _ctx_attn(q, k, v, n_batch)
    else:
        na = _na_attn(q, k, v, ctx_kv[0], ctx_kv[1], w['tb'], w['rm'], n_batch, l=l)
    x = _outproj(x, mod, y, na, gm, w['glu_w'], w['glu_b'], w['w_out'], **seg)
    x = _ffn(x, mod, w['g3'], w['f2_in'], w['f2_out'], mod_base=6, **seg)
    return x, (tuple(outs[5:]), fin)


def kernel(x_prompt, x_sample, c, cache_k, cache_v, state_ssm, c_ctx, w_ada, b_ada, norm_ffn1, ffn1_w_in, ffn1_w_out, norm_mix, w_in, w_out, ssm_lambda_re, ssm_lambda_im, ssm_log_dt, ssm_b_re, ssm_b_im, ssm_c_re, ssm_c_im, ssm_d, ssm_glu_w, ssm_glu_b, na_q_norm, na_k_norm, na_rpb, gm_ws, gm_bs, norm_ffn2, ffn2_w_in, ffn2_w_out):
    p = dict(norm_ffn1=norm_ffn1, ffn1_w_in=ffn1_w_in, ffn1_w_out=ffn1_w_out, norm_mix=norm_mix, w_in=w_in,
             w_out=w_out, ssm_lambda_re=ssm_lambda_re, ssm_lambda_im=ssm_lambda_im, ssm_log_dt=ssm_log_dt,
             ssm_b_re=ssm_b_re, ssm_b_im=ssm_b_im, ssm_c_re=ssm_c_re, ssm_c_im=ssm_c_im, ssm_d=ssm_d,
             ssm_glu_w=ssm_glu_w, ssm_glu_b=ssm_glu_b, na_q_norm=na_q_norm, na_k_norm=na_k_norm, na_rpb=na_rpb,
             gm_ws=gm_ws, gm_bs=gm_bs, norm_ffn2=norm_ffn2, ffn2_w_in=ffn2_w_in, ffn2_w_out=ffn2_w_out)
    batch, dec_batch = x_prompt.shape[0], x_sample.shape[0]
    cond8 = jnp.zeros((8, D_MODEL), F32).at[0].set(c_ctx).at[1:1 + dec_batch].set(c)
    mod = _adaln(cond8, w_ada, b_ada).reshape(DEPTH, 8, N_MOD, D_MODEL)

    xp = x_prompt.reshape(batch * SEQ, D_MODEL)
    xs = x_sample.reshape(dec_batch * DEC_SEQ, D_MODEL)
    w = _prep(p)
    ctx_kv = (_pairs_major(cache_k), _pairs_major(cache_v))
    ssm_init = _ssm_states_in(state_ssm)
    caches, fins = None, []
    for l in range(DEPTH):
        xp, (caches, fin) = _trunk_layer(xp, mod, w, l, n_batch=batch, seq_len=SEQ, caches=caches, is_ctx=True)
        fins.append(fin)
        xs, _ = _trunk_layer(xs, mod, w, l, n_batch=dec_batch, seq_len=DEC_SEQ, ctx_kv=ctx_kv, ssm_init=ssm_init)
    cache_shape = (batch, DEPTH, SEQ, NA_HEADS, NA_HEAD_DIM)
    return (xp.reshape(batch, SEQ, D_MODEL), xs.reshape(dec_batch, DEC_SEQ, D_MODEL),
            caches[0].reshape(cache_shape), caches[1].reshape(cache_shape), _ssm_states_out(fins))
```

```python
import functools

import numpy as np
import jax
import jax.numpy as jnp
from jax import lax
from jax.experimental import pallas as pl
from jax.experimental.pallas import tpu as pltpu

D_MODEL = 1024
DEPTH = 2
SEQ = 256
DEC_SEQ = 4096
PAST_LEN = 256
GRID_W = 64
SSM_WIDTH = 256
SSM_CH = 16
SSM_GROUPS = 16
SSM_STATE = 64
NA_WIDTH = 512
NA_HEAD_DIM = 64
NA_HEADS = 8
NA_MAX_ROWS = 8
NA_COLS = 16
GM_WIDTH = 256
GM_GROUPS = 4
GM_CHUNK = 128
D_FF = 2816
N_MOD = 9
RMS_EPS = 1e-6
LN_EPS = 1e-5
NEG_INF = -1e30

F32 = jnp.float32
BF16 = jnp.bfloat16

LANES = 128
SUBLANES = 8
MXU_DIM = 256
VMEM_LIMIT_BYTES = 56 * 1024 * 1024

FFN_TILE = 1024
MIX_TILE = 1024
FF_CHUNK = MXU_DIM
N_FF_CHUNKS = D_FF // FF_CHUNK
SSM_CHUNK = 16
SSM_ROW = SSM_CHUNK * SSM_CH
SSM_GROUP_BLOCK = 4
SSM_W_COLS = 4 * LANES
SCAN_SHIFTS = (1, 2, 4)
N_SCAN_CONST = 3 * len(SCAN_SHIFTS) + 3
HEAD_PAIRS = NA_HEADS // 2
NA_Q_ROWS = 8
NA_K_ROWS = 16
NA_KCOLS = 2 * NA_COLS
NA_COL_BLOCKS = GRID_W // NA_COLS
NA_KC0 = (0, 8, 24, 32)
SEGS_PER_TILE = LANES // SSM_CH
NA_OFF_PAD = (NA_Q_ROWS - 1) - ((NA_MAX_ROWS - 1) - (NA_K_ROWS - NA_Q_ROWS) // 2)
NA_OFF_START = NA_Q_ROWS - 1
NA_OFF_ROWS = NA_OFF_START + NA_K_ROWS


def _silu(x):
    return x * (1.0 / (1.0 + jnp.exp(-x)))


def _sigmoid(x):
    return 1.0 / (1.0 + jnp.exp(-x))


def _gelu(x):
    return 0.5 * x * (1.0 + jnp.tanh(0.7978845608028654 * (x + 0.044715 * (x * x * x))))


def _cparams(sem):
    return pltpu.CompilerParams(dimension_semantics=sem, vmem_limit_bytes=VMEM_LIMIT_BYTES)


def _const_spec(shape):
    nd = len(shape)
    return pl.BlockSpec(shape, lambda *_: (0,) * nd, pipeline_mode=pl.Buffered(1))


def _layer_spec(stacked, l):
    nd = stacked.ndim
    return pl.BlockSpec((None,) + stacked.shape[1:], lambda *_: (l,) + (0,) * (nd - 1), pipeline_mode=pl.Buffered(1))


def _ada_kernel(c_ref, w_ref, b_ref, o_ref):
    s = _silu(c_ref[...]).astype(BF16)
    w = w_ref[0].astype(BF16)
    o_ref[0] = jnp.dot(s, w, preferred_element_type=F32) + b_ref[0]


def _adaln(cond8, w_ada, b_ada):
    tn = D_MODEL
    ncol = N_MOD * D_MODEL
    return pl.pallas_call(
        _ada_kernel,
        grid=(DEPTH, ncol // tn),
        in_specs=[
            pl.BlockSpec((8, D_MODEL), lambda l, j: (0, 0)),
            pl.BlockSpec((1, D_MODEL, tn), lambda l, j: (l, 0, j)),
            pl.BlockSpec((1, 1, tn), lambda l, j: (l, 0, j)),
        ],
        out_specs=pl.BlockSpec((1, 8, tn), lambda l, j: (l, 0, j)),
        out_shape=jax.ShapeDtypeStruct((DEPTH, 8, ncol), F32),
        compiler_params=_cparams(("arbitrary", "arbitrary")),
        name="adaln",
    )(cond8, w_ada, b_ada.reshape(DEPTH, 1, ncol))


def _normed(x, g_row, shift, scale):
    ms = jnp.mean(x * x, axis=-1, keepdims=True)
    h = x * lax.rsqrt(ms + RMS_EPS) * g_row
    return h * (1.0 + scale) + shift


def _mod_spec(l, seg0, seg_tokens, tile):
    tiles_per_seg = seg_tokens // tile
    return pl.BlockSpec((None, 1, N_MOD, D_MODEL), lambda i: (l, seg0 + i // tiles_per_seg, 0, 0))


def _ffn_kernel(x_ref, mod_ref, g_ref, win_ref, wout_ref, o_ref, h_ref, act_ref, *, mod_base):
    shift = mod_ref[0, mod_base:mod_base + 1, :]
    scale = mod_ref[0, mod_base + 1:mod_base + 2, :]
    gate = mod_ref[0, mod_base + 2:mod_base + 3, :]
    h_ref[...] = _normed(x_ref[...], g_ref[...], shift, scale).astype(BF16)
    for j in range(N_FF_CHUNKS):
        cols = slice(j * FF_CHUNK, (j + 1) * FF_CHUNK)
        gate_j = jnp.dot(h_ref[...], win_ref[:, cols], preferred_element_type=F32)
        up_j = jnp.dot(h_ref[...], win_ref[:, D_FF + j * FF_CHUNK:D_FF + (j + 1) * FF_CHUNK], preferred_element_type=F32)
        act_ref[:, cols] = (_silu(gate_j) * up_j).astype(BF16)
    for n in range(D_MODEL // MXU_DIM):
        cols = slice(n * MXU_DIM, (n + 1) * MXU_DIM)
        y = jnp.dot(act_ref[...], wout_ref[:, cols], preferred_element_type=F32)
        o_ref[:, cols] = x_ref[:, cols] + 0.5 * gate[:, cols] * y


def _ffn(x, mod, g, w_in, w_out, *, l, mod_base, seg0, seg_tokens):
    n_tok = x.shape[0]
    return pl.pallas_call(
        functools.partial(_ffn_kernel, mod_base=mod_base),
        grid=(n_tok // FFN_TILE,),
        in_specs=[
            pl.BlockSpec((FFN_TILE, D_MODEL), lambda i: (i, 0)),
            _mod_spec(l, seg0, seg_tokens, FFN_TILE),
            _layer_spec(g, l), _layer_spec(w_in, l), _layer_spec(w_out, l),
        ],
        out_specs=pl.BlockSpec((FFN_TILE, D_MODEL), lambda i: (i, 0)),
        out_shape=jax.ShapeDtypeStruct((n_tok, D_MODEL), F32),
        scratch_shapes=[pltpu.VMEM((FFN_TILE, D_MODEL), BF16), pltpu.VMEM((FFN_TILE, D_FF), BF16)],
        compiler_params=_cparams(("arbitrary",)),
        name="ffn",
    )(x, mod, g, w_in, w_out)


def _segment_ids():
    return lax.broadcasted_iota(jnp.int32, (1, LANES), 1) // SSM_CH


def _merge_segments(load_piece, rot):
    seg = _segment_ids()
    tiles = []
    for lt in range(SSM_ROW // LANES):
        acc = None
        for jj in range(SEGS_PER_TILE):
            src = (lt * SEGS_PER_TILE + jj - rot) % SSM_CHUNK
            piece = load_piece(src, lt)
            acc = piece if acc is None else jnp.where(seg == jj, piece, acc)
        tiles.append(acc)
    return jnp.concatenate(tiles, axis=-1)


def _roll_lanes(x, shift):
    shift %= x.shape[-1]
    return pltpu.roll(x, shift, axis=x.ndim - 1) if shift else x


def _inproj_kernel(*refs, n_in, n_out):
    x_ref, mod_ref, g_ref, w_ref, hsum_ref, qg_ref, kg_ref, ws_ref, gmb_ref = refs[:9]
    u_ref, q_ref, k_ref, v_ref, gm_ref, *tok_refs = refs[n_in:n_in + n_out]
    h_ref, xs_ref, z_ref = refs[n_in + n_out:]
    shift = mod_ref[0, 3:4, :]
    scale = mod_ref[0, 4:5, :]
    h_ref[...] = _normed(x_ref[...], g_ref[...], shift, scale).astype(BF16)
    n_rows = MIX_TILE // SSM_CHUNK

    def proj(c0, width):
        return jnp.dot(h_ref[...], w_ref[:, c0:c0 + width], preferred_element_type=F32)

    xs = proj(0, SSM_WIDTH)
    for j in range(SSM_WIDTH // LANES):
        xs_ref[j] = xs[:, j * LANES:(j + 1) * LANES]
    for t in range(SSM_CHUNK):
        zt = jnp.concatenate([xs_ref[j, pl.ds(t, n_rows, stride=SSM_CHUNK), :] for j in range(SSM_WIDTH // LANES)],
                             axis=-1)
        z_ref[t] = _roll_lanes(zt, SSM_CH * t)
    for g in range(SSM_GROUPS):
        ug = _merge_segments(lambda src, lt: z_ref[src, :, lt * LANES:(lt + 1) * LANES], g)
        u_ref[g] = _roll_lanes(ug, -SSM_CH * g)

    def head_norm(t, gain_row):
        sq = (t * t).astype(BF16)
        ms = jnp.concatenate([jnp.dot(sq[:, c:c + MXU_DIM], hsum_ref[...], preferred_element_type=F32)
                              for c in range(0, NA_WIDTH, MXU_DIM)], axis=-1)
        return t * lax.rsqrt(ms + RMS_EPS) * gain_row

    c0 = SSM_WIDTH
    q = head_norm(proj(c0, NA_WIDTH), qg_ref[...]) * (NA_HEAD_DIM ** -0.5)
    k = head_norm(proj(c0 + NA_WIDTH, NA_WIDTH), kg_ref[...])
    v = proj(c0 + 2 * NA_WIDTH, NA_WIDTH)
    for p in range(HEAD_PAIRS):
        lanes = slice(p * LANES, (p + 1) * LANES)
        q_ref[p] = q[:, lanes].astype(q_ref.dtype)
        k_ref[p] = k[:, lanes].astype(k_ref.dtype)
        v_ref[p] = v[:, lanes].astype(v_ref.dtype)
    if tok_refs:
        for slot in range(tok_refs[0].shape[1]):
            tok_refs[0][:, slot] = k.reshape(MIX_TILE // SEQ, SEQ, NA_WIDTH)
            tok_refs[1][:, slot] = v.reshape(MIX_TILE // SEQ, SEQ, NA_WIDTH)

    a = _gelu(proj(c0 + 3 * NA_WIDTH, 2 * GM_WIDTH))
    u = a[:, :GM_WIDTH]
    vv = a[:, GM_WIDTH:]
    mu = jnp.mean(vv, axis=-1, keepdims=True)
    vc = vv - mu
    var = jnp.mean(vc * vc, axis=-1, keepdims=True)
    vln = (vc * lax.rsqrt(var + LN_EPS)).astype(BF16)
    lane = lax.broadcasted_iota(jnp.int32, (1, GM_WIDTH), 1)
    gw = GM_WIDTH // GM_GROUPS
    for ci in range(MIX_TILE // GM_CHUNK):
        rows = slice(ci * GM_CHUNK, (ci + 1) * GM_CHUNK)
        vch = vln[rows, :]
        sp = gmb_ref[...]
        for gi in range(GM_GROUPS):
            t = jnp.dot(ws_ref[gi], vch, preferred_element_type=F32)
            sp = sp + jnp.where((lane >= gi * gw) & (lane < (gi + 1) * gw), t, 0.0)
        gm_ref[rows, :] = (u[rows, :] * sp).astype(BF16)


def _inproj(x, mod, g, w, hsum, qg, kg, ws, gmb, *, l, seg0, seg_tokens, layer=None, caches=None):
    n_tok = x.shape[0]
    is_ctx = layer is not None
    kv_dtype = BF16 if is_ctx else F32
    n_rows = MIX_TILE // SSM_CHUNK
    pair_spec = pl.BlockSpec((HEAD_PAIRS, MIX_TILE, LANES), lambda i: (0, i, 0))
    out_specs = [
        pl.BlockSpec((SSM_GROUPS, n_rows, SSM_ROW), lambda i: (0, i, 0)),
        pair_spec, pair_spec, pair_spec,
        pl.BlockSpec((MIX_TILE, GM_WIDTH), lambda i: (i, 0)),
    ]
    out_shape = [
        jax.ShapeDtypeStruct((SSM_GROUPS, n_tok // SSM_CHUNK, SSM_ROW), F32),
        jax.ShapeDtypeStruct((HEAD_PAIRS, n_tok, LANES), BF16),
        jax.ShapeDtypeStruct((HEAD_PAIRS, n_tok, LANES), kv_dtype),
        jax.ShapeDtypeStruct((HEAD_PAIRS, n_tok, LANES), kv_dtype),
        jax.ShapeDtypeStruct((n_tok, GM_WIDTH), BF16),
    ]
    args = [x, mod, g, w, hsum, qg, kg, ws, gmb]
    in_specs = [
        pl.BlockSpec((MIX_TILE, D_MODEL), lambda i: (i, 0)),
        _mod_spec(l, seg0, seg_tokens, MIX_TILE),
        _layer_spec(g, l), _layer_spec(w, l),
        _const_spec((MXU_DIM, MXU_DIM)),
        _layer_spec(qg, l), _layer_spec(kg, l), _layer_spec(ws, l), _layer_spec(gmb, l),
    ]
    aliases = {}
    if is_ctx:
        bt = MIX_TILE // SEQ
        cache_shape = jax.ShapeDtypeStruct((n_tok // SEQ, DEPTH, SEQ, NA_WIDTH), F32)
        if caches is None:
            out_specs += [pl.BlockSpec((bt, DEPTH, SEQ, NA_WIDTH), lambda i: (i, 0, 0, 0))] * 2
        else:
            out_specs += [pl.BlockSpec((bt, 1, SEQ, NA_WIDTH), lambda i: (i, layer, 0, 0))] * 2
        out_shape += [cache_shape] * 2
        if caches is not None:
            aliases = {len(args): len(out_shape) - 2, len(args) + 1: len(out_shape) - 1}
            args += list(caches)
            in_specs += [pl.BlockSpec(memory_space=pl.ANY)] * 2
    return pl.pallas_call(
        functools.partial(_inproj_kernel, n_in=len(args), n_out=len(out_shape)),
        grid=(n_tok // MIX_TILE,),
        in_specs=in_specs,
        out_specs=out_specs,
        out_shape=out_shape,
        input_output_aliases=aliases,
        scratch_shapes=[pltpu.VMEM((MIX_TILE, D_MODEL), BF16), pltpu.VMEM((SSM_WIDTH // LANES, MIX_TILE, LANES), F32),
                        pltpu.VMEM((SSM_CHUNK, n_rows, SSM_ROW), F32)],
        compiler_params=_cparams(("arbitrary",)),
        name="inproj_ctx" if is_ctx else "inproj_lat",
    )(*args)


def _shift_rows(x, d, row, down):
    if down:
        return jnp.where(row >= d, pltpu.roll(x, d, axis=0), 0.0)
    return jnp.where(row < SUBLANES - d, pltpu.roll(x, SUBLANES - d, axis=0), 0.0)


def _scan_tile(w, wp, c, cp, cst, row, down):
    x, xp = w, wp
    for si, d in enumerate(SCAN_SHIFTS):
        a, b, bp = cst[3 * si], cst[3 * si + 1], cst[3 * si + 2]
        xs, xps = _shift_rows(x, d, row, down), _shift_rows(xp, d, row, down)
        x, xp = x + a * xs + b * xps, xp + a * xps + bp * xs
    a, b, bp = cst[N_SCAN_CONST - 3], cst[N_SCAN_CONST - 2], cst[N_SCAN_CONST - 1]
    s = x + a * c + b * cp
    sp = xp + a * cp + bp * c
    edge = 0 if down else SUBLANES - 1
    last = SUBLANES - 1 if down else 0
    seen = jnp.where(row == edge, c, _shift_rows(s, 1, row, down))
    c_new = jnp.broadcast_to(s[last:last + 1, :], s.shape)
    cp_new = jnp.broadcast_to(sp[last:last + 1, :], sp.shape)
    return seen, c_new, cp_new


def _ssm_kernel(u_ref, mp_ref, q_ref, cst_ref, dvec_ref, s0_ref, y_ref, fin_ref, w_ref, sq_ref, *, n_batch, n_chunks):
    gb = u_ref.shape[0]
    n_rows = n_batch * n_chunks
    row_blk = min(n_rows, 512)
    for g in range(gb):
        for r0 in range(0, n_rows, row_blk):
            rows = slice(r0, r0 + row_blk)
            ug = u_ref[g, rows, :]
            wy = jnp.dot(ug.astype(BF16), mp_ref[g], preferred_element_type=F32)
            y_ref[g, rows, :] = wy[:, :SSM_ROW] + ug * dvec_ref[g]
            w_ref[g, rows, :] = wy[:, SSM_ROW:]

    n_tiles = n_chunks // SUBLANES
    row = lax.broadcasted_iota(jnp.int32, (SUBLANES, LANES), 0)

    def tile_step(b, kt, carry):
        out = []
        for g in range(gb):
            cf, cfp, cb, cbp = carry[4 * g:4 * g + 4]
            rf = pl.multiple_of(b * n_chunks + kt * SUBLANES, SUBLANES)
            rb = pl.multiple_of(b * n_chunks + (n_tiles - 1 - kt) * SUBLANES, SUBLANES)
            wf = w_ref[g, pl.ds(rf, SUBLANES), :]
            wb = w_ref[g, pl.ds(rb, SUBLANES), :]
            cst_f = [cst_ref[g, 0, i] for i in range(N_SCAN_CONST)]
            cst_b = [cst_ref[g, 1, i] for i in range(N_SCAN_CONST)]
            seen_f, cf, cfp = _scan_tile(wf[:, 0:LANES], wf[:, LANES:2 * LANES], cf, cfp, cst_f, row, True)
            seen_b, cb, cbp = _scan_tile(wb[:, 2 * LANES:3 * LANES], wb[:, 3 * LANES:], cb, cbp, cst_b, row, False)
            sq_ref[g, pl.ds(rf, SUBLANES), 0:LANES] = seen_f
            sq_ref[g, pl.ds(rb, SUBLANES), LANES:2 * LANES] = seen_b
            out += [cf, cfp, cb, cbp]
        return tuple(out)

    def batch_body(b, _):
        carry = []
        for g in range(gb):
            for i in range(4):
                carry.append(jnp.broadcast_to(s0_ref[g, i, pl.ds(b, 1), :], (SUBLANES, LANES)))
        carry = tuple(carry)
        if n_tiles <= 2:
            for kt in range(n_tiles):
                carry = tile_step(b, kt, carry)
        else:
            carry = lax.fori_loop(0, n_tiles, lambda kt, c: tile_step(b, kt, c), carry)
        for g in range(gb):
            fin_ref[g, 0, pl.ds(b, 1), :] = carry[4 * g][0:1, :]
            fin_ref[g, 1, pl.ds(b, 1), :] = carry[4 * g + 2][0:1, :]
        return 0

    lax.fori_loop(0, n_batch, batch_body, 0)

    for g in range(gb):
        for r0 in range(0, n_rows, row_blk):
            rows = slice(r0, r0 + row_blk)
            y_ref[g, rows, :] += jnp.dot(sq_ref[g, rows, :].astype(BF16), q_ref[g], preferred_element_type=F32)


def _ssm_core(u, mp, q, cst, dvec, s0, *, l, n_batch, n_chunks):
    n_rows = n_batch * n_chunks
    gb = SSM_GROUP_BLOCK
    g3 = lambda i: (i, 0, 0)
    g4 = lambda i: (i, 0, 0, 0)
    return pl.pallas_call(
        functools.partial(_ssm_kernel, n_batch=n_batch, n_chunks=n_chunks),
        grid=(SSM_GROUPS // gb,),
        in_specs=[
            pl.BlockSpec((gb, n_rows, SSM_ROW), g3),
            pl.BlockSpec((None, gb, SSM_ROW, SSM_ROW + SSM_W_COLS), lambda i: (l, i, 0, 0)),
            pl.BlockSpec((None, gb, 2 * LANES, SSM_ROW), lambda i: (l, i, 0, 0)),
            pl.BlockSpec((None, gb, 2, N_SCAN_CONST, SUBLANES, LANES), lambda i: (l, i, 0, 0, 0, 0)),
            pl.BlockSpec((None, gb, 1, SSM_ROW), lambda i: (l, i, 0, 0)),
            pl.BlockSpec((gb, 4, n_batch, LANES), g4),
        ],
        out_specs=[
            pl.BlockSpec((gb, n_rows, SSM_ROW), g3),
            pl.BlockSpec((gb, 2, n_batch, LANES), g4),
        ],
        out_shape=[
            jax.ShapeDtypeStruct((SSM_GROUPS, n_rows, SSM_ROW), F32),
            jax.ShapeDtypeStruct((SSM_GROUPS, 2, n_batch, LANES), F32),
        ],
        scratch_shapes=[pltpu.VMEM((gb, n_rows, SSM_W_COLS), F32), pltpu.VMEM((gb, n_rows, 2 * LANES), F32)],
        compiler_params=_cparams(("arbitrary",)),
        name="ssm_core",
    )(u, mp, q, cst, dvec, s0)


def _cmul(a, b):
    return a[0] * b[0] - a[1] * b[1], a[0] * b[1] + a[1] * b[0]


_CHUNK_IDX = np.arange(SSM_CHUNK)


def _ssm_exponents():
    t = SSM_CHUNK
    idx = _CHUNK_IDX.astype(np.float32)
    both = lambda a: np.stack([a, a], axis=1)
    lane_step = idx - (t - 1) / 2
    tile_rows = np.arange(1, SUBLANES + 1, dtype=np.float32) * t
    tables = dict(
        one=both(np.ones(1, np.float32)),
        incr=np.stack([t - 1 - idx, idx], axis=1),
        seen=np.stack([idx + 1, t - idx], axis=1),
        resp_in=np.stack([-lane_step, lane_step], axis=1),
        resp_out=np.stack([lane_step, -lane_step], axis=1),
        shift=both(np.asarray(SCAN_SHIFTS, np.float32) * t),
        tile=np.stack([tile_rows, tile_rows[::-1]], axis=1),
    )
    spans, start = {}, 0
    for name, tab in tables.items():
        spans[name] = slice(start, start + len(tab))
        start += len(tab)
    return np.concatenate(list(tables.values()), axis=0), spans


def _ssm_params(p):
    t = SSM_CHUNK
    n_l = p['ssm_lambda_re'].shape[0]
    lam = (p['ssm_lambda_re'].astype(F32), p['ssm_lambda_im'].astype(F32))
    dt = jnp.exp(p['ssm_log_dt'].astype(F32))[..., None]
    arg = (lam[0] * dt, lam[1] * dt)
    expo, span = _ssm_exponents()
    n = jnp.asarray(expo)[:, None, :, None, None]
    mag = jnp.exp(n * arg[0][None])
    pw = (mag * jnp.cos(n * arg[1][None]), mag * jnp.sin(n * arg[1][None]))
    take = lambda name: [x[span[name]] for x in pw]
    lanes_last = lambda xs: [jnp.moveaxis(x, 0, -1) for x in xs]

    lbar = [x[0] for x in take('one')]
    num = (lbar[0] - 1.0, lbar[1])
    den = lam[0] * lam[0] + lam[1] * lam[1]
    zoh = ((num[0] * lam[0] + num[1] * lam[1]) / den, (num[1] * lam[0] - num[0] * lam[1]) / den)
    b_mat = (p['ssm_b_re'].astype(F32), p['ssm_b_im'].astype(F32))
    bbar = _cmul((zoh[0][..., None], zoh[1][..., None]), b_mat)
    c_mat = [jnp.swapaxes(p[k].astype(F32), -1, -2) for k in ('ssm_c_re', 'ssm_c_im')]
    ct = [jnp.tile(x, (1, 1, 1, 1, t)) for x in c_mat]
    bt = [jnp.tile(x, (1, 1, 1, 1, t)) for x in bbar]

    per_lane = lambda name: [jnp.repeat(x, SSM_CH, axis=-1) for x in lanes_last(take(name))]
    xm = _cmul(per_lane('resp_in'), bt)
    ym = _cmul(per_lane('resp_out'), ct)
    resp = jnp.einsum('ldgpx,ldgpy->ldgxy', jnp.concatenate([xm[0], -xm[1]], axis=3),
                      jnp.concatenate([ym[0], ym[1]], axis=3), precision='highest')
    steps_x = np.repeat(_CHUNK_IDX, SSM_CH)
    causal = np.stack([steps_x[None, :] >= steps_x[:, None], steps_x[:, None] >= steps_x[None, :]])
    mats = jnp.sum(jnp.where(jnp.asarray(causal)[None, :, None], resp, 0.0), axis=1)

    incr = _cmul([x[..., None] for x in take('incr')], [x[None] for x in bbar])
    incr = [jnp.transpose(x, (1, 2, 3, 0, 5, 4)) for x in incr]
    pcat = jnp.concatenate([incr[0], incr[1], incr[1], incr[0]], axis=-1)
    pcat = jnp.concatenate([pcat[:, 0], pcat[:, 1]], axis=-1).reshape(n_l, SSM_GROUPS, SSM_ROW, SSM_W_COLS)
    mp = jnp.concatenate([mats, pcat], axis=-1).astype(BF16)

    qm = _cmul(ct, per_lane('seen'))
    q = jnp.concatenate([qm[0][:, 0], -qm[1][:, 0], qm[0][:, 1], -qm[1][:, 1]], axis=2).astype(BF16)

    def forms(name, rows):
        pr, pi = [jnp.moveaxis(x, 0, 3) for x in take(name)]
        f = jnp.stack([jnp.concatenate([pr, pr], -1), jnp.concatenate([-pi, pi], -1),
                       jnp.concatenate([pi, -pi], -1)], axis=4)
        return f if rows else jnp.broadcast_to(
            f.reshape(n_l, 2, SSM_GROUPS, 3 * len(SCAN_SHIFTS), 1, LANES),
            (n_l, 2, SSM_GROUPS, 3 * len(SCAN_SHIFTS), SUBLANES, LANES))
    per_row = jnp.swapaxes(forms('tile', True), 3, 4)
    cst = jnp.swapaxes(jnp.concatenate([forms('shift', False), per_row], axis=3), 1, 2)
    dvec = jnp.tile(p['ssm_d'].astype(F32).reshape(n_l, SSM_GROUPS, 1, SSM_CH), (1, 1, 1, SSM_CHUNK))
    return mp, q, cst, dvec


def _ssm_states_in(state_ssm):
    st = jnp.transpose(state_ssm.astype(F32), (1, 3, 2, 0, 4, 5))
    re, im = st[..., 0], st[..., 1]
    ri, ir = jnp.concatenate([re, im], -1), jnp.concatenate([im, re], -1)
    return jnp.stack([ri[:, :, 0], ir[:, :, 0], ri[:, :, 1], ir[:, :, 1]], axis=2)


def _ssm_states_out(fins):
    fin = jnp.stack(fins, axis=0)
    fin = jnp.stack([fin[..., :SSM_STATE], fin[..., SSM_STATE:]], axis=-1)
    return jnp.transpose(fin, (3, 0, 2, 1, 4, 5))


def _head_lane_mask(hh):
    lane = lax.broadcasted_iota(jnp.int32, (1, LANES), 1)
    return (lane >= hh * NA_HEAD_DIM) & (lane < (hh + 1) * NA_HEAD_DIM)


_NT_DIMS = (((1,), (1,)), ((), ()))


def _ctx_attn_kernel(q_ref, k_ref, v_ref, o_ref):
    for p in range(HEAD_PAIRS):
        qp, kp, vp = q_ref[p], k_ref[p], v_ref[p]
        lm0 = _head_lane_mask(0)
        zero = jnp.zeros_like(qp)
        q2 = jnp.concatenate([jnp.where(lm0, qp, zero), jnp.where(lm0, zero, qp)], axis=0)
        s = lax.dot_general(q2, kp, _NT_DIMS, preferred_element_type=F32)
        e = jnp.exp(s - jnp.max(s, axis=-1, keepdims=True))
        o = jnp.dot(e.astype(BF16), vp, preferred_element_type=F32) / jnp.sum(e, axis=-1, keepdims=True)
        o_ref[:, p * LANES:(p + 1) * LANES] = jnp.where(lm0, o[:SEQ, :], o[SEQ:, :]).astype(BF16)


def _ctx_attn(q, k, v, n_batch):
    spec = pl.BlockSpec((HEAD_PAIRS, SEQ, LANES), lambda b: (0, b, 0))
    return pl.pallas_call(
        _ctx_attn_kernel,
        grid=(n_batch,),
        in_specs=[spec, spec, spec],
        out_specs=pl.BlockSpec((SEQ, NA_WIDTH), lambda b: (b, 0)),
        out_shape=jax.ShapeDtypeStruct((n_batch * SEQ, NA_WIDTH), BF16),
        compiler_params=_cparams(("arbitrary",)),
        name="ctx_attn",
    )(q, k, v)


def _na_kernel(q_ref, k_ref, v_ref, kc_ref, vc_ref, t2_ref, rm_ref, o_ref, qb_ref, os_ref, tb_ref):
    rt = pl.program_id(1)
    n_rt = pl.num_programs(1)

    @pl.when((pl.program_id(0) == 0) & (rt == 0))
    def _():
        for h in range(NA_HEADS):
            for n in range(NA_COL_BLOCKS):
                for rl in range(NA_Q_ROWS):
                    lane0 = (NA_OFF_START - rl) * NA_KCOLS
                    tb_ref[h, n, rl * NA_COLS:(rl + 1) * NA_COLS, :] = (
                        t2_ref[h, n, :, lane0:lane0 + NA_K_ROWS * NA_KCOLS])

    r0 = rt * NA_Q_ROWS
    ttype = jnp.where(rt == 0, 0, jnp.where(rt == n_rt - 1, 2, 1))
    rmask = rm_ref[ttype]
    blk_q = NA_Q_ROWS * NA_COLS

    def pair_body(p, carry):
        lm0 = _head_lane_mask(0)
        for n in range(NA_COL_BLOCKS):
            for rl in range(NA_Q_ROWS):
                qrow = q_ref[p, rl * GRID_W + n * NA_COLS:rl * GRID_W + (n + 1) * NA_COLS, :]
                zero = jnp.zeros_like(qrow)
                base = 2 * n * blk_q + rl * NA_COLS
                qb_ref[base:base + NA_COLS, :] = jnp.where(lm0, qrow, zero)
                qb_ref[base + blk_q:base + blk_q + NA_COLS, :] = jnp.where(lm0, zero, qrow)
        kcp, vcp = kc_ref[0, p], vc_ref[0, p]
        sc_all = lax.dot_general(qb_ref[...], kcp, _NT_DIMS, preferred_element_type=F32)
        for n in range(NA_COL_BLOCKS):
            kparts, vparts = [], []
            for i in range(NA_K_ROWS):
                kr = jnp.clip(r0 - (NA_K_ROWS - NA_Q_ROWS) // 2 + i, 0, GRID_W - 1)
                start = pl.multiple_of(kr * GRID_W + NA_KC0[n], 8)
                kparts.append(k_ref[p, pl.ds(start, NA_KCOLS), :])
                vparts.append(v_ref[p, pl.ds(start, NA_KCOLS), :])
            kblk = jnp.concatenate(kparts, axis=0).astype(BF16)
            vblk = jnp.concatenate(vparts, axis=0).astype(BF16)
            rows2 = slice(2 * n * blk_q, 2 * (n + 1) * blk_q)
            s = lax.dot_general(qb_ref[rows2, :], kblk, _NT_DIMS, preferred_element_type=F32)
            s = s + jnp.concatenate([tb_ref[2 * p, n] + rmask, tb_ref[2 * p + 1, n] + rmask], axis=0)
            sc = sc_all[rows2, :]
            m = jnp.maximum(jnp.max(s, axis=-1, keepdims=True), jnp.max(sc, axis=-1, keepdims=True))
            e = jnp.exp(s - m)
            ec = jnp.exp(sc - m)
            den = jnp.sum(e, axis=-1, keepdims=True) + jnp.sum(ec, axis=-1, keepdims=True)
            o = (jnp.dot(e.astype(BF16), vblk, preferred_element_type=F32)
                 + jnp.dot(ec.astype(BF16), vcp, preferred_element_type=F32)) / den
            oblk = jnp.where(lm0, o[:blk_q, :], o[blk_q:, :])
            for rl in range(NA_Q_ROWS):
                os_ref[p, rl * GRID_W + n * NA_COLS:rl * GRID_W + (n + 1) * NA_COLS, :] = (
                    oblk[rl * NA_COLS:(rl + 1) * NA_COLS, :])
        return carry

    lax.fori_loop(0, HEAD_PAIRS, pair_body, 0)
    for p in range(HEAD_PAIRS):
        o_ref[:, p * LANES:(p + 1) * LANES] = os_ref[p].astype(BF16)


def _na_attn(q, k, v, kc, vc, tb, rm, n_batch, *, l):
    tile_tok = NA_Q_ROWS * GRID_W
    n_rt = DEC_SEQ // tile_tok
    blk_q, blk_k = NA_Q_ROWS * NA_COLS, NA_K_ROWS * NA_KCOLS
    return pl.pallas_call(
        _na_kernel,
        grid=(n_batch, n_rt),
        in_specs=[
            pl.BlockSpec((HEAD_PAIRS, tile_tok, LANES), lambda b, r: (0, b * n_rt + r, 0)),
            pl.BlockSpec((HEAD_PAIRS, DEC_SEQ, LANES), lambda b, r: (0, b, 0)),
            pl.BlockSpec((HEAD_PAIRS, DEC_SEQ, LANES), lambda b, r: (0, b, 0)),
            pl.BlockSpec((1, None, HEAD_PAIRS, PAST_LEN, LANES), lambda b, r: (b, l, 0, 0, 0)),
            pl.BlockSpec((1, None, HEAD_PAIRS, PAST_LEN, LANES), lambda b, r: (b, l, 0, 0, 0)),
            _layer_spec(tb, l),
            _const_spec(rm.shape),
        ],
        out_specs=pl.BlockSpec((tile_tok, NA_WIDTH), lambda b, r: (b * n_rt + r, 0)),
        out_shape=jax.ShapeDtypeStruct((n_batch * DEC_SEQ, NA_WIDTH), BF16),
        scratch_shapes=[
            pltpu.VMEM((2 * tile_tok, LANES), BF16),
            pltpu.VMEM((HEAD_PAIRS, tile_tok, LANES), F32),
            pltpu.VMEM((NA_HEADS, NA_COL_BLOCKS, blk_q, blk_k), F32)],
        compiler_params=_cparams(("arbitrary", "arbitrary")),
        name="na_attn",
    )(q, k, v, kc, vc, tb, rm)


def _na_tables(rpb):
    n = np.arange(NA_COL_BLOCKS)
    cc = np.arange(NA_COLS)
    kk = np.arange(NA_KCOLS)
    rl = np.arange(NA_Q_ROWS)
    ki = np.arange(NA_K_ROWS)
    n_dr, n_dc = 2 * NA_MAX_ROWS - 1, 2 * NA_COLS - 1
    c = n[:, None] * NA_COLS + cc[None, :]
    kcol = np.asarray(NA_KC0)[:, None] + kk[None, :]
    dc = np.clip(kcol[:, None, :] - c[:, :, None], -(NA_COLS - 1), NA_COLS - 1) + (NA_COLS - 1)
    cs = np.clip(c - NA_COLS // 2, 0, GRID_W - NA_COLS)
    col_ok = (kcol[:, None, :] >= cs[:, :, None]) & (kcol[:, None, :] < cs[:, :, None] + NA_COLS)
    half = (NA_K_ROWS - NA_Q_ROWS) // 2
    dr = np.clip(ki[None, :] - half - rl[:, None] + (NA_MAX_ROWS - 1), 0, n_dr - 1)
    oh_dc = jnp.asarray(dc[None] == np.arange(n_dc)[:, None, None, None], F32)
    off = np.clip(np.arange(NA_OFF_ROWS) - NA_OFF_PAD, 0, n_dr - 1)
    assert np.array_equal(dr, off[ki[None] + (NA_OFF_START - rl)[:, None]])
    oh_off = jnp.asarray(off[None, :] == np.arange(n_dr)[:, None], F32)
    rows_p = jnp.einsum('lhab,aA->lhAb', rpb.astype(F32), oh_off, precision='highest')
    cols = jnp.einsum('lhAb,bnck->lhncAk', rows_p, oh_dc, precision='highest')
    cols = jnp.where(jnp.asarray(col_ok)[None, None, :, :, None, :], cols, NEG_INF)
    tb = cols.reshape(-1, NA_HEADS, NA_COL_BLOCKS, NA_COLS, NA_OFF_ROWS * NA_KCOLS)
    rows = GRID_W
    masks = []
    for r0 in (0, NA_Q_ROWS, rows - NA_Q_ROWS):
        r = r0 + rl
        rs = np.clip(r - NA_MAX_ROWS // 2, 0, rows - NA_MAX_ROWS)
        kr = r0 - half + ki
        ok = (kr[None, :] >= rs[:, None]) & (kr[None, :] < rs[:, None] + NA_MAX_ROWS)
        full = np.broadcast_to(ok[:, None, :, None], (NA_Q_ROWS, NA_COLS, NA_K_ROWS, NA_KCOLS))
        masks.append(np.where(full, 0.0, NEG_INF).reshape(NA_Q_ROWS * NA_COLS, NA_K_ROWS * NA_KCOLS))
    rm = jnp.asarray(np.stack(masks), F32)
    return tb, rm


def _outproj_kernel(x_ref, mod_ref, y_ref, na_ref, gm_ref, gw_ref, gb_ref, w_ref, o_ref, yr_ref, ys_ref, cat_ref):
    gate = mod_ref[0, 5:6, :]
    n_rows = MIX_TILE // SSM_CHUNK
    for g in range(SSM_GROUPS):
        yr_ref[g] = _roll_lanes(y_ref[g], SSM_CH * g)
    for t in range(SSM_CHUNK):
        yt = _merge_segments(lambda src, lt: yr_ref[src, :, lt * LANES:(lt + 1) * LANES], t)
        yt = _roll_lanes(yt, -SSM_CH * t)
        for j in range(SSM_WIDTH // LANES):
            ys_ref[j, pl.ds(t, n_rows, stride=SSM_CHUNK), :] = yt[:, j * LANES:(j + 1) * LANES]
    y = _gelu(jnp.concatenate([ys_ref[j] for j in range(SSM_WIDTH // LANES)], axis=-1))
    z = jnp.dot(y.astype(BF16), gw_ref[...], preferred_element_type=F32) + gb_ref[...]
    cat_ref[:, 0:SSM_WIDTH] = (y * _sigmoid(z)).astype(BF16)
    cat_ref[:, SSM_WIDTH:SSM_WIDTH + NA_WIDTH] = na_ref[...]
    cat_ref[:, SSM_WIDTH + NA_WIDTH:] = gm_ref[...]
    mix = jnp.dot(cat_ref[...], w_ref[...], preferred_element_type=F32)
    o_ref[...] = x_ref[...] + gate * mix


def _outproj(x, mod, y, na, gm, gw, gb, w, *, l, seg0, seg_tokens):
    n_tok = x.shape[0]
    n_rows = MIX_TILE // SSM_CHUNK
    tok = lambda width: pl.BlockSpec((MIX_TILE, width), lambda i: (i, 0))
    return pl.pallas_call(
        _outproj_kernel,
        grid=(n_tok // MIX_TILE,),
        in_specs=[
            tok(D_MODEL),
            _mod_spec(l, seg0, seg_tokens, MIX_TILE),
            pl.BlockSpec((SSM_GROUPS, n_rows, SSM_ROW), lambda i: (0, i, 0)),
            tok(NA_WIDTH), tok(GM_WIDTH),
            _layer_spec(gw, l), _layer_spec(gb, l), _layer_spec(w, l),
        ],
        out_specs=tok(D_MODEL),
        out_shape=jax.ShapeDtypeStruct((n_tok, D_MODEL), F32),
        scratch_shapes=[pltpu.VMEM((SSM_GROUPS, n_rows, SSM_ROW), F32),
                        pltpu.VMEM((SSM_WIDTH // LANES, MIX_TILE, LANES), F32),
                        pltpu.VMEM((MIX_TILE, D_MODEL), BF16)],
        compiler_params=_cparams(("arbitrary",)),
        name="outproj",
    )(x, mod, y, na, gm, gw, gb, w)


def _prep(p):
    heads = np.arange(MXU_DIM) // NA_HEAD_DIM
    hsum = jnp.asarray((heads[:, None] == heads[None, :]) / NA_HEAD_DIM, BF16)
    gmb = jnp.repeat(jnp.swapaxes(p['gm_bs'], 1, 2).astype(F32), GM_WIDTH // GM_GROUPS, axis=2)
    tb, rm = _na_tables(p['na_rpb'])
    row = lambda a: a.astype(F32).reshape(DEPTH, 1, -1)
    return dict(
        g1=row(p['norm_ffn1']), g2=row(p['norm_mix']), g3=row(p['norm_ffn2']),
        f1_in=p['ffn1_w_in'].astype(BF16), f1_out=p['ffn1_w_out'].astype(BF16),
        f2_in=p['ffn2_w_in'].astype(BF16), f2_out=p['ffn2_w_out'].astype(BF16),
        w_in=p['w_in'].astype(BF16), w_out=p['w_out'].astype(BF16),
        hsum=hsum,
        qg=row(jnp.tile(p['na_q_norm'], (1, NA_HEADS))), kg=row(jnp.tile(p['na_k_norm'], (1, NA_HEADS))),
        ws=p['gm_ws'].astype(BF16), gmb=gmb,
        ssm=_ssm_params(p),
        glu_w=p['ssm_glu_w'].astype(BF16), glu_b=row(p['ssm_glu_b']),
        tb=tb, rm=rm,
    )


def _pairs_major(t):
    b, n_l, s = t.shape[:3]
    return jnp.transpose(t.reshape(b, n_l, s, HEAD_PAIRS, LANES), (0, 1, 3, 2, 4)).astype(BF16)


def _trunk_layer(x, mod, w, l, *, n_batch, seq_len, caches=None, is_ctx=False, ctx_kv=None, ssm_init=None):
    seg = dict(l=l, seg0=0 if is_ctx else 1, seg_tokens=x.shape[0] if is_ctx else seq_len)
    x = _ffn(x, mod, w['g1'], w['f1_in'], w['f1_out'], mod_base=0, **seg)
    outs = _inproj(x, mod, w['g2'], w['w_in'], w['hsum'], w['qg'], w['kg'], w['ws'], w['gmb'],
                   layer=l if is_ctx else None, caches=caches, **seg)
    u, q, k, v, gm = outs[:5]
    s0 = jnp.zeros((SSM_GROUPS, 4, n_batch, LANES), F32) if is_ctx else ssm_init[l]
    y, fin = _ssm_core(u, *w['ssm'], s0, l=l, n_batch=n_batch, n_chunks=seq_len // SSM_CHUNK)
    if is_ctx:
        na = _ctx_attn(q, k, v, n_batch)
    else:
        na = _na_attn(q, k, v, ctx_kv[0], ctx_kv[1], w['tb'], w['rm'], n_batch, l=l)
    x = _outproj(x, mod, y, na, gm, w['glu_w'], w['glu_b'], w['w_out'], **seg)
    x = _ffn(x, mod, w['g3'], w['f2_in'], w['f2_out'], mod_base=6, **seg)
    return x, (tuple(outs[5:]), fin)


def kernel(x_prompt, x_sample, c, cache_k, cache_v, state_ssm, c_ctx, w_ada, b_ada, norm_ffn1, ffn1_w_in, ffn1_w_out, norm_mix, w_in, w_out, ssm_lambda_re, ssm_lambda_im, ssm_log_dt, ssm_b_re, ssm_b_im, ssm_c_re, ssm_c_im, ssm_d, ssm_glu_w, ssm_glu_b, na_q_norm, na_k_norm, na_rpb, gm_ws, gm_bs, norm_ffn2, ffn2_w_in, ffn2_w_out):
    p = dict(norm_ffn1=norm_ffn1, ffn1_w_in=ffn1_w_in, ffn1_w_out=ffn1_w_out, norm_mix=norm_mix, w_in=w_in,
             w_out=w_out, ssm_lambda_re=ssm_lambda_re, ssm_lambda_im=ssm_lambda_im, ssm_log_dt=ssm_log_dt,
             ssm_b_re=ssm_b_re, ssm_b_im=ssm_b_im, ssm_c_re=ssm_c_re, ssm_c_im=ssm_c_im, ssm_d=ssm_d,
             ssm_glu_w=ssm_glu_w, ssm_glu_b=ssm_glu_b, na_q_norm=na_q_norm, na_k_norm=na_k_norm, na_rpb=na_rpb,
             gm_ws=gm_ws, gm_bs=gm_bs, norm_ffn2=norm_ffn2, ffn2_w_in=ffn2_w_in, ffn2_w_out=ffn2_w_out)
    batch, dec_batch = x_prompt.shape[0], x_sample.shape[0]
    cond8 = jnp.zeros((8, D_MODEL), F32).at[0].set(c_ctx).at[1:1 + dec_batch].set(c)
    mod = _adaln(cond8, w_ada, b_ada).reshape(DEPTH, 8, N_MOD, D_MODEL)

    xp = x_prompt.reshape(batch * SEQ, D_MODEL)
    xs = x_sample.reshape(dec_batch * DEC_SEQ, D_MODEL)
    w = _prep(p)
    ctx_kv = (_pairs_major(cache_k), _pairs_major(cache_v))
    ssm_init = _ssm_states_in(state_ssm)
    caches, fins = None, []
    for l in range(DEPTH):
        xp, (caches, fin) = _trunk_layer(xp, mod, w, l, n_batch=batch, seq_len=SEQ, caches=caches, is_ctx=True)
        fins.append(fin)
        xs, _ = _trunk_layer(xs, mod, w, l, n_batch=dec_batch, seq_len=DEC_SEQ, ctx_kv=ctx_kv, ssm_init=ssm_init)
    cache_shape = (batch, DEPTH, SEQ, NA_HEADS, NA_HEAD_DIM)
    return (xp.reshape(batch, SEQ, D_MODEL), xs.reshape(dec_batch, DEC_SEQ, D_MODEL),
            caches[0].reshape(cache_shape), caches[1].reshape(cache_shape), _ssm_states_out(fins))
```

```python
import functools

import numpy as np
import jax
import jax.numpy as jnp
from jax import lax
from jax.experimental import pallas as pl
from jax.experimental.pallas import tpu as pltpu

D_MODEL = 1024
DEPTH = 2
SEQ = 256
DEC_SEQ = 4096
PAST_LEN = 256
GRID_W = 64
SSM_WIDTH = 256
SSM_CH = 16
SSM_GROUPS = 16
SSM_STATE = 64
NA_WIDTH = 512
NA_HEAD_DIM = 64
NA_HEADS = 8
NA_MAX_ROWS = 8
NA_COLS = 16
GM_WIDTH = 256
GM_GROUPS = 4
GM_CHUNK = 128
D_FF = 2816
N_MOD = 9
RMS_EPS = 1e-6
LN_EPS = 1e-5
NEG_INF = -1e30

F32 = jnp.float32
BF16 = jnp.bfloat16

LANES = 128
SUBLANES = 8
MXU_DIM = 256
VMEM_LIMIT_BYTES = 56 * 1024 * 1024

FFN_TILE = 1024
MIX_TILE = 1024
OUT_TILE = 512
FF_CHUNK = MXU_DIM
N_FF_CHUNKS = D_FF // FF_CHUNK
SSM_CHUNK = 16
SSM_ROW = SSM_CHUNK * SSM_CH
SSM_GROUP_BLOCK = 4
SSM_W_COLS = 4 * LANES
SCAN_SHIFTS = (1, 2, 4)
N_SCAN_CONST = 3 * len(SCAN_SHIFTS) + 3
HEAD_PAIRS = NA_HEADS // 2
NA_Q_ROWS = 8
NA_K_ROWS = 16
NA_KCOLS = 2 * NA_COLS
NA_COL_BLOCKS = GRID_W // NA_COLS
NA_KC0 = (0, 8, 24, 32)
SEGS_PER_TILE = LANES // SSM_CH
NA_OFF_PAD = (NA_Q_ROWS - 1) - ((NA_MAX_ROWS - 1) - (NA_K_ROWS - NA_Q_ROWS) // 2)
NA_OFF_START = NA_Q_ROWS - 1
NA_OFF_ROWS = NA_OFF_START + NA_K_ROWS


def _silu(x):
    return x * (1.0 / (1.0 + jnp.exp(-x)))


def _sigmoid(x):
    return 1.0 / (1.0 + jnp.exp(-x))


def _gelu(x):
    return 0.5 * x * (1.0 + jnp.tanh(0.7978845608028654 * (x + 0.044715 * (x * x * x))))


def _cparams(sem):
    return pltpu.CompilerParams(dimension_semantics=sem, vmem_limit_bytes=VMEM_LIMIT_BYTES)


def _const_spec(shape):
    nd = len(shape)
    return pl.BlockSpec(shape, lambda *_: (0,) * nd, pipeline_mode=pl.Buffered(1))


def _layer_spec(stacked, l):
    nd = stacked.ndim
    return pl.BlockSpec((None,) + stacked.shape[1:], lambda *_: (l,) + (0,) * (nd - 1), pipeline_mode=pl.Buffered(1))


def _ada_kernel(c_ref, w_ref, b_ref, o_ref):
    s = _silu(c_ref[...]).astype(BF16)
    w = w_ref[0].astype(BF16)
    o_ref[0] = jnp.dot(s, w, preferred_element_type=F32) + b_ref[0]


def _adaln(cond8, w_ada, b_ada):
    tn = D_MODEL
    ncol = N_MOD * D_MODEL
    return pl.pallas_call(
        _ada_kernel,
        grid=(DEPTH, ncol // tn),
        in_specs=[
            pl.BlockSpec((8, D_MODEL), lambda l, j: (0, 0)),
            pl.BlockSpec((1, D_MODEL, tn), lambda l, j: (l, 0, j)),
            pl.BlockSpec((1, 1, tn), lambda l, j: (l, 0, j)),
        ],
        out_specs=pl.BlockSpec((1, 8, tn), lambda l, j: (l, 0, j)),
        out_shape=jax.ShapeDtypeStruct((DEPTH, 8, ncol), F32),
        compiler_params=_cparams(("arbitrary", "arbitrary")),
        name="adaln",
    )(cond8, w_ada, b_ada.reshape(DEPTH, 1, ncol))


def _normed(x, g_row, shift, scale):
    ms = jnp.mean(x * x, axis=-1, keepdims=True)
    h = x * lax.rsqrt(ms + RMS_EPS) * g_row
    return h * (1.0 + scale) + shift


def _mod_spec(l, seg0, seg_tokens, tile):
    tiles_per_seg = seg_tokens // tile
    return pl.BlockSpec((None, 1, N_MOD, D_MODEL), lambda i: (l, seg0 + i // tiles_per_seg, 0, 0))


def _ffn_kernel(x_ref, mod_ref, g_ref, win_ref, wout_ref, o_ref, h_ref, act_ref, *, mod_base):
    shift = mod_ref[0, mod_base:mod_base + 1, :]
    scale = mod_ref[0, mod_base + 1:mod_base + 2, :]
    gate = mod_ref[0, mod_base + 2:mod_base + 3, :]
    h_ref[...] = _normed(x_ref[...], g_ref[...], shift, scale).astype(BF16)
    for j in range(N_FF_CHUNKS):
        cols = slice(j * FF_CHUNK, (j + 1) * FF_CHUNK)
        gate_j = jnp.dot(h_ref[...], win_ref[:, cols], preferred_element_type=F32)
        up_j = jnp.dot(h_ref[...], win_ref[:, D_FF + j * FF_CHUNK:D_FF + (j + 1) * FF_CHUNK], preferred_element_type=F32)
        act_ref[:, cols] = (_silu(gate_j) * up_j).astype(BF16)
    for n in range(D_MODEL // MXU_DIM):
        cols = slice(n * MXU_DIM, (n + 1) * MXU_DIM)
        y = jnp.dot(act_ref[...], wout_ref[:, cols], preferred_element_type=F32)
        o_ref[:, cols] = x_ref[:, cols] + 0.5 * gate[:, cols] * y


def _ffn(x, mod, g, w_in, w_out, *, l, mod_base, seg0, seg_tokens):
    n_tok = x.shape[0]
    return pl.pallas_call(
        functools.partial(_ffn_kernel, mod_base=mod_base),
        grid=(n_tok // FFN_TILE,),
        in_specs=[
            pl.BlockSpec((FFN_TILE, D_MODEL), lambda i: (i, 0)),
            _mod_spec(l, seg0, seg_tokens, FFN_TILE),
            _layer_spec(g, l), _layer_spec(w_in, l), _layer_spec(w_out, l),
        ],
        out_specs=pl.BlockSpec((FFN_TILE, D_MODEL), lambda i: (i, 0)),
        out_shape=jax.ShapeDtypeStruct((n_tok, D_MODEL), F32),
        scratch_shapes=[pltpu.VMEM((FFN_TILE, D_MODEL), BF16), pltpu.VMEM((FFN_TILE, D_FF), BF16)],
        compiler_params=_cparams(("arbitrary",)),
        name="ffn",
    )(x, mod, g, w_in, w_out)


def _segment_ids():
    return lax.broadcasted_iota(jnp.int32, (1, LANES), 1) // SSM_CH


def _merge_segments(load_piece, rot):
    seg = _segment_ids()
    tiles = []
    for lt in range(SSM_ROW // LANES):
        acc = None
        for jj in range(SEGS_PER_TILE):
            src = (lt * SEGS_PER_TILE + jj - rot) % SSM_CHUNK
            piece = load_piece(src, lt)
            acc = piece if acc is None else jnp.where(seg == jj, piece, acc)
        tiles.append(acc)
    return jnp.concatenate(tiles, axis=-1)


def _roll_lanes(x, shift):
    shift %= x.shape[-1]
    return pltpu.roll(x, shift, axis=x.ndim - 1) if shift else x


def _inproj_kernel(*refs, n_in, n_out):
    x_ref, mod_ref, g_ref, w_ref, hsum_ref, qg_ref, kg_ref, ws_ref, gmb_ref = refs[:9]
    u_ref, q_ref, k_ref, v_ref, gm_ref, *tok_refs = refs[n_in:n_in + n_out]
    h_ref, xs_ref, z_ref = refs[n_in + n_out:]
    shift = mod_ref[0, 3:4, :]
    scale = mod_ref[0, 4:5, :]
    h_ref[...] = _normed(x_ref[...], g_ref[...], shift, scale).astype(BF16)
    n_rows = MIX_TILE // SSM_CHUNK

    def proj(c0, width):
        return jnp.dot(h_ref[...], w_ref[:, c0:c0 + width], preferred_element_type=F32)

    xs = proj(0, SSM_WIDTH)
    for j in range(SSM_WIDTH // LANES):
        xs_ref[j] = xs[:, j * LANES:(j + 1) * LANES]
    for t in range(SSM_CHUNK):
        zt = jnp.concatenate([xs_ref[j, pl.ds(t, n_rows, stride=SSM_CHUNK), :] for j in range(SSM_WIDTH // LANES)],
                             axis=-1)
        z_ref[t] = _roll_lanes(zt, SSM_CH * t)
    for g in range(SSM_GROUPS):
        ug = _merge_segments(lambda src, lt: z_ref[src, :, lt * LANES:(lt + 1) * LANES], g)
        u_ref[g] = _roll_lanes(ug, -SSM_CH * g)

    def head_norm(t, gain_row):
        sq = (t * t).astype(BF16)
        ms = jnp.concatenate([jnp.dot(sq[:, c:c + MXU_DIM], hsum_ref[...], preferred_element_type=F32)
                              for c in range(0, NA_WIDTH, MXU_DIM)], axis=-1)
        return t * lax.rsqrt(ms + RMS_EPS) * gain_row

    c0 = SSM_WIDTH
    q = head_norm(proj(c0, NA_WIDTH), qg_ref[...]) * (NA_HEAD_DIM ** -0.5)
    k = head_norm(proj(c0 + NA_WIDTH, NA_WIDTH), kg_ref[...])
    v = proj(c0 + 2 * NA_WIDTH, NA_WIDTH)
    for p in range(HEAD_PAIRS):
        lanes = slice(p * LANES, (p + 1) * LANES)
        q_ref[p] = q[:, lanes].astype(q_ref.dtype)
        k_ref[p] = k[:, lanes].astype(k_ref.dtype)
        v_ref[p] = v[:, lanes].astype(v_ref.dtype)
    if tok_refs:
        for slot in range(tok_refs[0].shape[1]):
            tok_refs[0][:, slot] = k.reshape(MIX_TILE // SEQ, SEQ, NA_WIDTH)
            tok_refs[1][:, slot] = v.reshape(MIX_TILE // SEQ, SEQ, NA_WIDTH)

    a = _gelu(proj(c0 + 3 * NA_WIDTH, 2 * GM_WIDTH))
    u = a[:, :GM_WIDTH]
    vv = a[:, GM_WIDTH:]
    mu = jnp.mean(vv, axis=-1, keepdims=True)
    vc = vv - mu
    var = jnp.mean(vc * vc, axis=-1, keepdims=True)
    vln = (vc * lax.rsqrt(var + LN_EPS)).astype(BF16)
    lane = lax.broadcasted_iota(jnp.int32, (1, GM_WIDTH), 1)
    gw = GM_WIDTH // GM_GROUPS
    for ci in range(MIX_TILE // GM_CHUNK):
        rows = slice(ci * GM_CHUNK, (ci + 1) * GM_CHUNK)
        vch = vln[rows, :]
        sp = gmb_ref[...]
        for gi in range(GM_GROUPS):
            t = jnp.dot(ws_ref[gi], vch, preferred_element_type=F32)
            sp = sp + jnp.where((lane >= gi * gw) & (lane < (gi + 1) * gw), t, 0.0)
        gm_ref[rows, :] = (u[rows, :] * sp).astype(BF16)


def _inproj(x, mod, g, w, hsum, qg, kg, ws, gmb, *, l, seg0, seg_tokens, layer=None, caches=None):
    n_tok = x.shape[0]
    is_ctx = layer is not None
    kv_dtype = BF16 if is_ctx else F32
    n_rows = MIX_TILE // SSM_CHUNK
    pair_spec = pl.BlockSpec((HEAD_PAIRS, MIX_TILE, LANES), lambda i: (0, i, 0))
    out_specs = [
        pl.BlockSpec((SSM_GROUPS, n_rows, SSM_ROW), lambda i: (0, i, 0)),
        pair_spec, pair_spec, pair_spec,
        pl.BlockSpec((MIX_TILE, GM_WIDTH), lambda i: (i, 0)),
    ]
    out_shape = [
        jax.ShapeDtypeStruct((SSM_GROUPS, n_tok // SSM_CHUNK, SSM_ROW), F32),
        jax.ShapeDtypeStruct((HEAD_PAIRS, n_tok, LANES), BF16),
        jax.ShapeDtypeStruct((HEAD_PAIRS, n_tok, LANES), kv_dtype),
        jax.ShapeDtypeStruct((HEAD_PAIRS, n_tok, LANES), kv_dtype),
        jax.ShapeDtypeStruct((n_tok, GM_WIDTH), BF16),
    ]
    args = [x, mod, g, w, hsum, qg, kg, ws, gmb]
    in_specs = [
        pl.BlockSpec((MIX_TILE, D_MODEL), lambda i: (i, 0)),
        _mod_spec(l, seg0, seg_tokens, MIX_TILE),
        _layer_spec(g, l), _layer_spec(w, l),
        _const_spec((MXU_DIM, MXU_DIM)),
        _layer_spec(qg, l), _layer_spec(kg, l), _layer_spec(ws, l), _layer_spec(gmb, l),
    ]
    aliases = {}
    if is_ctx:
        bt = MIX_TILE // SEQ
        cache_shape = jax.ShapeDtypeStruct((n_tok // SEQ, DEPTH, SEQ, NA_WIDTH), F32)
        if caches is None:
            out_specs += [pl.BlockSpec((bt, DEPTH, SEQ, NA_WIDTH), lambda i: (i, 0, 0, 0))] * 2
        else:
            out_specs += [pl.BlockSpec((bt, 1, SEQ, NA_WIDTH), lambda i: (i, layer, 0, 0))] * 2
        out_shape += [cache_shape] * 2
        if caches is not None:
            aliases = {len(args): len(out_shape) - 2, len(args) + 1: len(out_shape) - 1}
            args += list(caches)
            in_specs += [pl.BlockSpec(memory_space=pl.ANY)] * 2
    return pl.pallas_call(
        functools.partial(_inproj_kernel, n_in=len(args), n_out=len(out_shape)),
        grid=(n_tok // MIX_TILE,),
        in_specs=in_specs,
        out_specs=out_specs,
        out_shape=out_shape,
        input_output_aliases=aliases,
        scratch_shapes=[pltpu.VMEM((MIX_TILE, D_MODEL), BF16), pltpu.VMEM((SSM_WIDTH // LANES, MIX_TILE, LANES), F32),
                        pltpu.VMEM((SSM_CHUNK, n_rows, SSM_ROW), F32)],
        compiler_params=_cparams(("arbitrary",)),
        name="inproj_ctx" if is_ctx else "inproj_lat",
    )(*args)


def _shift_rows(x, d, row, down):
    if down:
        return jnp.where(row >= d, pltpu.roll(x, d, axis=0), 0.0)
    return jnp.where(row < SUBLANES - d, pltpu.roll(x, SUBLANES - d, axis=0), 0.0)


def _scan_tile(w, wp, c, cp, cst, row, down):
    x, xp = w, wp
    for si, d in enumerate(SCAN_SHIFTS):
        a, b, bp = cst[3 * si], cst[3 * si + 1], cst[3 * si + 2]
        xs, xps = _shift_rows(x, d, row, down), _shift_rows(xp, d, row, down)
        x, xp = x + a * xs + b * xps, xp + a * xps + bp * xs
    a, b, bp = cst[N_SCAN_CONST - 3], cst[N_SCAN_CONST - 2], cst[N_SCAN_CONST - 1]
    s = x + a * c + b * cp
    sp = xp + a * cp + bp * c
    edge = 0 if down else SUBLANES - 1
    last = SUBLANES - 1 if down else 0
    seen = jnp.where(row == edge, c, _shift_rows(s, 1, row, down))
    c_new = jnp.broadcast_to(s[last:last + 1, :], s.shape)
    cp_new = jnp.broadcast_to(sp[last:last + 1, :], sp.shape)
    return seen, c_new, cp_new


def _ssm_kernel(u_ref, mp_ref, q_ref, cst_ref, dvec_ref, s0_ref, y_ref, fin_ref, w_ref, sq_ref, *, n_batch, n_chunks):
    gb = u_ref.shape[0]
    n_rows = n_batch * n_chunks
    row_blk = min(n_rows, 512)
    for g in range(gb):
        for r0 in range(0, n_rows, row_blk):
            rows = slice(r0, r0 + row_blk)
            ug = u_ref[g, rows, :]
            wy = jnp.dot(ug.astype(BF16), mp_ref[g], preferred_element_type=F32)
            y_ref[g, rows, :] = wy[:, :SSM_ROW] + ug * dvec_ref[g]
            w_ref[g, rows, :] = wy[:, SSM_ROW:]

    n_tiles = n_chunks // SUBLANES
    row = lax.broadcasted_iota(jnp.int32, (SUBLANES, LANES), 0)

    def tile_step(b, kt, carry):
        out = []
        for g in range(gb):
            cf, cfp, cb, cbp = carry[4 * g:4 * g + 4]
            rf = pl.multiple_of(b * n_chunks + kt * SUBLANES, SUBLANES)
            rb = pl.multiple_of(b * n_chunks + (n_tiles - 1 - kt) * SUBLANES, SUBLANES)
            wf = w_ref[g, pl.ds(rf, SUBLANES), :]
            wb = w_ref[g, pl.ds(rb, SUBLANES), :]
            cst_f = [cst_ref[g, 0, i] for i in range(N_SCAN_CONST)]
            cst_b = [cst_ref[g, 1, i] for i in range(N_SCAN_CONST)]
            seen_f, cf, cfp = _scan_tile(wf[:, 0:LANES], wf[:, LANES:2 * LANES], cf, cfp, cst_f, row, True)
            seen_b, cb, cbp = _scan_tile(wb[:, 2 * LANES:3 * LANES], wb[:, 3 * LANES:], cb, cbp, cst_b, row, False)
            sq_ref[g, pl.ds(rf, SUBLANES), 0:LANES] = seen_f
            sq_ref[g, pl.ds(rb, SUBLANES), LANES:2 * LANES] = seen_b
            out += [cf, cfp, cb, cbp]
        return tuple(out)

    def batch_body(b, _):
        carry = []
        for g in range(gb):
            for i in range(4):
                carry.append(jnp.broadcast_to(s0_ref[g, i, pl.ds(b, 1), :], (SUBLANES, LANES)))
        carry = tuple(carry)
        if n_tiles <= 2:
            for kt in range(n_tiles):
                carry = tile_step(b, kt, carry)
        else:
            carry = lax.fori_loop(0, n_tiles, lambda kt, c: tile_step(b, kt, c), carry)
        for g in range(gb):
            fin_ref[g, 0, pl.ds(b, 1), :] = carry[4 * g][0:1, :]
            fin_ref[g, 1, pl.ds(b, 1), :] = carry[4 * g + 2][0:1, :]
        return 0

    lax.fori_loop(0, n_batch, batch_body, 0)

    for g in range(gb):
        for r0 in range(0, n_rows, row_blk):
            rows = slice(r0, r0 + row_blk)
            y_ref[g, rows, :] += jnp.dot(sq_ref[g, rows, :].astype(BF16), q_ref[g], preferred_element_type=F32)


def _ssm_core(u, mp, q, cst, dvec, s0, *, l, n_batch, n_chunks):
    n_rows = n_batch * n_chunks
    gb = SSM_GROUP_BLOCK
    g3 = lambda i: (i, 0, 0)
    g4 = lambda i: (i, 0, 0, 0)
    return pl.pallas_call(
        functools.partial(_ssm_kernel, n_batch=n_batch, n_chunks=n_chunks),
        grid=(SSM_GROUPS // gb,),
        in_specs=[
            pl.BlockSpec((gb, n_rows, SSM_ROW), g3),
            pl.BlockSpec((None, gb, SSM_ROW, SSM_ROW + SSM_W_COLS), lambda i: (l, i, 0, 0)),
            pl.BlockSpec((None, gb, 2 * LANES, SSM_ROW), lambda i: (l, i, 0, 0)),
            pl.BlockSpec((None, gb, 2, N_SCAN_CONST, SUBLANES, LANES), lambda i: (l, i, 0, 0, 0, 0)),
            pl.BlockSpec((None, gb, 1, SSM_ROW), lambda i: (l, i, 0, 0)),
            pl.BlockSpec((gb, 4, n_batch, LANES), g4),
        ],
        out_specs=[
            pl.BlockSpec((gb, n_rows, SSM_ROW), g3),
            pl.BlockSpec((gb, 2, n_batch, LANES), g4),
        ],
        out_shape=[
            jax.ShapeDtypeStruct((SSM_GROUPS, n_rows, SSM_ROW), F32),
            jax.ShapeDtypeStruct((SSM_GROUPS, 2, n_batch, LANES), F32),
        ],
        scratch_shapes=[pltpu.VMEM((gb, n_rows, SSM_W_COLS), F32), pltpu.VMEM((gb, n_rows, 2 * LANES), F32)],
        compiler_params=_cparams(("arbitrary",)),
        name="ssm_core",
    )(u, mp, q, cst, dvec, s0)


def _cmul(a, b):
    return a[0] * b[0] - a[1] * b[1], a[0] * b[1] + a[1] * b[0]


_CHUNK_IDX = np.arange(SSM_CHUNK)


def _ssm_exponents():
    t = SSM_CHUNK
    idx = _CHUNK_IDX.astype(np.float32)
    both = lambda a: np.stack([a, a], axis=1)
    lane_step = idx - (t - 1) / 2
    tile_rows = np.arange(1, SUBLANES + 1, dtype=np.float32) * t
    tables = dict(
        one=both(np.ones(1, np.float32)),
        incr=np.stack([t - 1 - idx, idx], axis=1),
        seen=np.stack([idx + 1, t - idx], axis=1),
        resp_in=np.stack([-lane_step, lane_step], axis=1),
        resp_out=np.stack([lane_step, -lane_step], axis=1),
        shift=both(np.asarray(SCAN_SHIFTS, np.float32) * t),
        tile=np.stack([tile_rows, tile_rows[::-1]], axis=1),
    )
    spans, start = {}, 0
    for name, tab in tables.items():
        spans[name] = slice(start, start + len(tab))
        start += len(tab)
    return np.concatenate(list(tables.values()), axis=0), spans


def _ssm_params(p):
    t = SSM_CHUNK
    n_l = p['ssm_lambda_re'].shape[0]
    lam = (p['ssm_lambda_re'].astype(F32), p['ssm_lambda_im'].astype(F32))
    dt = jnp.exp(p['ssm_log_dt'].astype(F32))[..., None]
    arg = (lam[0] * dt, lam[1] * dt)
    expo, span = _ssm_exponents()
    n = jnp.asarray(expo)[:, None, :, None, None]
    mag = jnp.exp(n * arg[0][None])
    pw = (mag * jnp.cos(n * arg[1][None]), mag * jnp.sin(n * arg[1][None]))
    take = lambda name: [x[span[name]] for x in pw]
    lanes_last = lambda xs: [jnp.moveaxis(x, 0, -1) for x in xs]

    lbar = [x[0] for x in take('one')]
    num = (lbar[0] - 1.0, lbar[1])
    den = lam[0] * lam[0] + lam[1] * lam[1]
    zoh = ((num[0] * lam[0] + num[1] * lam[1]) / den, (num[1] * lam[0] - num[0] * lam[1]) / den)
    b_mat = (p['ssm_b_re'].astype(F32), p['ssm_b_im'].astype(F32))
    bbar = _cmul((zoh[0][..., None], zoh[1][..., None]), b_mat)
    c_mat = [jnp.swapaxes(p[k].astype(F32), -1, -2) for k in ('ssm_c_re', 'ssm_c_im')]
    ct = [jnp.tile(x, (1, 1, 1, 1, t)) for x in c_mat]
    bt = [jnp.tile(x, (1, 1, 1, 1, t)) for x in bbar]

    per_lane = lambda name: [jnp.repeat(x, SSM_CH, axis=-1) for x in lanes_last(take(name))]
    xm = _cmul(per_lane('resp_in'), bt)
    ym = _cmul(per_lane('resp_out'), ct)
    resp = jnp.einsum('ldgpx,ldgpy->ldgxy', jnp.concatenate([xm[0], -xm[1]], axis=3),
                      jnp.concatenate([ym[0], ym[1]], axis=3), precision='highest')
    steps_x = np.repeat(_CHUNK_IDX, SSM_CH)
    causal = np.stack([steps_x[None, :] >= steps_x[:, None], steps_x[:, None] >= steps_x[None, :]])
    mats = jnp.sum(jnp.where(jnp.asarray(causal)[None, :, None], resp, 0.0), axis=1)

    incr = _cmul([x[..., None] for x in take('incr')], [x[None] for x in bbar])
    incr = [jnp.transpose(x, (1, 2, 3, 0, 5, 4)) for x in incr]
    pcat = jnp.concatenate([incr[0], incr[1], incr[1], incr[0]], axis=-1)
    pcat = jnp.concatenate([pcat[:, 0], pcat[:, 1]], axis=-1).reshape(n_l, SSM_GROUPS, SSM_ROW, SSM_W_COLS)
    mp = jnp.concatenate([mats, pcat], axis=-1).astype(BF16)

    qm = _cmul(ct, per_lane('seen'))
    q = jnp.concatenate([qm[0][:, 0], -qm[1][:, 0], qm[0][:, 1], -qm[1][:, 1]], axis=2).astype(BF16)

    def forms(name, rows):
        pr, pi = [jnp.moveaxis(x, 0, 3) for x in take(name)]
        f = jnp.stack([jnp.concatenate([pr, pr], -1), jnp.concatenate([-pi, pi], -1),
                       jnp.concatenate([pi, -pi], -1)], axis=4)
        return f if rows else jnp.broadcast_to(
            f.reshape(n_l, 2, SSM_GROUPS, 3 * len(SCAN_SHIFTS), 1, LANES),
            (n_l, 2, SSM_GROUPS, 3 * len(SCAN_SHIFTS), SUBLANES, LANES))
    per_row = jnp.swapaxes(forms('tile', True), 3, 4)
    cst = jnp.swapaxes(jnp.concatenate([forms('shift', False), per_row], axis=3), 1, 2)
    dvec = jnp.tile(p['ssm_d'].astype(F32).reshape(n_l, SSM_GROUPS, 1, SSM_CH), (1, 1, 1, SSM_CHUNK))
    return mp, q, cst, dvec


def _ssm_states_in(state_ssm):
    st = jnp.transpose(state_ssm.astype(F32), (1, 3, 2, 0, 4, 5))
    re, im = st[..., 0], st[..., 1]
    ri, ir = jnp.concatenate([re, im], -1), jnp.concatenate([im, re], -1)
    return jnp.stack([ri[:, :, 0], ir[:, :, 0], ri[:, :, 1], ir[:, :, 1]], axis=2)


def _ssm_states_out(fins):
    fin = jnp.stack(fins, axis=0)
    fin = jnp.stack([fin[..., :SSM_STATE], fin[..., SSM_STATE:]], axis=-1)
    return jnp.transpose(fin, (3, 0, 2, 1, 4, 5))


def _head_lane_mask(hh):
    lane = lax.broadcasted_iota(jnp.int32, (1, LANES), 1)
    return (lane >= hh * NA_HEAD_DIM) & (lane < (hh + 1) * NA_HEAD_DIM)


_NT_DIMS = (((1,), (1,)), ((), ()))


def _ctx_attn_kernel(q_ref, k_ref, v_ref, o_ref):
    for p in range(HEAD_PAIRS):
        qp, kp, vp = q_ref[p], k_ref[p], v_ref[p]
        lm0 = _head_lane_mask(0)
        zero = jnp.zeros_like(qp)
        q2 = jnp.concatenate([jnp.where(lm0, qp, zero), jnp.where(lm0, zero, qp)], axis=0)
        s = lax.dot_general(q2, kp, _NT_DIMS, preferred_element_type=F32)
        e = jnp.exp(s - jnp.max(s, axis=-1, keepdims=True))
        o = jnp.dot(e.astype(BF16), vp, preferred_element_type=F32) / jnp.sum(e, axis=-1, keepdims=True)
        o_ref[:, p * LANES:(p + 1) * LANES] = jnp.where(lm0, o[:SEQ, :], o[SEQ:, :]).astype(BF16)


def _ctx_attn(q, k, v, n_batch):
    spec = pl.BlockSpec((HEAD_PAIRS, SEQ, LANES), lambda b: (0, b, 0))
    return pl.pallas_call(
        _ctx_attn_kernel,
        grid=(n_batch,),
        in_specs=[spec, spec, spec],
        out_specs=pl.BlockSpec((SEQ, NA_WIDTH), lambda b: (b, 0)),
        out_shape=jax.ShapeDtypeStruct((n_batch * SEQ, NA_WIDTH), BF16),
        compiler_params=_cparams(("arbitrary",)),
        name="ctx_attn",
    )(q, k, v)


def _na_kernel(q_ref, k_ref, v_ref, kc_ref, vc_ref, t2_ref, rm_ref, o_ref, qb_ref, os_ref, tb_ref):
    rt = pl.program_id(1)
    n_rt = pl.num_programs(1)

    @pl.when((pl.program_id(0) == 0) & (rt == 0))
    def _():
        for h in range(NA_HEADS):
            for n in range(NA_COL_BLOCKS):
                for rl in range(NA_Q_ROWS):
                    lane0 = (NA_OFF_START - rl) * NA_KCOLS
                    tb_ref[h, n, rl * NA_COLS:(rl + 1) * NA_COLS, :] = (
                        t2_ref[h, n, :, lane0:lane0 + NA_K_ROWS * NA_KCOLS])

    r0 = rt * NA_Q_ROWS
    ttype = jnp.where(rt == 0, 0, jnp.where(rt == n_rt - 1, 2, 1))
    rmask = rm_ref[ttype]
    blk_q = NA_Q_ROWS * NA_COLS

    def pair_body(p, carry):
        lm0 = _head_lane_mask(0)
        for n in range(NA_COL_BLOCKS):
            for rl in range(NA_Q_ROWS):
                qrow = q_ref[p, rl * GRID_W + n * NA_COLS:rl * GRID_W + (n + 1) * NA_COLS, :]
                zero = jnp.zeros_like(qrow)
                base = 2 * n * blk_q + rl * NA_COLS
                qb_ref[base:base + NA_COLS, :] = jnp.where(lm0, qrow, zero)
                qb_ref[base + blk_q:base + blk_q + NA_COLS, :] = jnp.where(lm0, zero, qrow)
        kcp, vcp = kc_ref[0, p], vc_ref[0, p]
        sc_all = lax.dot_general(qb_ref[...], kcp, _NT_DIMS, preferred_element_type=F32)
        for n in range(NA_COL_BLOCKS):
            kparts, vparts = [], []
            for i in range(NA_K_ROWS):
                kr = jnp.clip(r0 - (NA_K_ROWS - NA_Q_ROWS) // 2 + i, 0, GRID_W - 1)
                start = pl.multiple_of(kr * GRID_W + NA_KC0[n], 8)
                kparts.append(k_ref[p, pl.ds(start, NA_KCOLS), :])
                vparts.append(v_ref[p, pl.ds(start, NA_KCOLS), :])
            kblk = jnp.concatenate(kparts, axis=0).astype(BF16)
            vblk = jnp.concatenate(vparts, axis=0).astype(BF16)
            rows2 = slice(2 * n * blk_q, 2 * (n + 1) * blk_q)
            s = lax.dot_general(qb_ref[rows2, :], kblk, _NT_DIMS, preferred_element_type=F32)
            s = s + jnp.concatenate([tb_ref[2 * p, n] + rmask, tb_ref[2 * p + 1, n] + rmask], axis=0)
            sc = sc_all[rows2, :]
            m = jnp.maximum(jnp.max(s, axis=-1, keepdims=True), jnp.max(sc, axis=-1, keepdims=True))
            e = jnp.exp(s - m)
            ec = jnp.exp(sc - m)
            den = jnp.sum(e, axis=-1, keepdims=True) + jnp.sum(ec, axis=-1, keepdims=True)
            o = (jnp.dot(e.astype(BF16), vblk, preferred_element_type=F32)
                 + jnp.dot(ec.astype(BF16), vcp, preferred_element_type=F32)) / den
            oblk = jnp.where(lm0, o[:blk_q, :], o[blk_q:, :])
            for rl in range(NA_Q_ROWS):
                os_ref[p, rl * GRID_W + n * NA_COLS:rl * GRID_W + (n + 1) * NA_COLS, :] = (
                    oblk[rl * NA_COLS:(rl + 1) * NA_COLS, :])
        return carry

    lax.fori_loop(0, HEAD_PAIRS, pair_body, 0)
    for p in range(HEAD_PAIRS):
        o_ref[:, p * LANES:(p + 1) * LANES] = os_ref[p].astype(BF16)


def _na_attn(q, k, v, kc, vc, tb, rm, n_batch, *, l):
    tile_tok = NA_Q_ROWS * GRID_W
    n_rt = DEC_SEQ // tile_tok
    blk_q, blk_k = NA_Q_ROWS * NA_COLS, NA_K_ROWS * NA_KCOLS
    return pl.pallas_call(
        _na_kernel,
        grid=(n_batch, n_rt),
        in_specs=[
            pl.BlockSpec((HEAD_PAIRS, tile_tok, LANES), lambda b, r: (0, b * n_rt + r, 0)),
            pl.BlockSpec((HEAD_PAIRS, DEC_SEQ, LANES), lambda b, r: (0, b, 0)),
            pl.BlockSpec((HEAD_PAIRS, DEC_SEQ, LANES), lambda b, r: (0, b, 0)),
            pl.BlockSpec((1, None, HEAD_PAIRS, PAST_LEN, LANES), lambda b, r: (b, l, 0, 0, 0)),
            pl.BlockSpec((1, None, HEAD_PAIRS, PAST_LEN, LANES), lambda b, r: (b, l, 0, 0, 0)),
            _layer_spec(tb, l),
            _const_spec(rm.shape),
        ],
        out_specs=pl.BlockSpec((tile_tok, NA_WIDTH), lambda b, r: (b * n_rt + r, 0)),
        out_shape=jax.ShapeDtypeStruct((n_batch * DEC_SEQ, NA_WIDTH), BF16),
        scratch_shapes=[
            pltpu.VMEM((2 * tile_tok, LANES), BF16),
            pltpu.VMEM((HEAD_PAIRS, tile_tok, LANES), F32),
            pltpu.VMEM((NA_HEADS, NA_COL_BLOCKS, blk_q, blk_k), F32)],
        compiler_params=_cparams(("arbitrary", "arbitrary")),
        name="na_attn",
    )(q, k, v, kc, vc, tb, rm)


def _na_tables(rpb):
    n = np.arange(NA_COL_BLOCKS)
    cc = np.arange(NA_COLS)
    kk = np.arange(NA_KCOLS)
    rl = np.arange(NA_Q_ROWS)
    ki = np.arange(NA_K_ROWS)
    n_dr, n_dc = 2 * NA_MAX_ROWS - 1, 2 * NA_COLS - 1
    c = n[:, None] * NA_COLS + cc[None, :]
    kcol = np.asarray(NA_KC0)[:, None] + kk[None, :]
    dc = np.clip(kcol[:, None, :] - c[:, :, None], -(NA_COLS - 1), NA_COLS - 1) + (NA_COLS - 1)
    cs = np.clip(c - NA_COLS // 2, 0, GRID_W - NA_COLS)
    col_ok = (kcol[:, None, :] >= cs[:, :, None]) & (kcol[:, None, :] < cs[:, :, None] + NA_COLS)
    half = (NA_K_ROWS - NA_Q_ROWS) // 2
    dr = np.clip(ki[None, :] - half - rl[:, None] + (NA_MAX_ROWS - 1), 0, n_dr - 1)
    oh_dc = jnp.asarray(dc[None] == np.arange(n_dc)[:, None, None, None], F32)
    off = np.clip(np.arange(NA_OFF_ROWS) - NA_OFF_PAD, 0, n_dr - 1)
    assert np.array_equal(dr, off[ki[None] + (NA_OFF_START - rl)[:, None]])
    oh_off = jnp.asarray(off[None, :] == np.arange(n_dr)[:, None], F32)
    rows_p = jnp.einsum('lhab,aA->lhAb', rpb.astype(F32), oh_off, precision='highest')
    cols = jnp.einsum('lhAb,bnck->lhncAk', rows_p, oh_dc, precision='highest')
    cols = jnp.where(jnp.asarray(col_ok)[None, None, :, :, None, :], cols, NEG_INF)
    tb = cols.reshape(-1, NA_HEADS, NA_COL_BLOCKS, NA_COLS, NA_OFF_ROWS * NA_KCOLS)
    rows = GRID_W
    masks = []
    for r0 in (0, NA_Q_ROWS, rows - NA_Q_ROWS):
        r = r0 + rl
        rs = np.clip(r - NA_MAX_ROWS // 2, 0, rows - NA_MAX_ROWS)
        kr = r0 - half + ki
        ok = (kr[None, :] >= rs[:, None]) & (kr[None, :] < rs[:, None] + NA_MAX_ROWS)
        full = np.broadcast_to(ok[:, None, :, None], (NA_Q_ROWS, NA_COLS, NA_K_ROWS, NA_KCOLS))
        masks.append(np.where(full, 0.0, NEG_INF).reshape(NA_Q_ROWS * NA_COLS, NA_K_ROWS * NA_KCOLS))
    rm = jnp.asarray(np.stack(masks), F32)
    return tb, rm


def _outffn_kernel(x_ref, mod_ref, y_ref, na_ref, gm_ref, gw_ref, gb_ref, wo_ref, g_ref, win_ref, wout_ref, o_ref,
                   yr_ref, ys_ref, cat_ref, xm_ref, h_ref, act_ref):
    gate = mod_ref[0, 5:6, :]
    n_rows = OUT_TILE // SSM_CHUNK
    for g in range(SSM_GROUPS):
        yr_ref[g] = _roll_lanes(y_ref[g], SSM_CH * g)
    for t in range(SSM_CHUNK):
        yt = _merge_segments(lambda src, lt: yr_ref[src, :, lt * LANES:(lt + 1) * LANES], t)
        yt = _roll_lanes(yt, -SSM_CH * t)
        for j in range(SSM_WIDTH // LANES):
            ys_ref[j, pl.ds(t, n_rows, stride=SSM_CHUNK), :] = yt[:, j * LANES:(j + 1) * LANES]
    y = _gelu(jnp.concatenate([ys_ref[j] for j in range(SSM_WIDTH // LANES)], axis=-1))
    z = jnp.dot(y.astype(BF16), gw_ref[...], preferred_element_type=F32) + gb_ref[...]
    cat_ref[:, 0:SSM_WIDTH] = (y * _sigmoid(z)).astype(BF16)
    cat_ref[:, SSM_WIDTH:SSM_WIDTH + NA_WIDTH] = na_ref[...]
    cat_ref[:, SSM_WIDTH + NA_WIDTH:] = gm_ref[...]
    mix = jnp.dot(cat_ref[...], wo_ref[...], preferred_element_type=F32)
    xm_ref[...] = x_ref[...] + gate * mix
    _ffn_kernel(xm_ref, mod_ref, g_ref, win_ref, wout_ref, o_ref, h_ref, act_ref, mod_base=6)


def _outffn(x, mod, y, na, gm, gw, gb, wo, g, w_in, w_out, *, l, seg0, seg_tokens):
    n_tok = x.shape[0]
    n_rows = OUT_TILE // SSM_CHUNK
    tok = lambda width: pl.BlockSpec((OUT_TILE, width), lambda i: (i, 0))
    return pl.pallas_call(
        _outffn_kernel,
        grid=(n_tok // OUT_TILE,),
        in_specs=[
            tok(D_MODEL),
            _mod_spec(l, seg0, seg_tokens, OUT_TILE),
            pl.BlockSpec((SSM_GROUPS, n_rows, SSM_ROW), lambda i: (0, i, 0)),
            tok(NA_WIDTH), tok(GM_WIDTH),
            _layer_spec(gw, l), _layer_spec(gb, l), _layer_spec(wo, l),
            _layer_spec(g, l), _layer_spec(w_in, l), _layer_spec(w_out, l),
        ],
        out_specs=tok(D_MODEL),
        out_shape=jax.ShapeDtypeStruct((n_tok, D_MODEL), F32),
        scratch_shapes=[pltpu.VMEM((SSM_GROUPS, n_rows, SSM_ROW), F32),
                        pltpu.VMEM((SSM_WIDTH // LANES, OUT_TILE, LANES), F32),
                        pltpu.VMEM((OUT_TILE, D_MODEL), BF16),
                        pltpu.VMEM((OUT_TILE, D_MODEL), F32),
                        pltpu.VMEM((OUT_TILE, D_MODEL), BF16), pltpu.VMEM((OUT_TILE, D_FF), BF16)],
        compiler_params=_cparams(("arbitrary",)),
        name="outffn",
    )(x, mod, y, na, gm, gw, gb, wo, g, w_in, w_out)


def _prep(p):
    heads = np.arange(MXU_DIM) // NA_HEAD_DIM
    hsum = jnp.asarray((heads[:, None] == heads[None, :]) / NA_HEAD_DIM, BF16)
    gmb = jnp.repeat(jnp.swapaxes(p['gm_bs'], 1, 2).astype(F32), GM_WIDTH // GM_GROUPS, axis=2)
    tb, rm = _na_tables(p['na_rpb'])
    row = lambda a: a.astype(F32).reshape(DEPTH, 1, -1)
    return dict(
        g1=row(p['norm_ffn1']), g2=row(p['norm_mix']), g3=row(p['norm_ffn2']),
        f1_in=p['ffn1_w_in'].astype(BF16), f1_out=p['ffn1_w_out'].astype(BF16),
        f2_in=p['ffn2_w_in'].astype(BF16), f2_out=p['ffn2_w_out'].astype(BF16),
        w_in=p['w_in'].astype(BF16), w_out=p['w_out'].astype(BF16),
        hsum=hsum,
        qg=row(jnp.tile(p['na_q_norm'], (1, NA_HEADS))), kg=row(jnp.tile(p['na_k_norm'], (1, NA_HEADS))),
        ws=p['gm_ws'].astype(BF16), gmb=gmb,
        ssm=_ssm_params(p),
        glu_w=p['ssm_glu_w'].astype(BF16), glu_b=row(p['ssm_glu_b']),
        tb=tb, rm=rm,
    )


def _pairs_major(t):
    b, n_l, s = t.shape[:3]
    return jnp.transpose(t.reshape(b, n_l, s, HEAD_PAIRS, LANES), (0, 1, 3, 2, 4)).astype(BF16)


def _trunk_layer(x, mod, w, l, *, n_batch, seq_len, caches=None, is_ctx=False, ctx_kv=None, ssm_init=None):
    seg = dict(l=l, seg0=0 if is_ctx else 1, seg_tokens=x.shape[0] if is_ctx else seq_len)
    x = _ffn(x, mod, w['g1'], w['f1_in'], w['f1_out'], mod_base=0, **seg)
    outs = _inproj(x, mod, w['g2'], w['w_in'], w['hsum'], w['qg'], w['kg'], w['ws'], w['gmb'],
                   layer=l if is_ctx else None, caches=caches, **seg)
    u, q, k, v, gm = outs[:5]
    s0 = jnp.zeros((SSM_GROUPS, 4, n_batch, LANES), F32) if is_ctx else ssm_init[l]
    y, fin = _ssm_core(u, *w['ssm'], s0, l=l, n_batch=n_batch, n_chunks=seq_len // SSM_CHUNK)
    if is_ctx:
        na = _ctx_attn(q, k, v, n_batch)
    else:
        na = _na_attn(q, k, v, ctx_kv[0], ctx_kv[1], w['tb'], w['rm'], n_batch, l=l)
    x = _outffn(x, mod, y, na, gm, w['glu_w'], w['glu_b'], w['w_out'], w['g3'], w['f2_in'], w['f2_out'], **seg)
    return x, (tuple(outs[5:]), fin)


def kernel(x_prompt, x_sample, c, cache_k, cache_v, state_ssm, c_ctx, w_ada, b_ada, norm_ffn1, ffn1_w_in, ffn1_w_out, norm_mix, w_in, w_out, ssm_lambda_re, ssm_lambda_im, ssm_log_dt, ssm_b_re, ssm_b_im, ssm_c_re, ssm_c_im, ssm_d, ssm_glu_w, ssm_glu_b, na_q_norm, na_k_norm, na_rpb, gm_ws, gm_bs, norm_ffn2, ffn2_w_in, ffn2_w_out):
    p = dict(norm_ffn1=norm_ffn1, ffn1_w_in=ffn1_w_in, ffn1_w_out=ffn1_w_out, norm_mix=norm_mix, w_in=w_in,
             w_out=w_out, ssm_lambda_re=ssm_lambda_re, ssm_lambda_im=ssm_lambda_im, ssm_log_dt=ssm_log_dt,
             ssm_b_re=ssm_b_re, ssm_b_im=ssm_b_im, ssm_c_re=ssm_c_re, ssm_c_im=ssm_c_im, ssm_d=ssm_d,
             ssm_glu_w=ssm_glu_w, ssm_glu_b=ssm_glu_b, na_q_norm=na_q_norm, na_k_norm=na_k_norm, na_rpb=na_rpb,
             gm_ws=gm_ws, gm_bs=gm_bs, norm_ffn2=norm_ffn2, ffn2_w_in=ffn2_w_in, ffn2_w_out=ffn2_w_out)
    batch, dec_batch = x_prompt.shape[0], x_sample.shape[0]
    cond8 = jnp.zeros((8, D_MODEL), F32).at[0].set(c_ctx).at[1:1 + dec_batch].set(c)
    mod = _adaln(cond8, w_ada, b_ada).reshape(DEPTH, 8, N_MOD, D_MODEL)

    xp = x_prompt.reshape(batch * SEQ, D_MODEL)
    xs = x_sample.reshape(dec_batch * DEC_SEQ, D_MODEL)
    w = _prep(p)
    ctx_kv = (_pairs_major(cache_k), _pairs_major(cache_v))
    ssm_init = _ssm_states_in(state_ssm)
    caches, fins = None, []
    for l in range(DEPTH):
        xp, (caches, fin) = _trunk_layer(xp, mod, w, l, n_batch=batch, seq_len=SEQ, caches=caches, is_ctx=True)
        fins.append(fin)
        xs, _ = _trunk_layer(xs, mod, w, l, n_batch=dec_batch, seq_len=DEC_SEQ, ctx_kv=ctx_kv, ssm_init=ssm_init)
    cache_shape = (batch, DEPTH, SEQ, NA_HEADS, NA_HEAD_DIM)
    return (xp.reshape(batch, SEQ, D_MODEL), xs.reshape(dec_batch, DEC_SEQ, D_MODEL),
            caches[0].reshape(cache_shape), caches[1].reshape(cache_shape), _ssm_states_out(fins))
```

```python
import functools

import numpy as np
import jax
import jax.numpy as jnp
from jax import lax
from jax.experimental import pallas as pl
from jax.experimental.pallas import tpu as pltpu

D_MODEL = 1024
DEPTH = 2
SEQ = 256
DEC_SEQ = 4096
PAST_LEN = 256
GRID_W = 64
SSM_WIDTH = 256
SSM_CH = 16
SSM_GROUPS = 16
SSM_STATE = 64
NA_WIDTH = 512
NA_HEAD_DIM = 64
NA_HEADS = 8
NA_MAX_ROWS = 8
NA_COLS = 16
GM_WIDTH = 256
GM_GROUPS = 4
GM_CHUNK = 128
D_FF = 2816
N_MOD = 9
RMS_EPS = 1e-6
LN_EPS = 1e-5
NEG_INF = -1e30

F32 = jnp.float32
BF16 = jnp.bfloat16

LANES = 128
SUBLANES = 8
MXU_DIM = 256
VMEM_LIMIT_BYTES = 56 * 1024 * 1024

FFN_TILE = 1024
MIX_TILE = 1024
OUT_TILE = 512
FF_CHUNK = MXU_DIM
N_FF_CHUNKS = D_FF // FF_CHUNK
SSM_CHUNK = 16
SSM_ROW = SSM_CHUNK * SSM_CH
SSM_GROUP_BLOCK = 4
SSM_ROW_BLOCK = 512
SSM_W_COLS = 4 * LANES
SCAN_SHIFTS = (1, 2, 4)
N_SCAN_CONST = 3 * len(SCAN_SHIFTS) + 3
HEAD_PAIRS = NA_HEADS // 2
NA_Q_ROWS = 8
NA_K_ROWS = 16
NA_KCOLS = 2 * NA_COLS
NA_COL_BLOCKS = GRID_W // NA_COLS
NA_KC0 = (0, 8, 24, 32)
SEGS_PER_TILE = LANES // SSM_CH
NA_OFF_PAD = (NA_Q_ROWS - 1) - ((NA_MAX_ROWS - 1) - (NA_K_ROWS - NA_Q_ROWS) // 2)
NA_OFF_START = NA_Q_ROWS - 1
NA_OFF_ROWS = NA_OFF_START + NA_K_ROWS


def _silu(x):
    return x * (1.0 / (1.0 + jnp.exp(-x)))


def _sigmoid(x):
    return 1.0 / (1.0 + jnp.exp(-x))


def _gelu(x):
    return 0.5 * x * (1.0 + jnp.tanh(0.7978845608028654 * (x + 0.044715 * (x * x * x))))


def _cparams(sem):
    return pltpu.CompilerParams(dimension_semantics=sem, vmem_limit_bytes=VMEM_LIMIT_BYTES)


def _const_spec(shape):
    nd = len(shape)
    return pl.BlockSpec(shape, lambda *_: (0,) * nd, pipeline_mode=pl.Buffered(1))


def _layer_spec(stacked, l):
    nd = stacked.ndim
    return pl.BlockSpec((None,) + stacked.shape[1:], lambda *_: (l,) + (0,) * (nd - 1), pipeline_mode=pl.Buffered(1))


def _ada_kernel(c_ref, w_ref, b_ref, o_ref):
    s = _silu(c_ref[...]).astype(BF16)
    w = w_ref[0].astype(BF16)
    o_ref[0] = jnp.dot(s, w, preferred_element_type=F32) + b_ref[0]


def _adaln(cond8, w_ada, b_ada):
    tn = D_MODEL
    ncol = N_MOD * D_MODEL
    return pl.pallas_call(
        _ada_kernel,
        grid=(DEPTH, ncol // tn),
        in_specs=[
            pl.BlockSpec((8, D_MODEL), lambda l, j: (0, 0)),
            pl.BlockSpec((1, D_MODEL, tn), lambda l, j: (l, 0, j)),
            pl.BlockSpec((1, 1, tn), lambda l, j: (l, 0, j)),
        ],
        out_specs=pl.BlockSpec((1, 8, tn), lambda l, j: (l, 0, j)),
        out_shape=jax.ShapeDtypeStruct((DEPTH, 8, ncol), F32),
        compiler_params=_cparams(("arbitrary", "arbitrary")),
        name="adaln",
    )(cond8, w_ada, b_ada.reshape(DEPTH, 1, ncol))


def _normed(x, g_row, shift, scale):
    ms = jnp.mean(x * x, axis=-1, keepdims=True)
    h = x * lax.rsqrt(ms + RMS_EPS) * g_row
    return h * (1.0 + scale) + shift


def _mod_spec(l, seg0, seg_tokens, tile):
    tiles_per_seg = seg_tokens // tile
    return pl.BlockSpec((None, 1, N_MOD, D_MODEL), lambda i: (l, seg0 + i // tiles_per_seg, 0, 0))


def _ffn_kernel(x_ref, mod_ref, g_ref, win_ref, wout_ref, o_ref, h_ref, act_ref, *, mod_base):
    shift = mod_ref[0, mod_base:mod_base + 1, :]
    scale = mod_ref[0, mod_base + 1:mod_base + 2, :]
    gate = mod_ref[0, mod_base + 2:mod_base + 3, :]
    h_ref[...] = _normed(x_ref[...], g_ref[...], shift, scale).astype(BF16)
    for j in range(N_FF_CHUNKS):
        cols = slice(j * FF_CHUNK, (j + 1) * FF_CHUNK)
        gate_j = jnp.dot(h_ref[...], win_ref[:, cols], preferred_element_type=F32)
        up_j = jnp.dot(h_ref[...], win_ref[:, D_FF + j * FF_CHUNK:D_FF + (j + 1) * FF_CHUNK], preferred_element_type=F32)
        act_ref[:, cols] = (_silu(gate_j) * up_j).astype(BF16)
    for n in range(D_MODEL // MXU_DIM):
        cols = slice(n * MXU_DIM, (n + 1) * MXU_DIM)
        y = jnp.dot(act_ref[...], wout_ref[:, cols], preferred_element_type=F32)
        o_ref[:, cols] = x_ref[:, cols] + 0.5 * gate[:, cols] * y


def _ffn(x, mod, g, w_in, w_out, *, l, mod_base, seg0, seg_tokens):
    n_tok = x.shape[0]
    return pl.pallas_call(
        functools.partial(_ffn_kernel, mod_base=mod_base),
        grid=(n_tok // FFN_TILE,),
        in_specs=[
            pl.BlockSpec((FFN_TILE, D_MODEL), lambda i: (i, 0)),
            _mod_spec(l, seg0, seg_tokens, FFN_TILE),
            _layer_spec(g, l), _layer_spec(w_in, l), _layer_spec(w_out, l),
        ],
        out_specs=pl.BlockSpec((FFN_TILE, D_MODEL), lambda i: (i, 0)),
        out_shape=jax.ShapeDtypeStruct((n_tok, D_MODEL), F32),
        scratch_shapes=[pltpu.VMEM((FFN_TILE, D_MODEL), BF16), pltpu.VMEM((FFN_TILE, D_FF), BF16)],
        compiler_params=_cparams(("arbitrary",)),
        name="ffn",
    )(x, mod, g, w_in, w_out)


def _segment_ids():
    return lax.broadcasted_iota(jnp.int32, (1, LANES), 1) // SSM_CH


def _merge_segments(load_piece, rot):
    seg = _segment_ids()
    tiles = []
    for lt in range(SSM_ROW // LANES):
        acc = None
        for jj in range(SEGS_PER_TILE):
            src = (lt * SEGS_PER_TILE + jj - rot) % SSM_CHUNK
            piece = load_piece(src, lt)
            acc = piece if acc is None else jnp.where(seg == jj, piece, acc)
        tiles.append(acc)
    return jnp.concatenate(tiles, axis=-1)


def _roll_lanes(x, shift):
    shift %= x.shape[-1]
    return pltpu.roll(x, shift, axis=x.ndim - 1) if shift else x


def _inproj_kernel(*refs, n_in, n_out):
    x_ref, mod_ref, g_ref, w_ref, hsum_ref, qg_ref, kg_ref, ws_ref, gmb_ref = refs[:9]
    u_ref, q_ref, k_ref, v_ref, gm_ref, *tok_refs = refs[n_in:n_in + n_out]
    h_ref, xs_ref, z_ref = refs[n_in + n_out:]
    shift = mod_ref[0, 3:4, :]
    scale = mod_ref[0, 4:5, :]
    h_ref[...] = _normed(x_ref[...], g_ref[...], shift, scale).astype(BF16)
    n_rows = MIX_TILE // SSM_CHUNK

    def proj(c0, width):
        return jnp.dot(h_ref[...], w_ref[:, c0:c0 + width], preferred_element_type=F32)

    xs = proj(0, SSM_WIDTH)
    for j in range(SSM_WIDTH // LANES):
        xs_ref[j] = xs[:, j * LANES:(j + 1) * LANES]
    for t in range(SSM_CHUNK):
        zt = jnp.concatenate([xs_ref[j, pl.ds(t, n_rows, stride=SSM_CHUNK), :] for j in range(SSM_WIDTH // LANES)],
                             axis=-1)
        z_ref[t] = _roll_lanes(zt, SSM_CH * t)
    for g in range(SSM_GROUPS):
        ug = _merge_segments(lambda src, lt: z_ref[src, :, lt * LANES:(lt + 1) * LANES], g)
        u_ref[g] = _roll_lanes(ug, -SSM_CH * g)

    def head_norm(t, gain_row):
        sq = (t * t).astype(BF16)
        ms = jnp.concatenate([jnp.dot(sq[:, c:c + MXU_DIM], hsum_ref[...], preferred_element_type=F32)
                              for c in range(0, NA_WIDTH, MXU_DIM)], axis=-1)
        return t * lax.rsqrt(ms + RMS_EPS) * gain_row

    c0 = SSM_WIDTH
    q = head_norm(proj(c0, NA_WIDTH), qg_ref[...]) * (NA_HEAD_DIM ** -0.5)
    k = head_norm(proj(c0 + NA_WIDTH, NA_WIDTH), kg_ref[...])
    v = proj(c0 + 2 * NA_WIDTH, NA_WIDTH)
    for p in range(HEAD_PAIRS):
        lanes = slice(p * LANES, (p + 1) * LANES)
        q_ref[p] = q[:, lanes].astype(q_ref.dtype)
        k_ref[p] = k[:, lanes].astype(k_ref.dtype)
        v_ref[p] = v[:, lanes].astype(v_ref.dtype)
    if tok_refs:
        for slot in range(tok_refs[0].shape[1]):
            tok_refs[0][:, slot] = k.reshape(MIX_TILE // SEQ, SEQ, NA_WIDTH)
            tok_refs[1][:, slot] = v.reshape(MIX_TILE // SEQ, SEQ, NA_WIDTH)

    a = _gelu(proj(c0 + 3 * NA_WIDTH, 2 * GM_WIDTH))
    u = a[:, :GM_WIDTH]
    vv = a[:, GM_WIDTH:]
    mu = jnp.mean(vv, axis=-1, keepdims=True)
    vc = vv - mu
    var = jnp.mean(vc * vc, axis=-1, keepdims=True)
    vln = (vc * lax.rsqrt(var + LN_EPS)).astype(BF16)
    lane = lax.broadcasted_iota(jnp.int32, (1, GM_WIDTH), 1)
    gw = GM_WIDTH // GM_GROUPS
    for ci in range(MIX_TILE // GM_CHUNK):
        rows = slice(ci * GM_CHUNK, (ci + 1) * GM_CHUNK)
        vch = vln[rows, :]
        sp = gmb_ref[...]
        for gi in range(GM_GROUPS):
            t = jnp.dot(ws_ref[gi], vch, preferred_element_type=F32)
            sp = sp + jnp.where((lane >= gi * gw) & (lane < (gi + 1) * gw), t, 0.0)
        gm_ref[rows, :] = (u[rows, :] * sp).astype(BF16)


def _inproj(x, mod, g, w, hsum, qg, kg, ws, gmb, *, l, seg0, seg_tokens, is_ctx, caches=None):
    n_tok = x.shape[0]
    kv_dtype = BF16 if is_ctx else F32
    n_rows = MIX_TILE // SSM_CHUNK
    pair_spec = pl.BlockSpec((HEAD_PAIRS, MIX_TILE, LANES), lambda i: (0, i, 0))
    out_specs = [
        pl.BlockSpec((SSM_GROUPS, n_rows, SSM_ROW), lambda i: (0, i, 0)),
        pair_spec, pair_spec, pair_spec,
        pl.BlockSpec((MIX_TILE, GM_WIDTH), lambda i: (i, 0)),
    ]
    out_shape = [
        jax.ShapeDtypeStruct((SSM_GROUPS, n_tok // SSM_CHUNK, SSM_ROW), F32),
        jax.ShapeDtypeStruct((HEAD_PAIRS, n_tok, LANES), BF16),
        jax.ShapeDtypeStruct((HEAD_PAIRS, n_tok, LANES), kv_dtype),
        jax.ShapeDtypeStruct((HEAD_PAIRS, n_tok, LANES), kv_dtype),
        jax.ShapeDtypeStruct((n_tok, GM_WIDTH), BF16),
    ]
    args = [x, mod, g, w, hsum, qg, kg, ws, gmb]
    in_specs = [
        pl.BlockSpec((MIX_TILE, D_MODEL), lambda i: (i, 0)),
        _mod_spec(l, seg0, seg_tokens, MIX_TILE),
        _layer_spec(g, l), _layer_spec(w, l),
        _const_spec((MXU_DIM, MXU_DIM)),
        _layer_spec(qg, l), _layer_spec(kg, l), _layer_spec(ws, l), _layer_spec(gmb, l),
    ]
    aliases = {}
    if is_ctx:
        bt = MIX_TILE // SEQ
        cache_shape = jax.ShapeDtypeStruct((n_tok // SEQ, DEPTH, SEQ, NA_WIDTH), F32)
        if caches is None:
            out_specs += [pl.BlockSpec((bt, DEPTH, SEQ, NA_WIDTH), lambda i: (i, 0, 0, 0))] * 2
        else:
            out_specs += [pl.BlockSpec((bt, 1, SEQ, NA_WIDTH), lambda i: (i, l, 0, 0))] * 2
        out_shape += [cache_shape] * 2
        if caches is not None:
            aliases = {len(args): len(out_shape) - 2, len(args) + 1: len(out_shape) - 1}
            args += list(caches)
            in_specs += [pl.BlockSpec(memory_space=pl.ANY)] * 2
    return pl.pallas_call(
        functools.partial(_inproj_kernel, n_in=len(args), n_out=len(out_shape)),
        grid=(n_tok // MIX_TILE,),
        in_specs=in_specs,
        out_specs=out_specs,
        out_shape=out_shape,
        input_output_aliases=aliases,
        scratch_shapes=[pltpu.VMEM((MIX_TILE, D_MODEL), BF16), pltpu.VMEM((SSM_WIDTH // LANES, MIX_TILE, LANES), F32),
                        pltpu.VMEM((SSM_CHUNK, n_rows, SSM_ROW), F32)],
        compiler_params=_cparams(("arbitrary",)),
        name="inproj_ctx" if is_ctx else "inproj_lat",
    )(*args)


def _shift_rows(x, d, row, down):
    if down:
        return jnp.where(row >= d, pltpu.roll(x, d, axis=0), 0.0)
    return jnp.where(row < SUBLANES - d, pltpu.roll(x, SUBLANES - d, axis=0), 0.0)


def _scan_tile(w, wp, c, cp, cst, row, down):
    x, xp = w, wp
    for si, d in enumerate(SCAN_SHIFTS):
        a, b, bp = cst[3 * si], cst[3 * si + 1], cst[3 * si + 2]
        xs, xps = _shift_rows(x, d, row, down), _shift_rows(xp, d, row, down)
        x, xp = x + a * xs + b * xps, xp + a * xps + bp * xs
    a, b, bp = cst[N_SCAN_CONST - 3], cst[N_SCAN_CONST - 2], cst[N_SCAN_CONST - 1]
    s = x + a * c + b * cp
    sp = xp + a * cp + bp * c
    edge = 0 if down else SUBLANES - 1
    last = SUBLANES - 1 if down else 0
    seen = jnp.where(row == edge, c, _shift_rows(s, 1, row, down))
    c_new = jnp.broadcast_to(s[last:last + 1, :], s.shape)
    cp_new = jnp.broadcast_to(sp[last:last + 1, :], sp.shape)
    return seen, c_new, cp_new


def _ssm_kernel(u_ref, mp_ref, q_ref, cst_ref, dvec_ref, s0_ref, y_ref, fin_ref, w_ref, sq_ref, *, n_batch, n_chunks):
    gb = u_ref.shape[0]
    n_rows = n_batch * n_chunks
    row_blk = min(n_rows, SSM_ROW_BLOCK)
    for g in range(gb):
        for r0 in range(0, n_rows, row_blk):
            rows = slice(r0, r0 + row_blk)
            ug = u_ref[g, rows, :]
            wy = jnp.dot(ug.astype(BF16), mp_ref[g], preferred_element_type=F32)
            y_ref[g, rows, :] = wy[:, :SSM_ROW] + ug * dvec_ref[g]
            w_ref[g, rows, :] = wy[:, SSM_ROW:]

    n_tiles = n_chunks // SUBLANES
    row = lax.broadcasted_iota(jnp.int32, (SUBLANES, LANES), 0)

    def tile_step(b, kt, carry):
        out = []
        for g in range(gb):
            cf, cfp, cb, cbp = carry[4 * g:4 * g + 4]
            rf = pl.multiple_of(b * n_chunks + kt * SUBLANES, SUBLANES)
            rb = pl.multiple_of(b * n_chunks + (n_tiles - 1 - kt) * SUBLANES, SUBLANES)
            wf = w_ref[g, pl.ds(rf, SUBLANES), :]
            wb = w_ref[g, pl.ds(rb, SUBLANES), :]
            cst_f = [cst_ref[g, 0, i] for i in range(N_SCAN_CONST)]
            cst_b = [cst_ref[g, 1, i] for i in range(N_SCAN_CONST)]
            seen_f, cf, cfp = _scan_tile(wf[:, 0:LANES], wf[:, LANES:2 * LANES], cf, cfp, cst_f, row, True)
            seen_b, cb, cbp = _scan_tile(wb[:, 2 * LANES:3 * LANES], wb[:, 3 * LANES:], cb, cbp, cst_b, row, False)
            sq_ref[g, pl.ds(rf, SUBLANES), 0:LANES] = seen_f
            sq_ref[g, pl.ds(rb, SUBLANES), LANES:2 * LANES] = seen_b
            out += [cf, cfp, cb, cbp]
        return tuple(out)

    def batch_body(b, _):
        carry = []
        for g in range(gb):
            for i in range(4):
                carry.append(jnp.broadcast_to(s0_ref[g, i, pl.ds(b, 1), :], (SUBLANES, LANES)))
        carry = tuple(carry)
        if n_tiles <= 2:
            for kt in range(n_tiles):
                carry = tile_step(b, kt, carry)
        else:
            carry = lax.fori_loop(0, n_tiles, lambda kt, c: tile_step(b, kt, c), carry)
        for g in range(gb):
            fin_ref[g, 0, pl.ds(b, 1), :] = carry[4 * g][0:1, :]
            fin_ref[g, 1, pl.ds(b, 1), :] = carry[4 * g + 2][0:1, :]
        return 0

    lax.fori_loop(0, n_batch, batch_body, 0)

    for g in range(gb):
        for r0 in range(0, n_rows, row_blk):
            rows = slice(r0, r0 + row_blk)
            y_ref[g, rows, :] += jnp.dot(sq_ref[g, rows, :].astype(BF16), q_ref[g], preferred_element_type=F32)


def _ssm_core(u, mp, q, cst, dvec, s0, *, l, n_batch, n_chunks):
    n_rows = n_batch * n_chunks
    gb = SSM_GROUP_BLOCK
    g3 = lambda i: (i, 0, 0)
    g4 = lambda i: (i, 0, 0, 0)
    return pl.pallas_call(
        functools.partial(_ssm_kernel, n_batch=n_batch, n_chunks=n_chunks),
        grid=(SSM_GROUPS // gb,),
        in_specs=[
            pl.BlockSpec((gb, n_rows, SSM_ROW), g3),
            pl.BlockSpec((None, gb, SSM_ROW, SSM_ROW + SSM_W_COLS), lambda i: (l, i, 0, 0)),
            pl.BlockSpec((None, gb, 2 * LANES, SSM_ROW), lambda i: (l, i, 0, 0)),
            pl.BlockSpec((None, gb, 2, N_SCAN_CONST, SUBLANES, LANES), lambda i: (l, i, 0, 0, 0, 0)),
            pl.BlockSpec((None, gb, 1, SSM_ROW), lambda i: (l, i, 0, 0)),
            pl.BlockSpec((gb, 4, n_batch, LANES), g4),
        ],
        out_specs=[
            pl.BlockSpec((gb, n_rows, SSM_ROW), g3),
            pl.BlockSpec((gb, 2, n_batch, LANES), g4),
        ],
        out_shape=[
            jax.ShapeDtypeStruct((SSM_GROUPS, n_rows, SSM_ROW), F32),
            jax.ShapeDtypeStruct((SSM_GROUPS, 2, n_batch, LANES), F32),
        ],
        scratch_shapes=[pltpu.VMEM((gb, n_rows, SSM_W_COLS), F32), pltpu.VMEM((gb, n_rows, 2 * LANES), F32)],
        compiler_params=_cparams(("arbitrary",)),
        name="ssm_core",
    )(u, mp, q, cst, dvec, s0)


def _cmul(a, b):
    return a[0] * b[0] - a[1] * b[1], a[0] * b[1] + a[1] * b[0]


_CHUNK_IDX = np.arange(SSM_CHUNK)


def _ssm_exponents():
    t = SSM_CHUNK
    idx = _CHUNK_IDX.astype(np.float32)
    both = lambda a: np.stack([a, a], axis=1)
    lane_step = idx - (t - 1) / 2
    tile_rows = np.arange(1, SUBLANES + 1, dtype=np.float32) * t
    tables = dict(
        one=both(np.ones(1, np.float32)),
        incr=np.stack([t - 1 - idx, idx], axis=1),
        seen=np.stack([idx + 1, t - idx], axis=1),
        resp_in=np.stack([-lane_step, lane_step], axis=1),
        resp_out=np.stack([lane_step, -lane_step], axis=1),
        shift=both(np.asarray(SCAN_SHIFTS, np.float32) * t),
        tile=np.stack([tile_rows, tile_rows[::-1]], axis=1),
    )
    spans, start = {}, 0
    for name, tab in tables.items():
        spans[name] = slice(start, start + len(tab))
        start += len(tab)
    return np.concatenate(list(tables.values()), axis=0), spans


def _ssm_params(p):
    t = SSM_CHUNK
    n_l = p['ssm_lambda_re'].shape[0]
    lam = (p['ssm_lambda_re'].astype(F32), p['ssm_lambda_im'].astype(F32))
    dt = jnp.exp(p['ssm_log_dt'].astype(F32))[..., None]
    arg = (lam[0] * dt, lam[1] * dt)
    expo, span = _ssm_exponents()
    n = jnp.asarray(expo)[:, None, :, None, None]
    mag = jnp.exp(n * arg[0][None])
    pw = (mag * jnp.cos(n * arg[1][None]), mag * jnp.sin(n * arg[1][None]))
    take = lambda name: [x[span[name]] for x in pw]
    lanes_last = lambda xs: [jnp.moveaxis(x, 0, -1) for x in xs]

    lbar = [x[0] for x in take('one')]
    num = (lbar[0] - 1.0, lbar[1])
    den = lam[0] * lam[0] + lam[1] * lam[1]
    zoh = ((num[0] * lam[0] + num[1] * lam[1]) / den, (num[1] * lam[0] - num[0] * lam[1]) / den)
    b_mat = (p['ssm_b_re'].astype(F32), p['ssm_b_im'].astype(F32))
    bbar = _cmul((zoh[0][..., None], zoh[1][..., None]), b_mat)
    c_mat = [jnp.swapaxes(p[k].astype(F32), -1, -2) for k in ('ssm_c_re', 'ssm_c_im')]
    ct = [jnp.tile(x, (1, 1, 1, 1, t)) for x in c_mat]
    bt = [jnp.tile(x, (1, 1, 1, 1, t)) for x in bbar]

    per_lane = lambda name: [jnp.repeat(x, SSM_CH, axis=-1) for x in lanes_last(take(name))]
    xm = _cmul(per_lane('resp_in'), bt)
    ym = _cmul(per_lane('resp_out'), ct)
    resp = jnp.einsum('ldgpx,ldgpy->ldgxy', jnp.concatenate([xm[0], -xm[1]], axis=3),
                      jnp.concatenate([ym[0], ym[1]], axis=3), precision='highest')
    steps_x = np.repeat(_CHUNK_IDX, SSM_CH)
    causal = np.stack([steps_x[None, :] >= steps_x[:, None], steps_x[:, None] >= steps_x[None, :]])
    mats = jnp.sum(jnp.where(jnp.asarray(causal)[None, :, None], resp, 0.0), axis=1)

    incr = _cmul([x[..., None] for x in take('incr')], [x[None] for x in bbar])
    incr = [jnp.transpose(x, (1, 2, 3, 0, 5, 4)) for x in incr]
    pcat = jnp.concatenate([incr[0], incr[1], incr[1], incr[0]], axis=-1)
    pcat = jnp.concatenate([pcat[:, 0], pcat[:, 1]], axis=-1).reshape(n_l, SSM_GROUPS, SSM_ROW, SSM_W_COLS)
    mp = jnp.concatenate([mats, pcat], axis=-1).astype(BF16)

    qm = _cmul(ct, per_lane('seen'))
    q = jnp.concatenate([qm[0][:, 0], -qm[1][:, 0], qm[0][:, 1], -qm[1][:, 1]], axis=2).astype(BF16)

    def forms(name, rows):
        pr, pi = [jnp.moveaxis(x, 0, 3) for x in take(name)]
        f = jnp.stack([jnp.concatenate([pr, pr], -1), jnp.concatenate([-pi, pi], -1),
                       jnp.concatenate([pi, -pi], -1)], axis=4)
        return f if rows else jnp.broadcast_to(
            f.reshape(n_l, 2, SSM_GROUPS, 3 * len(SCAN_SHIFTS), 1, LANES),
            (n_l, 2, SSM_GROUPS, 3 * len(SCAN_SHIFTS), SUBLANES, LANES))
    per_row = jnp.swapaxes(forms('tile', True), 3, 4)
    cst = jnp.swapaxes(jnp.concatenate([forms('shift', False), per_row], axis=3), 1, 2)
    dvec = jnp.tile(p['ssm_d'].astype(F32).reshape(n_l, SSM_GROUPS, 1, SSM_CH), (1, 1, 1, SSM_CHUNK))
    return mp, q, cst, dvec


def _ssm_states_in(state_ssm):
    st = jnp.transpose(state_ssm.astype(F32), (1, 3, 2, 0, 4, 5))
    re, im = st[..., 0], st[..., 1]
    ri, ir = jnp.concatenate([re, im], -1), jnp.concatenate([im, re], -1)
    return jnp.stack([ri[:, :, 0], ir[:, :, 0], ri[:, :, 1], ir[:, :, 1]], axis=2)


def _ssm_states_out(fins):
    fin = jnp.stack(fins, axis=0)
    fin = jnp.stack([fin[..., :SSM_STATE], fin[..., SSM_STATE:]], axis=-1)
    return jnp.transpose(fin, (3, 0, 2, 1, 4, 5))


def _head_lane_mask(hh):
    lane = lax.broadcasted_iota(jnp.int32, (1, LANES), 1)
    return (lane >= hh * NA_HEAD_DIM) & (lane < (hh + 1) * NA_HEAD_DIM)


_NT_DIMS = (((1,), (1,)), ((), ()))


def _ctx_attn_kernel(q_ref, k_ref, v_ref, o_ref):
    for p in range(HEAD_PAIRS):
        qp, kp, vp = q_ref[p], k_ref[p], v_ref[p]
        lm0 = _head_lane_mask(0)
        zero = jnp.zeros_like(qp)
        q2 = jnp.concatenate([jnp.where(lm0, qp, zero), jnp.where(lm0, zero, qp)], axis=0)
        s = lax.dot_general(q2, kp, _NT_DIMS, preferred_element_type=F32)
        e = jnp.exp(s - jnp.max(s, axis=-1, keepdims=True))
        o = jnp.dot(e.astype(BF16), vp, preferred_element_type=F32) / jnp.sum(e, axis=-1, keepdims=True)
        o_ref[:, p * LANES:(p + 1) * LANES] = jnp.where(lm0, o[:SEQ, :], o[SEQ:, :]).astype(BF16)


def _ctx_attn(q, k, v, n_batch):
    spec = pl.BlockSpec((HEAD_PAIRS, SEQ, LANES), lambda b: (0, b, 0))
    return pl.pallas_call(
        _ctx_attn_kernel,
        grid=(n_batch,),
        in_specs=[spec, spec, spec],
        out_specs=pl.BlockSpec((SEQ, NA_WIDTH), lambda b: (b, 0)),
        out_shape=jax.ShapeDtypeStruct((n_batch * SEQ, NA_WIDTH), BF16),
        compiler_params=_cparams(("arbitrary",)),
        name="ctx_attn",
    )(q, k, v)


def _na_kernel(q_ref, k_ref, v_ref, kc_ref, vc_ref, t2_ref, rm_ref, o_ref, qb_ref, os_ref, tb_ref):
    rt = pl.program_id(1)
    n_rt = pl.num_programs(1)

    @pl.when((pl.program_id(0) == 0) & (rt == 0))
    def _():
        for h in range(NA_HEADS):
            for n in range(NA_COL_BLOCKS):
                for rl in range(NA_Q_ROWS):
                    lane0 = (NA_OFF_START - rl) * NA_KCOLS
                    tb_ref[h, n, rl * NA_COLS:(rl + 1) * NA_COLS, :] = (
                        t2_ref[h, n, :, lane0:lane0 + NA_K_ROWS * NA_KCOLS])

    r0 = rt * NA_Q_ROWS
    ttype = jnp.where(rt == 0, 0, jnp.where(rt == n_rt - 1, 2, 1))
    rmask = rm_ref[ttype]
    blk_q = NA_Q_ROWS * NA_COLS

    def pair_body(p, carry):
        lm0 = _head_lane_mask(0)
        for n in range(NA_COL_BLOCKS):
            for rl in range(NA_Q_ROWS):
                qrow = q_ref[p, rl * GRID_W + n * NA_COLS:rl * GRID_W + (n + 1) * NA_COLS, :]
                zero = jnp.zeros_like(qrow)
                base = 2 * n * blk_q + rl * NA_COLS
                qb_ref[base:base + NA_COLS, :] = jnp.where(lm0, qrow, zero)
                qb_ref[base + blk_q:base + blk_q + NA_COLS, :] = jnp.where(lm0, zero, qrow)
        kcp, vcp = kc_ref[0, p], vc_ref[0, p]
        sc_all = lax.dot_general(qb_ref[...], kcp, _NT_DIMS, preferred_element_type=F32)
        for n in range(NA_COL_BLOCKS):
            kparts, vparts = [], []
            for i in range(NA_K_ROWS):
                kr = jnp.clip(r0 - (NA_K_ROWS - NA_Q_ROWS) // 2 + i, 0, GRID_W - 1)
                start = pl.multiple_of(kr * GRID_W + NA_KC0[n], 8)
                kparts.append(k_ref[p, pl.ds(start, NA_KCOLS), :])
                vparts.append(v_ref[p, pl.ds(start, NA_KCOLS), :])
            kblk = jnp.concatenate(kparts, axis=0).astype(BF16)
            vblk = jnp.concatenate(vparts, axis=0).astype(BF16)
            rows2 = slice(2 * n * blk_q, 2 * (n + 1) * blk_q)
            s = lax.dot_general(qb_ref[rows2, :], kblk, _NT_DIMS, preferred_element_type=F32)
            s = s + jnp.concatenate([tb_ref[2 * p, n] + rmask, tb_ref[2 * p + 1, n] + rmask], axis=0)
            sc = sc_all[rows2, :]
            m = jnp.maximum(jnp.max(s, axis=-1, keepdims=True), jnp.max(sc, axis=-1, keepdims=True))
            e = jnp.exp(s - m)
            ec = jnp.exp(sc - m)
            den = jnp.sum(e, axis=-1, keepdims=True) + jnp.sum(ec, axis=-1, keepdims=True)
            o = (jnp.dot(e.astype(BF16), vblk, preferred_element_type=F32)
                 + jnp.dot(ec.astype(BF16), vcp, preferred_element_type=F32)) / den
            oblk = jnp.where(lm0, o[:blk_q, :], o[blk_q:, :])
            for rl in range(NA_Q_ROWS):
                os_ref[p, rl * GRID_W + n * NA_COLS:rl * GRID_W + (n + 1) * NA_COLS, :] = (
                    oblk[rl * NA_COLS:(rl + 1) * NA_COLS, :])
        return carry

    lax.fori_loop(0, HEAD_PAIRS, pair_body, 0)
    for p in range(HEAD_PAIRS):
        o_ref[:, p * LANES:(p + 1) * LANES] = os_ref[p].astype(BF16)


def _na_attn(q, k, v, kc, vc, tb, rm, n_batch, *, l):
    tile_tok = NA_Q_ROWS * GRID_W
    n_rt = DEC_SEQ // tile_tok
    blk_q, blk_k = NA_Q_ROWS * NA_COLS, NA_K_ROWS * NA_KCOLS
    return pl.pallas_call(
        _na_kernel,
        grid=(n_batch, n_rt),
        in_specs=[
            pl.BlockSpec((HEAD_PAIRS, tile_tok, LANES), lambda b, r: (0, b * n_rt + r, 0)),
            pl.BlockSpec((HEAD_PAIRS, DEC_SEQ, LANES), lambda b, r: (0, b, 0)),
            pl.BlockSpec((HEAD_PAIRS, DEC_SEQ, LANES), lambda b, r: (0, b, 0)),
            pl.BlockSpec((1, None, HEAD_PAIRS, PAST_LEN, LANES), lambda b, r: (b, l, 0, 0, 0)),
            pl.BlockSpec((1, None, HEAD_PAIRS, PAST_LEN, LANES), lambda b, r: (b, l, 0, 0, 0)),
            _layer_spec(tb, l),
            _const_spec(rm.shape),
        ],
        out_specs=pl.BlockSpec((tile_tok, NA_WIDTH), lambda b, r: (b * n_rt + r, 0)),
        out_shape=jax.ShapeDtypeStruct((n_batch * DEC_SEQ, NA_WIDTH), BF16),
        scratch_shapes=[
            pltpu.VMEM((2 * tile_tok, LANES), BF16),
            pltpu.VMEM((HEAD_PAIRS, tile_tok, LANES), F32),
            pltpu.VMEM((NA_HEADS, NA_COL_BLOCKS, blk_q, blk_k), F32)],
        compiler_params=_cparams(("arbitrary", "arbitrary")),
        name="na_attn",
    )(q, k, v, kc, vc, tb, rm)


def _na_tables(rpb):
    n = np.arange(NA_COL_BLOCKS)
    cc = np.arange(NA_COLS)
    kk = np.arange(NA_KCOLS)
    rl = np.arange(NA_Q_ROWS)
    ki = np.arange(NA_K_ROWS)
    n_dr, n_dc = 2 * NA_MAX_ROWS - 1, 2 * NA_COLS - 1
    c = n[:, None] * NA_COLS + cc[None, :]
    kcol = np.asarray(NA_KC0)[:, None] + kk[None, :]
    dc = np.clip(kcol[:, None, :] - c[:, :, None], -(NA_COLS - 1), NA_COLS - 1) + (NA_COLS - 1)
    cs = np.clip(c - NA_COLS // 2, 0, GRID_W - NA_COLS)
    col_ok = (kcol[:, None, :] >= cs[:, :, None]) & (kcol[:, None, :] < cs[:, :, None] + NA_COLS)
    half = (NA_K_ROWS - NA_Q_ROWS) // 2
    dr = np.clip(ki[None, :] - half - rl[:, None] + (NA_MAX_ROWS - 1), 0, n_dr - 1)
    oh_dc = jnp.asarray(dc[None] == np.arange(n_dc)[:, None, None, None], F32)
    off = np.clip(np.arange(NA_OFF_ROWS) - NA_OFF_PAD, 0, n_dr - 1)
    assert np.array_equal(dr, off[ki[None] + (NA_OFF_START - rl)[:, None]])
    oh_off = jnp.asarray(off[None, :] == np.arange(n_dr)[:, None], F32)
    rows_p = jnp.einsum('lhab,aA->lhAb', rpb.astype(F32), oh_off, precision='highest')
    cols = jnp.einsum('lhAb,bnck->lhncAk', rows_p, oh_dc, precision='highest')
    cols = jnp.where(jnp.asarray(col_ok)[None, None, :, :, None, :], cols, NEG_INF)
    tb = cols.reshape(-1, NA_HEADS, NA_COL_BLOCKS, NA_COLS, NA_OFF_ROWS * NA_KCOLS)
    rows = GRID_W
    masks = []
    for r0 in (0, NA_Q_ROWS, rows - NA_Q_ROWS):
        r = r0 + rl
        rs = np.clip(r - NA_MAX_ROWS // 2, 0, rows - NA_MAX_ROWS)
        kr = r0 - half + ki
        ok = (kr[None, :] >= rs[:, None]) & (kr[None, :] < rs[:, None] + NA_MAX_ROWS)
        full = np.broadcast_to(ok[:, None, :, None], (NA_Q_ROWS, NA_COLS, NA_K_ROWS, NA_KCOLS))
        masks.append(np.where(full, 0.0, NEG_INF).reshape(NA_Q_ROWS * NA_COLS, NA_K_ROWS * NA_KCOLS))
    rm = jnp.asarray(np.stack(masks), F32)
    return tb, rm


def _outffn_kernel(x_ref, mod_ref, y_ref, na_ref, gm_ref, gw_ref, gb_ref, wo_ref, g_ref, win_ref, wout_ref, o_ref,
                   yr_ref, ys_ref, cat_ref, xm_ref, h_ref, act_ref):
    gate = mod_ref[0, 5:6, :]
    n_rows = OUT_TILE // SSM_CHUNK
    for g in range(SSM_GROUPS):
        yr_ref[g] = _roll_lanes(y_ref[g], SSM_CH * g)
    for t in range(SSM_CHUNK):
        yt = _merge_segments(lambda src, lt: yr_ref[src, :, lt * LANES:(lt + 1) * LANES], t)
        yt = _roll_lanes(yt, -SSM_CH * t)
        for j in range(SSM_WIDTH // LANES):
            ys_ref[j, pl.ds(t, n_rows, stride=SSM_CHUNK), :] = yt[:, j * LANES:(j + 1) * LANES]
    y = _gelu(jnp.concatenate([ys_ref[j] for j in range(SSM_WIDTH // LANES)], axis=-1))
    z = jnp.dot(y.astype(BF16), gw_ref[...], preferred_element_type=F32) + gb_ref[...]
    cat_ref[:, 0:SSM_WIDTH] = (y * _sigmoid(z)).astype(BF16)
    cat_ref[:, SSM_WIDTH:SSM_WIDTH + NA_WIDTH] = na_ref[...]
    cat_ref[:, SSM_WIDTH + NA_WIDTH:] = gm_ref[...]
    mix = jnp.dot(cat_ref[...], wo_ref[...], preferred_element_type=F32)
    xm_ref[...] = x_ref[...] + gate * mix
    _ffn_kernel(xm_ref, mod_ref, g_ref, win_ref, wout_ref, o_ref, h_ref, act_ref, mod_base=6)


def _outffn(x, mod, y, na, gm, gw, gb, wo, g, w_in, w_out, *, l, seg0, seg_tokens):
    n_tok = x.shape[0]
    n_rows = OUT_TILE // SSM_CHUNK
    tok = lambda width: pl.BlockSpec((OUT_TILE, width), lambda i: (i, 0))
    return pl.pallas_call(
        _outffn_kernel,
        grid=(n_tok // OUT_TILE,),
        in_specs=[
            tok(D_MODEL),
            _mod_spec(l, seg0, seg_tokens, OUT_TILE),
            pl.BlockSpec((SSM_GROUPS, n_rows, SSM_ROW), lambda i: (0, i, 0)),
            tok(NA_WIDTH), tok(GM_WIDTH),
            _layer_spec(gw, l), _layer_spec(gb, l), _layer_spec(wo, l),
            _layer_spec(g, l), _layer_spec(w_in, l), _layer_spec(w_out, l),
        ],
        out_specs=tok(D_MODEL),
        out_shape=jax.ShapeDtypeStruct((n_tok, D_MODEL), F32),
        scratch_shapes=[pltpu.VMEM((SSM_GROUPS, n_rows, SSM_ROW), F32),
                        pltpu.VMEM((SSM_WIDTH // LANES, OUT_TILE, LANES), F32),
                        pltpu.VMEM((OUT_TILE, D_MODEL), BF16),
                        pltpu.VMEM((OUT_TILE, D_MODEL), F32),
                        pltpu.VMEM((OUT_TILE, D_MODEL), BF16), pltpu.VMEM((OUT_TILE, D_FF), BF16)],
        compiler_params=_cparams(("arbitrary",)),
        name="outffn",
    )(x, mod, y, na, gm, gw, gb, wo, g, w_in, w_out)


def _prep(p):
    heads = np.arange(MXU_DIM) // NA_HEAD_DIM
    hsum = jnp.asarray((heads[:, None] == heads[None, :]) / NA_HEAD_DIM, BF16)
    gmb = jnp.repeat(jnp.swapaxes(p['gm_bs'], 1, 2).astype(F32), GM_WIDTH // GM_GROUPS, axis=2)
    tb, rm = _na_tables(p['na_rpb'])
    row = lambda a: a.astype(F32).reshape(DEPTH, 1, -1)
    return dict(
        g1=row(p['norm_ffn1']), g2=row(p['norm_mix']), g3=row(p['norm_ffn2']),
        f1_in=p['ffn1_w_in'].astype(BF16), f1_out=p['ffn1_w_out'].astype(BF16),
        f2_in=p['ffn2_w_in'].astype(BF16), f2_out=p['ffn2_w_out'].astype(BF16),
        w_in=p['w_in'].astype(BF16), w_out=p['w_out'].astype(BF16),
        hsum=hsum,
        qg=row(jnp.tile(p['na_q_norm'], (1, NA_HEADS))), kg=row(jnp.tile(p['na_k_norm'], (1, NA_HEADS))),
        ws=p['gm_ws'].astype(BF16), gmb=gmb,
        ssm=_ssm_params(p),
        glu_w=p['ssm_glu_w'].astype(BF16), glu_b=row(p['ssm_glu_b']),
        tb=tb, rm=rm,
    )


def _pairs_major(t):
    b, n_l, s = t.shape[:3]
    return jnp.transpose(t.reshape(b, n_l, s, HEAD_PAIRS, LANES), (0, 1, 3, 2, 4)).astype(BF16)


def _trunk_layer(x, mod, w, l, *, n_batch, seq_len, caches=None, is_ctx=False, ctx_kv=None, ssm_init=None):
    seg = dict(l=l, seg0=0 if is_ctx else 1, seg_tokens=x.shape[0] if is_ctx else seq_len)
    x = _ffn(x, mod, w['g1'], w['f1_in'], w['f1_out'], mod_base=0, **seg)
    outs = _inproj(x, mod, w['g2'], w['w_in'], w['hsum'], w['qg'], w['kg'], w['ws'], w['gmb'],
                   is_ctx=is_ctx, caches=caches, **seg)
    u, q, k, v, gm = outs[:5]
    s0 = jnp.zeros((SSM_GROUPS, 4, n_batch, LANES), F32) if is_ctx else ssm_init[l]
    y, fin = _ssm_core(u, *w['ssm'], s0, l=l, n_batch=n_batch, n_chunks=seq_len // SSM_CHUNK)
    if is_ctx:
        na = _ctx_attn(q, k, v, n_batch)
    else:
        na = _na_attn(q, k, v, ctx_kv[0], ctx_kv[1], w['tb'], w['rm'], n_batch, l=l)
    x = _outffn(x, mod, y, na, gm, w['glu_w'], w['glu_b'], w['w_out'], w['g3'], w['f2_in'], w['f2_out'], **seg)
    return x, (tuple(outs[5:]), fin)


def kernel(x_prompt, x_sample, c, cache_k, cache_v, state_ssm, c_ctx, w_ada, b_ada, norm_ffn1, ffn1_w_in, ffn1_w_out, norm_mix, w_in, w_out, ssm_lambda_re, ssm_lambda_im, ssm_log_dt, ssm_b_re, ssm_b_im, ssm_c_re, ssm_c_im, ssm_d, ssm_glu_w, ssm_glu_b, na_q_norm, na_k_norm, na_rpb, gm_ws, gm_bs, norm_ffn2, ffn2_w_in, ffn2_w_out):
    p = dict(norm_ffn1=norm_ffn1, ffn1_w_in=ffn1_w_in, ffn1_w_out=ffn1_w_out, norm_mix=norm_mix, w_in=w_in,
             w_out=w_out, ssm_lambda_re=ssm_lambda_re, ssm_lambda_im=ssm_lambda_im, ssm_log_dt=ssm_log_dt,
             ssm_b_re=ssm_b_re, ssm_b_im=ssm_b_im, ssm_c_re=ssm_c_re, ssm_c_im=ssm_c_im, ssm_d=ssm_d,
             ssm_glu_w=ssm_glu_w, ssm_glu_b=ssm_glu_b, na_q_norm=na_q_norm, na_k_norm=na_k_norm, na_rpb=na_rpb,
             gm_ws=gm_ws, gm_bs=gm_bs, norm_ffn2=norm_ffn2, ffn2_w_in=ffn2_w_in, ffn2_w_out=ffn2_w_out)
    batch, dec_batch = x_prompt.shape[0], x_sample.shape[0]
    cond8 = jnp.zeros((8, D_MODEL), F32).at[0].set(c_ctx).at[1:1 + dec_batch].set(c)
    mod = _adaln(cond8, w_ada, b_ada).reshape(DEPTH, 8, N_MOD, D_MODEL)

    xp = x_prompt.reshape(batch * SEQ, D_MODEL)
    xs = x_sample.reshape(dec_batch * DEC_SEQ, D_MODEL)
    w = _prep(p)
    ctx_kv = (_pairs_major(cache_k), _pairs_major(cache_v))
    ssm_init = _ssm_states_in(state_ssm)
    caches, fins = None, []
    for l in range(DEPTH):
        xp, (caches, fin) = _trunk_layer(xp, mod, w, l, n_batch=batch, seq_len=SEQ, caches=caches, is_ctx=True)
        fins.append(fin)
        xs, _ = _trunk_layer(xs, mod, w, l, n_batch=dec_batch, seq_len=DEC_SEQ, ctx_kv=ctx_kv, ssm_init=ssm_init)
    cache_shape = (batch, DEPTH, SEQ, NA_HEADS, NA_HEAD_DIM)
    return (xp.reshape(batch, SEQ, D_MODEL), xs.reshape(dec_batch, DEC_SEQ, D_MODEL),
            caches[0].reshape(cache_shape), caches[1].reshape(cache_shape), _ssm_states_out(fins))
```

```python
import functools

import numpy as np
import jax
import jax.numpy as jnp
from jax import lax
from jax.experimental import pallas as pl
from jax.experimental.pallas import tpu as pltpu

D_MODEL = 1024
DEPTH = 2
SEQ = 256
DEC_SEQ = 4096
PAST_LEN = 256
GRID_W = 64
SSM_WIDTH = 256
SSM_CH = 16
SSM_GROUPS = 16
SSM_STATE = 64
NA_WIDTH = 512
NA_HEAD_DIM = 64
NA_HEADS = 8
NA_MAX_ROWS = 8
NA_COLS = 16
GM_WIDTH = 256
GM_GROUPS = 4
GM_CHUNK = 128
D_FF = 2816
N_MOD = 9
RMS_EPS = 1e-6
LN_EPS = 1e-5
NEG_INF = -1e30

F32 = jnp.float32
BF16 = jnp.bfloat16

LANES = 128
SUBLANES = 8
MXU_DIM = 256
VMEM_LIMIT_BYTES = 56 * 1024 * 1024

FFN_TILE = 1024
MIX_TILE = 1024
OUT_TILE = 512
FF_CHUNK = MXU_DIM
N_FF_CHUNKS = D_FF // FF_CHUNK
SSM_CHUNK = 16
SSM_ROW = SSM_CHUNK * SSM_CH
SSM_GROUP_BLOCK = 4
SSM_ROW_BLOCK = 512
SSM_W_COLS = 4 * LANES
SCAN_SHIFTS = (1, 2, 4)
N_SCAN_CONST = 3 * len(SCAN_SHIFTS) + 3
HEAD_PAIRS = NA_HEADS // 2
NA_Q_ROWS = 8
NA_K_ROWS = 16
NA_KCOLS = 2 * NA_COLS
NA_COL_BLOCKS = GRID_W // NA_COLS
NA_KC0 = (0, 8, 24, 32)
SEGS_PER_TILE = LANES // SSM_CH
NA_OFF_PAD = (NA_Q_ROWS - 1) - ((NA_MAX_ROWS - 1) - (NA_K_ROWS - NA_Q_ROWS) // 2)
NA_OFF_START = NA_Q_ROWS - 1
NA_OFF_ROWS = NA_OFF_START + NA_K_ROWS


def _silu(x):
    return x * (1.0 / (1.0 + jnp.exp(-x)))


def _sigmoid(x):
    return 1.0 / (1.0 + jnp.exp(-x))


def _gelu(x):
    return 0.5 * x * (1.0 + jnp.tanh(0.7978845608028654 * (x + 0.044715 * (x * x * x))))


def _cparams(sem):
    return pltpu.CompilerParams(dimension_semantics=sem, vmem_limit_bytes=VMEM_LIMIT_BYTES)


def _const_spec(shape):
    nd = len(shape)
    return pl.BlockSpec(shape, lambda *_: (0,) * nd, pipeline_mode=pl.Buffered(1))


def _layer_spec(stacked, l):
    nd = stacked.ndim
    return pl.BlockSpec((None,) + stacked.shape[1:], lambda *_: (l,) + (0,) * (nd - 1), pipeline_mode=pl.Buffered(1))


def _ada_kernel(c_ref, w_ref, b_ref, o_ref):
    s = _silu(c_ref[...]).astype(BF16)
    w = w_ref[0].astype(BF16)
    o_ref[0] = jnp.dot(s, w, preferred_element_type=F32) + b_ref[0]


def _adaln(cond8, w_ada, b_ada):
    tn = D_MODEL
    ncol = N_MOD * D_MODEL
    return pl.pallas_call(
        _ada_kernel,
        grid=(DEPTH, ncol // tn),
        in_specs=[
            pl.BlockSpec((8, D_MODEL), lambda l, j: (0, 0)),
            pl.BlockSpec((1, D_MODEL, tn), lambda l, j: (l, 0, j)),
            pl.BlockSpec((1, 1, tn), lambda l, j: (l, 0, j)),
        ],
        out_specs=pl.BlockSpec((1, 8, tn), lambda l, j: (l, 0, j)),
        out_shape=jax.ShapeDtypeStruct((DEPTH, 8, ncol), F32),
        compiler_params=_cparams(("arbitrary", "arbitrary")),
        name="adaln",
    )(cond8, w_ada, b_ada.reshape(DEPTH, 1, ncol))


def _normed(x, g_row, shift, scale):
    ms = jnp.mean(x * x, axis=-1, keepdims=True)
    h = x * lax.rsqrt(ms + RMS_EPS) * g_row
    return h * (1.0 + scale) + shift


def _mod_spec(l, seg0, seg_tokens, tile):
    tiles_per_seg = seg_tokens // tile
    return pl.BlockSpec((None, 1, N_MOD, D_MODEL), lambda i: (l, seg0 + i // tiles_per_seg, 0, 0))


def _ffn_kernel(x_ref, mod_ref, g_ref, win_ref, wout_ref, o_ref, h_ref, act_ref, *, mod_base):
    shift = mod_ref[0, mod_base:mod_base + 1, :]
    scale = mod_ref[0, mod_base + 1:mod_base + 2, :]
    gate = mod_ref[0, mod_base + 2:mod_base + 3, :]
    h_ref[...] = _normed(x_ref[...], g_ref[...], shift, scale).astype(BF16)
    for j in range(N_FF_CHUNKS):
        cols = slice(j * FF_CHUNK, (j + 1) * FF_CHUNK)
        gate_j = jnp.dot(h_ref[...], win_ref[:, cols], preferred_element_type=F32)
        up_j = jnp.dot(h_ref[...], win_ref[:, D_FF + j * FF_CHUNK:D_FF + (j + 1) * FF_CHUNK], preferred_element_type=F32)
        act_ref[:, cols] = (_silu(gate_j) * up_j).astype(BF16)
    for n in range(D_MODEL // MXU_DIM):
        cols = slice(n * MXU_DIM, (n + 1) * MXU_DIM)
        y = jnp.dot(act_ref[...], wout_ref[:, cols], preferred_element_type=F32)
        o_ref[:, cols] = x_ref[:, cols] + 0.5 * gate[:, cols] * y


def _ffn(x, mod, g, w_in, w_out, *, l, mod_base, seg0, seg_tokens):
    n_tok = x.shape[0]
    return pl.pallas_call(
        functools.partial(_ffn_kernel, mod_base=mod_base),
        grid=(n_tok // FFN_TILE,),
        in_specs=[
            pl.BlockSpec((FFN_TILE, D_MODEL), lambda i: (i, 0)),
            _mod_spec(l, seg0, seg_tokens, FFN_TILE),
            _layer_spec(g, l), _layer_spec(w_in, l), _layer_spec(w_out, l),
        ],
        out_specs=pl.BlockSpec((FFN_TILE, D_MODEL), lambda i: (i, 0)),
        out_shape=jax.ShapeDtypeStruct((n_tok, D_MODEL), F32),
        scratch_shapes=[pltpu.VMEM((FFN_TILE, D_MODEL), BF16), pltpu.VMEM((FFN_TILE, D_FF), BF16)],
        compiler_params=_cparams(("arbitrary",)),
        name="ffn",
    )(x, mod, g, w_in, w_out)


def _segment_ids():
    return lax.broadcasted_iota(jnp.int32, (1, LANES), 1) // SSM_CH


def _merge_segments(load_piece, rot):
    seg = _segment_ids()
    tiles = []
    for lt in range(SSM_ROW // LANES):
        acc = None
        for jj in range(SEGS_PER_TILE):
            src = (lt * SEGS_PER_TILE + jj - rot) % SSM_CHUNK
            piece = load_piece(src, lt)
            acc = piece if acc is None else jnp.where(seg == jj, piece, acc)
        tiles.append(acc)
    return jnp.concatenate(tiles, axis=-1)


def _roll_lanes(x, shift):
    shift %= x.shape[-1]
    return pltpu.roll(x, shift, axis=x.ndim - 1) if shift else x


def _inproj_kernel(*refs, n_in, n_out):
    x_ref, mod_ref, g_ref, w_ref, hsum_ref, qg_ref, kg_ref, ws_ref, gmb_ref = refs[:9]
    u_ref, q_ref, k_ref, v_ref, gm_ref, *tok_refs = refs[n_in:n_in + n_out]
    h_ref, xs_ref, z_ref = refs[n_in + n_out:]
    shift = mod_ref[0, 3:4, :]
    scale = mod_ref[0, 4:5, :]
    h_ref[...] = _normed(x_ref[...], g_ref[...], shift, scale).astype(BF16)
    n_rows = MIX_TILE // SSM_CHUNK

    def proj(c0, width):
        return jnp.dot(h_ref[...], w_ref[:, c0:c0 + width], preferred_element_type=F32)

    xs_qk = proj(0, SSM_WIDTH + 2 * NA_WIDTH)
    for j in range(SSM_WIDTH // LANES):
        xs_ref[j] = xs_qk[:, j * LANES:(j + 1) * LANES]
    for t in range(SSM_CHUNK):
        zt = jnp.concatenate([xs_ref[j, pl.ds(t, n_rows, stride=SSM_CHUNK), :] for j in range(SSM_WIDTH // LANES)],
                             axis=-1)
        z_ref[t] = _roll_lanes(zt, SSM_CH * t)
    for g in range(SSM_GROUPS):
        ug = _merge_segments(lambda src, lt: z_ref[src, :, lt * LANES:(lt + 1) * LANES], g)
        u_ref[g] = _roll_lanes(ug, -SSM_CH * g)

    c0 = SSM_WIDTH
    qk = xs_qk[:, c0:]
    sq = (qk * qk).astype(BF16)
    groups = range(0, 2 * NA_WIDTH, MXU_DIM)
    ms = jnp.dot(jnp.concatenate([sq[:, c:c + MXU_DIM] for c in groups], axis=0), hsum_ref[...],
                 preferred_element_type=F32)
    ms = jnp.concatenate([ms[i * MIX_TILE:(i + 1) * MIX_TILE, :] for i in range(len(groups))], axis=-1)
    qk = qk * lax.rsqrt(ms + RMS_EPS)
    q = qk[:, :NA_WIDTH] * qg_ref[...] * (NA_HEAD_DIM ** -0.5)
    k = qk[:, NA_WIDTH:] * kg_ref[...]
    v_uv = proj(c0 + 2 * NA_WIDTH, NA_WIDTH + 2 * GM_WIDTH)
    v = v_uv[:, :NA_WIDTH]
    for p in range(HEAD_PAIRS):
        lanes = slice(p * LANES, (p + 1) * LANES)
        q_ref[p] = q[:, lanes].astype(q_ref.dtype)
        k_ref[p] = k[:, lanes].astype(k_ref.dtype)
        v_ref[p] = v[:, lanes].astype(v_ref.dtype)
    if tok_refs:
        for slot in range(tok_refs[0].shape[1]):
            tok_refs[0][:, slot] = k.reshape(MIX_TILE // SEQ, SEQ, NA_WIDTH)
            tok_refs[1][:, slot] = v.reshape(MIX_TILE // SEQ, SEQ, NA_WIDTH)

    a = _gelu(v_uv[:, NA_WIDTH:])
    u = a[:, :GM_WIDTH]
    vv = a[:, GM_WIDTH:]
    mu = jnp.mean(vv, axis=-1, keepdims=True)
    vc = vv - mu
    var = jnp.mean(vc * vc, axis=-1, keepdims=True)
    vln = (vc * lax.rsqrt(var + LN_EPS)).astype(BF16)
    lane = lax.broadcasted_iota(jnp.int32, (1, GM_WIDTH), 1)
    gw = GM_WIDTH // GM_GROUPS
    for ci in range(MIX_TILE // GM_CHUNK):
        rows = slice(ci * GM_CHUNK, (ci + 1) * GM_CHUNK)
        vch = vln[rows, :]
        sp = gmb_ref[...]
        for gi in range(GM_GROUPS):
            t = jnp.dot(ws_ref[gi], vch, preferred_element_type=F32)
            sp = sp + jnp.where((lane >= gi * gw) & (lane < (gi + 1) * gw), t, 0.0)
        gm_ref[rows, :] = (u[rows, :] * sp).astype(BF16)


def _inproj(x, mod, g, w, hsum, qg, kg, ws, gmb, *, l, seg0, seg_tokens, is_ctx, caches=None):
    n_tok = x.shape[0]
    kv_dtype = BF16 if is_ctx else F32
    n_rows = MIX_TILE // SSM_CHUNK
    pair_spec = pl.BlockSpec((HEAD_PAIRS, MIX_TILE, LANES), lambda i: (0, i, 0))
    out_specs = [
        pl.BlockSpec((SSM_GROUPS, n_rows, SSM_ROW), lambda i: (0, i, 0)),
        pair_spec, pair_spec, pair_spec,
        pl.BlockSpec((MIX_TILE, GM_WIDTH), lambda i: (i, 0)),
    ]
    out_shape = [
        jax.ShapeDtypeStruct((SSM_GROUPS, n_tok // SSM_CHUNK, SSM_ROW), F32),
        jax.ShapeDtypeStruct((HEAD_PAIRS, n_tok, LANES), BF16),
        jax.ShapeDtypeStruct((HEAD_PAIRS, n_tok, LANES), kv_dtype),
        jax.ShapeDtypeStruct((HEAD_PAIRS, n_tok, LANES), kv_dtype),
        jax.ShapeDtypeStruct((n_tok, GM_WIDTH), BF16),
    ]
    args = [x, mod, g, w, hsum, qg, kg, ws, gmb]
    in_specs = [
        pl.BlockSpec((MIX_TILE, D_MODEL), lambda i: (i, 0)),
        _mod_spec(l, seg0, seg_tokens, MIX_TILE),
        _layer_spec(g, l), _layer_spec(w, l),
        _const_spec((MXU_DIM, MXU_DIM)),
        _layer_spec(qg, l), _layer_spec(kg, l), _layer_spec(ws, l), _layer_spec(gmb, l),
    ]
    aliases = {}
    if is_ctx:
        bt = MIX_TILE // SEQ
        cache_shape = jax.ShapeDtypeStruct((n_tok // SEQ, DEPTH, SEQ, NA_WIDTH), F32)
        if caches is None:
            out_specs += [pl.BlockSpec((bt, DEPTH, SEQ, NA_WIDTH), lambda i: (i, 0, 0, 0))] * 2
        else:
            out_specs += [pl.BlockSpec((bt, 1, SEQ, NA_WIDTH), lambda i: (i, l, 0, 0))] * 2
        out_shape += [cache_shape] * 2
        if caches is not None:
            aliases = {len(args): len(out_shape) - 2, len(args) + 1: len(out_shape) - 1}
            args += list(caches)
            in_specs += [pl.BlockSpec(memory_space=pl.ANY)] * 2
    return pl.pallas_call(
        functools.partial(_inproj_kernel, n_in=len(args), n_out=len(out_shape)),
        grid=(n_tok // MIX_TILE,),
        in_specs=in_specs,
        out_specs=out_specs,
        out_shape=out_shape,
        input_output_aliases=aliases,
        scratch_shapes=[pltpu.VMEM((MIX_TILE, D_MODEL), BF16), pltpu.VMEM((SSM_WIDTH // LANES, MIX_TILE, LANES), F32),
                        pltpu.VMEM((SSM_CHUNK, n_rows, SSM_ROW), F32)],
        compiler_params=_cparams(("arbitrary",)),
        name="inproj_ctx" if is_ctx else "inproj_lat",
    )(*args)


def _shift_rows(x, d, row, down):
    if down:
        return jnp.where(row >= d, pltpu.roll(x, d, axis=0), 0.0)
    return jnp.where(row < SUBLANES - d, pltpu.roll(x, SUBLANES - d, axis=0), 0.0)


def _scan_tile(w, wp, c, cp, cst, row, down):
    x, xp = w, wp
    for si, d in enumerate(SCAN_SHIFTS):
        a, b, bp = cst[3 * si], cst[3 * si + 1], cst[3 * si + 2]
        xs, xps = _shift_rows(x, d, row, down), _shift_rows(xp, d, row, down)
        x, xp = x + a * xs + b * xps, xp + a * xps + bp * xs
    a, b, bp = cst[N_SCAN_CONST - 3], cst[N_SCAN_CONST - 2], cst[N_SCAN_CONST - 1]
    s = x + a * c + b * cp
    sp = xp + a * cp + bp * c
    edge = 0 if down else SUBLANES - 1
    last = SUBLANES - 1 if down else 0
    seen = jnp.where(row == edge, c, _shift_rows(s, 1, row, down))
    c_new = jnp.broadcast_to(s[last:last + 1, :], s.shape)
    cp_new = jnp.broadcast_to(sp[last:last + 1, :], sp.shape)
    return seen, c_new, cp_new


def _ssm_kernel(u_ref, mp_ref, q_ref, cst_ref, dvec_ref, s0_ref, y_ref, fin_ref, w_ref, sq_ref, *, n_batch, n_chunks):
    gb = u_ref.shape[0]
    n_rows = n_batch * n_chunks
    row_blk = min(n_rows, SSM_ROW_BLOCK)
    for g in range(gb):
        for r0 in range(0, n_rows, row_blk):
            rows = slice(r0, r0 + row_blk)
            ug = u_ref[g, rows, :]
            wy = jnp.dot(ug.astype(BF16), mp_ref[g], preferred_element_type=F32)
            y_ref[g, rows, :] = wy[:, :SSM_ROW] + ug * dvec_ref[g]
            w_ref[g, rows, :] = wy[:, SSM_ROW:]

    n_tiles = n_chunks // SUBLANES
    row = lax.broadcasted_iota(jnp.int32, (SUBLANES, LANES), 0)

    def tile_step(b, kt, carry):
        out = []
        for g in range(gb):
            cf, cfp, cb, cbp = carry[4 * g:4 * g + 4]
            rf = pl.multiple_of(b * n_chunks + kt * SUBLANES, SUBLANES)
            rb = pl.multiple_of(b * n_chunks + (n_tiles - 1 - kt) * SUBLANES, SUBLANES)
            wf = w_ref[g, pl.ds(rf, SUBLANES), :]
            wb = w_ref[g, pl.ds(rb, SUBLANES), :]
            cst_f = [cst_ref[g, 0, i] for i in range(N_SCAN_CONST)]
            cst_b = [cst_ref[g, 1, i] for i in range(N_SCAN_CONST)]
            seen_f, cf, cfp = _scan_tile(wf[:, 0:LANES], wf[:, LANES:2 * LANES], cf, cfp, cst_f, row, True)
            seen_b, cb, cbp = _scan_tile(wb[:, 2 * LANES:3 * LANES], wb[:, 3 * LANES:], cb, cbp, cst_b, row, False)
            sq_ref[g, pl.ds(rf, SUBLANES), 0:LANES] = seen_f
            sq_ref[g, pl.ds(rb, SUBLANES), LANES:2 * LANES] = seen_b
            out += [cf, cfp, cb, cbp]
        return tuple(out)

    def batch_body(b, _):
        carry = []
        for g in range(gb):
            for i in range(4):
                carry.append(jnp.broadcast_to(s0_ref[g, i, pl.ds(b, 1), :], (SUBLANES, LANES)))
        carry = tuple(carry)
        if n_tiles <= 2:
            for kt in range(n_tiles):
                carry = tile_step(b, kt, carry)
        else:
            carry = lax.fori_loop(0, n_tiles, lambda kt, c: tile_step(b, kt, c), carry)
        for g in range(gb):
            fin_ref[g, 0, pl.ds(b, 1), :] = carry[4 * g][0:1, :]
            fin_ref[g, 1, pl.ds(b, 1), :] = carry[4 * g + 2][0:1, :]
        return 0

    lax.fori_loop(0, n_batch, batch_body, 0)

    for g in range(gb):
        for r0 in range(0, n_rows, row_blk):
            rows = slice(r0, r0 + row_blk)
            y_ref[g, rows, :] += jnp.dot(sq_ref[g, rows, :].astype(BF16), q_ref[g], preferred_element_type=F32)


def _ssm_core(u, mp, q, cst, dvec, s0, *, l, n_batch, n_chunks):
    n_rows = n_batch * n_chunks
    gb = SSM_GROUP_BLOCK
    g3 = lambda i: (i, 0, 0)
    g4 = lambda i: (i, 0, 0, 0)
    return pl.pallas_call(
        functools.partial(_ssm_kernel, n_batch=n_batch, n_chunks=n_chunks),
        grid=(SSM_GROUPS // gb,),
        in_specs=[
            pl.BlockSpec((gb, n_rows, SSM_ROW), g3),
            pl.BlockSpec((None, gb, SSM_ROW, SSM_ROW + SSM_W_COLS), lambda i: (l, i, 0, 0)),
            pl.BlockSpec((None, gb, 2 * LANES, SSM_ROW), lambda i: (l, i, 0, 0)),
            pl.BlockSpec((None, gb, 2, N_SCAN_CONST, SUBLANES, LANES), lambda i: (l, i, 0, 0, 0, 0)),
            pl.BlockSpec((None, gb, 1, SSM_ROW), lambda i: (l, i, 0, 0)),
            pl.BlockSpec((gb, 4, n_batch, LANES), g4),
        ],
        out_specs=[
            pl.BlockSpec((gb, n_rows, SSM_ROW), g3),
            pl.BlockSpec((gb, 2, n_batch, LANES), g4),
        ],
        out_shape=[
            jax.ShapeDtypeStruct((SSM_GROUPS, n_rows, SSM_ROW), F32),
            jax.ShapeDtypeStruct((SSM_GROUPS, 2, n_batch, LANES), F32),
        ],
        scratch_shapes=[pltpu.VMEM((gb, n_rows, SSM_W_COLS), F32), pltpu.VMEM((gb, n_rows, 2 * LANES), F32)],
        compiler_params=_cparams(("arbitrary",)),
        name="ssm_core",
    )(u, mp, q, cst, dvec, s0)


def _cmul(a, b):
    return a[0] * b[0] - a[1] * b[1], a[0] * b[1] + a[1] * b[0]


_CHUNK_IDX = np.arange(SSM_CHUNK)


def _ssm_exponents():
    t = SSM_CHUNK
    idx = _CHUNK_IDX.astype(np.float32)
    both = lambda a: np.stack([a, a], axis=1)
    lane_step = idx - (t - 1) / 2
    tile_rows = np.arange(1, SUBLANES + 1, dtype=np.float32) * t
    tables = dict(
        one=both(np.ones(1, np.float32)),
        incr=np.stack([t - 1 - idx, idx], axis=1),
        seen=np.stack([idx + 1, t - idx], axis=1),
        resp_in=np.stack([-lane_step, lane_step], axis=1),
        resp_out=np.stack([lane_step, -lane_step], axis=1),
        shift=both(np.asarray(SCAN_SHIFTS, np.float32) * t),
        tile=np.stack([tile_rows, tile_rows[::-1]], axis=1),
    )
    spans, start = {}, 0
    for name, tab in tables.items():
        spans[name] = slice(start, start + len(tab))
        start += len(tab)
    return np.concatenate(list(tables.values()), axis=0), spans


def _ssm_params(p):
    t = SSM_CHUNK
    n_l = p['ssm_lambda_re'].shape[0]
    lam = (p['ssm_lambda_re'].astype(F32), p['ssm_lambda_im'].astype(F32))
    dt = jnp.exp(p['ssm_log_dt'].astype(F32))[..., None]
    arg = (lam[0] * dt, lam[1] * dt)
    expo, span = _ssm_exponents()
    n = jnp.asarray(expo)[:, None, :, None, None]
    mag = jnp.exp(n * arg[0][None])
    pw = (mag * jnp.cos(n * arg[1][None]), mag * jnp.sin(n * arg[1][None]))
    take = lambda name: [x[span[name]] for x in pw]
    lanes_last = lambda xs: [jnp.moveaxis(x, 0, -1) for x in xs]

    lbar = [x[0] for x in take('one')]
    num = (lbar[0] - 1.0, lbar[1])
    den = lam[0] * lam[0] + lam[1] * lam[1]
    zoh = ((num[0] * lam[0] + num[1] * lam[1]) / den, (num[1] * lam[0] - num[0] * lam[1]) / den)
    b_mat = (p['ssm_b_re'].astype(F32), p['ssm_b_im'].astype(F32))
    bbar = _cmul((zoh[0][..., None], zoh[1][..., None]), b_mat)
    c_mat = [jnp.swapaxes(p[k].astype(F32), -1, -2) for k in ('ssm_c_re', 'ssm_c_im')]
    ct = [jnp.tile(x, (1, 1, 1, 1, t)) for x in c_mat]
    bt = [jnp.tile(x, (1, 1, 1, 1, t)) for x in bbar]

    per_lane = lambda name: [jnp.repeat(x, SSM_CH, axis=-1) for x in lanes_last(take(name))]
    xm = _cmul(per_lane('resp_in'), bt)
    ym = _cmul(per_lane('resp_out'), ct)
    resp = jnp.einsum('ldgpx,ldgpy->ldgxy', jnp.concatenate([xm[0], -xm[1]], axis=3),
                      jnp.concatenate([ym[0], ym[1]], axis=3), precision='highest')
    steps_x = np.repeat(_CHUNK_IDX, SSM_CH)
    causal = np.stack([steps_x[None, :] >= steps_x[:, None], steps_x[:, None] >= steps_x[None, :]])
    mats = jnp.sum(jnp.where(jnp.asarray(causal)[None, :, None], resp, 0.0), axis=1)

    incr = _cmul([x[..., None] for x in take('incr')], [x[None] for x in bbar])
    incr = [jnp.transpose(x, (1, 2, 3, 0, 5, 4)) for x in incr]
    pcat = jnp.concatenate([incr[0], incr[1], incr[1], incr[0]], axis=-1)
    pcat = jnp.concatenate([pcat[:, 0], pcat[:, 1]], axis=-1).reshape(n_l, SSM_GROUPS, SSM_ROW, SSM_W_COLS)
    mp = jnp.concatenate([mats, pcat], axis=-1).astype(BF16)

    qm = _cmul(ct, per_lane('seen'))
    q = jnp.concatenate([qm[0][:, 0], -qm[1][:, 0], qm[0][:, 1], -qm[1][:, 1]], axis=2).astype(BF16)

    def forms(name, rows):
        pr, pi = [jnp.moveaxis(x, 0, 3) for x in take(name)]
        f = jnp.stack([jnp.concatenate([pr, pr], -1), jnp.concatenate([-pi, pi], -1),
                       jnp.concatenate([pi, -pi], -1)], axis=4)
        return f if rows else jnp.broadcast_to(
            f.reshape(n_l, 2, SSM_GROUPS, 3 * len(SCAN_SHIFTS), 1, LANES),
            (n_l, 2, SSM_GROUPS, 3 * len(SCAN_SHIFTS), SUBLANES, LANES))
    per_row = jnp.swapaxes(forms('tile', True), 3, 4)
    cst = jnp.swapaxes(jnp.concatenate([forms('shift', False), per_row], axis=3), 1, 2)
    dvec = jnp.tile(p['ssm_d'].astype(F32).reshape(n_l, SSM_GROUPS, 1, SSM_CH), (1, 1, 1, SSM_CHUNK))
    return mp, q, cst, dvec


def _ssm_states_in(state_ssm):
    st = jnp.transpose(state_ssm.astype(F32), (1, 3, 2, 0, 4, 5))
    re, im = st[..., 0], st[..., 1]
    ri, ir = jnp.concatenate([re, im], -1), jnp.concatenate([im, re], -1)
    return jnp.stack([ri[:, :, 0], ir[:, :, 0], ri[:, :, 1], ir[:, :, 1]], axis=2)


def _ssm_states_out(fins):
    fin = jnp.stack(fins, axis=0)
    fin = jnp.stack([fin[..., :SSM_STATE], fin[..., SSM_STATE:]], axis=-1)
    return jnp.transpose(fin, (3, 0, 2, 1, 4, 5))


def _head_lane_mask(hh):
    lane = lax.broadcasted_iota(jnp.int32, (1, LANES), 1)
    return (lane >= hh * NA_HEAD_DIM) & (lane < (hh + 1) * NA_HEAD_DIM)


_NT_DIMS = (((1,), (1,)), ((), ()))


def _ctx_attn_kernel(q_ref, k_ref, v_ref, o_ref):
    for p in range(HEAD_PAIRS):
        qp, kp, vp = q_ref[p], k_ref[p], v_ref[p]
        lm0 = _head_lane_mask(0)
        zero = jnp.zeros_like(qp)
        q2 = jnp.concatenate([jnp.where(lm0, qp, zero), jnp.where(lm0, zero, qp)], axis=0)
        s = lax.dot_general(q2, kp, _NT_DIMS, preferred_element_type=F32)
        e = jnp.exp(s - jnp.max(s, axis=-1, keepdims=True))
        o = jnp.dot(e.astype(BF16), vp, preferred_element_type=F32) / jnp.sum(e, axis=-1, keepdims=True)
        o_ref[:, p * LANES:(p + 1) * LANES] = jnp.where(lm0, o[:SEQ, :], o[SEQ:, :]).astype(BF16)


def _ctx_attn(q, k, v, n_batch):
    spec = pl.BlockSpec((HEAD_PAIRS, SEQ, LANES), lambda b: (0, b, 0))
    return pl.pallas_call(
        _ctx_attn_kernel,
        grid=(n_batch,),
        in_specs=[spec, spec, spec],
        out_specs=pl.BlockSpec((SEQ, NA_WIDTH), lambda b: (b, 0)),
        out_shape=jax.ShapeDtypeStruct((n_batch * SEQ, NA_WIDTH), BF16),
        compiler_params=_cparams(("arbitrary",)),
        name="ctx_attn",
    )(q, k, v)


def _na_kernel(q_ref, k_ref, v_ref, kc_ref, vc_ref, t2_ref, rm_ref, o_ref, qb_ref, os_ref, tb_ref):
    rt = pl.program_id(1)
    n_rt = pl.num_programs(1)

    @pl.when((pl.program_id(0) == 0) & (rt == 0))
    def _():
        for h in range(NA_HEADS):
            for n in range(NA_COL_BLOCKS):
                for rl in range(NA_Q_ROWS):
                    lane0 = (NA_OFF_START - rl) * NA_KCOLS
                    tb_ref[h, n, rl * NA_COLS:(rl + 1) * NA_COLS, :] = (
                        t2_ref[h, n, :, lane0:lane0 + NA_K_ROWS * NA_KCOLS])

    r0 = rt * NA_Q_ROWS
    ttype = jnp.where(rt == 0, 0, jnp.where(rt == n_rt - 1, 2, 1))
    rmask = rm_ref[ttype]
    blk_q = NA_Q_ROWS * NA_COLS

    def pair_body(p, carry):
        lm0 = _head_lane_mask(0)
        for n in range(NA_COL_BLOCKS):
            for rl in range(NA_Q_ROWS):
                qrow = q_ref[p, rl * GRID_W + n * NA_COLS:rl * GRID_W + (n + 1) * NA_COLS, :]
                zero = jnp.zeros_like(qrow)
                base = 2 * n * blk_q + rl * NA_COLS
                qb_ref[base:base + NA_COLS, :] = jnp.where(lm0, qrow, zero)
                qb_ref[base + blk_q:base + blk_q + NA_COLS, :] = jnp.where(lm0, zero, qrow)
        kcp, vcp = kc_ref[0, p], vc_ref[0, p]
        sc_all = lax.dot_general(qb_ref[...], kcp, _NT_DIMS, preferred_element_type=F32)
        for n in range(NA_COL_BLOCKS):
            kparts, vparts = [], []
            for i in range(NA_K_ROWS):
                kr = jnp.clip(r0 - (NA_K_ROWS - NA_Q_ROWS) // 2 + i, 0, GRID_W - 1)
                start = pl.multiple_of(kr * GRID_W + NA_KC0[n], 8)
                kparts.append(k_ref[p, pl.ds(start, NA_KCOLS), :])
                vparts.append(v_ref[p, pl.ds(start, NA_KCOLS), :])
            kblk = jnp.concatenate(kparts, axis=0).astype(BF16)
            vblk = jnp.concatenate(vparts, axis=0).astype(BF16)
            rows2 = slice(2 * n * blk_q, 2 * (n + 1) * blk_q)
            s = lax.dot_general(qb_ref[rows2, :], kblk, _NT_DIMS, preferred_element_type=F32)
            s = s + jnp.concatenate([tb_ref[2 * p, n] + rmask, tb_ref[2 * p + 1, n] + rmask], axis=0)
            sc = sc_all[rows2, :]
            m = jnp.maximum(jnp.max(s, axis=-1, keepdims=True), jnp.max(sc, axis=-1, keepdims=True))
            e = jnp.exp(s - m)
            ec = jnp.exp(sc - m)
            den = jnp.sum(e, axis=-1, keepdims=True) + jnp.sum(ec, axis=-1, keepdims=True)
            o = (jnp.dot(e.astype(BF16), vblk, preferred_element_type=F32)
                 + jnp.dot(ec.astype(BF16), vcp, preferred_element_type=F32)) / den
            oblk = jnp.where(lm0, o[:blk_q, :], o[blk_q:, :])
            for rl in range(NA_Q_ROWS):
                os_ref[p, rl * GRID_W + n * NA_COLS:rl * GRID_W + (n + 1) * NA_COLS, :] = (
                    oblk[rl * NA_COLS:(rl + 1) * NA_COLS, :])
        return carry

    lax.fori_loop(0, HEAD_PAIRS, pair_body, 0)
    for p in range(HEAD_PAIRS):
        o_ref[:, p * LANES:(p + 1) * LANES] = os_ref[p].astype(BF16)


def _na_attn(q, k, v, kc, vc, tb, rm, n_batch, *, l):
    tile_tok = NA_Q_ROWS * GRID_W
    n_rt = DEC_SEQ // tile_tok
    blk_q, blk_k = NA_Q_ROWS * NA_COLS, NA_K_ROWS * NA_KCOLS
    return pl.pallas_call(
        _na_kernel,
        grid=(n_batch, n_rt),
        in_specs=[
            pl.BlockSpec((HEAD_PAIRS, tile_tok, LANES), lambda b, r: (0, b * n_rt + r, 0)),
            pl.BlockSpec((HEAD_PAIRS, DEC_SEQ, LANES), lambda b, r: (0, b, 0)),
            pl.BlockSpec((HEAD_PAIRS, DEC_SEQ, LANES), lambda b, r: (0, b, 0)),
            pl.BlockSpec((1, None, HEAD_PAIRS, PAST_LEN, LANES), lambda b, r: (b, l, 0, 0, 0)),
            pl.BlockSpec((1, None, HEAD_PAIRS, PAST_LEN, LANES), lambda b, r: (b, l, 0, 0, 0)),
            _layer_spec(tb, l),
            _const_spec(rm.shape),
        ],
        out_specs=pl.BlockSpec((tile_tok, NA_WIDTH), lambda b, r: (b * n_rt + r, 0)),
        out_shape=jax.ShapeDtypeStruct((n_batch * DEC_SEQ, NA_WIDTH), BF16),
        scratch_shapes=[
            pltpu.VMEM((2 * tile_tok, LANES), BF16),
            pltpu.VMEM((HEAD_PAIRS, tile_tok, LANES), F32),
            pltpu.VMEM((NA_HEADS, NA_COL_BLOCKS, blk_q, blk_k), F32)],
        compiler_params=_cparams(("arbitrary", "arbitrary")),
        name="na_attn",
    )(q, k, v, kc, vc, tb, rm)


def _na_tables(rpb):
    n = np.arange(NA_COL_BLOCKS)
    cc = np.arange(NA_COLS)
    kk = np.arange(NA_KCOLS)
    rl = np.arange(NA_Q_ROWS)
    ki = np.arange(NA_K_ROWS)
    n_dr, n_dc = 2 * NA_MAX_ROWS - 1, 2 * NA_COLS - 1
    c = n[:, None] * NA_COLS + cc[None, :]
    kcol = np.asarray(NA_KC0)[:, None] + kk[None, :]
    dc = np.clip(kcol[:, None, :] - c[:, :, None], -(NA_COLS - 1), NA_COLS - 1) + (NA_COLS - 1)
    cs = np.clip(c - NA_COLS // 2, 0, GRID_W - NA_COLS)
    col_ok = (kcol[:, None, :] >= cs[:, :, None]) & (kcol[:, None, :] < cs[:, :, None] + NA_COLS)
    half = (NA_K_ROWS - NA_Q_ROWS) // 2
    dr = np.clip(ki[None, :] - half - rl[:, None] + (NA_MAX_ROWS - 1), 0, n_dr - 1)
    oh_dc = jnp.asarray(dc[None] == np.arange(n_dc)[:, None, None, None], F32)
    off = np.clip(np.arange(NA_OFF_ROWS) - NA_OFF_PAD, 0, n_dr - 1)
    assert np.array_equal(dr, off[ki[None] + (NA_OFF_START - rl)[:, None]])
    oh_off = jnp.asarray(off[None, :] == np.arange(n_dr)[:, None], F32)
    rows_p = jnp.einsum('lhab,aA->lhAb', rpb.astype(F32), oh_off, precision='highest')
    cols = jnp.einsum('lhAb,bnck->lhncAk', rows_p, oh_dc, precision='highest')
    cols = jnp.where(jnp.asarray(col_ok)[None, None, :, :, None, :], cols, NEG_INF)
    tb = cols.reshape(-1, NA_HEADS, NA_COL_BLOCKS, NA_COLS, NA_OFF_ROWS * NA_KCOLS)
    rows = GRID_W
    masks = []
    for r0 in (0, NA_Q_ROWS, rows - NA_Q_ROWS):
        r = r0 + rl
        rs = np.clip(r - NA_MAX_ROWS // 2, 0, rows - NA_MAX_ROWS)
        kr = r0 - half + ki
        ok = (kr[None, :] >= rs[:, None]) & (kr[None, :] < rs[:, None] + NA_MAX_ROWS)
        full = np.broadcast_to(ok[:, None, :, None], (NA_Q_ROWS, NA_COLS, NA_K_ROWS, NA_KCOLS))
        masks.append(np.where(full, 0.0, NEG_INF).reshape(NA_Q_ROWS * NA_COLS, NA_K_ROWS * NA_KCOLS))
    rm = jnp.asarray(np.stack(masks), F32)
    return tb, rm


def _outffn_kernel(x_ref, mod_ref, y_ref, na_ref, gm_ref, gw_ref, gb_ref, wo_ref, g_ref, win_ref, wout_ref, o_ref,
                   yr_ref, ys_ref, cat_ref, xm_ref, h_ref, act_ref):
    gate = mod_ref[0, 5:6, :]
    n_rows = OUT_TILE // SSM_CHUNK
    for g in range(SSM_GROUPS):
        yr_ref[g] = _roll_lanes(y_ref[g], SSM_CH * g)
    for t in range(SSM_CHUNK):
        yt = _merge_segments(lambda src, lt: yr_ref[src, :, lt * LANES:(lt + 1) * LANES], t)
        yt = _roll_lanes(yt, -SSM_CH * t)
        for j in range(SSM_WIDTH // LANES):
            ys_ref[j, pl.ds(t, n_rows, stride=SSM_CHUNK), :] = yt[:, j * LANES:(j + 1) * LANES]
    y = _gelu(jnp.concatenate([ys_ref[j] for j in range(SSM_WIDTH // LANES)], axis=-1))
    z = jnp.dot(y.astype(BF16), gw_ref[...], preferred_element_type=F32) + gb_ref[...]
    cat_ref[:, 0:SSM_WIDTH] = (y * _sigmoid(z)).astype(BF16)
    cat_ref[:, SSM_WIDTH:SSM_WIDTH + NA_WIDTH] = na_ref[...]
    cat_ref[:, SSM_WIDTH + NA_WIDTH:] = gm_ref[...]
    mix = jnp.dot(cat_ref[...], wo_ref[...], preferred_element_type=F32)
    xm_ref[...] = x_ref[...] + gate * mix
    _ffn_kernel(xm_ref, mod_ref, g_ref, win_ref, wout_ref, o_ref, h_ref, act_ref, mod_base=6)


def _outffn(x, mod, y, na, gm, gw, gb, wo, g, w_in, w_out, *, l, seg0, seg_tokens):
    n_tok = x.shape[0]
    n_rows = OUT_TILE // SSM_CHUNK
    tok = lambda width: pl.BlockSpec((OUT_TILE, width), lambda i: (i, 0))
    return pl.pallas_call(
        _outffn_kernel,
        grid=(n_tok // OUT_TILE,),
        in_specs=[
            tok(D_MODEL),
            _mod_spec(l, seg0, seg_tokens, OUT_TILE),
            pl.BlockSpec((SSM_GROUPS, n_rows, SSM_ROW), lambda i: (0, i, 0)),
            tok(NA_WIDTH), tok(GM_WIDTH),
            _layer_spec(gw, l), _layer_spec(gb, l), _layer_spec(wo, l),
            _layer_spec(g, l), _layer_spec(w_in, l), _layer_spec(w_out, l),
        ],
        out_specs=tok(D_MODEL),
        out_shape=jax.ShapeDtypeStruct((n_tok, D_MODEL), F32),
        scratch_shapes=[pltpu.VMEM((SSM_GROUPS, n_rows, SSM_ROW), F32),
                        pltpu.VMEM((SSM_WIDTH // LANES, OUT_TILE, LANES), F32),
                        pltpu.VMEM((OUT_TILE, D_MODEL), BF16),
                        pltpu.VMEM((OUT_TILE, D_MODEL), F32),
                        pltpu.VMEM((OUT_TILE, D_MODEL), BF16), pltpu.VMEM((OUT_TILE, D_FF), BF16)],
        compiler_params=_cparams(("arbitrary",)),
        name="outffn",
    )(x, mod, y, na, gm, gw, gb, wo, g, w_in, w_out)


def _prep(p):
    heads = np.arange(MXU_DIM) // NA_HEAD_DIM
    hsum = jnp.asarray((heads[:, None] == heads[None, :]) / NA_HEAD_DIM, BF16)
    gmb = jnp.repeat(jnp.swapaxes(p['gm_bs'], 1, 2).astype(F32), GM_WIDTH // GM_GROUPS, axis=2)
    tb, rm = _na_tables(p['na_rpb'])
    row = lambda a: a.astype(F32).reshape(DEPTH, 1, -1)
    return dict(
        g1=row(p['norm_ffn1']), g2=row(p['norm_mix']), g3=row(p['norm_ffn2']),
        f1_in=p['ffn1_w_in'].astype(BF16), f1_out=p['ffn1_w_out'].astype(BF16),
        f2_in=p['ffn2_w_in'].astype(BF16), f2_out=p['ffn2_w_out'].astype(BF16),
        w_in=p['w_in'].astype(BF16), w_out=p['w_out'].astype(BF16),
        hsum=hsum,
        qg=row(jnp.tile(p['na_q_norm'], (1, NA_HEADS))), kg=row(jnp.tile(p['na_k_norm'], (1, NA_HEADS))),
        ws=p['gm_ws'].astype(BF16), gmb=gmb,
        ssm=_ssm_params(p),
        glu_w=p['ssm_glu_w'].astype(BF16), glu_b=row(p['ssm_glu_b']),
        tb=tb, rm=rm,
    )


def _pairs_major(t):
    b, n_l, s = t.shape[:3]
    return jnp.transpose(t.reshape(b, n_l, s, HEAD_PAIRS, LANES), (0, 1, 3, 2, 4)).astype(BF16)


def _trunk_layer(x, mod, w, l, *, n_batch, seq_len, caches=None, is_ctx=False, ctx_kv=None, ssm_init=None):
    seg = dict(l=l, seg0=0 if is_ctx else 1, seg_tokens=x.shape[0] if is_ctx else seq_len)
    x = _ffn(x, mod, w['g1'], w['f1_in'], w['f1_out'], mod_base=0, **seg)
    outs = _inproj(x, mod, w['g2'], w['w_in'], w['hsum'], w['qg'], w['kg'], w['ws'], w['gmb'],
                   is_ctx=is_ctx, caches=caches, **seg)
    u, q, k, v, gm = outs[:5]
    s0 = jnp.zeros((SSM_GROUPS, 4, n_batch, LANES), F32) if is_ctx else ssm_init[l]
    y, fin = _ssm_core(u, *w['ssm'], s0, l=l, n_batch=n_batch, n_chunks=seq_len // SSM_CHUNK)
    if is_ctx:
        na = _ctx_attn(q, k, v, n_batch)
    else:
        na = _na_attn(q, k, v, ctx_kv[0], ctx_kv[1], w['tb'], w['rm'], n_batch, l=l)
    x = _outffn(x, mod, y, na, gm, w['glu_w'], w['glu_b'], w['w_out'], w['g3'], w['f2_in'], w['f2_out'], **seg)
    return x, (tuple(outs[5:]), fin)


def kernel(x_prompt, x_sample, c, cache_k, cache_v, state_ssm, c_ctx, w_ada, b_ada, norm_ffn1, ffn1_w_in, ffn1_w_out, norm_mix, w_in, w_out, ssm_lambda_re, ssm_lambda_im, ssm_log_dt, ssm_b_re, ssm_b_im, ssm_c_re, ssm_c_im, ssm_d, ssm_glu_w, ssm_glu_b, na_q_norm, na_k_norm, na_rpb, gm_ws, gm_bs, norm_ffn2, ffn2_w_in, ffn2_w_out):
    p = dict(norm_ffn1=norm_ffn1, ffn1_w_in=ffn1_w_in, ffn1_w_out=ffn1_w_out, norm_mix=norm_mix, w_in=w_in,
             w_out=w_out, ssm_lambda_re=ssm_lambda_re, ssm_lambda_im=ssm_lambda_im, ssm_log_dt=ssm_log_dt,
             ssm_b_re=ssm_b_re, ssm_b_im=ssm_b_im, ssm_c_re=ssm_c_re, ssm_c_im=ssm_c_im, ssm_d=ssm_d,
             ssm_glu_w=ssm_glu_w, ssm_glu_b=ssm_glu_b, na_q_norm=na_q_norm, na_k_norm=na_k_norm, na_rpb=na_rpb,
             gm_ws=gm_ws, gm_bs=gm_bs, norm_ffn2=norm_ffn2, ffn2_w_in=ffn2_w_in, ffn2_w_out=ffn2_w_out)
    batch, dec_batch = x_prompt.shape[0], x_sample.shape[0]
    cond8 = jnp.zeros((8, D_MODEL), F32).at[0].set(c_ctx).at[1:1 + dec_batch].set(c)
    mod = _adaln(cond8, w_ada, b_ada).reshape(DEPTH, 8, N_MOD, D_MODEL)

    xp = x_prompt.reshape(batch * SEQ, D_MODEL)
    xs = x_sample.reshape(dec_batch * DEC_SEQ, D_MODEL)
    w = _prep(p)
    ctx_kv = (_pairs_major(cache_k), _pairs_major(cache_v))
    ssm_init = _ssm_states_in(state_ssm)
    caches, fins = None, []
    for l in range(DEPTH):
        xp, (caches, fin) = _trunk_layer(xp, mod, w, l, n_batch=batch, seq_len=SEQ, caches=caches, is_ctx=True)
        fins.append(fin)
        xs, _ = _trunk_layer(xs, mod, w, l, n_batch=dec_batch, seq_len=DEC_SEQ, ctx_kv=ctx_kv, ssm_init=ssm_init)
    cache_shape = (batch, DEPTH, SEQ, NA_HEADS, NA_HEAD_DIM)
    return (xp.reshape(batch, SEQ, D_MODEL), xs.reshape(dec_batch, DEC_SEQ, D_MODEL),
            caches[0].reshape(cache_shape), caches[1].reshape(cache_shape), _ssm_states_out(fins))
```

```python
import functools

import numpy as np
import jax
import jax.numpy as jnp
from jax import lax
from jax.experimental import pallas as pl
from jax.experimental.pallas import tpu as pltpu

D_MODEL = 1024
DEPTH = 2
SEQ = 256
DEC_SEQ = 4096
PAST_LEN = 256
GRID_W = 64
SSM_WIDTH = 256
SSM_CH = 16
SSM_GROUPS = 16
SSM_STATE = 64
NA_WIDTH = 512
NA_HEAD_DIM = 64
NA_HEADS = 8
NA_MAX_ROWS = 8
NA_COLS = 16
GM_WIDTH = 256
GM_GROUPS = 4
GM_CHUNK = 128
D_FF = 2816
N_MOD = 9
RMS_EPS = 1e-6
LN_EPS = 1e-5
NEG_INF = -1e30

F32 = jnp.float32
BF16 = jnp.bfloat16

LANES = 128
SUBLANES = 8
MXU_DIM = 256
VMEM_LIMIT_BYTES = 56 * 1024 * 1024

FFN_TILE = 1024
MIX_TILE = 1024
OUT_TILE = 512
FF_CHUNK = MXU_DIM
N_FF_CHUNKS = D_FF // FF_CHUNK
SSM_CHUNK = 16
SSM_ROW = SSM_CHUNK * SSM_CH
SSM_GROUP_BLOCK = 4
SSM_ROW_BLOCK = 512
SSM_W_COLS = 4 * LANES
SCAN_SHIFTS = (1, 2, 4)
N_SCAN_CONST = 3 * len(SCAN_SHIFTS) + 3
HEAD_PAIRS = NA_HEADS // 2
CTX_ATTN_BATCH = 4
NA_Q_ROWS = 8
NA_K_ROWS = 16
NA_KCOLS = 2 * NA_COLS
NA_COL_BLOCKS = GRID_W // NA_COLS
NA_KC0 = (0, 8, 24, 32)
SEGS_PER_TILE = LANES // SSM_CH
NA_OFF_PAD = (NA_Q_ROWS - 1) - ((NA_MAX_ROWS - 1) - (NA_K_ROWS - NA_Q_ROWS) // 2)
NA_OFF_START = NA_Q_ROWS - 1
NA_OFF_ROWS = NA_OFF_START + NA_K_ROWS


def _silu(x):
    return x * (1.0 / (1.0 + jnp.exp(-x)))


def _sigmoid(x):
    return 1.0 / (1.0 + jnp.exp(-x))


def _gelu(x):
    return 0.5 * x * (1.0 + jnp.tanh(0.7978845608028654 * (x + 0.044715 * (x * x * x))))


def _cparams(sem):
    return pltpu.CompilerParams(dimension_semantics=sem, vmem_limit_bytes=VMEM_LIMIT_BYTES)


def _const_spec(shape):
    nd = len(shape)
    return pl.BlockSpec(shape, lambda *_: (0,) * nd, pipeline_mode=pl.Buffered(1))


def _layer_spec(stacked, l):
    nd = stacked.ndim
    return pl.BlockSpec((None,) + stacked.shape[1:], lambda *_: (l,) + (0,) * (nd - 1), pipeline_mode=pl.Buffered(1))


def _ada_kernel(c_ref, w_ref, b_ref, o_ref):
    s = _silu(c_ref[...]).astype(BF16)
    w = w_ref[0].astype(BF16)
    o_ref[0] = jnp.dot(s, w, preferred_element_type=F32) + b_ref[0]


def _adaln(cond8, w_ada, b_ada):
    tn = D_MODEL
    ncol = N_MOD * D_MODEL
    return pl.pallas_call(
        _ada_kernel,
        grid=(DEPTH, ncol // tn),
        in_specs=[
            pl.BlockSpec((8, D_MODEL), lambda l, j: (0, 0)),
            pl.BlockSpec((1, D_MODEL, tn), lambda l, j: (l, 0, j)),
            pl.BlockSpec((1, 1, tn), lambda l, j: (l, 0, j)),
        ],
        out_specs=pl.BlockSpec((1, 8, tn), lambda l, j: (l, 0, j)),
        out_shape=jax.ShapeDtypeStruct((DEPTH, 8, ncol), F32),
        compiler_params=_cparams(("arbitrary", "arbitrary")),
        name="adaln",
    )(cond8, w_ada, b_ada.reshape(DEPTH, 1, ncol))


def _normed(x, g_row, shift, scale):
    ms = jnp.mean(x * x, axis=-1, keepdims=True)
    h = x * lax.rsqrt(ms + RMS_EPS) * g_row
    return h * (1.0 + scale) + shift


def _mod_spec(l, seg0, seg_tokens, tile):
    tiles_per_seg = seg_tokens // tile
    return pl.BlockSpec((None, 1, N_MOD, D_MODEL), lambda i: (l, seg0 + i // tiles_per_seg, 0, 0))


def _ffn_kernel(x_ref, mod_ref, g_ref, win_ref, wout_ref, o_ref, h_ref, act_ref, *, mod_base):
    shift = mod_ref[0, mod_base:mod_base + 1, :]
    scale = mod_ref[0, mod_base + 1:mod_base + 2, :]
    gate = mod_ref[0, mod_base + 2:mod_base + 3, :]
    h_ref[...] = _normed(x_ref[...], g_ref[...], shift, scale).astype(BF16)
    for j in range(N_FF_CHUNKS):
        cols = slice(j * FF_CHUNK, (j + 1) * FF_CHUNK)
        gate_j = jnp.dot(h_ref[...], win_ref[:, cols], preferred_element_type=F32)
        up_j = jnp.dot(h_ref[...], win_ref[:, D_FF + j * FF_CHUNK:D_FF + (j + 1) * FF_CHUNK], preferred_element_type=F32)
        act_ref[:, cols] = (_silu(gate_j) * up_j).astype(BF16)
    for n in range(D_MODEL // MXU_DIM):
        cols = slice(n * MXU_DIM, (n + 1) * MXU_DIM)
        y = jnp.dot(act_ref[...], wout_ref[:, cols], preferred_element_type=F32)
        o_ref[:, cols] = x_ref[:, cols] + 0.5 * gate[:, cols] * y


def _ffn(x, mod, g, w_in, w_out, *, l, mod_base, seg0, seg_tokens):
    n_tok = x.shape[0]
    return pl.pallas_call(
        functools.partial(_ffn_kernel, mod_base=mod_base),
        grid=(n_tok // FFN_TILE,),
        in_specs=[
            pl.BlockSpec((FFN_TILE, D_MODEL), lambda i: (i, 0)),
            _mod_spec(l, seg0, seg_tokens, FFN_TILE),
            _layer_spec(g, l), _layer_spec(w_in, l), _layer_spec(w_out, l),
        ],
        out_specs=pl.BlockSpec((FFN_TILE, D_MODEL), lambda i: (i, 0)),
        out_shape=jax.ShapeDtypeStruct((n_tok, D_MODEL), F32),
        scratch_shapes=[pltpu.VMEM((FFN_TILE, D_MODEL), BF16), pltpu.VMEM((FFN_TILE, D_FF), BF16)],
        compiler_params=_cparams(("arbitrary",)),
        name="ffn",
    )(x, mod, g, w_in, w_out)


def _segment_ids():
    return lax.broadcasted_iota(jnp.int32, (1, LANES), 1) // SSM_CH


def _merge_segments(load_piece, rot):
    seg = _segment_ids()
    tiles = []
    for lt in range(SSM_ROW // LANES):
        acc = None
        for jj in range(SEGS_PER_TILE):
            src = (lt * SEGS_PER_TILE + jj - rot) % SSM_CHUNK
            piece = load_piece(src, lt)
            acc = piece if acc is None else jnp.where(seg == jj, piece, acc)
        tiles.append(acc)
    return jnp.concatenate(tiles, axis=-1)


def _roll_lanes(x, shift):
    shift %= x.shape[-1]
    return pltpu.roll(x, shift, axis=x.ndim - 1) if shift else x


def _inproj_kernel(*refs, n_in, n_out):
    x_ref, mod_ref, g_ref, w_ref, hsum_ref, qg_ref, kg_ref, ws_ref, gmb_ref = refs[:9]
    u_ref, q_ref, k_ref, v_ref, gm_ref, *tok_refs = refs[n_in:n_in + n_out]
    h_ref, xs_ref, z_ref = refs[n_in + n_out:]
    shift = mod_ref[0, 3:4, :]
    scale = mod_ref[0, 4:5, :]
    h_ref[...] = _normed(x_ref[...], g_ref[...], shift, scale).astype(BF16)
    n_rows = MIX_TILE // SSM_CHUNK

    def proj(c0, width):
        return jnp.dot(h_ref[...], w_ref[:, c0:c0 + width], preferred_element_type=F32)

    xs_qk = proj(0, SSM_WIDTH + 2 * NA_WIDTH)
    for j in range(SSM_WIDTH // LANES):
        xs_ref[j] = xs_qk[:, j * LANES:(j + 1) * LANES]
    for t in range(SSM_CHUNK):
        zt = jnp.concatenate([xs_ref[j, pl.ds(t, n_rows, stride=SSM_CHUNK), :] for j in range(SSM_WIDTH // LANES)],
                             axis=-1)
        z_ref[t] = _roll_lanes(zt, SSM_CH * t)
    for g in range(SSM_GROUPS):
        ug = _merge_segments(lambda src, lt: z_ref[src, :, lt * LANES:(lt + 1) * LANES], g)
        u_ref[g] = _roll_lanes(ug, -SSM_CH * g)

    c0 = SSM_WIDTH
    qk = xs_qk[:, c0:]
    sq = (qk * qk).astype(BF16)
    groups = range(0, 2 * NA_WIDTH, MXU_DIM)
    ms = jnp.dot(jnp.concatenate([sq[:, c:c + MXU_DIM] for c in groups], axis=0), hsum_ref[...],
                 preferred_element_type=F32)
    ms = jnp.concatenate([ms[i * MIX_TILE:(i + 1) * MIX_TILE, :] for i in range(len(groups))], axis=-1)
    qk = qk * lax.rsqrt(ms + RMS_EPS)
    q = qk[:, :NA_WIDTH] * qg_ref[...] * (NA_HEAD_DIM ** -0.5)
    k = qk[:, NA_WIDTH:] * kg_ref[...]
    v_uv = proj(c0 + 2 * NA_WIDTH, NA_WIDTH + 2 * GM_WIDTH)
    v = v_uv[:, :NA_WIDTH]
    for p in range(HEAD_PAIRS):
        lanes = slice(p * LANES, (p + 1) * LANES)
        q_ref[p] = q[:, lanes].astype(q_ref.dtype)
        k_ref[p] = k[:, lanes].astype(k_ref.dtype)
        v_ref[p] = v[:, lanes].astype(v_ref.dtype)
    if tok_refs:
        for slot in range(tok_refs[0].shape[1]):
            tok_refs[0][:, slot] = k.reshape(MIX_TILE // SEQ, SEQ, NA_WIDTH)
            tok_refs[1][:, slot] = v.reshape(MIX_TILE // SEQ, SEQ, NA_WIDTH)

    a = _gelu(v_uv[:, NA_WIDTH:])
    u = a[:, :GM_WIDTH]
    vv = a[:, GM_WIDTH:]
    mu = jnp.mean(vv, axis=-1, keepdims=True)
    vc = vv - mu
    var = jnp.mean(vc * vc, axis=-1, keepdims=True)
    vln = (vc * lax.rsqrt(var + LN_EPS)).astype(BF16)
    lane = lax.broadcasted_iota(jnp.int32, (1, GM_WIDTH), 1)
    gw = GM_WIDTH // GM_GROUPS
    for ci in range(MIX_TILE // GM_CHUNK):
        rows = slice(ci * GM_CHUNK, (ci + 1) * GM_CHUNK)
        vch = vln[rows, :]
        sp = gmb_ref[...]
        for gi in range(GM_GROUPS):
            t = jnp.dot(ws_ref[gi], vch, preferred_element_type=F32)
            sp = sp + jnp.where((lane >= gi * gw) & (lane < (gi + 1) * gw), t, 0.0)
        gm_ref[rows, :] = (u[rows, :] * sp).astype(BF16)


def _inproj(x, mod, g, w, hsum, qg, kg, ws, gmb, *, l, seg0, seg_tokens, is_ctx, caches=None):
    n_tok = x.shape[0]
    kv_dtype = BF16 if is_ctx else F32
    n_rows = MIX_TILE // SSM_CHUNK
    pair_spec = pl.BlockSpec((HEAD_PAIRS, MIX_TILE, LANES), lambda i: (0, i, 0))
    out_specs = [
        pl.BlockSpec((SSM_GROUPS, n_rows, SSM_ROW), lambda i: (0, i, 0)),
        pair_spec, pair_spec, pair_spec,
        pl.BlockSpec((MIX_TILE, GM_WIDTH), lambda i: (i, 0)),
    ]
    out_shape = [
        jax.ShapeDtypeStruct((SSM_GROUPS, n_tok // SSM_CHUNK, SSM_ROW), F32),
        jax.ShapeDtypeStruct((HEAD_PAIRS, n_tok, LANES), BF16),
        jax.ShapeDtypeStruct((HEAD_PAIRS, n_tok, LANES), kv_dtype),
        jax.ShapeDtypeStruct((HEAD_PAIRS, n_tok, LANES), kv_dtype),
        jax.ShapeDtypeStruct((n_tok, GM_WIDTH), BF16),
    ]
    args = [x, mod, g, w, hsum, qg, kg, ws, gmb]
    in_specs = [
        pl.BlockSpec((MIX_TILE, D_MODEL), lambda i: (i, 0)),
        _mod_spec(l, seg0, seg_tokens, MIX_TILE),
        _layer_spec(g, l), _layer_spec(w, l),
        _const_spec((MXU_DIM, MXU_DIM)),
        _layer_spec(qg, l), _layer_spec(kg, l), _layer_spec(ws, l), _layer_spec(gmb, l),
    ]
    aliases = {}
    if is_ctx:
        bt = MIX_TILE // SEQ
        cache_shape = jax.ShapeDtypeStruct((n_tok // SEQ, DEPTH, SEQ, NA_WIDTH), F32)
        if caches is None:
            out_specs += [pl.BlockSpec((bt, DEPTH, SEQ, NA_WIDTH), lambda i: (i, 0, 0, 0))] * 2
        else:
            out_specs += [pl.BlockSpec((bt, 1, SEQ, NA_WIDTH), lambda i: (i, l, 0, 0))] * 2
        out_shape += [cache_shape] * 2
        if caches is not None:
            aliases = {len(args): len(out_shape) - 2, len(args) + 1: len(out_shape) - 1}
            args += list(caches)
            in_specs += [pl.BlockSpec(memory_space=pl.ANY)] * 2
    return pl.pallas_call(
        functools.partial(_inproj_kernel, n_in=len(args), n_out=len(out_shape)),
        grid=(n_tok // MIX_TILE,),
        in_specs=in_specs,
        out_specs=out_specs,
        out_shape=out_shape,
        input_output_aliases=aliases,
        scratch_shapes=[pltpu.VMEM((MIX_TILE, D_MODEL), BF16), pltpu.VMEM((SSM_WIDTH // LANES, MIX_TILE, LANES), F32),
                        pltpu.VMEM((SSM_CHUNK, n_rows, SSM_ROW), F32)],
        compiler_params=_cparams(("arbitrary",)),
        name="inproj_ctx" if is_ctx else "inproj_lat",
    )(*args)


def _roll_rows(x, d, down):
    return pltpu.roll(x, d if down else SUBLANES - d, axis=0)


def _scan_tile(w, wp, c, cp, cst, row, down):
    x, xp = w, wp
    for si, d in enumerate(SCAN_SHIFTS):
        a, b, bp = cst[3 * si], cst[3 * si + 1], cst[3 * si + 2]
        xs, xps = _roll_rows(x, d, down), _roll_rows(xp, d, down)
        x, xp = x + a * xs + b * xps, xp + a * xps + bp * xs
    a, b, bp = cst[N_SCAN_CONST - 3], cst[N_SCAN_CONST - 2], cst[N_SCAN_CONST - 1]
    s = x + a * c + b * cp
    sp = xp + a * cp + bp * c
    edge = 0 if down else SUBLANES - 1
    last = SUBLANES - 1 if down else 0
    seen = jnp.where(row == edge, c, _roll_rows(s, 1, down))
    c_new = jnp.broadcast_to(s[last:last + 1, :], s.shape)
    cp_new = jnp.broadcast_to(sp[last:last + 1, :], sp.shape)
    return seen, c_new, cp_new


def _ssm_kernel(u_ref, mp_ref, q_ref, cst_ref, dvec_ref, s0_ref, y_ref, fin_ref, w_ref, sq_ref, *, n_batch, n_chunks):
    gb = u_ref.shape[0]
    n_rows = n_batch * n_chunks
    row_blk = min(n_rows, SSM_ROW_BLOCK)
    for g in range(gb):
        for r0 in range(0, n_rows, row_blk):
            rows = slice(r0, r0 + row_blk)
            ug = u_ref[g, rows, :]
            wy = jnp.dot(ug.astype(BF16), mp_ref[g], preferred_element_type=F32)
            y_ref[g, rows, :] = wy[:, :SSM_ROW] + ug * dvec_ref[g]
            w_ref[g, rows, :] = wy[:, SSM_ROW:]

    n_tiles = n_chunks // SUBLANES
    row = lax.broadcasted_iota(jnp.int32, (SUBLANES, LANES), 0)

    def tile_step(b, kt, carry):
        out = []
        for g in range(gb):
            cf, cfp, cb, cbp = carry[4 * g:4 * g + 4]
            rf = pl.multiple_of(b * n_chunks + kt * SUBLANES, SUBLANES)
            rb = pl.multiple_of(b * n_chunks + (n_tiles - 1 - kt) * SUBLANES, SUBLANES)
            wf = w_ref[g, pl.ds(rf, SUBLANES), :]
            wb = w_ref[g, pl.ds(rb, SUBLANES), :]
            cst_f = [cst_ref[g, 0, i] for i in range(N_SCAN_CONST)]
            cst_b = [cst_ref[g, 1, i] for i in range(N_SCAN_CONST)]
            seen_f, cf, cfp = _scan_tile(wf[:, 0:LANES], wf[:, LANES:2 * LANES], cf, cfp, cst_f, row, True)
            seen_b, cb, cbp = _scan_tile(wb[:, 2 * LANES:3 * LANES], wb[:, 3 * LANES:], cb, cbp, cst_b, row, False)
            sq_ref[g, pl.ds(rf, SUBLANES), 0:LANES] = seen_f
            sq_ref[g, pl.ds(rb, SUBLANES), LANES:2 * LANES] = seen_b
            out += [cf, cfp, cb, cbp]
        return tuple(out)

    def batch_body(b, _):
        carry = []
        for g in range(gb):
            for i in range(4):
                carry.append(jnp.broadcast_to(s0_ref[g, i, pl.ds(b, 1), :], (SUBLANES, LANES)))
        carry = tuple(carry)
        if n_tiles <= 2:
            for kt in range(n_tiles):
                carry = tile_step(b, kt, carry)
        else:
            carry = lax.fori_loop(0, n_tiles, lambda kt, c: tile_step(b, kt, c), carry)
        for g in range(gb):
            fin_ref[g, 0, pl.ds(b, 1), :] = carry[4 * g][0:1, :]
            fin_ref[g, 1, pl.ds(b, 1), :] = carry[4 * g + 2][0:1, :]
        return 0

    lax.fori_loop(0, n_batch, batch_body, 0)

    for g in range(gb):
        for r0 in range(0, n_rows, row_blk):
            rows = slice(r0, r0 + row_blk)
            y_ref[g, rows, :] += jnp.dot(sq_ref[g, rows, :].astype(BF16), q_ref[g], preferred_element_type=F32)


def _ssm_core(u, mp, q, cst, dvec, s0, *, l, n_batch, n_chunks):
    n_rows = n_batch * n_chunks
    gb = SSM_GROUP_BLOCK
    g3 = lambda i: (i, 0, 0)
    g4 = lambda i: (i, 0, 0, 0)
    return pl.pallas_call(
        functools.partial(_ssm_kernel, n_batch=n_batch, n_chunks=n_chunks),
        grid=(SSM_GROUPS // gb,),
        in_specs=[
            pl.BlockSpec((gb, n_rows, SSM_ROW), g3),
            pl.BlockSpec((None, gb, SSM_ROW, SSM_ROW + SSM_W_COLS), lambda i: (l, i, 0, 0)),
            pl.BlockSpec((None, gb, 2 * LANES, SSM_ROW), lambda i: (l, i, 0, 0)),
            pl.BlockSpec((None, gb, 2, N_SCAN_CONST, SUBLANES, LANES), lambda i: (l, i, 0, 0, 0, 0)),
            pl.BlockSpec((None, gb, 1, SSM_ROW), lambda i: (l, i, 0, 0)),
            pl.BlockSpec((gb, 4, n_batch, LANES), g4),
        ],
        out_specs=[
            pl.BlockSpec((gb, n_rows, SSM_ROW), g3),
            pl.BlockSpec((gb, 2, n_batch, LANES), g4),
        ],
        out_shape=[
            jax.ShapeDtypeStruct((SSM_GROUPS, n_rows, SSM_ROW), F32),
            jax.ShapeDtypeStruct((SSM_GROUPS, 2, n_batch, LANES), F32),
        ],
        scratch_shapes=[pltpu.VMEM((gb, n_rows, SSM_W_COLS), F32), pltpu.VMEM((gb, n_rows, 2 * LANES), F32)],
        compiler_params=_cparams(("arbitrary",)),
        name="ssm_core",
    )(u, mp, q, cst, dvec, s0)


def _cmul(a, b):
    return a[0] * b[0] - a[1] * b[1], a[0] * b[1] + a[1] * b[0]


_CHUNK_IDX = np.arange(SSM_CHUNK)


def _ssm_exponents():
    t = SSM_CHUNK
    idx = _CHUNK_IDX.astype(np.float32)
    both = lambda a: np.stack([a, a], axis=1)
    lane_step = idx - (t - 1) / 2
    tile_rows = np.arange(1, SUBLANES + 1, dtype=np.float32) * t
    tables = dict(
        one=both(np.ones(1, np.float32)),
        incr=np.stack([t - 1 - idx, idx], axis=1),
        seen=np.stack([idx + 1, t - idx], axis=1),
        resp_in=np.stack([-lane_step, lane_step], axis=1),
        resp_out=np.stack([lane_step, -lane_step], axis=1),
        shift=both(np.asarray(SCAN_SHIFTS, np.float32) * t),
        tile=np.stack([tile_rows, tile_rows[::-1]], axis=1),
    )
    spans, start = {}, 0
    for name, tab in tables.items():
        spans[name] = slice(start, start + len(tab))
        start += len(tab)
    return np.concatenate(list(tables.values()), axis=0), spans


def _ssm_params(p):
    t = SSM_CHUNK
    n_l = p['ssm_lambda_re'].shape[0]
    lam = (p['ssm_lambda_re'].astype(F32), p['ssm_lambda_im'].astype(F32))
    dt = jnp.exp(p['ssm_log_dt'].astype(F32))[..., None]
    arg = (lam[0] * dt, lam[1] * dt)
    expo, span = _ssm_exponents()
    n = jnp.asarray(expo)[:, None, :, None, None]
    mag = jnp.exp(n * arg[0][None])
    pw = (mag * jnp.cos(n * arg[1][None]), mag * jnp.sin(n * arg[1][None]))
    take = lambda name: [x[span[name]] for x in pw]
    lanes_last = lambda xs: [jnp.moveaxis(x, 0, -1) for x in xs]

    lbar = [x[0] for x in take('one')]
    num = (lbar[0] - 1.0, lbar[1])
    den = lam[0] * lam[0] + lam[1] * lam[1]
    zoh = ((num[0] * lam[0] + num[1] * lam[1]) / den, (num[1] * lam[0] - num[0] * lam[1]) / den)
    b_mat = (p['ssm_b_re'].astype(F32), p['ssm_b_im'].astype(F32))
    bbar = _cmul((zoh[0][..., None], zoh[1][..., None]), b_mat)
    c_mat = [jnp.swapaxes(p[k].astype(F32), -1, -2) for k in ('ssm_c_re', 'ssm_c_im')]
    ct = [jnp.tile(x, (1, 1, 1, 1, t)) for x in c_mat]
    bt = [jnp.tile(x, (1, 1, 1, 1, t)) for x in bbar]

    per_lane = lambda name: [jnp.repeat(x, SSM_CH, axis=-1) for x in lanes_last(take(name))]
    xm = _cmul(per_lane('resp_in'), bt)
    ym = _cmul(per_lane('resp_out'), ct)
    resp = jnp.einsum('ldgpx,ldgpy->ldgxy', jnp.concatenate([xm[0], -xm[1]], axis=3),
                      jnp.concatenate([ym[0], ym[1]], axis=3), precision='highest')
    steps_x = np.repeat(_CHUNK_IDX, SSM_CH)
    causal = np.stack([steps_x[None, :] >= steps_x[:, None], steps_x[:, None] >= steps_x[None, :]])
    mats = jnp.sum(jnp.where(jnp.asarray(causal)[None, :, None], resp, 0.0), axis=1)

    incr = _cmul([x[..., None] for x in take('incr')], [x[None] for x in bbar])
    incr = [jnp.transpose(x, (1, 2, 3, 0, 5, 4)) for x in incr]
    pcat = jnp.concatenate([incr[0], incr[1], incr[1], incr[0]], axis=-1)
    pcat = jnp.concatenate([pcat[:, 0], pcat[:, 1]], axis=-1).reshape(n_l, SSM_GROUPS, SSM_ROW, SSM_W_COLS)
    mp = jnp.concatenate([mats, pcat], axis=-1).astype(BF16)

    qm = _cmul(ct, per_lane('seen'))
    q = jnp.concatenate([qm[0][:, 0], -qm[1][:, 0], qm[0][:, 1], -qm[1][:, 1]], axis=2).astype(BF16)

    def forms(name, rows):
        pr, pi = [jnp.moveaxis(x, 0, 3) for x in take(name)]
        f = jnp.stack([jnp.concatenate([pr, pr], -1), jnp.concatenate([-pi, pi], -1),
                       jnp.concatenate([pi, -pi], -1)], axis=4)
        if rows:
            return f
        r = np.arange(SUBLANES)[None, :]
        d = np.repeat(np.asarray(SCAN_SHIFTS), 3)[:, None]
        keep = np.stack([r >= d, r < SUBLANES - d]).astype(np.float32)
        return f.reshape(n_l, 2, SSM_GROUPS, 3 * len(SCAN_SHIFTS), 1, LANES) * keep[None, :, None, :, :, None]
    per_row = jnp.swapaxes(forms('tile', True), 3, 4)
    cst = jnp.swapaxes(jnp.concatenate([forms('shift', False), per_row], axis=3), 1, 2)
    dvec = jnp.tile(p['ssm_d'].astype(F32).reshape(n_l, SSM_GROUPS, 1, SSM_CH), (1, 1, 1, SSM_CHUNK))
    return mp, q, cst, dvec


def _ssm_states_in(state_ssm):
    st = jnp.transpose(state_ssm.astype(F32), (1, 3, 2, 0, 4, 5))
    re, im = st[..., 0], st[..., 1]
    ri, ir = jnp.concatenate([re, im], -1), jnp.concatenate([im, re], -1)
    return jnp.stack([ri[:, :, 0], ir[:, :, 0], ri[:, :, 1], ir[:, :, 1]], axis=2)


def _ssm_states_out(fins):
    fin = jnp.stack(fins, axis=0)
    fin = jnp.stack([fin[..., :SSM_STATE], fin[..., SSM_STATE:]], axis=-1)
    return jnp.transpose(fin, (3, 0, 2, 1, 4, 5))


def _head_lane_mask(hh):
    lane = lax.broadcasted_iota(jnp.int32, (1, LANES), 1)
    return (lane >= hh * NA_HEAD_DIM) & (lane < (hh + 1) * NA_HEAD_DIM)


_NT_DIMS = (((1,), (1,)), ((), ()))


def _ctx_attn_kernel(q_ref, k_ref, v_ref, o_ref):
    lm0 = _head_lane_mask(0)
    for b in range(CTX_ATTN_BATCH):
        rows = slice(b * SEQ, (b + 1) * SEQ)
        for p in range(HEAD_PAIRS):
            qp, kp, vp = q_ref[p, rows, :], k_ref[p, rows, :], v_ref[p, rows, :]
            zero = jnp.zeros_like(qp)
            q2 = jnp.concatenate([jnp.where(lm0, qp, zero), jnp.where(lm0, zero, qp)], axis=0)
            s = lax.dot_general(q2, kp, _NT_DIMS, preferred_element_type=F32)
            e = jnp.exp(s - jnp.max(s, axis=-1, keepdims=True))
            o = jnp.dot(e.astype(BF16), vp, preferred_element_type=F32) / jnp.sum(e, axis=-1, keepdims=True)
            o_ref[rows, p * LANES:(p + 1) * LANES] = jnp.where(lm0, o[:SEQ, :], o[SEQ:, :]).astype(BF16)


def _ctx_attn(q, k, v, n_batch):
    rows = CTX_ATTN_BATCH * SEQ
    spec = pl.BlockSpec((HEAD_PAIRS, rows, LANES), lambda b: (0, b, 0))
    return pl.pallas_call(
        _ctx_attn_kernel,
        grid=(n_batch // CTX_ATTN_BATCH,),
        in_specs=[spec, spec, spec],
        out_specs=pl.BlockSpec((rows, NA_WIDTH), lambda b: (b, 0)),
        out_shape=jax.ShapeDtypeStruct((n_batch * SEQ, NA_WIDTH), BF16),
        compiler_params=_cparams(("arbitrary",)),
        name="ctx_attn",
    )(q, k, v)


def _na_kernel(q_ref, k_ref, v_ref, kc_ref, vc_ref, t2_ref, rm_ref, o_ref, qb_ref, os_ref, tb_ref):
    rt = pl.program_id(1)
    n_rt = pl.num_programs(1)

    @pl.when((pl.program_id(0) == 0) & (rt == 0))
    def _():
        for h in range(NA_HEADS):
            for n in range(NA_COL_BLOCKS):
                for rl in range(NA_Q_ROWS):
                    lane0 = (NA_OFF_START - rl) * NA_KCOLS
                    tb_ref[h, n, rl * NA_COLS:(rl + 1) * NA_COLS, :] = (
                        t2_ref[h, n, :, lane0:lane0 + NA_K_ROWS * NA_KCOLS])

    r0 = rt * NA_Q_ROWS
    ttype = jnp.where(rt == 0, 0, jnp.where(rt == n_rt - 1, 2, 1))
    rmask = rm_ref[ttype]
    blk_q = NA_Q_ROWS * NA_COLS

    def pair_body(p, carry):
        lm0 = _head_lane_mask(0)
        for n in range(NA_COL_BLOCKS):
            for rl in range(NA_Q_ROWS):
                qrow = q_ref[p, rl * GRID_W + n * NA_COLS:rl * GRID_W + (n + 1) * NA_COLS, :]
                zero = jnp.zeros_like(qrow)
                base = 2 * n * blk_q + rl * NA_COLS
                qb_ref[base:base + NA_COLS, :] = jnp.where(lm0, qrow, zero)
                qb_ref[base + blk_q:base + blk_q + NA_COLS, :] = jnp.where(lm0, zero, qrow)
        kcp, vcp = kc_ref[0, p], vc_ref[0, p]
        sc_all = lax.dot_general(qb_ref[...], kcp, _NT_DIMS, preferred_element_type=F32)
        for n in range(NA_COL_BLOCKS):
            kparts, vparts = [], []
            for i in range(NA_K_ROWS):
                kr = jnp.clip(r0 - (NA_K_ROWS - NA_Q_ROWS) // 2 + i, 0, GRID_W - 1)
                start = pl.multiple_of(kr * GRID_W + NA_KC0[n], 8)
                kparts.append(k_ref[p, pl.ds(start, NA_KCOLS), :])
                vparts.append(v_ref[p, pl.ds(start, NA_KCOLS), :])
            kblk = jnp.concatenate(kparts, axis=0).astype(BF16)
            vblk = jnp.concatenate(vparts, axis=0).astype(BF16)
            rows2 = slice(2 * n * blk_q, 2 * (n + 1) * blk_q)
            s = lax.dot_general(qb_ref[rows2, :], kblk, _NT_DIMS, preferred_element_type=F32)
            s = s + jnp.concatenate([tb_ref[2 * p, n] + rmask, tb_ref[2 * p + 1, n] + rmask], axis=0)
            sc = sc_all[rows2, :]
            m = jnp.maximum(jnp.max(s, axis=-1, keepdims=True), jnp.max(sc, axis=-1, keepdims=True))
            e = jnp.exp(s - m)
            ec = jnp.exp(sc - m)
            den = jnp.sum(e, axis=-1, keepdims=True) + jnp.sum(ec, axis=-1, keepdims=True)
            o = (jnp.dot(e.astype(BF16), vblk, preferred_element_type=F32)
                 + jnp.dot(ec.astype(BF16), vcp, preferred_element_type=F32)) / den
            oblk = jnp.where(lm0, o[:blk_q, :], o[blk_q:, :])
            for rl in range(NA_Q_ROWS):
                os_ref[p, rl * GRID_W + n * NA_COLS:rl * GRID_W + (n + 1) * NA_COLS, :] = (
                    oblk[rl * NA_COLS:(rl + 1) * NA_COLS, :])
        return carry

    lax.fori_loop(0, HEAD_PAIRS, pair_body, 0)
    for p in range(HEAD_PAIRS):
        o_ref[:, p * LANES:(p + 1) * LANES] = os_ref[p].astype(BF16)


def _na_attn(q, k, v, kc, vc, tb, rm, n_batch, *, l):
    tile_tok = NA_Q_ROWS * GRID_W
    n_rt = DEC_SEQ // tile_tok
    blk_q, blk_k = NA_Q_ROWS * NA_COLS, NA_K_ROWS * NA_KCOLS
    return pl.pallas_call(
        _na_kernel,
        grid=(n_batch, n_rt),
        in_specs=[
            pl.BlockSpec((HEAD_PAIRS, tile_tok, LANES), lambda b, r: (0, b * n_rt + r, 0)),
            pl.BlockSpec((HEAD_PAIRS, DEC_SEQ, LANES), lambda b, r: (0, b, 0)),
            pl.BlockSpec((HEAD_PAIRS, DEC_SEQ, LANES), lambda b, r: (0, b, 0)),
            pl.BlockSpec((1, None, HEAD_PAIRS, PAST_LEN, LANES), lambda b, r: (b, l, 0, 0, 0)),
            pl.BlockSpec((1, None, HEAD_PAIRS, PAST_LEN, LANES), lambda b, r: (b, l, 0, 0, 0)),
            _layer_spec(tb, l),
            _const_spec(rm.shape),
        ],
        out_specs=pl.BlockSpec((tile_tok, NA_WIDTH), lambda b, r: (b * n_rt + r, 0)),
        out_shape=jax.ShapeDtypeStruct((n_batch * DEC_SEQ, NA_WIDTH), BF16),
        scratch_shapes=[
            pltpu.VMEM((2 * tile_tok, LANES), BF16),
            pltpu.VMEM((HEAD_PAIRS, tile_tok, LANES), F32),
            pltpu.VMEM((NA_HEADS, NA_COL_BLOCKS, blk_q, blk_k), F32)],
        compiler_params=_cparams(("arbitrary", "arbitrary")),
        name="na_attn",
    )(q, k, v, kc, vc, tb, rm)


def _na_tables(rpb):
    n = np.arange(NA_COL_BLOCKS)
    cc = np.arange(NA_COLS)
    kk = np.arange(NA_KCOLS)
    rl = np.arange(NA_Q_ROWS)
    ki = np.arange(NA_K_ROWS)
    n_dr, n_dc = 2 * NA_MAX_ROWS - 1, 2 * NA_COLS - 1
    c = n[:, None] * NA_COLS + cc[None, :]
    kcol = np.asarray(NA_KC0)[:, None] + kk[None, :]
    dc = np.clip(kcol[:, None, :] - c[:, :, None], -(NA_COLS - 1), NA_COLS - 1) + (NA_COLS - 1)
    cs = np.clip(c - NA_COLS // 2, 0, GRID_W - NA_COLS)
    col_ok = (kcol[:, None, :] >= cs[:, :, None]) & (kcol[:, None, :] < cs[:, :, None] + NA_COLS)
    half = (NA_K_ROWS - NA_Q_ROWS) // 2
    dr = np.clip(ki[None, :] - half - rl[:, None] + (NA_MAX_ROWS - 1), 0, n_dr - 1)
    oh_dc = jnp.asarray(dc[None] == np.arange(n_dc)[:, None, None, None], F32)
    off = np.clip(np.arange(NA_OFF_ROWS) - NA_OFF_PAD, 0, n_dr - 1)
    assert np.array_equal(dr, off[ki[None] + (NA_OFF_START - rl)[:, None]])
    oh_off = jnp.asarray(off[None, :] == np.arange(n_dr)[:, None], F32)
    rows_p = jnp.einsum('lhab,aA->lhAb', rpb.astype(F32), oh_off, precision='highest')
    cols = jnp.einsum('lhAb,bnck->lhncAk', rows_p, oh_dc, precision='highest')
    cols = jnp.where(jnp.asarray(col_ok)[None, None, :, :, None, :], cols, NEG_INF)
    tb = cols.reshape(-1, NA_HEADS, NA_COL_BLOCKS, NA_COLS, NA_OFF_ROWS * NA_KCOLS)
    rows = GRID_W
    masks = []
    for r0 in (0, NA_Q_ROWS, rows - NA_Q_ROWS):
        r = r0 + rl
        rs = np.clip(r - NA_MAX_ROWS // 2, 0, rows - NA_MAX_ROWS)
        kr = r0 - half + ki
        ok = (kr[None, :] >= rs[:, None]) & (kr[None, :] < rs[:, None] + NA_MAX_ROWS)
        full = np.broadcast_to(ok[:, None, :, None], (NA_Q_ROWS, NA_COLS, NA_K_ROWS, NA_KCOLS))
        masks.append(np.where(full, 0.0, NEG_INF).reshape(NA_Q_ROWS * NA_COLS, NA_K_ROWS * NA_KCOLS))
    rm = jnp.asarray(np.stack(masks), F32)
    return tb, rm


def _outffn_kernel(x_ref, mod_ref, y_ref, na_ref, gm_ref, gw_ref, gb_ref, wo_ref, g_ref, win_ref, wout_ref, o_ref,
                   yr_ref, ys_ref, cat_ref, xm_ref, h_ref, act_ref):
    gate = mod_ref[0, 5:6, :]
    n_rows = OUT_TILE // SSM_CHUNK
    for g in range(SSM_GROUPS):
        yr_ref[g] = _roll_lanes(y_ref[g], SSM_CH * g)
    for t in range(SSM_CHUNK):
        yt = _merge_segments(lambda src, lt: yr_ref[src, :, lt * LANES:(lt + 1) * LANES], t)
        yt = _roll_lanes(yt, -SSM_CH * t)
        for j in range(SSM_WIDTH // LANES):
            ys_ref[j, pl.ds(t, n_rows, stride=SSM_CHUNK), :] = yt[:, j * LANES:(j + 1) * LANES]
    y = _gelu(jnp.concatenate([ys_ref[j] for j in range(SSM_WIDTH // LANES)], axis=-1))
    z = jnp.dot(y.astype(BF16), gw_ref[...], preferred_element_type=F32) + gb_ref[...]
    cat_ref[:, 0:SSM_WIDTH] = (y * _sigmoid(z)).astype(BF16)
    cat_ref[:, SSM_WIDTH:SSM_WIDTH + NA_WIDTH] = na_ref[...]
    cat_ref[:, SSM_WIDTH + NA_WIDTH:] = gm_ref[...]
    mix = jnp.dot(cat_ref[...], wo_ref[...], preferred_element_type=F32)
    xm_ref[...] = x_ref[...] + gate * mix
    _ffn_kernel(xm_ref, mod_ref, g_ref, win_ref, wout_ref, o_ref, h_ref, act_ref, mod_base=6)


def _outffn(x, mod, y, na, gm, gw, gb, wo, g, w_in, w_out, *, l, seg0, seg_tokens):
    n_tok = x.shape[0]
    n_rows = OUT_TILE // SSM_CHUNK
    tok = lambda width: pl.BlockSpec((OUT_TILE, width), lambda i: (i, 0))
    return pl.pallas_call(
        _outffn_kernel,
        grid=(n_tok // OUT_TILE,),
        in_specs=[
            tok(D_MODEL),
            _mod_spec(l, seg0, seg_tokens, OUT_TILE),
            pl.BlockSpec((SSM_GROUPS, n_rows, SSM_ROW), lambda i: (0, i, 0)),
            tok(NA_WIDTH), tok(GM_WIDTH),
            _layer_spec(gw, l), _layer_spec(gb, l), _layer_spec(wo, l),
            _layer_spec(g, l), _layer_spec(w_in, l), _layer_spec(w_out, l),
        ],
        out_specs=tok(D_MODEL),
        out_shape=jax.ShapeDtypeStruct((n_tok, D_MODEL), F32),
        scratch_shapes=[pltpu.VMEM((SSM_GROUPS, n_rows, SSM_ROW), F32),
                        pltpu.VMEM((SSM_WIDTH // LANES, OUT_TILE, LANES), F32),
                        pltpu.VMEM((OUT_TILE, D_MODEL), BF16),
                        pltpu.VMEM((OUT_TILE, D_MODEL), F32),
                        pltpu.VMEM((OUT_TILE, D_MODEL), BF16), pltpu.VMEM((OUT_TILE, D_FF), BF16)],
        compiler_params=_cparams(("arbitrary",)),
        name="outffn",
    )(x, mod, y, na, gm, gw, gb, wo, g, w_in, w_out)


def _prep(p):
    heads = np.arange(MXU_DIM) // NA_HEAD_DIM
    hsum = jnp.asarray((heads[:, None] == heads[None, :]) / NA_HEAD_DIM, BF16)
    gmb = jnp.repeat(jnp.swapaxes(p['gm_bs'], 1, 2).astype(F32), GM_WIDTH // GM_GROUPS, axis=2)
    tb, rm = _na_tables(p['na_rpb'])
    row = lambda a: a.astype(F32).reshape(DEPTH, 1, -1)
    return dict(
        g1=row(p['norm_ffn1']), g2=row(p['norm_mix']), g3=row(p['norm_ffn2']),
        f1_in=p['ffn1_w_in'].astype(BF16), f1_out=p['ffn1_w_out'].astype(BF16),
        f2_in=p['ffn2_w_in'].astype(BF16), f2_out=p['ffn2_w_out'].astype(BF16),
        w_in=p['w_in'].astype(BF16), w_out=p['w_out'].astype(BF16),
        hsum=hsum,
        qg=row(jnp.tile(p['na_q_norm'], (1, NA_HEADS))), kg=row(jnp.tile(p['na_k_norm'], (1, NA_HEADS))),
        ws=p['gm_ws'].astype(BF16), gmb=gmb,
        ssm=_ssm_params(p),
        glu_w=p['ssm_glu_w'].astype(BF16), glu_b=row(p['ssm_glu_b']),
        tb=tb, rm=rm,
    )


def _pairs_major(t):
    b, n_l, s = t.shape[:3]
    return jnp.transpose(t.reshape(b, n_l, s, HEAD_PAIRS, LANES), (0, 1, 3, 2, 4)).astype(BF16)


def _trunk_layer(x, mod, w, l, *, n_batch, seq_len, caches=None, is_ctx=False, ctx_kv=None, ssm_init=None):
    seg = dict(l=l, seg0=0 if is_ctx else 1, seg_tokens=x.shape[0] if is_ctx else seq_len)
    x = _ffn(x, mod, w['g1'], w['f1_in'], w['f1_out'], mod_base=0, **seg)
    outs = _inproj(x, mod, w['g2'], w['w_in'], w['hsum'], w['qg'], w['kg'], w['ws'], w['gmb'],
                   is_ctx=is_ctx, caches=caches, **seg)
    u, q, k, v, gm = outs[:5]
    s0 = jnp.zeros((SSM_GROUPS, 4, n_batch, LANES), F32) if is_ctx else ssm_init[l]
    y, fin = _ssm_core(u, *w['ssm'], s0, l=l, n_batch=n_batch, n_chunks=seq_len // SSM_CHUNK)
    if is_ctx:
        na = _ctx_attn(q, k, v, n_batch)
    else:
        na = _na_attn(q, k, v, ctx_kv[0], ctx_kv[1], w['tb'], w['rm'], n_batch, l=l)
    x = _outffn(x, mod, y, na, gm, w['glu_w'], w['glu_b'], w['w_out'], w['g3'], w['f2_in'], w['f2_out'], **seg)
    return x, (tuple(outs[5:]), fin)


def kernel(x_prompt, x_sample, c, cache_k, cache_v, state_ssm, c_ctx, w_ada, b_ada, norm_ffn1, ffn1_w_in, ffn1_w_out, norm_mix, w_in, w_out, ssm_lambda_re, ssm_lambda_im, ssm_log_dt, ssm_b_re, ssm_b_im, ssm_c_re, ssm_c_im, ssm_d, ssm_glu_w, ssm_glu_b, na_q_norm, na_k_norm, na_rpb, gm_ws, gm_bs, norm_ffn2, ffn2_w_in, ffn2_w_out):
    p = dict(norm_ffn1=norm_ffn1, ffn1_w_in=ffn1_w_in, ffn1_w_out=ffn1_w_out, norm_mix=norm_mix, w_in=w_in,
             w_out=w_out, ssm_lambda_re=ssm_lambda_re, ssm_lambda_im=ssm_lambda_im, ssm_log_dt=ssm_log_dt,
             ssm_b_re=ssm_b_re, ssm_b_im=ssm_b_im, ssm_c_re=ssm_c_re, ssm_c_im=ssm_c_im, ssm_d=ssm_d,
             ssm_glu_w=ssm_glu_w, ssm_glu_b=ssm_glu_b, na_q_norm=na_q_norm, na_k_norm=na_k_norm, na_rpb=na_rpb,
             gm_ws=gm_ws, gm_bs=gm_bs, norm_ffn2=norm_ffn2, ffn2_w_in=ffn2_w_in, ffn2_w_out=ffn2_w_out)
    batch, dec_batch = x_prompt.shape[0], x_sample.shape[0]
    cond8 = jnp.zeros((8, D_MODEL), F32).at[0].set(c_ctx).at[1:1 + dec_batch].set(c)
    mod = _adaln(cond8, w_ada, b_ada).reshape(DEPTH, 8, N_MOD, D_MODEL)

    xp = x_prompt.reshape(batch * SEQ, D_MODEL)
    xs = x_sample.reshape(dec_batch * DEC_SEQ, D_MODEL)
    w = _prep(p)
    ctx_kv = (_pairs_major(cache_k), _pairs_major(cache_v))
    ssm_init = _ssm_states_in(state_ssm)
    caches, fins = None, []
    for l in range(DEPTH):
        xp, (caches, fin) = _trunk_layer(xp, mod, w, l, n_batch=batch, seq_len=SEQ, caches=caches, is_ctx=True)
        fins.append(fin)
        xs, _ = _trunk_layer(xs, mod, w, l, n_batch=dec_batch, seq_len=DEC_SEQ, ctx_kv=ctx_kv, ssm_init=ssm_init)
    cache_shape = (batch, DEPTH, SEQ, NA_HEADS, NA_HEAD_DIM)
    return (xp.reshape(batch, SEQ, D_MODEL), xs.reshape(dec_batch, DEC_SEQ, D_MODEL),
            caches[0].reshape(cache_shape), caches[1].reshape(cache_shape), _ssm_states_out(fins))
```

```python
import functools

import numpy as np
import jax
import jax.numpy as jnp
from jax import lax
from jax.experimental import pallas as pl
from jax.experimental.pallas import tpu as pltpu

D_MODEL = 1024
DEPTH = 2
SEQ = 256
DEC_SEQ = 4096
PAST_LEN = 256
GRID_W = 64
SSM_WIDTH = 256
SSM_CH = 16
SSM_GROUPS = 16
SSM_STATE = 64
NA_WIDTH = 512
NA_HEAD_DIM = 64
NA_HEADS = 8
NA_MAX_ROWS = 8
NA_COLS = 16
GM_WIDTH = 256
GM_GROUPS = 4
GM_CHUNK = 128
D_FF = 2816
N_MOD = 9
RMS_EPS = 1e-6
LN_EPS = 1e-5
NEG_INF = -1e30

F32 = jnp.float32
BF16 = jnp.bfloat16

LANES = 128
SUBLANES = 8
MXU_DIM = 256
VMEM_LIMIT_BYTES = 56 * 1024 * 1024

FFN_TILE = 1024
MIX_TILE = 1024
OUT_TILE = 512
FF_CHUNK = MXU_DIM
N_FF_CHUNKS = D_FF // FF_CHUNK
SSM_CHUNK = 16
SSM_ROW = SSM_CHUNK * SSM_CH
SSM_GROUP_BLOCK = 4
SSM_ROW_BLOCK = 512
SSM_W_COLS = 4 * LANES
SCAN_SHIFTS = (1, 2, 4)
N_SCAN_CONST = 3 * len(SCAN_SHIFTS) + 3
HEAD_PAIRS = NA_HEADS // 2
CTX_ATTN_BATCH = 4
NA_Q_ROWS = 8
NA_K_ROWS = 16
NA_KCOLS = 2 * NA_COLS
NA_COL_BLOCKS = GRID_W // NA_COLS
NA_KC0 = (0, 8, 24, 32)
SEGS_PER_TILE = LANES // SSM_CH
NA_OFF_PAD = (NA_Q_ROWS - 1) - ((NA_MAX_ROWS - 1) - (NA_K_ROWS - NA_Q_ROWS) // 2)
NA_OFF_START = NA_Q_ROWS - 1
NA_OFF_ROWS = NA_OFF_START + NA_K_ROWS


def _silu(x):
    return x * (1.0 / (1.0 + jnp.exp(-x)))


def _sigmoid(x):
    return 1.0 / (1.0 + jnp.exp(-x))


def _gelu(x):
    return 0.5 * x * (1.0 + jnp.tanh(0.7978845608028654 * (x + 0.044715 * (x * x * x))))


def _cparams(sem):
    return pltpu.CompilerParams(dimension_semantics=sem, vmem_limit_bytes=VMEM_LIMIT_BYTES)


def _const_spec(shape):
    nd = len(shape)
    return pl.BlockSpec(shape, lambda *_: (0,) * nd, pipeline_mode=pl.Buffered(1))


def _layer_spec(stacked, l):
    nd = stacked.ndim
    return pl.BlockSpec((None,) + stacked.shape[1:], lambda *_: (l,) + (0,) * (nd - 1), pipeline_mode=pl.Buffered(1))


def _ada_kernel(c_ref, w_ref, b_ref, o_ref):
    s = _silu(c_ref[...]).astype(BF16)
    w = w_ref[0].astype(BF16)
    o_ref[0] = jnp.dot(s, w, preferred_element_type=F32) + b_ref[0]


def _adaln(cond8, w_ada, b_ada):
    tn = D_MODEL
    ncol = N_MOD * D_MODEL
    return pl.pallas_call(
        _ada_kernel,
        grid=(DEPTH, ncol // tn),
        in_specs=[
            pl.BlockSpec((8, D_MODEL), lambda l, j: (0, 0)),
            pl.BlockSpec((1, D_MODEL, tn), lambda l, j: (l, 0, j)),
            pl.BlockSpec((1, 1, tn), lambda l, j: (l, 0, j)),
        ],
        out_specs=pl.BlockSpec((1, 8, tn), lambda l, j: (l, 0, j)),
        out_shape=jax.ShapeDtypeStruct((DEPTH, 8, ncol), F32),
        compiler_params=_cparams(("arbitrary", "arbitrary")),
        name="adaln",
    )(cond8, w_ada, b_ada.reshape(DEPTH, 1, ncol))


def _normed(x, g_row, shift, scale):
    ms = jnp.mean(x * x, axis=-1, keepdims=True)
    h = x * lax.rsqrt(ms + RMS_EPS) * g_row
    return h * (1.0 + scale) + shift


def _mod_spec(l, seg0, seg_tokens, tile):
    tiles_per_seg = seg_tokens // tile
    return pl.BlockSpec((None, 1, N_MOD, D_MODEL), lambda i: (l, seg0 + i // tiles_per_seg, 0, 0))


def _ffn_kernel(x_ref, mod_ref, g_ref, win_ref, wout_ref, o_ref, h_ref, act_ref, *, mod_base):
    shift = mod_ref[0, mod_base:mod_base + 1, :]
    scale = mod_ref[0, mod_base + 1:mod_base + 2, :]
    gate = mod_ref[0, mod_base + 2:mod_base + 3, :]
    h_ref[...] = _normed(x_ref[...], g_ref[...], shift, scale).astype(BF16)
    for j in range(N_FF_CHUNKS):
        cols = slice(j * FF_CHUNK, (j + 1) * FF_CHUNK)
        gate_j = jnp.dot(h_ref[...], win_ref[:, cols], preferred_element_type=F32)
        up_j = jnp.dot(h_ref[...], win_ref[:, D_FF + j * FF_CHUNK:D_FF + (j + 1) * FF_CHUNK], preferred_element_type=F32)
        act_ref[:, cols] = (_silu(gate_j) * up_j).astype(BF16)
    for n in range(D_MODEL // MXU_DIM):
        cols = slice(n * MXU_DIM, (n + 1) * MXU_DIM)
        y = jnp.dot(act_ref[...], wout_ref[:, cols], preferred_element_type=F32)
        o_ref[:, cols] = x_ref[:, cols] + 0.5 * gate[:, cols] * y


def _ffn(x, mod, g, w_in, w_out, *, l, mod_base, seg0, seg_tokens):
    n_tok = x.shape[0]
    return pl.pallas_call(
        functools.partial(_ffn_kernel, mod_base=mod_base),
        grid=(n_tok // FFN_TILE,),
        in_specs=[
            pl.BlockSpec((FFN_TILE, D_MODEL), lambda i: (i, 0)),
            _mod_spec(l, seg0, seg_tokens, FFN_TILE),
            _layer_spec(g, l), _layer_spec(w_in, l), _layer_spec(w_out, l),
        ],
        out_specs=pl.BlockSpec((FFN_TILE, D_MODEL), lambda i: (i, 0)),
        out_shape=jax.ShapeDtypeStruct((n_tok, D_MODEL), F32),
        scratch_shapes=[pltpu.VMEM((FFN_TILE, D_MODEL), BF16), pltpu.VMEM((FFN_TILE, D_FF), BF16)],
        compiler_params=_cparams(("arbitrary",)),
        name="ffn",
    )(x, mod, g, w_in, w_out)


def _segment_ids():
    return lax.broadcasted_iota(jnp.int32, (1, LANES), 1) // SSM_CH


def _merge_segments(load_piece, rot):
    seg = _segment_ids()
    tiles = []
    for lt in range(SSM_ROW // LANES):
        acc = None
        for jj in range(SEGS_PER_TILE):
            src = (lt * SEGS_PER_TILE + jj - rot) % SSM_CHUNK
            piece = load_piece(src, lt)
            acc = piece if acc is None else jnp.where(seg == jj, piece, acc)
        tiles.append(acc)
    return jnp.concatenate(tiles, axis=-1)


def _roll_lanes(x, shift):
    shift %= x.shape[-1]
    return pltpu.roll(x, shift, axis=x.ndim - 1) if shift else x


def _inproj_kernel(*refs, n_in, n_out):
    x_ref, mod_ref, g_ref, w_ref, hsum_ref, qg_ref, kg_ref, ws_ref, gmb_ref = refs[:9]
    u_ref, q_ref, k_ref, v_ref, gm_ref, *tok_refs = refs[n_in:n_in + n_out]
    h_ref, xs_ref, z_ref = refs[n_in + n_out:]
    shift = mod_ref[0, 3:4, :]
    scale = mod_ref[0, 4:5, :]
    h_ref[...] = _normed(x_ref[...], g_ref[...], shift, scale).astype(BF16)
    n_rows = MIX_TILE // SSM_CHUNK

    def proj(c0, width):
        return jnp.dot(h_ref[...], w_ref[:, c0:c0 + width], preferred_element_type=F32)

    xs_qk = proj(0, SSM_WIDTH + 2 * NA_WIDTH)
    for j in range(SSM_WIDTH // LANES):
        xs_ref[j] = xs_qk[:, j * LANES:(j + 1) * LANES]
    for t in range(SSM_CHUNK):
        zt = jnp.concatenate([xs_ref[j, pl.ds(t, n_rows, stride=SSM_CHUNK), :] for j in range(SSM_WIDTH // LANES)],
                             axis=-1)
        z_ref[t] = _roll_lanes(zt, SSM_CH * t)
    for g in range(SSM_GROUPS):
        ug = _merge_segments(lambda src, lt: z_ref[src, :, lt * LANES:(lt + 1) * LANES], g)
        u_ref[g] = _roll_lanes(ug, -SSM_CH * g)

    c0 = SSM_WIDTH
    qk = xs_qk[:, c0:]
    sq = (qk * qk).astype(BF16)
    groups = range(0, 2 * NA_WIDTH, MXU_DIM)
    ms = jnp.dot(jnp.concatenate([sq[:, c:c + MXU_DIM] for c in groups], axis=0), hsum_ref[...],
                 preferred_element_type=F32)
    ms = jnp.concatenate([ms[i * MIX_TILE:(i + 1) * MIX_TILE, :] for i in range(len(groups))], axis=-1)
    qk = qk * lax.rsqrt(ms + RMS_EPS)
    q = qk[:, :NA_WIDTH] * qg_ref[...] * (NA_HEAD_DIM ** -0.5)
    k = qk[:, NA_WIDTH:] * kg_ref[...]
    v_uv = proj(c0 + 2 * NA_WIDTH, NA_WIDTH + 2 * GM_WIDTH)
    v = v_uv[:, :NA_WIDTH]
    for p in range(HEAD_PAIRS):
        lanes = slice(p * LANES, (p + 1) * LANES)
        q_ref[p] = q[:, lanes].astype(q_ref.dtype)
        k_ref[p] = k[:, lanes].astype(k_ref.dtype)
        v_ref[p] = v[:, lanes].astype(v_ref.dtype)
    if tok_refs:
        for slot in range(tok_refs[0].shape[1]):
            tok_refs[0][:, slot] = k.reshape(MIX_TILE // SEQ, SEQ, NA_WIDTH)
            tok_refs[1][:, slot] = v.reshape(MIX_TILE // SEQ, SEQ, NA_WIDTH)

    a = _gelu(v_uv[:, NA_WIDTH:])
    u = a[:, :GM_WIDTH]
    vv = a[:, GM_WIDTH:]
    mu = jnp.mean(vv, axis=-1, keepdims=True)
    vc = vv - mu
    var = jnp.mean(vc * vc, axis=-1, keepdims=True)
    vln = (vc * lax.rsqrt(var + LN_EPS)).astype(BF16)
    lane = lax.broadcasted_iota(jnp.int32, (1, GM_WIDTH), 1)
    gw = GM_WIDTH // GM_GROUPS
    for ci in range(MIX_TILE // GM_CHUNK):
        rows = slice(ci * GM_CHUNK, (ci + 1) * GM_CHUNK)
        vch = vln[rows, :]
        sp = gmb_ref[...]
        for gi in range(GM_GROUPS):
            t = jnp.dot(ws_ref[gi], vch, preferred_element_type=F32)
            sp = sp + jnp.where((lane >= gi * gw) & (lane < (gi + 1) * gw), t, 0.0)
        gm_ref[rows, :] = (u[rows, :] * sp).astype(BF16)


def _inproj(x, mod, g, w, hsum, qg, kg, ws, gmb, *, l, seg0, seg_tokens, is_ctx, caches=None):
    n_tok = x.shape[0]
    kv_dtype = BF16 if is_ctx else F32
    n_rows = MIX_TILE // SSM_CHUNK
    pair_spec = pl.BlockSpec((HEAD_PAIRS, MIX_TILE, LANES), lambda i: (0, i, 0))
    out_specs = [
        pl.BlockSpec((SSM_GROUPS, n_rows, SSM_ROW), lambda i: (0, i, 0)),
        pair_spec, pair_spec, pair_spec,
        pl.BlockSpec((MIX_TILE, GM_WIDTH), lambda i: (i, 0)),
    ]
    out_shape = [
        jax.ShapeDtypeStruct((SSM_GROUPS, n_tok // SSM_CHUNK, SSM_ROW), F32),
        jax.ShapeDtypeStruct((HEAD_PAIRS, n_tok, LANES), BF16),
        jax.ShapeDtypeStruct((HEAD_PAIRS, n_tok, LANES), kv_dtype),
        jax.ShapeDtypeStruct((HEAD_PAIRS, n_tok, LANES), kv_dtype),
        jax.ShapeDtypeStruct((n_tok, GM_WIDTH), BF16),
    ]
    args = [x, mod, g, w, hsum, qg, kg, ws, gmb]
    in_specs = [
        pl.BlockSpec((MIX_TILE, D_MODEL), lambda i: (i, 0)),
        _mod_spec(l, seg0, seg_tokens, MIX_TILE),
        _layer_spec(g, l), _layer_spec(w, l),
        _const_spec((MXU_DIM, MXU_DIM)),
        _layer_spec(qg, l), _layer_spec(kg, l), _layer_spec(ws, l), _layer_spec(gmb, l),
    ]
    aliases = {}
    if is_ctx:
        bt = MIX_TILE // SEQ
        cache_shape = jax.ShapeDtypeStruct((n_tok // SEQ, DEPTH, SEQ, NA_WIDTH), F32)
        if caches is None:
            out_specs += [pl.BlockSpec((bt, DEPTH, SEQ, NA_WIDTH), lambda i: (i, 0, 0, 0))] * 2
        else:
            out_specs += [pl.BlockSpec((bt, 1, SEQ, NA_WIDTH), lambda i: (i, l, 0, 0))] * 2
        out_shape += [cache_shape] * 2
        if caches is not None:
            aliases = {len(args): len(out_shape) - 2, len(args) + 1: len(out_shape) - 1}
            args += list(caches)
            in_specs += [pl.BlockSpec(memory_space=pl.ANY)] * 2
    return pl.pallas_call(
        functools.partial(_inproj_kernel, n_in=len(args), n_out=len(out_shape)),
        grid=(n_tok // MIX_TILE,),
        in_specs=in_specs,
        out_specs=out_specs,
        out_shape=out_shape,
        input_output_aliases=aliases,
        scratch_shapes=[pltpu.VMEM((MIX_TILE, D_MODEL), BF16), pltpu.VMEM((SSM_WIDTH // LANES, MIX_TILE, LANES), F32),
                        pltpu.VMEM((SSM_CHUNK, n_rows, SSM_ROW), F32)],
        compiler_params=_cparams(("arbitrary",)),
        name="inproj_ctx" if is_ctx else "inproj_lat",
    )(*args)


def _roll_rows(x, d, down):
    return pltpu.roll(x, d if down else SUBLANES - d, axis=0)


def _scan_tile(w, wp, c, cp, cst, row, down):
    x, xp = w, wp
    for si, d in enumerate(SCAN_SHIFTS):
        a, b, bp = cst[3 * si], cst[3 * si + 1], cst[3 * si + 2]
        xs, xps = _roll_rows(x, d, down), _roll_rows(xp, d, down)
        x, xp = x + a * xs + b * xps, xp + a * xps + bp * xs
    a, b, bp = cst[N_SCAN_CONST - 3], cst[N_SCAN_CONST - 2], cst[N_SCAN_CONST - 1]
    s = x + a * c + b * cp
    sp = xp + a * cp + bp * c
    edge = 0 if down else SUBLANES - 1
    last = SUBLANES - 1 if down else 0
    seen = jnp.where(row == edge, c, _roll_rows(s, 1, down))
    c_new = jnp.broadcast_to(s[last:last + 1, :], s.shape)
    cp_new = jnp.broadcast_to(sp[last:last + 1, :], sp.shape)
    return seen, c_new, cp_new


def _ssm_kernel(u_ref, mp_ref, q_ref, cst_ref, dvec_ref, s0_ref, y_ref, fin_ref, w_ref, sq_ref, *, n_batch, n_chunks):
    gb = u_ref.shape[0]
    n_rows = n_batch * n_chunks
    row_blk = min(n_rows, SSM_ROW_BLOCK)
    for g in range(gb):
        for r0 in range(0, n_rows, row_blk):
            rows = slice(r0, r0 + row_blk)
            ug = u_ref[g, rows, :]
            wy = jnp.dot(ug.astype(BF16), mp_ref[g], preferred_element_type=F32)
            y_ref[g, rows, :] = wy[:, :SSM_ROW] + ug * dvec_ref[g]
            w_ref[g, rows, :] = wy[:, SSM_ROW:]

    n_tiles = n_chunks // SUBLANES
    row = lax.broadcasted_iota(jnp.int32, (SUBLANES, LANES), 0)

    def tile_step(b, kt, carry):
        out = []
        for g in range(gb):
            cf, cfp, cb, cbp = carry[4 * g:4 * g + 4]
            rf = pl.multiple_of(b * n_chunks + kt * SUBLANES, SUBLANES)
            rb = pl.multiple_of(b * n_chunks + (n_tiles - 1 - kt) * SUBLANES, SUBLANES)
            wf = w_ref[g, pl.ds(rf, SUBLANES), :]
            wb = w_ref[g, pl.ds(rb, SUBLANES), :]
            cst_f = [cst_ref[g, 0, i] for i in range(N_SCAN_CONST)]
            cst_b = [cst_ref[g, 1, i] for i in range(N_SCAN_CONST)]
            seen_f, cf, cfp = _scan_tile(wf[:, 0:LANES], wf[:, LANES:2 * LANES], cf, cfp, cst_f, row, True)
            seen_b, cb, cbp = _scan_tile(wb[:, 2 * LANES:3 * LANES], wb[:, 3 * LANES:], cb, cbp, cst_b, row, False)
            sq_ref[g, pl.ds(rf, SUBLANES), 0:LANES] = seen_f
            sq_ref[g, pl.ds(rb, SUBLANES), LANES:2 * LANES] = seen_b
            out += [cf, cfp, cb, cbp]
        return tuple(out)

    def batch_body(b, _):
        carry = []
        for g in range(gb):
            for i in range(4):
                carry.append(jnp.broadcast_to(s0_ref[g, i, pl.ds(b, 1), :], (SUBLANES, LANES)))
        carry = tuple(carry)
        if n_tiles <= 2:
            for kt in range(n_tiles):
                carry = tile_step(b, kt, carry)
        else:
            carry = lax.fori_loop(0, n_tiles, lambda kt, c: tile_step(b, kt, c), carry)
        for g in range(gb):
            fin_ref[g, 0, pl.ds(b, 1), :] = carry[4 * g][0:1, :]
            fin_ref[g, 1, pl.ds(b, 1), :] = carry[4 * g + 2][0:1, :]
        return 0

    lax.fori_loop(0, n_batch, batch_body, 0)

    for g in range(gb):
        for r0 in range(0, n_rows, row_blk):
            rows = slice(r0, r0 + row_blk)
            y_ref[g, rows, :] += jnp.dot(sq_ref[g, rows, :].astype(BF16), q_ref[g], preferred_element_type=F32)


def _ssm_core(u, mp, q, cst, dvec, s0, *, l, n_batch, n_chunks):
    n_rows = n_batch * n_chunks
    gb = SSM_GROUP_BLOCK
    g3 = lambda i: (i, 0, 0)
    g4 = lambda i: (i, 0, 0, 0)
    return pl.pallas_call(
        functools.partial(_ssm_kernel, n_batch=n_batch, n_chunks=n_chunks),
        grid=(SSM_GROUPS // gb,),
        in_specs=[
            pl.BlockSpec((gb, n_rows, SSM_ROW), g3),
            pl.BlockSpec((None, gb, SSM_ROW, SSM_ROW + SSM_W_COLS), lambda i: (l, i, 0, 0)),
            pl.BlockSpec((None, gb, 2 * LANES, SSM_ROW), lambda i: (l, i, 0, 0)),
            pl.BlockSpec((None, gb, 2, N_SCAN_CONST, SUBLANES, LANES), lambda i: (l, i, 0, 0, 0, 0)),
            pl.BlockSpec((None, gb, 1, SSM_ROW), lambda i: (l, i, 0, 0)),
            pl.BlockSpec((gb, 4, n_batch, LANES), g4),
        ],
        out_specs=[
            pl.BlockSpec((gb, n_rows, SSM_ROW), g3),
            pl.BlockSpec((gb, 2, n_batch, LANES), g4),
        ],
        out_shape=[
            jax.ShapeDtypeStruct((SSM_GROUPS, n_rows, SSM_ROW), F32),
            jax.ShapeDtypeStruct((SSM_GROUPS, 2, n_batch, LANES), F32),
        ],
        scratch_shapes=[pltpu.VMEM((gb, n_rows, SSM_W_COLS), F32), pltpu.VMEM((gb, n_rows, 2 * LANES), F32)],
        compiler_params=_cparams(("arbitrary",)),
        name="ssm_core",
    )(u, mp, q, cst, dvec, s0)


def _cmul(a, b):
    return a[0] * b[0] - a[1] * b[1], a[0] * b[1] + a[1] * b[0]


_CHUNK_IDX = np.arange(SSM_CHUNK)


def _ssm_exponents():
    t = SSM_CHUNK
    idx = _CHUNK_IDX.astype(np.float32)
    both = lambda a: np.stack([a, a], axis=1)
    lane_step = idx - (t - 1) / 2
    tile_rows = np.arange(1, SUBLANES + 1, dtype=np.float32) * t
    tables = dict(
        one=both(np.ones(1, np.float32)),
        incr=np.stack([t - 1 - idx, idx], axis=1),
        seen=np.stack([idx + 1, t - idx], axis=1),
        resp_in=np.stack([-lane_step, lane_step], axis=1),
        resp_out=np.stack([lane_step, -lane_step], axis=1),
        shift=both(np.asarray(SCAN_SHIFTS, np.float32) * t),
        tile=np.stack([tile_rows, tile_rows[::-1]], axis=1),
    )
    spans, start = {}, 0
    for name, tab in tables.items():
        spans[name] = slice(start, start + len(tab))
        start += len(tab)
    return np.concatenate(list(tables.values()), axis=0), spans


def _ssm_params(p):
    t = SSM_CHUNK
    n_l = p['ssm_lambda_re'].shape[0]
    lam = (p['ssm_lambda_re'].astype(F32), p['ssm_lambda_im'].astype(F32))
    dt = jnp.exp(p['ssm_log_dt'].astype(F32))[..., None]
    arg = (lam[0] * dt, lam[1] * dt)
    expo, span = _ssm_exponents()
    n = jnp.asarray(expo)[:, None, :, None, None]
    mag = jnp.exp(n * arg[0][None])
    pw = (mag * jnp.cos(n * arg[1][None]), mag * jnp.sin(n * arg[1][None]))
    take = lambda name: [x[span[name]] for x in pw]
    lanes_last = lambda xs: [jnp.moveaxis(x, 0, -1) for x in xs]

    lbar = [x[0] for x in take('one')]
    num = (lbar[0] - 1.0, lbar[1])
    den = lam[0] * lam[0] + lam[1] * lam[1]
    zoh = ((num[0] * lam[0] + num[1] * lam[1]) / den, (num[1] * lam[0] - num[0] * lam[1]) / den)
    b_mat = (p['ssm_b_re'].astype(F32), p['ssm_b_im'].astype(F32))
    bbar = _cmul((zoh[0][..., None], zoh[1][..., None]), b_mat)
    c_mat = [jnp.swapaxes(p[k].astype(F32), -1, -2) for k in ('ssm_c_re', 'ssm_c_im')]
    ct = [jnp.tile(x, (1, 1, 1, 1, t)) for x in c_mat]
    bt = [jnp.tile(x, (1, 1, 1, 1, t)) for x in bbar]

    per_lane = lambda name: [jnp.repeat(x, SSM_CH, axis=-1) for x in lanes_last(take(name))]
    xm = _cmul(per_lane('resp_in'), bt)
    ym = _cmul(per_lane('resp_out'), ct)
    resp = jnp.einsum('ldgpx,ldgpy->ldgxy', jnp.concatenate([xm[0], -xm[1]], axis=3),
                      jnp.concatenate([ym[0], ym[1]], axis=3), precision='highest')
    steps_x = np.repeat(_CHUNK_IDX, SSM_CH)
    causal = np.stack([steps_x[None, :] >= steps_x[:, None], steps_x[:, None] >= steps_x[None, :]])
    mats = jnp.sum(jnp.where(jnp.asarray(causal)[None, :, None], resp, 0.0), axis=1)

    incr = _cmul([x[..., None] for x in take('incr')], [x[None] for x in bbar])
    incr = [jnp.transpose(x, (1, 2, 3, 0, 5, 4)) for x in incr]
    pcat = jnp.concatenate([incr[0], incr[1], incr[1], incr[0]], axis=-1)
    pcat = jnp.concatenate([pcat[:, 0], pcat[:, 1]], axis=-1).reshape(n_l, SSM_GROUPS, SSM_ROW, SSM_W_COLS)
    mp = jnp.concatenate([mats, pcat], axis=-1).astype(BF16)

    qm = _cmul(ct, per_lane('seen'))
    q = jnp.concatenate([qm[0][:, 0], -qm[1][:, 0], qm[0][:, 1], -qm[1][:, 1]], axis=2).astype(BF16)

    def forms(name, rows):
        pr, pi = [jnp.moveaxis(x, 0, 3) for x in take(name)]
        f = jnp.stack([jnp.concatenate([pr, pr], -1), jnp.concatenate([-pi, pi], -1),
                       jnp.concatenate([pi, -pi], -1)], axis=4)
        if rows:
            return f
        r = np.arange(SUBLANES)[None, :]
        d = np.repeat(np.asarray(SCAN_SHIFTS), 3)[:, None]
        keep = np.stack([r >= d, r < SUBLANES - d]).astype(np.float32)
        return f.reshape(n_l, 2, SSM_GROUPS, 3 * len(SCAN_SHIFTS), 1, LANES) * keep[None, :, None, :, :, None]
    per_row = jnp.swapaxes(forms('tile', True), 3, 4)
    cst = jnp.swapaxes(jnp.concatenate([forms('shift', False), per_row], axis=3), 1, 2)
    dvec = jnp.tile(p['ssm_d'].astype(F32).reshape(n_l, SSM_GROUPS, 1, SSM_CH), (1, 1, 1, SSM_CHUNK))
    return mp, q, cst, dvec


def _ssm_states_in(state_ssm):
    st = jnp.transpose(state_ssm.astype(F32), (1, 3, 2, 0, 4, 5))
    re, im = st[..., 0], st[..., 1]
    ri, ir = jnp.concatenate([re, im], -1), jnp.concatenate([im, re], -1)
    return jnp.stack([ri[:, :, 0], ir[:, :, 0], ri[:, :, 1], ir[:, :, 1]], axis=2)


def _ssm_states_out(fins):
    fin = jnp.stack(fins, axis=0)
    fin = jnp.stack([fin[..., :SSM_STATE], fin[..., SSM_STATE:]], axis=-1)
    return jnp.transpose(fin, (3, 0, 2, 1, 4, 5))


def _head_lane_mask(hh):
    lane = lax.broadcasted_iota(jnp.int32, (1, LANES), 1)
    return (lane >= hh * NA_HEAD_DIM) & (lane < (hh + 1) * NA_HEAD_DIM)


_NT_DIMS = (((1,), (1,)), ((), ()))


def _ctx_attn_kernel(q_ref, k_ref, v_ref, o_ref):
    lm0 = _head_lane_mask(0)
    for b in range(CTX_ATTN_BATCH):
        rows = slice(b * SEQ, (b + 1) * SEQ)
        for p in range(HEAD_PAIRS):
            qp, kp, vp = q_ref[p, rows, :], k_ref[p, rows, :], v_ref[p, rows, :]
            zero = jnp.zeros_like(qp)
            q2 = jnp.concatenate([jnp.where(lm0, qp, zero), jnp.where(lm0, zero, qp)], axis=0)
            s = lax.dot_general(q2, kp, _NT_DIMS, preferred_element_type=F32)
            e = jnp.exp(s - jnp.max(s, axis=-1, keepdims=True))
            o = jnp.dot(e.astype(BF16), vp, preferred_element_type=F32) / jnp.sum(e, axis=-1, keepdims=True)
            o_ref[rows, p * LANES:(p + 1) * LANES] = jnp.where(lm0, o[:SEQ, :], o[SEQ:, :]).astype(BF16)


def _ctx_attn(q, k, v, n_batch):
    rows = CTX_ATTN_BATCH * SEQ
    spec = pl.BlockSpec((HEAD_PAIRS, rows, LANES), lambda b: (0, b, 0))
    return pl.pallas_call(
        _ctx_attn_kernel,
        grid=(n_batch // CTX_ATTN_BATCH,),
        in_specs=[spec, spec, spec],
        out_specs=pl.BlockSpec((rows, NA_WIDTH), lambda b: (b, 0)),
        out_shape=jax.ShapeDtypeStruct((n_batch * SEQ, NA_WIDTH), BF16),
        compiler_params=_cparams(("arbitrary",)),
        name="ctx_attn",
    )(q, k, v)


def _na_kernel(q_ref, k_ref, v_ref, kc_ref, vc_ref, t2_ref, rm_ref, o_ref, qb_ref, os_ref, tb_ref):
    rt = pl.program_id(1)
    n_rt = pl.num_programs(1)

    @pl.when((pl.program_id(0) == 0) & (rt == 0))
    def _():
        for h in range(NA_HEADS):
            for n in range(NA_COL_BLOCKS):
                for rl in range(NA_Q_ROWS):
                    lane0 = (NA_OFF_START - rl) * NA_KCOLS
                    tb_ref[h, n, rl * NA_COLS:(rl + 1) * NA_COLS, :] = (
                        t2_ref[h, n, :, lane0:lane0 + NA_K_ROWS * NA_KCOLS])

    r0 = rt * NA_Q_ROWS
    ttype = jnp.where(rt == 0, 0, jnp.where(rt == n_rt - 1, 2, 1))
    rmask = rm_ref[ttype]
    blk_q = NA_Q_ROWS * NA_COLS

    def pair_body(p, carry):
        lm0 = _head_lane_mask(0)
        for n in range(NA_COL_BLOCKS):
            for rl in range(NA_Q_ROWS):
                qrow = q_ref[p, rl * GRID_W + n * NA_COLS:rl * GRID_W + (n + 1) * NA_COLS, :]
                zero = jnp.zeros_like(qrow)
                base = 2 * n * blk_q + rl * NA_COLS
                qb_ref[base:base + NA_COLS, :] = jnp.where(lm0, qrow, zero)
                qb_ref[base + blk_q:base + blk_q + NA_COLS, :] = jnp.where(lm0, zero, qrow)
        kcp, vcp = kc_ref[0, p], vc_ref[0, p]
        n_win = NA_K_ROWS * NA_KCOLS
        for n in range(NA_COL_BLOCKS):
            kparts, vparts = [], []
            for i in range(NA_K_ROWS):
                kr = jnp.clip(r0 - (NA_K_ROWS - NA_Q_ROWS) // 2 + i, 0, GRID_W - 1)
                start = pl.multiple_of(kr * GRID_W + NA_KC0[n], 8)
                kparts.append(k_ref[p, pl.ds(start, NA_KCOLS), :])
                vparts.append(v_ref[p, pl.ds(start, NA_KCOLS), :])
            kall = jnp.concatenate([jnp.concatenate(kparts, axis=0).astype(BF16), kcp], axis=0)
            vall = jnp.concatenate([jnp.concatenate(vparts, axis=0).astype(BF16), vcp], axis=0)
            rows2 = slice(2 * n * blk_q, 2 * (n + 1) * blk_q)
            s = lax.dot_general(qb_ref[rows2, :], kall, _NT_DIMS, preferred_element_type=F32)
            bias = jnp.concatenate([tb_ref[2 * p, n] + rmask, tb_ref[2 * p + 1, n] + rmask], axis=0)
            s = jnp.concatenate([s[:, :n_win] + bias, s[:, n_win:]], axis=-1)
            e = jnp.exp(s - jnp.max(s, axis=-1, keepdims=True))
            o = jnp.dot(e.astype(BF16), vall, preferred_element_type=F32) / jnp.sum(e, axis=-1, keepdims=True)
            oblk = jnp.where(lm0, o[:blk_q, :], o[blk_q:, :])
            for rl in range(NA_Q_ROWS):
                os_ref[p, rl * GRID_W + n * NA_COLS:rl * GRID_W + (n + 1) * NA_COLS, :] = (
                    oblk[rl * NA_COLS:(rl + 1) * NA_COLS, :])
        return carry

    lax.fori_loop(0, HEAD_PAIRS, pair_body, 0)
    for p in range(HEAD_PAIRS):
        o_ref[:, p * LANES:(p + 1) * LANES] = os_ref[p].astype(BF16)


def _na_attn(q, k, v, kc, vc, tb, rm, n_batch, *, l):
    tile_tok = NA_Q_ROWS * GRID_W
    n_rt = DEC_SEQ // tile_tok
    blk_q, blk_k = NA_Q_ROWS * NA_COLS, NA_K_ROWS * NA_KCOLS
    return pl.pallas_call(
        _na_kernel,
        grid=(n_batch, n_rt),
        in_specs=[
            pl.BlockSpec((HEAD_PAIRS, tile_tok, LANES), lambda b, r: (0, b * n_rt + r, 0)),
            pl.BlockSpec((HEAD_PAIRS, DEC_SEQ, LANES), lambda b, r: (0, b, 0)),
            pl.BlockSpec((HEAD_PAIRS, DEC_SEQ, LANES), lambda b, r: (0, b, 0)),
            pl.BlockSpec((1, None, HEAD_PAIRS, PAST_LEN, LANES), lambda b, r: (b, l, 0, 0, 0)),
            pl.BlockSpec((1, None, HEAD_PAIRS, PAST_LEN, LANES), lambda b, r: (b, l, 0, 0, 0)),
            _layer_spec(tb, l),
            _const_spec(rm.shape),
        ],
        out_specs=pl.BlockSpec((tile_tok, NA_WIDTH), lambda b, r: (b * n_rt + r, 0)),
        out_shape=jax.ShapeDtypeStruct((n_batch * DEC_SEQ, NA_WIDTH), BF16),
        scratch_shapes=[
            pltpu.VMEM((2 * tile_tok, LANES), BF16),
            pltpu.VMEM((HEAD_PAIRS, tile_tok, LANES), F32),
            pltpu.VMEM((NA_HEADS, NA_COL_BLOCKS, blk_q, blk_k), F32)],
        compiler_params=_cparams(("arbitrary", "arbitrary")),
        name="na_attn",
    )(q, k, v, kc, vc, tb, rm)


def _na_tables(rpb):
    n = np.arange(NA_COL_BLOCKS)
    cc = np.arange(NA_COLS)
    kk = np.arange(NA_KCOLS)
    rl = np.arange(NA_Q_ROWS)
    ki = np.arange(NA_K_ROWS)
    n_dr, n_dc = 2 * NA_MAX_ROWS - 1, 2 * NA_COLS - 1
    c = n[:, None] * NA_COLS + cc[None, :]
    kcol = np.asarray(NA_KC0)[:, None] + kk[None, :]
    dc = np.clip(kcol[:, None, :] - c[:, :, None], -(NA_COLS - 1), NA_COLS - 1) + (NA_COLS - 1)
    cs = np.clip(c - NA_COLS // 2, 0, GRID_W - NA_COLS)
    col_ok = (kcol[:, None, :] >= cs[:, :, None]) & (kcol[:, None, :] < cs[:, :, None] + NA_COLS)
    half = (NA_K_ROWS - NA_Q_ROWS) // 2
    dr = np.clip(ki[None, :] - half - rl[:, None] + (NA_MAX_ROWS - 1), 0, n_dr - 1)
    oh_dc = jnp.asarray(dc[None] == np.arange(n_dc)[:, None, None, None], F32)
    off = np.clip(np.arange(NA_OFF_ROWS) - NA_OFF_PAD, 0, n_dr - 1)
    assert np.array_equal(dr, off[ki[None] + (NA_OFF_START - rl)[:, None]])
    oh_off = jnp.asarray(off[None, :] == np.arange(n_dr)[:, None], F32)
    rows_p = jnp.einsum('lhab,aA->lhAb', rpb.astype(F32), oh_off, precision='highest')
    cols = jnp.einsum('lhAb,bnck->lhncAk', rows_p, oh_dc, precision='highest')
    cols = jnp.where(jnp.asarray(col_ok)[None, None, :, :, None, :], cols, NEG_INF)
    tb = cols.reshape(-1, NA_HEADS, NA_COL_BLOCKS, NA_COLS, NA_OFF_ROWS * NA_KCOLS)
    rows = GRID_W
    masks = []
    for r0 in (0, NA_Q_ROWS, rows - NA_Q_ROWS):
        r = r0 + rl
        rs = np.clip(r - NA_MAX_ROWS // 2, 0, rows - NA_MAX_ROWS)
        kr = r0 - half + ki
        ok = (kr[None, :] >= rs[:, None]) & (kr[None, :] < rs[:, None] + NA_MAX_ROWS)
        full = np.broadcast_to(ok[:, None, :, None], (NA_Q_ROWS, NA_COLS, NA_K_ROWS, NA_KCOLS))
        masks.append(np.where(full, 0.0, NEG_INF).reshape(NA_Q_ROWS * NA_COLS, NA_K_ROWS * NA_KCOLS))
    rm = jnp.asarray(np.stack(masks), F32)
    return tb, rm


def _outffn_kernel(x_ref, mod_ref, y_ref, na_ref, gm_ref, gw_ref, gb_ref, wo_ref, g_ref, win_ref, wout_ref, o_ref,
                   yr_ref, ys_ref, cat_ref, xm_ref, h_ref, act_ref):
    gate = mod_ref[0, 5:6, :]
    n_rows = OUT_TILE // SSM_CHUNK
    for g in range(SSM_GROUPS):
        yr_ref[g] = _roll_lanes(y_ref[g], SSM_CH * g)
    for t in range(SSM_CHUNK):
        yt = _merge_segments(lambda src, lt: yr_ref[src, :, lt * LANES:(lt + 1) * LANES], t)
        yt = _roll_lanes(yt, -SSM_CH * t)
        for j in range(SSM_WIDTH // LANES):
            ys_ref[j, pl.ds(t, n_rows, stride=SSM_CHUNK), :] = yt[:, j * LANES:(j + 1) * LANES]
    y = _gelu(jnp.concatenate([ys_ref[j] for j in range(SSM_WIDTH // LANES)], axis=-1))
    z = jnp.dot(y.astype(BF16), gw_ref[...], preferred_element_type=F32) + gb_ref[...]
    cat_ref[:, 0:SSM_WIDTH] = (y * _sigmoid(z)).astype(BF16)
    cat_ref[:, SSM_WIDTH:SSM_WIDTH + NA_WIDTH] = na_ref[...]
    cat_ref[:, SSM_WIDTH + NA_WIDTH:] = gm_ref[...]
    mix = jnp.dot(cat_ref[...], wo_ref[...], preferred_element_type=F32)
    xm_ref[...] = x_ref[...] + gate * mix
    _ffn_kernel(xm_ref, mod_ref, g_ref, win_ref, wout_ref, o_ref, h_ref, act_ref, mod_base=6)


def _outffn(x, mod, y, na, gm, gw, gb, wo, g, w_in, w_out, *, l, seg0, seg_tokens):
    n_tok = x.shape[0]
    n_rows = OUT_TILE // SSM_CHUNK
    tok = lambda width: pl.BlockSpec((OUT_TILE, width), lambda i: (i, 0))
    return pl.pallas_call(
        _outffn_kernel,
        grid=(n_tok // OUT_TILE,),
        in_specs=[
            tok(D_MODEL),
            _mod_spec(l, seg0, seg_tokens, OUT_TILE),
            pl.BlockSpec((SSM_GROUPS, n_rows, SSM_ROW), lambda i: (0, i, 0)),
            tok(NA_WIDTH), tok(GM_WIDTH),
            _layer_spec(gw, l), _layer_spec(gb, l), _layer_spec(wo, l),
            _layer_spec(g, l), _layer_spec(w_in, l), _layer_spec(w_out, l),
        ],
        out_specs=tok(D_MODEL),
        out_shape=jax.ShapeDtypeStruct((n_tok, D_MODEL), F32),
        scratch_shapes=[pltpu.VMEM((SSM_GROUPS, n_rows, SSM_ROW), F32),
                        pltpu.VMEM((SSM_WIDTH // LANES, OUT_TILE, LANES), F32),
                        pltpu.VMEM((OUT_TILE, D_MODEL), BF16),
                        pltpu.VMEM((OUT_TILE, D_MODEL), F32),
                        pltpu.VMEM((OUT_TILE, D_MODEL), BF16), pltpu.VMEM((OUT_TILE, D_FF), BF16)],
        compiler_params=_cparams(("arbitrary",)),
        name="outffn",
    )(x, mod, y, na, gm, gw, gb, wo, g, w_in, w_out)


def _prep(p):
    heads = np.arange(MXU_DIM) // NA_HEAD_DIM
    hsum = jnp.asarray((heads[:, None] == heads[None, :]) / NA_HEAD_DIM, BF16)
    gmb = jnp.repeat(jnp.swapaxes(p['gm_bs'], 1, 2).astype(F32), GM_WIDTH // GM_GROUPS, axis=2)
    tb, rm = _na_tables(p['na_rpb'])
    row = lambda a: a.astype(F32).reshape(DEPTH, 1, -1)
    return dict(
        g1=row(p['norm_ffn1']), g2=row(p['norm_mix']), g3=row(p['norm_ffn2']),
        f1_in=p['ffn1_w_in'].astype(BF16), f1_out=p['ffn1_w_out'].astype(BF16),
        f2_in=p['ffn2_w_in'].astype(BF16), f2_out=p['ffn2_w_out'].astype(BF16),
        w_in=p['w_in'].astype(BF16), w_out=p['w_out'].astype(BF16),
        hsum=hsum,
        qg=row(jnp.tile(p['na_q_norm'], (1, NA_HEADS))), kg=row(jnp.tile(p['na_k_norm'], (1, NA_HEADS))),
        ws=p['gm_ws'].astype(BF16), gmb=gmb,
        ssm=_ssm_params(p),
        glu_w=p['ssm_glu_w'].astype(BF16), glu_b=row(p['ssm_glu_b']),
        tb=tb, rm=rm,
    )


def _pairs_major(t):
    b, n_l, s = t.shape[:3]
    return jnp.transpose(t.reshape(b, n_l, s, HEAD_PAIRS, LANES), (0, 1, 3, 2, 4)).astype(BF16)


def _trunk_layer(x, mod, w, l, *, n_batch, seq_len, caches=None, is_ctx=False, ctx_kv=None, ssm_init=None):
    seg = dict(l=l, seg0=0 if is_ctx else 1, seg_tokens=x.shape[0] if is_ctx else seq_len)
    x = _ffn(x, mod, w['g1'], w['f1_in'], w['f1_out'], mod_base=0, **seg)
    outs = _inproj(x, mod, w['g2'], w['w_in'], w['hsum'], w['qg'], w['kg'], w['ws'], w['gmb'],
                   is_ctx=is_ctx, caches=caches, **seg)
    u, q, k, v, gm = outs[:5]
    s0 = jnp.zeros((SSM_GROUPS, 4, n_batch, LANES), F32) if is_ctx else ssm_init[l]
    y, fin = _ssm_core(u, *w['ssm'], s0, l=l, n_batch=n_batch, n_chunks=seq_len // SSM_CHUNK)
    if is_ctx:
        na = _ctx_attn(q, k, v, n_batch)
    else:
        na = _na_attn(q, k, v, ctx_kv[0], ctx_kv[1], w['tb'], w['rm'], n_batch, l=l)
    x = _outffn(x, mod, y, na, gm, w['glu_w'], w['glu_b'], w['w_out'], w['g3'], w['f2_in'], w['f2_out'], **seg)
    return x, (tuple(outs[5:]), fin)


def kernel(x_prompt, x_sample, c, cache_k, cache_v, state_ssm, c_ctx, w_ada, b_ada, norm_ffn1, ffn1_w_in, ffn1_w_out, norm_mix, w_in, w_out, ssm_lambda_re, ssm_lambda_im, ssm_log_dt, ssm_b_re, ssm_b_im, ssm_c_re, ssm_c_im, ssm_d, ssm_glu_w, ssm_glu_b, na_q_norm, na_k_norm, na_rpb, gm_ws, gm_bs, norm_ffn2, ffn2_w_in, ffn2_w_out):
    p = dict(norm_ffn1=norm_ffn1, ffn1_w_in=ffn1_w_in, ffn1_w_out=ffn1_w_out, norm_mix=norm_mix, w_in=w_in,
             w_out=w_out, ssm_lambda_re=ssm_lambda_re, ssm_lambda_im=ssm_lambda_im, ssm_log_dt=ssm_log_dt,
             ssm_b_re=ssm_b_re, ssm_b_im=ssm_b_im, ssm_c_re=ssm_c_re, ssm_c_im=ssm_c_im, ssm_d=ssm_d,
             ssm_glu_w=ssm_glu_w, ssm_glu_b=ssm_glu_b, na_q_norm=na_q_norm, na_k_norm=na_k_norm, na_rpb=na_rpb,
             gm_ws=gm_ws, gm_bs=gm_bs, norm_ffn2=norm_ffn2, ffn2_w_in=ffn2_w_in, ffn2_w_out=ffn2_w_out)
    batch, dec_batch = x_prompt.shape[0], x_sample.shape[0]
    cond8 = jnp.zeros((8, D_MODEL), F32).at[0].set(c_ctx).at[1:1 + dec_batch].set(c)
    mod = _adaln(cond8, w_ada, b_ada).reshape(DEPTH, 8, N_MOD, D_MODEL)

    xp = x_prompt.reshape(batch * SEQ, D_MODEL)
    xs = x_sample.reshape(dec_batch * DEC_SEQ, D_MODEL)
    w = _prep(p)
    ctx_kv = (_pairs_major(cache_k), _pairs_major(cache_v))
    ssm_init = _ssm_states_in(state_ssm)
    caches, fins = None, []
    for l in range(DEPTH):
        xp, (caches, fin) = _trunk_layer(xp, mod, w, l, n_batch=batch, seq_len=SEQ, caches=caches, is_ctx=True)
        fins.append(fin)
        xs, _ = _trunk_layer(xs, mod, w, l, n_batch=dec_batch, seq_len=DEC_SEQ, ctx_kv=ctx_kv, ssm_init=ssm_init)
    cache_shape = (batch, DEPTH, SEQ, NA_HEADS, NA_HEAD_DIM)
    return (xp.reshape(batch, SEQ, D_MODEL), xs.reshape(dec_batch, DEC_SEQ, D_MODEL),
            caches[0].reshape(cache_shape), caches[1].reshape(cache_shape), _ssm_states_out(fins))
```

```python
import functools

import numpy as np
import jax
import jax.numpy as jnp
from jax import lax
from jax.experimental import pallas as pl
from jax.experimental.pallas import tpu as pltpu

D_MODEL = 1024
DEPTH = 2
SEQ = 256
DEC_SEQ = 4096
PAST_LEN = 256
GRID_W = 64
SSM_WIDTH = 256
SSM_CH = 16
SSM_GROUPS = 16
SSM_STATE = 64
NA_WIDTH = 512
NA_HEAD_DIM = 64
NA_HEADS = 8
NA_MAX_ROWS = 8
NA_COLS = 16
GM_WIDTH = 256
GM_GROUPS = 4
GM_CHUNK = 128
D_FF = 2816
N_MOD = 9
RMS_EPS = 1e-6
LN_EPS = 1e-5
NEG_INF = -1e30

F32 = jnp.float32
BF16 = jnp.bfloat16

LANES = 128
SUBLANES = 8
MXU_DIM = 256
VMEM_LIMIT_BYTES = 56 * 1024 * 1024

FFN_TILE = 1024
MIX_TILE = 1024
OUT_TILE = 512
FF_CHUNK = MXU_DIM
N_FF_CHUNKS = D_FF // FF_CHUNK
SSM_CHUNK = 16
SSM_ROW = SSM_CHUNK * SSM_CH
SSM_GROUP_BLOCK = 4
SSM_ROW_BLOCK = 512
SSM_W_COLS = 4 * LANES
SCAN_SHIFTS = (1, 2, 4)
N_SCAN_CONST = 3 * len(SCAN_SHIFTS) + 3
HEAD_PAIRS = NA_HEADS // 2
CTX_ATTN_BATCH = 4
NA_Q_ROWS = 8
NA_K_ROWS = 16
NA_KCOLS = 2 * NA_COLS
NA_COL_BLOCKS = GRID_W // NA_COLS
NA_KC0 = (0, 8, 24, 32)
SEGS_PER_TILE = LANES // SSM_CH
NA_OFF_PAD = (NA_Q_ROWS - 1) - ((NA_MAX_ROWS - 1) - (NA_K_ROWS - NA_Q_ROWS) // 2)
NA_OFF_START = NA_Q_ROWS - 1
NA_OFF_ROWS = NA_OFF_START + NA_K_ROWS


def _silu(x):
    return x * (1.0 / (1.0 + jnp.exp(-x)))


def _sigmoid(x):
    return 1.0 / (1.0 + jnp.exp(-x))


def _gelu(x):
    return 0.5 * x * (1.0 + jnp.tanh(0.7978845608028654 * (x + 0.044715 * (x * x * x))))


def _cparams(sem):
    return pltpu.CompilerParams(dimension_semantics=sem, vmem_limit_bytes=VMEM_LIMIT_BYTES)


def _const_spec(shape):
    nd = len(shape)
    return pl.BlockSpec(shape, lambda *_: (0,) * nd, pipeline_mode=pl.Buffered(1))


def _layer_spec(stacked, l):
    nd = stacked.ndim
    return pl.BlockSpec((None,) + stacked.shape[1:], lambda *_: (l,) + (0,) * (nd - 1), pipeline_mode=pl.Buffered(1))


def _ada_kernel(c_ref, w_ref, b_ref, o_ref):
    s = _silu(c_ref[...]).astype(BF16)
    w = w_ref[0].astype(BF16)
    o_ref[0] = jnp.dot(s, w, preferred_element_type=F32) + b_ref[0]


def _adaln(cond8, w_ada, b_ada):
    tn = D_MODEL
    ncol = N_MOD * D_MODEL
    return pl.pallas_call(
        _ada_kernel,
        grid=(DEPTH, ncol // tn),
        in_specs=[
            pl.BlockSpec((8, D_MODEL), lambda l, j: (0, 0)),
            pl.BlockSpec((1, D_MODEL, tn), lambda l, j: (l, 0, j)),
            pl.BlockSpec((1, 1, tn), lambda l, j: (l, 0, j)),
        ],
        out_specs=pl.BlockSpec((1, 8, tn), lambda l, j: (l, 0, j)),
        out_shape=jax.ShapeDtypeStruct((DEPTH, 8, ncol), F32),
        compiler_params=_cparams(("arbitrary", "arbitrary")),
        name="adaln",
    )(cond8, w_ada, b_ada.reshape(DEPTH, 1, ncol))


def _normed(x, g_row, shift, scale):
    ms = jnp.mean(x * x, axis=-1, keepdims=True)
    h = x * lax.rsqrt(ms + RMS_EPS) * g_row
    return h * (1.0 + scale) + shift


def _mod_spec(l, seg0, seg_tokens, tile):
    tiles_per_seg = seg_tokens // tile
    return pl.BlockSpec((None, 1, N_MOD, D_MODEL), lambda i: (l, seg0 + i // tiles_per_seg, 0, 0))


def _ffn_kernel(x_ref, mod_ref, g_ref, win_ref, wout_ref, o_ref, h_ref, act_ref, *, mod_base):
    shift = mod_ref[0, mod_base:mod_base + 1, :]
    scale = mod_ref[0, mod_base + 1:mod_base + 2, :]
    gate = mod_ref[0, mod_base + 2:mod_base + 3, :]
    h_ref[...] = _normed(x_ref[...], g_ref[...], shift, scale).astype(BF16)
    for j in range(N_FF_CHUNKS):
        cols = slice(j * FF_CHUNK, (j + 1) * FF_CHUNK)
        gate_j = jnp.dot(h_ref[...], win_ref[:, cols], preferred_element_type=F32)
        up_j = jnp.dot(h_ref[...], win_ref[:, D_FF + j * FF_CHUNK:D_FF + (j + 1) * FF_CHUNK], preferred_element_type=F32)
        act_ref[:, cols] = (_silu(gate_j) * up_j).astype(BF16)
    for n in range(D_MODEL // MXU_DIM):
        cols = slice(n * MXU_DIM, (n + 1) * MXU_DIM)
        y = jnp.dot(act_ref[...], wout_ref[:, cols], preferred_element_type=F32)
        o_ref[:, cols] = x_ref[:, cols] + 0.5 * gate[:, cols] * y


def _ffn(x, mod, g, w_in, w_out, *, l, mod_base, seg0, seg_tokens):
    n_tok = x.shape[0]
    return pl.pallas_call(
        functools.partial(_ffn_kernel, mod_base=mod_base),
        grid=(n_tok // FFN_TILE,),
        in_specs=[
            pl.BlockSpec((FFN_TILE, D_MODEL), lambda i: (i, 0)),
            _mod_spec(l, seg0, seg_tokens, FFN_TILE),
            _layer_spec(g, l), _layer_spec(w_in, l), _layer_spec(w_out, l),
        ],
        out_specs=pl.BlockSpec((FFN_TILE, D_MODEL), lambda i: (i, 0)),
        out_shape=jax.ShapeDtypeStruct((n_tok, D_MODEL), F32),
        scratch_shapes=[pltpu.VMEM((FFN_TILE, D_MODEL), BF16), pltpu.VMEM((FFN_TILE, D_FF), BF16)],
        compiler_params=_cparams(("arbitrary",)),
        name="ffn",
    )(x, mod, g, w_in, w_out)


def _segment_ids():
    return lax.broadcasted_iota(jnp.int32, (1, LANES), 1) // SSM_CH


def _merge_segments(load_piece, rot):
    seg = _segment_ids()
    tiles = []
    for lt in range(SSM_ROW // LANES):
        acc = None
        for jj in range(SEGS_PER_TILE):
            src = (lt * SEGS_PER_TILE + jj - rot) % SSM_CHUNK
            piece = load_piece(src, lt)
            acc = piece if acc is None else jnp.where(seg == jj, piece, acc)
        tiles.append(acc)
    return jnp.concatenate(tiles, axis=-1)


def _roll_lanes(x, shift):
    shift %= x.shape[-1]
    return pltpu.roll(x, shift, axis=x.ndim - 1) if shift else x


def _inproj_kernel(*refs, n_in, n_out):
    x_ref, mod_ref, g_ref, w_ref, hsum_ref, qg_ref, kg_ref, ws_ref, gmb_ref = refs[:9]
    u_ref, q_ref, k_ref, v_ref, gm_ref, *tok_refs = refs[n_in:n_in + n_out]
    h_ref, xs_ref, z_ref = refs[n_in + n_out:]
    shift = mod_ref[0, 3:4, :]
    scale = mod_ref[0, 4:5, :]
    h_ref[...] = _normed(x_ref[...], g_ref[...], shift, scale).astype(BF16)
    n_rows = MIX_TILE // SSM_CHUNK

    def proj(c0, width):
        return jnp.dot(h_ref[...], w_ref[:, c0:c0 + width], preferred_element_type=F32)

    xs_qk = proj(0, SSM_WIDTH + 2 * NA_WIDTH)
    for j in range(SSM_WIDTH // LANES):
        xs_ref[j] = xs_qk[:, j * LANES:(j + 1) * LANES]
    for t in range(SSM_CHUNK):
        zt = jnp.concatenate([xs_ref[j, pl.ds(t, n_rows, stride=SSM_CHUNK), :] for j in range(SSM_WIDTH // LANES)],
                             axis=-1)
        z_ref[t] = _roll_lanes(zt, SSM_CH * t)
    for g in range(SSM_GROUPS):
        ug = _merge_segments(lambda src, lt: z_ref[src, :, lt * LANES:(lt + 1) * LANES], g)
        u_ref[g] = _roll_lanes(ug, -SSM_CH * g)

    c0 = SSM_WIDTH
    qk = xs_qk[:, c0:]
    sq = (qk * qk).astype(BF16)
    groups = range(0, 2 * NA_WIDTH, MXU_DIM)
    ms = jnp.dot(jnp.concatenate([sq[:, c:c + MXU_DIM] for c in groups], axis=0), hsum_ref[...],
                 preferred_element_type=F32)
    ms = jnp.concatenate([ms[i * MIX_TILE:(i + 1) * MIX_TILE, :] for i in range(len(groups))], axis=-1)
    qk = qk * lax.rsqrt(ms + RMS_EPS)
    q = qk[:, :NA_WIDTH] * qg_ref[...] * (NA_HEAD_DIM ** -0.5)
    k = qk[:, NA_WIDTH:] * kg_ref[...]
    v_uv = proj(c0 + 2 * NA_WIDTH, NA_WIDTH + 2 * GM_WIDTH)
    v = v_uv[:, :NA_WIDTH]
    for p in range(HEAD_PAIRS):
        lanes = slice(p * LANES, (p + 1) * LANES)
        q_ref[p] = q[:, lanes].astype(q_ref.dtype)
        k_ref[p] = k[:, lanes].astype(k_ref.dtype)
        v_ref[p] = v[:, lanes].astype(v_ref.dtype)
    if tok_refs:
        for slot in range(tok_refs[0].shape[1]):
            tok_refs[0][:, slot] = k.reshape(MIX_TILE // SEQ, SEQ, NA_WIDTH)
            tok_refs[1][:, slot] = v.reshape(MIX_TILE // SEQ, SEQ, NA_WIDTH)

    a = _gelu(v_uv[:, NA_WIDTH:])
    u = a[:, :GM_WIDTH]
    vv = a[:, GM_WIDTH:]
    mu = jnp.mean(vv, axis=-1, keepdims=True)
    vc = vv - mu
    var = jnp.mean(vc * vc, axis=-1, keepdims=True)
    vln = (vc * lax.rsqrt(var + LN_EPS)).astype(BF16)
    lane = lax.broadcasted_iota(jnp.int32, (1, GM_WIDTH), 1)
    gw = GM_WIDTH // GM_GROUPS
    for ci in range(MIX_TILE // GM_CHUNK):
        rows = slice(ci * GM_CHUNK, (ci + 1) * GM_CHUNK)
        vch = vln[rows, :]
        sp = gmb_ref[...]
        for gi in range(GM_GROUPS):
            t = jnp.dot(ws_ref[gi], vch, preferred_element_type=F32)
            sp = sp + jnp.where((lane >= gi * gw) & (lane < (gi + 1) * gw), t, 0.0)
        gm_ref[rows, :] = (u[rows, :] * sp).astype(BF16)


def _inproj(x, mod, g, w, hsum, qg, kg, ws, gmb, *, l, seg0, seg_tokens, is_ctx, caches=None):
    n_tok = x.shape[0]
    kv_dtype = BF16 if is_ctx else F32
    n_rows = MIX_TILE // SSM_CHUNK
    pair_spec = pl.BlockSpec((HEAD_PAIRS, MIX_TILE, LANES), lambda i: (0, i, 0))
    out_specs = [
        pl.BlockSpec((SSM_GROUPS, n_rows, SSM_ROW), lambda i: (0, i, 0)),
        pair_spec, pair_spec, pair_spec,
        pl.BlockSpec((MIX_TILE, GM_WIDTH), lambda i: (i, 0)),
    ]
    out_shape = [
        jax.ShapeDtypeStruct((SSM_GROUPS, n_tok // SSM_CHUNK, SSM_ROW), F32),
        jax.ShapeDtypeStruct((HEAD_PAIRS, n_tok, LANES), BF16),
        jax.ShapeDtypeStruct((HEAD_PAIRS, n_tok, LANES), kv_dtype),
        jax.ShapeDtypeStruct((HEAD_PAIRS, n_tok, LANES), kv_dtype),
        jax.ShapeDtypeStruct((n_tok, GM_WIDTH), BF16),
    ]
    args = [x, mod, g, w, hsum, qg, kg, ws, gmb]
    in_specs = [
        pl.BlockSpec((MIX_TILE, D_MODEL), lambda i: (i, 0)),
        _mod_spec(l, seg0, seg_tokens, MIX_TILE),
        _layer_spec(g, l), _layer_spec(w, l),
        _const_spec((MXU_DIM, MXU_DIM)),
        _layer_spec(qg, l), _layer_spec(kg, l), _layer_spec(ws, l), _layer_spec(gmb, l),
    ]
    aliases = {}
    if is_ctx:
        bt = MIX_TILE // SEQ
        cache_shape = jax.ShapeDtypeStruct((n_tok // SEQ, DEPTH, SEQ, NA_WIDTH), F32)
        if caches is None:
            out_specs += [pl.BlockSpec((bt, DEPTH, SEQ, NA_WIDTH), lambda i: (i, 0, 0, 0))] * 2
        else:
            out_specs += [pl.BlockSpec((bt, 1, SEQ, NA_WIDTH), lambda i: (i, l, 0, 0))] * 2
        out_shape += [cache_shape] * 2
        if caches is not None:
            aliases = {len(args): len(out_shape) - 2, len(args) + 1: len(out_shape) - 1}
            args += list(caches)
            in_specs += [pl.BlockSpec(memory_space=pl.ANY)] * 2
    return pl.pallas_call(
        functools.partial(_inproj_kernel, n_in=len(args), n_out=len(out_shape)),
        grid=(n_tok // MIX_TILE,),
        in_specs=in_specs,
        out_specs=out_specs,
        out_shape=out_shape,
        input_output_aliases=aliases,
        scratch_shapes=[pltpu.VMEM((MIX_TILE, D_MODEL), BF16), pltpu.VMEM((SSM_WIDTH // LANES, MIX_TILE, LANES), F32),
                        pltpu.VMEM((SSM_CHUNK, n_rows, SSM_ROW), F32)],
        compiler_params=_cparams(("arbitrary",)),
        name="inproj_ctx" if is_ctx else "inproj_lat",
    )(*args)


def _roll_rows(x, d, down):
    return pltpu.roll(x, d if down else SUBLANES - d, axis=0)


def _scan_tile(w, wp, c, cp, cst, row, down):
    x, xp = w, wp
    for si, d in enumerate(SCAN_SHIFTS):
        a, b, bp = cst[3 * si], cst[3 * si + 1], cst[3 * si + 2]
        xs, xps = _roll_rows(x, d, down), _roll_rows(xp, d, down)
        x, xp = x + a * xs + b * xps, xp + a * xps + bp * xs
    a, b, bp = cst[N_SCAN_CONST - 3], cst[N_SCAN_CONST - 2], cst[N_SCAN_CONST - 1]
    s = x + a * c + b * cp
    sp = xp + a * cp + bp * c
    edge = 0 if down else SUBLANES - 1
    last = SUBLANES - 1 if down else 0
    seen = jnp.where(row == edge, c, _roll_rows(s, 1, down))
    c_new = jnp.broadcast_to(s[last:last + 1, :], s.shape)
    cp_new = jnp.broadcast_to(sp[last:last + 1, :], sp.shape)
    return seen, c_new, cp_new


def _ssm_kernel(u_ref, mp_ref, q_ref, cst_ref, dvec_ref, s0_ref, y_ref, fin_ref, w_ref, sq_ref, *, n_batch, n_chunks):
    gb = u_ref.shape[0]
    n_rows = n_batch * n_chunks
    row_blk = min(n_rows, SSM_ROW_BLOCK)
    for g in range(gb):
        for r0 in range(0, n_rows, row_blk):
            rows = slice(r0, r0 + row_blk)
            ug = u_ref[g, rows, :]
            wy = jnp.dot(ug.astype(BF16), mp_ref[g], preferred_element_type=F32)
            y_ref[g, rows, :] = wy[:, :SSM_ROW] + ug * dvec_ref[g]
            w_ref[g, rows, :] = wy[:, SSM_ROW:]

    n_tiles = n_chunks // SUBLANES
    row = lax.broadcasted_iota(jnp.int32, (SUBLANES, LANES), 0)

    def tile_step(b, kt, carry):
        out = []
        for g in range(gb):
            cf, cfp, cb, cbp = carry[4 * g:4 * g + 4]
            rf = pl.multiple_of(b * n_chunks + kt * SUBLANES, SUBLANES)
            rb = pl.multiple_of(b * n_chunks + (n_tiles - 1 - kt) * SUBLANES, SUBLANES)
            wf = w_ref[g, pl.ds(rf, SUBLANES), :]
            wb = w_ref[g, pl.ds(rb, SUBLANES), :]
            cst_f = [cst_ref[g, 0, i] for i in range(N_SCAN_CONST)]
            cst_b = [cst_ref[g, 1, i] for i in range(N_SCAN_CONST)]
            seen_f, cf, cfp = _scan_tile(wf[:, 0:LANES], wf[:, LANES:2 * LANES], cf, cfp, cst_f, row, True)
            seen_b, cb, cbp = _scan_tile(wb[:, 2 * LANES:3 * LANES], wb[:, 3 * LANES:], cb, cbp, cst_b, row, False)
            sq_ref[g, pl.ds(rf, SUBLANES), 0:LANES] = seen_f
            sq_ref[g, pl.ds(rb, SUBLANES), LANES:2 * LANES] = seen_b
            out += [cf, cfp, cb, cbp]
        return tuple(out)

    def batch_body(b, _):
        carry = []
        for g in range(gb):
            for i in range(4):
                carry.append(jnp.broadcast_to(s0_ref[g, i, pl.ds(b, 1), :], (SUBLANES, LANES)))
        carry = tuple(carry)
        if n_tiles <= 2:
            for kt in range(n_tiles):
                carry = tile_step(b, kt, carry)
        else:
            carry = lax.fori_loop(0, n_tiles, lambda kt, c: tile_step(b, kt, c), carry)
        for g in range(gb):
            fin_ref[g, 0, pl.ds(b, 1), :] = carry[4 * g][0:1, :]
            fin_ref[g, 1, pl.ds(b, 1), :] = carry[4 * g + 2][0:1, :]
        return 0

    lax.fori_loop(0, n_batch, batch_body, 0)

    for g in range(gb):
        for r0 in range(0, n_rows, row_blk):
            rows = slice(r0, r0 + row_blk)
            y_ref[g, rows, :] += jnp.dot(sq_ref[g, rows, :].astype(BF16), q_ref[g], preferred_element_type=F32)


def _ssm_core(u, mp, q, cst, dvec, s0, *, l, n_batch, n_chunks):
    n_rows = n_batch * n_chunks
    gb = SSM_GROUP_BLOCK
    g3 = lambda i: (i, 0, 0)
    g4 = lambda i: (i, 0, 0, 0)
    return pl.pallas_call(
        functools.partial(_ssm_kernel, n_batch=n_batch, n_chunks=n_chunks),
        grid=(SSM_GROUPS // gb,),
        in_specs=[
            pl.BlockSpec((gb, n_rows, SSM_ROW), g3),
            pl.BlockSpec((None, gb, SSM_ROW, SSM_ROW + SSM_W_COLS), lambda i: (l, i, 0, 0)),
            pl.BlockSpec((None, gb, 2 * LANES, SSM_ROW), lambda i: (l, i, 0, 0)),
            pl.BlockSpec((None, gb, 2, N_SCAN_CONST, SUBLANES, LANES), lambda i: (l, i, 0, 0, 0, 0)),
            pl.BlockSpec((None, gb, 1, SSM_ROW), lambda i: (l, i, 0, 0)),
            pl.BlockSpec((gb, 4, n_batch, LANES), g4),
        ],
        out_specs=[
            pl.BlockSpec((gb, n_rows, SSM_ROW), g3),
            pl.BlockSpec((gb, 2, n_batch, LANES), g4),
        ],
        out_shape=[
            jax.ShapeDtypeStruct((SSM_GROUPS, n_rows, SSM_ROW), F32),
            jax.ShapeDtypeStruct((SSM_GROUPS, 2, n_batch, LANES), F32),
        ],
        scratch_shapes=[pltpu.VMEM((gb, n_rows, SSM_W_COLS), F32), pltpu.VMEM((gb, n_rows, 2 * LANES), F32)],
        compiler_params=_cparams(("arbitrary",)),
        name="ssm_core",
    )(u, mp, q, cst, dvec, s0)


def _cmul(a, b):
    return a[0] * b[0] - a[1] * b[1], a[0] * b[1] + a[1] * b[0]


_CHUNK_IDX = np.arange(SSM_CHUNK)


def _ssm_exponents():
    t = SSM_CHUNK
    idx = _CHUNK_IDX.astype(np.float32)
    both = lambda a: np.stack([a, a], axis=1)
    lane_step = idx - (t - 1) / 2
    tile_rows = np.arange(1, SUBLANES + 1, dtype=np.float32) * t
    tables = dict(
        one=both(np.ones(1, np.float32)),
        incr=np.stack([t - 1 - idx, idx], axis=1),
        seen=np.stack([idx + 1, t - idx], axis=1),
        resp_in=np.stack([-lane_step, lane_step], axis=1),
        resp_out=np.stack([lane_step, -lane_step], axis=1),
        shift=both(np.asarray(SCAN_SHIFTS, np.float32) * t),
        tile=np.stack([tile_rows, tile_rows[::-1]], axis=1),
    )
    spans, start = {}, 0
    for name, tab in tables.items():
        spans[name] = slice(start, start + len(tab))
        start += len(tab)
    return np.concatenate(list(tables.values()), axis=0), spans


def _ssm_params(p):
    t = SSM_CHUNK
    n_l = p['ssm_lambda_re'].shape[0]
    lam = (p['ssm_lambda_re'].astype(F32), p['ssm_lambda_im'].astype(F32))
    dt = jnp.exp(p['ssm_log_dt'].astype(F32))[..., None]
    arg = (lam[0] * dt, lam[1] * dt)
    expo, span = _ssm_exponents()
    n = jnp.asarray(expo)[:, None, :, None, None]
    mag = jnp.exp(n * arg[0][None])
    pw = (mag * jnp.cos(n * arg[1][None]), mag * jnp.sin(n * arg[1][None]))
    take = lambda name: [x[span[name]] for x in pw]
    lanes_last = lambda xs: [jnp.moveaxis(x, 0, -1) for x in xs]

    lbar = [x[0] for x in take('one')]
    num = (lbar[0] - 1.0, lbar[1])
    den = lam[0] * lam[0] + lam[1] * lam[1]
    zoh = ((num[0] * lam[0] + num[1] * lam[1]) / den, (num[1] * lam[0] - num[0] * lam[1]) / den)
    b_mat = (p['ssm_b_re'].astype(F32), p['ssm_b_im'].astype(F32))
    bbar = _cmul((zoh[0][..., None], zoh[1][..., None]), b_mat)
    c_mat = [jnp.swapaxes(p[k].astype(F32), -1, -2) for k in ('ssm_c_re', 'ssm_c_im')]
    ct = [jnp.tile(x, (1, 1, 1, 1, t)) for x in c_mat]
    bt = [jnp.tile(x, (1, 1, 1, 1, t)) for x in bbar]

    per_lane = lambda name: [jnp.repeat(x, SSM_CH, axis=-1) for x in lanes_last(take(name))]
    xm = _cmul(per_lane('resp_in'), bt)
    ym = _cmul(per_lane('resp_out'), ct)
    resp = jnp.einsum('ldgpx,ldgpy->ldgxy', jnp.concatenate([xm[0], -xm[1]], axis=3),
                      jnp.concatenate([ym[0], ym[1]], axis=3), precision='highest')
    steps_x = np.repeat(_CHUNK_IDX, SSM_CH)
    causal = np.stack([steps_x[None, :] >= steps_x[:, None], steps_x[:, None] >= steps_x[None, :]])
    mats = jnp.sum(jnp.where(jnp.asarray(causal)[None, :, None], resp, 0.0), axis=1)

    incr = _cmul([x[..., None] for x in take('incr')], [x[None] for x in bbar])
    incr = [jnp.transpose(x, (1, 2, 3, 0, 5, 4)) for x in incr]
    pcat = jnp.concatenate([incr[0], incr[1], incr[1], incr[0]], axis=-1)
    pcat = jnp.concatenate([pcat[:, 0], pcat[:, 1]], axis=-1).reshape(n_l, SSM_GROUPS, SSM_ROW, SSM_W_COLS)
    mp = jnp.concatenate([mats, pcat], axis=-1).astype(BF16)

    qm = _cmul(ct, per_lane('seen'))
    q = jnp.concatenate([qm[0][:, 0], -qm[1][:, 0], qm[0][:, 1], -qm[1][:, 1]], axis=2).astype(BF16)

    def forms(name, rows):
        pr, pi = [jnp.moveaxis(x, 0, 3) for x in take(name)]
        f = jnp.stack([jnp.concatenate([pr, pr], -1), jnp.concatenate([-pi, pi], -1),
                       jnp.concatenate([pi, -pi], -1)], axis=4)
        if rows:
            return f
        r = np.arange(SUBLANES)[None, :]
        d = np.repeat(np.asarray(SCAN_SHIFTS), 3)[:, None]
        keep = np.stack([r >= d, r < SUBLANES - d]).astype(np.float32)
        return f.reshape(n_l, 2, SSM_GROUPS, 3 * len(SCAN_SHIFTS), 1, LANES) * keep[None, :, None, :, :, None]
    per_row = jnp.swapaxes(forms('tile', True), 3, 4)
    cst = jnp.swapaxes(jnp.concatenate([forms('shift', False), per_row], axis=3), 1, 2)
    dvec = jnp.tile(p['ssm_d'].astype(F32).reshape(n_l, SSM_GROUPS, 1, SSM_CH), (1, 1, 1, SSM_CHUNK))
    return mp, q, cst, dvec


def _ssm_states_in(state_ssm):
    st = jnp.transpose(state_ssm.astype(F32), (1, 3, 2, 0, 4, 5))
    re, im = st[..., 0], st[..., 1]
    ri, ir = jnp.concatenate([re, im], -1), jnp.concatenate([im, re], -1)
    return jnp.stack([ri[:, :, 0], ir[:, :, 0], ri[:, :, 1], ir[:, :, 1]], axis=2)


def _ssm_states_out(fins):
    fin = jnp.stack(fins, axis=0)
    fin = jnp.stack([fin[..., :SSM_STATE], fin[..., SSM_STATE:]], axis=-1)
    return jnp.transpose(fin, (3, 0, 2, 1, 4, 5))


def _head_lane_mask(hh):
    lane = lax.broadcasted_iota(jnp.int32, (1, LANES), 1)
    return (lane >= hh * NA_HEAD_DIM) & (lane < (hh + 1) * NA_HEAD_DIM)


_NT_DIMS = (((1,), (1,)), ((), ()))


def _ctx_attn_kernel(q_ref, k_ref, v_ref, o_ref):
    lm0 = _head_lane_mask(0)
    for b in range(CTX_ATTN_BATCH):
        rows = slice(b * SEQ, (b + 1) * SEQ)
        for p in range(HEAD_PAIRS):
            qp, kp, vp = q_ref[p, rows, :], k_ref[p, rows, :], v_ref[p, rows, :]
            zero = jnp.zeros_like(qp)
            q2 = jnp.concatenate([jnp.where(lm0, qp, zero), jnp.where(lm0, zero, qp)], axis=0)
            s = lax.dot_general(q2, kp, _NT_DIMS, preferred_element_type=F32)
            e = jnp.exp(s - jnp.max(s, axis=-1, keepdims=True))
            o = jnp.dot(e.astype(BF16), vp, preferred_element_type=F32) / jnp.sum(e, axis=-1, keepdims=True)
            o_ref[rows, p * LANES:(p + 1) * LANES] = jnp.where(lm0, o[:SEQ, :], o[SEQ:, :]).astype(BF16)


def _ctx_attn(q, k, v, n_batch):
    rows = CTX_ATTN_BATCH * SEQ
    spec = pl.BlockSpec((HEAD_PAIRS, rows, LANES), lambda b: (0, b, 0))
    return pl.pallas_call(
        _ctx_attn_kernel,
        grid=(n_batch // CTX_ATTN_BATCH,),
        in_specs=[spec, spec, spec],
        out_specs=pl.BlockSpec((rows, NA_WIDTH), lambda b: (b, 0)),
        out_shape=jax.ShapeDtypeStruct((n_batch * SEQ, NA_WIDTH), BF16),
        compiler_params=_cparams(("arbitrary",)),
        name="ctx_attn",
    )(q, k, v)


def _na_kernel(q_ref, k_ref, v_ref, kc_ref, vc_ref, t2_ref, rm_ref, o_ref, qb_ref, os_ref, tb_ref):
    rt = pl.program_id(1)
    n_rt = pl.num_programs(1)

    @pl.when((pl.program_id(0) == 0) & (rt == 0))
    def _():
        for h in range(NA_HEADS):
            for n in range(NA_COL_BLOCKS):
                for rl in range(NA_Q_ROWS):
                    lane0 = (NA_OFF_START - rl) * NA_KCOLS
                    tb_ref[h, n, rl * NA_COLS:(rl + 1) * NA_COLS, :] = (
                        t2_ref[h, n, :, lane0:lane0 + NA_K_ROWS * NA_KCOLS])

    r0 = rt * NA_Q_ROWS
    ttype = jnp.where(rt == 0, 0, jnp.where(rt == n_rt - 1, 2, 1))
    rmask = rm_ref[ttype]
    blk_q = NA_Q_ROWS * NA_COLS

    def pair_body(p, carry):
        lm0 = _head_lane_mask(0)
        for n in range(NA_COL_BLOCKS):
            for rl in range(NA_Q_ROWS):
                qrow = q_ref[p, rl * GRID_W + n * NA_COLS:rl * GRID_W + (n + 1) * NA_COLS, :]
                zero = jnp.zeros_like(qrow)
                base = 2 * n * blk_q + rl * NA_COLS
                qb_ref[base:base + NA_COLS, :] = jnp.where(lm0, qrow, zero)
                qb_ref[base + blk_q:base + blk_q + NA_COLS, :] = jnp.where(lm0, zero, qrow)
        kcp, vcp = kc_ref[0, p], vc_ref[0, p]
        n_win = NA_K_ROWS * NA_KCOLS
        for n in range(NA_COL_BLOCKS):
            kparts, vparts = [], []
            for i in range(NA_K_ROWS):
                kr = jnp.clip(r0 - (NA_K_ROWS - NA_Q_ROWS) // 2 + i, 0, GRID_W - 1)
                start = pl.multiple_of(kr * GRID_W + NA_KC0[n], 8)
                kparts.append(k_ref[p, pl.ds(start, NA_KCOLS), :])
                vparts.append(v_ref[p, pl.ds(start, NA_KCOLS), :])
            kall = jnp.concatenate([jnp.concatenate(kparts, axis=0).astype(BF16), kcp], axis=0)
            vall = jnp.concatenate([jnp.concatenate(vparts, axis=0).astype(BF16), vcp], axis=0)
            rows2 = slice(2 * n * blk_q, 2 * (n + 1) * blk_q)
            s = lax.dot_general(qb_ref[rows2, :], kall, _NT_DIMS, preferred_element_type=F32)
            bias = jnp.concatenate([tb_ref[2 * p, n] + rmask, tb_ref[2 * p + 1, n] + rmask], axis=0)
            s = jnp.concatenate([s[:, :n_win] + bias, s[:, n_win:]], axis=-1)
            e = jnp.exp(s - jnp.max(s, axis=-1, keepdims=True))
            o = jnp.dot(e.astype(BF16), vall, preferred_element_type=F32) / jnp.sum(e, axis=-1, keepdims=True)
            oblk = jnp.where(lm0, o[:blk_q, :], o[blk_q:, :])
            for rl in range(NA_Q_ROWS):
                os_ref[p, rl * GRID_W + n * NA_COLS:rl * GRID_W + (n + 1) * NA_COLS, :] = (
                    oblk[rl * NA_COLS:(rl + 1) * NA_COLS, :])
        return carry

    lax.fori_loop(0, HEAD_PAIRS, pair_body, 0, unroll=2)
    for p in range(HEAD_PAIRS):
        o_ref[:, p * LANES:(p + 1) * LANES] = os_ref[p].astype(BF16)


def _na_attn(q, k, v, kc, vc, tb, rm, n_batch, *, l):
    tile_tok = NA_Q_ROWS * GRID_W
    n_rt = DEC_SEQ // tile_tok
    blk_q, blk_k = NA_Q_ROWS * NA_COLS, NA_K_ROWS * NA_KCOLS
    return pl.pallas_call(
        _na_kernel,
        grid=(n_batch, n_rt),
        in_specs=[
            pl.BlockSpec((HEAD_PAIRS, tile_tok, LANES), lambda b, r: (0, b * n_rt + r, 0)),
            pl.BlockSpec((HEAD_PAIRS, DEC_SEQ, LANES), lambda b, r: (0, b, 0)),
            pl.BlockSpec((HEAD_PAIRS, DEC_SEQ, LANES), lambda b, r: (0, b, 0)),
            pl.BlockSpec((1, None, HEAD_PAIRS, PAST_LEN, LANES), lambda b, r: (b, l, 0, 0, 0)),
            pl.BlockSpec((1, None, HEAD_PAIRS, PAST_LEN, LANES), lambda b, r: (b, l, 0, 0, 0)),
            _layer_spec(tb, l),
            _const_spec(rm.shape),
        ],
        out_specs=pl.BlockSpec((tile_tok, NA_WIDTH), lambda b, r: (b * n_rt + r, 0)),
        out_shape=jax.ShapeDtypeStruct((n_batch * DEC_SEQ, NA_WIDTH), BF16),
        scratch_shapes=[
            pltpu.VMEM((2 * tile_tok, LANES), BF16),
            pltpu.VMEM((HEAD_PAIRS, tile_tok, LANES), F32),
            pltpu.VMEM((NA_HEADS, NA_COL_BLOCKS, blk_q, blk_k), F32)],
        compiler_params=_cparams(("arbitrary", "arbitrary")),
        name="na_attn",
    )(q, k, v, kc, vc, tb, rm)


def _na_tables(rpb):
    n = np.arange(NA_COL_BLOCKS)
    cc = np.arange(NA_COLS)
    kk = np.arange(NA_KCOLS)
    rl = np.arange(NA_Q_ROWS)
    ki = np.arange(NA_K_ROWS)
    n_dr, n_dc = 2 * NA_MAX_ROWS - 1, 2 * NA_COLS - 1
    c = n[:, None] * NA_COLS + cc[None, :]
    kcol = np.asarray(NA_KC0)[:, None] + kk[None, :]
    dc = np.clip(kcol[:, None, :] - c[:, :, None], -(NA_COLS - 1), NA_COLS - 1) + (NA_COLS - 1)
    cs = np.clip(c - NA_COLS // 2, 0, GRID_W - NA_COLS)
    col_ok = (kcol[:, None, :] >= cs[:, :, None]) & (kcol[:, None, :] < cs[:, :, None] + NA_COLS)
    half = (NA_K_ROWS - NA_Q_ROWS) // 2
    dr = np.clip(ki[None, :] - half - rl[:, None] + (NA_MAX_ROWS - 1), 0, n_dr - 1)
    oh_dc = jnp.asarray(dc[None] == np.arange(n_dc)[:, None, None, None], F32)
    off = np.clip(np.arange(NA_OFF_ROWS) - NA_OFF_PAD, 0, n_dr - 1)
    assert np.array_equal(dr, off[ki[None] + (NA_OFF_START - rl)[:, None]])
    oh_off = jnp.asarray(off[None, :] == np.arange(n_dr)[:, None], F32)
    rows_p = jnp.einsum('lhab,aA->lhAb', rpb.astype(F32), oh_off, precision='highest')
    cols = jnp.einsum('lhAb,bnck->lhncAk', rows_p, oh_dc, precision='highest')
    cols = jnp.where(jnp.asarray(col_ok)[None, None, :, :, None, :], cols, NEG_INF)
    tb = cols.reshape(-1, NA_HEADS, NA_COL_BLOCKS, NA_COLS, NA_OFF_ROWS * NA_KCOLS)
    rows = GRID_W
    masks = []
    for r0 in (0, NA_Q_ROWS, rows - NA_Q_ROWS):
        r = r0 + rl
        rs = np.clip(r - NA_MAX_ROWS // 2, 0, rows - NA_MAX_ROWS)
        kr = r0 - half + ki
        ok = (kr[None, :] >= rs[:, None]) & (kr[None, :] < rs[:, None] + NA_MAX_ROWS)
        full = np.broadcast_to(ok[:, None, :, None], (NA_Q_ROWS, NA_COLS, NA_K_ROWS, NA_KCOLS))
        masks.append(np.where(full, 0.0, NEG_INF).reshape(NA_Q_ROWS * NA_COLS, NA_K_ROWS * NA_KCOLS))
    rm = jnp.asarray(np.stack(masks), F32)
    return tb, rm


def _outffn_kernel(x_ref, mod_ref, y_ref, na_ref, gm_ref, gw_ref, gb_ref, wo_ref, g_ref, win_ref, wout_ref, o_ref,
                   yr_ref, ys_ref, cat_ref, xm_ref, h_ref, act_ref):
    gate = mod_ref[0, 5:6, :]
    n_rows = OUT_TILE // SSM_CHUNK
    for g in range(SSM_GROUPS):
        yr_ref[g] = _roll_lanes(y_ref[g], SSM_CH * g)
    for t in range(SSM_CHUNK):
        yt = _merge_segments(lambda src, lt: yr_ref[src, :, lt * LANES:(lt + 1) * LANES], t)
        yt = _roll_lanes(yt, -SSM_CH * t)
        for j in range(SSM_WIDTH // LANES):
            ys_ref[j, pl.ds(t, n_rows, stride=SSM_CHUNK), :] = yt[:, j * LANES:(j + 1) * LANES]
    y = _gelu(jnp.concatenate([ys_ref[j] for j in range(SSM_WIDTH // LANES)], axis=-1))
    z = jnp.dot(y.astype(BF16), gw_ref[...], preferred_element_type=F32) + gb_ref[...]
    cat_ref[:, 0:SSM_WIDTH] = (y * _sigmoid(z)).astype(BF16)
    cat_ref[:, SSM_WIDTH:SSM_WIDTH + NA_WIDTH] = na_ref[...]
    cat_ref[:, SSM_WIDTH + NA_WIDTH:] = gm_ref[...]
    mix = jnp.dot(cat_ref[...], wo_ref[...], preferred_element_type=F32)
    xm_ref[...] = x_ref[...] + gate * mix
    _ffn_kernel(xm_ref, mod_ref, g_ref, win_ref, wout_ref, o_ref, h_ref, act_ref, mod_base=6)


def _outffn(x, mod, y, na, gm, gw, gb, wo, g, w_in, w_out, *, l, seg0, seg_tokens):
    n_tok = x.shape[0]
    n_rows = OUT_TILE // SSM_CHUNK
    tok = lambda width: pl.BlockSpec((OUT_TILE, width), lambda i: (i, 0))
    return pl.pallas_call(
        _outffn_kernel,
        grid=(n_tok // OUT_TILE,),
        in_specs=[
            tok(D_MODEL),
            _mod_spec(l, seg0, seg_tokens, OUT_TILE),
            pl.BlockSpec((SSM_GROUPS, n_rows, SSM_ROW), lambda i: (0, i, 0)),
            tok(NA_WIDTH), tok(GM_WIDTH),
            _layer_spec(gw, l), _layer_spec(gb, l), _layer_spec(wo, l),
            _layer_spec(g, l), _layer_spec(w_in, l), _layer_spec(w_out, l),
        ],
        out_specs=tok(D_MODEL),
        out_shape=jax.ShapeDtypeStruct((n_tok, D_MODEL), F32),
        scratch_shapes=[pltpu.VMEM((SSM_GROUPS, n_rows, SSM_ROW), F32),
                        pltpu.VMEM((SSM_WIDTH // LANES, OUT_TILE, LANES), F32),
                        pltpu.VMEM((OUT_TILE, D_MODEL), BF16),
                        pltpu.VMEM((OUT_TILE, D_MODEL), F32),
                        pltpu.VMEM((OUT_TILE, D_MODEL), BF16), pltpu.VMEM((OUT_TILE, D_FF), BF16)],
        compiler_params=_cparams(("arbitrary",)),
        name="outffn",
    )(x, mod, y, na, gm, gw, gb, wo, g, w_in, w_out)


def _prep(p):
    heads = np.arange(MXU_DIM) // NA_HEAD_DIM
    hsum = jnp.asarray((heads[:, None] == heads[None, :]) / NA_HEAD_DIM, BF16)
    gmb = jnp.repeat(jnp.swapaxes(p['gm_bs'], 1, 2).astype(F32), GM_WIDTH // GM_GROUPS, axis=2)
    tb, rm = _na_tables(p['na_rpb'])
    row = lambda a: a.astype(F32).reshape(DEPTH, 1, -1)
    return dict(
        g1=row(p['norm_ffn1']), g2=row(p['norm_mix']), g3=row(p['norm_ffn2']),
        f1_in=p['ffn1_w_in'].astype(BF16), f1_out=p['ffn1_w_out'].astype(BF16),
        f2_in=p['ffn2_w_in'].astype(BF16), f2_out=p['ffn2_w_out'].astype(BF16),
        w_in=p['w_in'].astype(BF16), w_out=p['w_out'].astype(BF16),
        hsum=hsum,
        qg=row(jnp.tile(p['na_q_norm'], (1, NA_HEADS))), kg=row(jnp.tile(p['na_k_norm'], (1, NA_HEADS))),
        ws=p['gm_ws'].astype(BF16), gmb=gmb,
        ssm=_ssm_params(p),
        glu_w=p['ssm_glu_w'].astype(BF16), glu_b=row(p['ssm_glu_b']),
        tb=tb, rm=rm,
    )


def _pairs_major(t):
    b, n_l, s = t.shape[:3]
    return jnp.transpose(t.reshape(b, n_l, s, HEAD_PAIRS, LANES), (0, 1, 3, 2, 4)).astype(BF16)


def _trunk_layer(x, mod, w, l, *, n_batch, seq_len, caches=None, is_ctx=False, ctx_kv=None, ssm_init=None):
    seg = dict(l=l, seg0=0 if is_ctx else 1, seg_tokens=x.shape[0] if is_ctx else seq_len)
    x = _ffn(x, mod, w['g1'], w['f1_in'], w['f1_out'], mod_base=0, **seg)
    outs = _inproj(x, mod, w['g2'], w['w_in'], w['hsum'], w['qg'], w['kg'], w['ws'], w['gmb'],
                   is_ctx=is_ctx, caches=caches, **seg)
    u, q, k, v, gm = outs[:5]
    s0 = jnp.zeros((SSM_GROUPS, 4, n_batch, LANES), F32) if is_ctx else ssm_init[l]
    y, fin = _ssm_core(u, *w['ssm'], s0, l=l, n_batch=n_batch, n_chunks=seq_len // SSM_CHUNK)
    if is_ctx:
        na = _ctx_attn(q, k, v, n_batch)
    else:
        na = _na_attn(q, k, v, ctx_kv[0], ctx_kv[1], w['tb'], w['rm'], n_batch, l=l)
    x = _outffn(x, mod, y, na, gm, w['glu_w'], w['glu_b'], w['w_out'], w['g3'], w['f2_in'], w['f2_out'], **seg)
    return x, (tuple(outs[5:]), fin)


def kernel(x_prompt, x_sample, c, cache_k, cache_v, state_ssm, c_ctx, w_ada, b_ada, norm_ffn1, ffn1_w_in, ffn1_w_out, norm_mix, w_in, w_out, ssm_lambda_re, ssm_lambda_im, ssm_log_dt, ssm_b_re, ssm_b_im, ssm_c_re, ssm_c_im, ssm_d, ssm_glu_w, ssm_glu_b, na_q_norm, na_k_norm, na_rpb, gm_ws, gm_bs, norm_ffn2, ffn2_w_in, ffn2_w_out):
    p = dict(norm_ffn1=norm_ffn1, ffn1_w_in=ffn1_w_in, ffn1_w_out=ffn1_w_out, norm_mix=norm_mix, w_in=w_in,
             w_out=w_out, ssm_lambda_re=ssm_lambda_re, ssm_lambda_im=ssm_lambda_im, ssm_log_dt=ssm_log_dt,
             ssm_b_re=ssm_b_re, ssm_b_im=ssm_b_im, ssm_c_re=ssm_c_re, ssm_c_im=ssm_c_im, ssm_d=ssm_d,
             ssm_glu_w=ssm_glu_w, ssm_glu_b=ssm_glu_b, na_q_norm=na_q_norm, na_k_norm=na_k_norm, na_rpb=na_rpb,
             gm_ws=gm_ws, gm_bs=gm_bs, norm_ffn2=norm_ffn2, ffn2_w_in=ffn2_w_in, ffn2_w_out=ffn2_w_out)
    batch, dec_batch = x_prompt.shape[0], x_sample.shape[0]
    cond8 = jnp.zeros((8, D_MODEL), F32).at[0].set(c_ctx).at[1:1 + dec_batch].set(c)
    mod = _adaln(cond8, w_ada, b_ada).reshape(DEPTH, 8, N_MOD, D_MODEL)

    xp = x_prompt.reshape(batch * SEQ, D_MODEL)
    xs = x_sample.reshape(dec_batch * DEC_SEQ, D_MODEL)
    w = _prep(p)
    ctx_kv = (_pairs_major(cache_k), _pairs_major(cache_v))
    ssm_init = _ssm_states_in(state_ssm)
    caches, fins = None, []
    for l in range(DEPTH):
        xp, (caches, fin) = _trunk_layer(xp, mod, w, l, n_batch=batch, seq_len=SEQ, caches=caches, is_ctx=True)
        fins.append(fin)
        xs, _ = _trunk_layer(xs, mod, w, l, n_batch=dec_batch, seq_len=DEC_SEQ, ctx_kv=ctx_kv, ssm_init=ssm_init)
    cache_shape = (batch, DEPTH, SEQ, NA_HEADS, NA_HEAD_DIM)
    return (xp.reshape(batch, SEQ, D_MODEL), xs.reshape(dec_batch, DEC_SEQ, D_MODEL),
            caches[0].reshape(cache_shape), caches[1].reshape(cache_shape), _ssm_states_out(fins))
```

```python
import functools

import numpy as np
import jax
import jax.numpy as jnp
from jax import lax
from jax.experimental import pallas as pl
from jax.experimental.pallas import tpu as pltpu

D_MODEL = 1024
DEPTH = 2
SEQ = 256
DEC_SEQ = 4096
PAST_LEN = 256
GRID_W = 64
SSM_WIDTH = 256
SSM_CH = 16
SSM_GROUPS = 16
SSM_STATE = 64
NA_WIDTH = 512
NA_HEAD_DIM = 64
NA_HEADS = 8
NA_MAX_ROWS = 8
NA_COLS = 16
GM_WIDTH = 256
GM_GROUPS = 4
GM_CHUNK = 128
D_FF = 2816
N_MOD = 9
RMS_EPS = 1e-6
LN_EPS = 1e-5
NEG_INF = -1e30

F32 = jnp.float32
BF16 = jnp.bfloat16

LANES = 128
SUBLANES = 8
MXU_DIM = 256
VMEM_LIMIT_BYTES = 56 * 1024 * 1024

FFN_TILE = 1024
MIX_TILE = 1024
OUT_TILE = 512
FF_CHUNK = MXU_DIM
N_FF_CHUNKS = D_FF // FF_CHUNK
SSM_CHUNK = 16
SSM_ROW = SSM_CHUNK * SSM_CH
SSM_GROUP_BLOCK = 4
SSM_ROW_BLOCK = 512
SSM_W_COLS = 4 * LANES
SCAN_SHIFTS = (1, 2, 4)
N_SCAN_CONST = 3 * len(SCAN_SHIFTS) + 3
HEAD_PAIRS = NA_HEADS // 2
CTX_ATTN_BATCH = 4
NA_Q_ROWS = 8
NA_K_ROWS = 16
NA_KCOLS = 2 * NA_COLS
NA_COL_BLOCKS = GRID_W // NA_COLS
NA_KC0 = (0, 8, 24, 32)
SEGS_PER_TILE = LANES // SSM_CH
NA_OFF_PAD = (NA_Q_ROWS - 1) - ((NA_MAX_ROWS - 1) - (NA_K_ROWS - NA_Q_ROWS) // 2)
NA_OFF_START = NA_Q_ROWS - 1
NA_OFF_ROWS = NA_OFF_START + NA_K_ROWS


def _silu(x):
    return x * (1.0 / (1.0 + jnp.exp(-x)))


def _sigmoid(x):
    return 1.0 / (1.0 + jnp.exp(-x))


def _gelu(x):
    return 0.5 * x * (1.0 + jnp.tanh(0.7978845608028654 * (x + 0.044715 * (x * x * x))))


def _cparams(sem):
    return pltpu.CompilerParams(dimension_semantics=sem, vmem_limit_bytes=VMEM_LIMIT_BYTES)


def _const_spec(shape):
    nd = len(shape)
    return pl.BlockSpec(shape, lambda *_: (0,) * nd, pipeline_mode=pl.Buffered(1))


def _layer_spec(stacked, l):
    nd = stacked.ndim
    return pl.BlockSpec((None,) + stacked.shape[1:], lambda *_: (l,) + (0,) * (nd - 1), pipeline_mode=pl.Buffered(1))


def _ada_kernel(c_ref, w_ref, b_ref, o_ref):
    s = _silu(c_ref[...]).astype(BF16)
    w = w_ref[0].astype(BF16)
    o_ref[0] = jnp.dot(s, w, preferred_element_type=F32) + b_ref[0]


def _adaln(cond8, w_ada, b_ada):
    tn = D_MODEL
    ncol = N_MOD * D_MODEL
    return pl.pallas_call(
        _ada_kernel,
        grid=(DEPTH, ncol // tn),
        in_specs=[
            pl.BlockSpec((8, D_MODEL), lambda l, j: (0, 0)),
            pl.BlockSpec((1, D_MODEL, tn), lambda l, j: (l, 0, j)),
            pl.BlockSpec((1, 1, tn), lambda l, j: (l, 0, j)),
        ],
        out_specs=pl.BlockSpec((1, 8, tn), lambda l, j: (l, 0, j)),
        out_shape=jax.ShapeDtypeStruct((DEPTH, 8, ncol), F32),
        compiler_params=_cparams(("arbitrary", "arbitrary")),
        name="adaln",
    )(cond8, w_ada, b_ada.reshape(DEPTH, 1, ncol))


def _normed(x, g_row, shift, scale):
    ms = jnp.mean(x * x, axis=-1, keepdims=True)
    h = x * lax.rsqrt(ms + RMS_EPS) * g_row
    return h * (1.0 + scale) + shift


def _mod_spec(l, seg0, seg_tokens, tile):
    tiles_per_seg = seg_tokens // tile
    return pl.BlockSpec((None, 1, N_MOD, D_MODEL), lambda i: (l, seg0 + i // tiles_per_seg, 0, 0))


def _ffn_kernel(x_ref, mod_ref, g_ref, win_ref, wout_ref, o_ref, h_ref, act_ref, *, mod_base):
    shift = mod_ref[0, mod_base:mod_base + 1, :]
    scale = mod_ref[0, mod_base + 1:mod_base + 2, :]
    gate = mod_ref[0, mod_base + 2:mod_base + 3, :]
    h_ref[...] = _normed(x_ref[...], g_ref[...], shift, scale).astype(BF16)
    for j in range(N_FF_CHUNKS):
        cols = slice(j * FF_CHUNK, (j + 1) * FF_CHUNK)
        gate_j = jnp.dot(h_ref[...], win_ref[:, cols], preferred_element_type=F32)
        up_j = jnp.dot(h_ref[...], win_ref[:, D_FF + j * FF_CHUNK:D_FF + (j + 1) * FF_CHUNK], preferred_element_type=F32)
        act_ref[:, cols] = (_silu(gate_j) * up_j).astype(BF16)
    for n in range(D_MODEL // MXU_DIM):
        cols = slice(n * MXU_DIM, (n + 1) * MXU_DIM)
        y = jnp.dot(act_ref[...], wout_ref[:, cols], preferred_element_type=F32)
        o_ref[:, cols] = x_ref[:, cols] + 0.5 * gate[:, cols] * y


def _ffn(x, mod, g, w_in, w_out, *, l, mod_base, seg0, seg_tokens):
    n_tok = x.shape[0]
    return pl.pallas_call(
        functools.partial(_ffn_kernel, mod_base=mod_base),
        grid=(n_tok // FFN_TILE,),
        in_specs=[
            pl.BlockSpec((FFN_TILE, D_MODEL), lambda i: (i, 0)),
            _mod_spec(l, seg0, seg_tokens, FFN_TILE),
            _layer_spec(g, l), _layer_spec(w_in, l), _layer_spec(w_out, l),
        ],
        out_specs=pl.BlockSpec((FFN_TILE, D_MODEL), lambda i: (i, 0)),
        out_shape=jax.ShapeDtypeStruct((n_tok, D_MODEL), F32),
        scratch_shapes=[pltpu.VMEM((FFN_TILE, D_MODEL), BF16), pltpu.VMEM((FFN_TILE, D_FF), BF16)],
        compiler_params=_cparams(("arbitrary",)),
        name="ffn",
    )(x, mod, g, w_in, w_out)


def _segment_ids():
    return lax.broadcasted_iota(jnp.int32, (1, LANES), 1) // SSM_CH


def _merge_segments(load_piece, rot):
    seg = _segment_ids()
    tiles = []
    for lt in range(SSM_ROW // LANES):
        acc = None
        for jj in range(SEGS_PER_TILE):
            src = (lt * SEGS_PER_TILE + jj - rot) % SSM_CHUNK
            piece = load_piece(src, lt)
            acc = piece if acc is None else jnp.where(seg == jj, piece, acc)
        tiles.append(acc)
    return jnp.concatenate(tiles, axis=-1)


def _roll_lanes(x, shift):
    shift %= x.shape[-1]
    return pltpu.roll(x, shift, axis=x.ndim - 1) if shift else x


def _inproj_kernel(*refs, n_in, n_out):
    x_ref, mod_ref, g_ref, w_ref, hsum_ref, qg_ref, kg_ref, ws_ref, gmb_ref = refs[:9]
    u_ref, q_ref, k_ref, v_ref, gm_ref, *tok_refs = refs[n_in:n_in + n_out]
    h_ref, xs_ref, z_ref = refs[n_in + n_out:]
    shift = mod_ref[0, 3:4, :]
    scale = mod_ref[0, 4:5, :]
    h_ref[...] = _normed(x_ref[...], g_ref[...], shift, scale).astype(BF16)
    n_rows = MIX_TILE // SSM_CHUNK

    def proj(c0, width):
        return jnp.dot(h_ref[...], w_ref[:, c0:c0 + width], preferred_element_type=F32)

    xs_qk = proj(0, SSM_WIDTH + 2 * NA_WIDTH)
    for j in range(SSM_WIDTH // LANES):
        xs_ref[j] = xs_qk[:, j * LANES:(j + 1) * LANES]
    for t in range(SSM_CHUNK):
        zt = jnp.concatenate([xs_ref[j, pl.ds(t, n_rows, stride=SSM_CHUNK), :] for j in range(SSM_WIDTH // LANES)],
                             axis=-1)
        z_ref[t] = _roll_lanes(zt, SSM_CH * t)
    for g in range(SSM_GROUPS):
        ug = _merge_segments(lambda src, lt: z_ref[src, :, lt * LANES:(lt + 1) * LANES], g)
        u_ref[g] = _roll_lanes(ug, -SSM_CH * g)

    c0 = SSM_WIDTH
    qk = xs_qk[:, c0:]
    sq = (qk * qk).astype(BF16)
    groups = range(0, 2 * NA_WIDTH, MXU_DIM)
    ms = jnp.dot(jnp.concatenate([sq[:, c:c + MXU_DIM] for c in groups], axis=0), hsum_ref[...],
                 preferred_element_type=F32)
    ms = jnp.concatenate([ms[i * MIX_TILE:(i + 1) * MIX_TILE, :] for i in range(len(groups))], axis=-1)
    qk = qk * lax.rsqrt(ms + RMS_EPS)
    q = qk[:, :NA_WIDTH] * qg_ref[...] * (NA_HEAD_DIM ** -0.5)
    k = qk[:, NA_WIDTH:] * kg_ref[...]
    v_uv = proj(c0 + 2 * NA_WIDTH, NA_WIDTH + 2 * GM_WIDTH)
    v = v_uv[:, :NA_WIDTH]
    for p in range(HEAD_PAIRS):
        lanes = slice(p * LANES, (p + 1) * LANES)
        q_ref[p] = q[:, lanes].astype(q_ref.dtype)
        k_ref[p] = k[:, lanes].astype(k_ref.dtype)
        v_ref[p] = v[:, lanes].astype(v_ref.dtype)
    if tok_refs:
        for slot in range(tok_refs[0].shape[1]):
            tok_refs[0][:, slot] = k.reshape(MIX_TILE // SEQ, SEQ, NA_WIDTH)
            tok_refs[1][:, slot] = v.reshape(MIX_TILE // SEQ, SEQ, NA_WIDTH)

    a = _gelu(v_uv[:, NA_WIDTH:])
    u = a[:, :GM_WIDTH]
    vv = a[:, GM_WIDTH:]
    mu = jnp.mean(vv, axis=-1, keepdims=True)
    vc = vv - mu
    var = jnp.mean(vc * vc, axis=-1, keepdims=True)
    vln = (vc * lax.rsqrt(var + LN_EPS)).astype(BF16)
    lane = lax.broadcasted_iota(jnp.int32, (1, GM_WIDTH), 1)
    gw = GM_WIDTH // GM_GROUPS
    for ci in range(MIX_TILE // GM_CHUNK):
        rows = slice(ci * GM_CHUNK, (ci + 1) * GM_CHUNK)
        vch = vln[rows, :]
        sp = gmb_ref[...]
        for gi in range(GM_GROUPS):
            t = jnp.dot(ws_ref[gi], vch, preferred_element_type=F32)
            sp = sp + jnp.where((lane >= gi * gw) & (lane < (gi + 1) * gw), t, 0.0)
        gm_ref[rows, :] = (u[rows, :] * sp).astype(BF16)


def _inproj(x, mod, g, w, hsum, qg, kg, ws, gmb, *, l, seg0, seg_tokens, is_ctx, caches=None):
    n_tok = x.shape[0]
    kv_dtype = BF16 if is_ctx else F32
    n_rows = MIX_TILE // SSM_CHUNK
    pair_spec = pl.BlockSpec((HEAD_PAIRS, MIX_TILE, LANES), lambda i: (0, i, 0))
    out_specs = [
        pl.BlockSpec((SSM_GROUPS, n_rows, SSM_ROW), lambda i: (0, i, 0)),
        pair_spec, pair_spec, pair_spec,
        pl.BlockSpec((MIX_TILE, GM_WIDTH), lambda i: (i, 0)),
    ]
    out_shape = [
        jax.ShapeDtypeStruct((SSM_GROUPS, n_tok // SSM_CHUNK, SSM_ROW), F32),
        jax.ShapeDtypeStruct((HEAD_PAIRS, n_tok, LANES), BF16),
        jax.ShapeDtypeStruct((HEAD_PAIRS, n_tok, LANES), kv_dtype),
        jax.ShapeDtypeStruct((HEAD_PAIRS, n_tok, LANES), kv_dtype),
        jax.ShapeDtypeStruct((n_tok, GM_WIDTH), BF16),
    ]
    args = [x, mod, g, w, hsum, qg, kg, ws, gmb]
    in_specs = [
        pl.BlockSpec((MIX_TILE, D_MODEL), lambda i: (i, 0)),
        _mod_spec(l, seg0, seg_tokens, MIX_TILE),
        _layer_spec(g, l), _layer_spec(w, l),
        _const_spec((MXU_DIM, MXU_DIM)),
        _layer_spec(qg, l), _layer_spec(kg, l), _layer_spec(ws, l), _layer_spec(gmb, l),
    ]
    aliases = {}
    if is_ctx:
        bt = MIX_TILE // SEQ
        cache_shape = jax.ShapeDtypeStruct((n_tok // SEQ, DEPTH, SEQ, NA_WIDTH), F32)
        if caches is None:
            out_specs += [pl.BlockSpec((bt, DEPTH, SEQ, NA_WIDTH), lambda i: (i, 0, 0, 0))] * 2
        else:
            out_specs += [pl.BlockSpec((bt, 1, SEQ, NA_WIDTH), lambda i: (i, l, 0, 0))] * 2
        out_shape += [cache_shape] * 2
        if caches is not None:
            aliases = {len(args): len(out_shape) - 2, len(args) + 1: len(out_shape) - 1}
            args += list(caches)
            in_specs += [pl.BlockSpec(memory_space=pl.ANY)] * 2
    return pl.pallas_call(
        functools.partial(_inproj_kernel, n_in=len(args), n_out=len(out_shape)),
        grid=(n_tok // MIX_TILE,),
        in_specs=in_specs,
        out_specs=out_specs,
        out_shape=out_shape,
        input_output_aliases=aliases,
        scratch_shapes=[pltpu.VMEM((MIX_TILE, D_MODEL), BF16), pltpu.VMEM((SSM_WIDTH // LANES, MIX_TILE, LANES), F32),
                        pltpu.VMEM((SSM_CHUNK, n_rows, SSM_ROW), F32)],
        compiler_params=_cparams(("arbitrary",)),
        name="inproj_ctx" if is_ctx else "inproj_lat",
    )(*args)


def _roll_rows(x, d, down):
    return pltpu.roll(x, d if down else SUBLANES - d, axis=0)


def _scan_tile(w, wp, c, cp, cst, row, down):
    x, xp = w, wp
    for si, d in enumerate(SCAN_SHIFTS):
        a, b, bp = cst[3 * si], cst[3 * si + 1], cst[3 * si + 2]
        xs, xps = _roll_rows(x, d, down), _roll_rows(xp, d, down)
        x, xp = x + a * xs + b * xps, xp + a * xps + bp * xs
    a, b, bp = cst[N_SCAN_CONST - 3], cst[N_SCAN_CONST - 2], cst[N_SCAN_CONST - 1]
    s = x + a * c + b * cp
    sp = xp + a * cp + bp * c
    edge = 0 if down else SUBLANES - 1
    last = SUBLANES - 1 if down else 0
    seen = jnp.where(row == edge, c, _roll_rows(s, 1, down))
    c_new = jnp.broadcast_to(s[last:last + 1, :], s.shape)
    cp_new = jnp.broadcast_to(sp[last:last + 1, :], sp.shape)
    return seen, c_new, cp_new


def _ssm_kernel(u_ref, mp_ref, q_ref, cst_ref, dvec_ref, s0_ref, y_ref, fin_ref, w_ref, sq_ref, *, n_batch, n_chunks):
    gb = u_ref.shape[0]
    n_rows = n_batch * n_chunks
    row_blk = min(n_rows, SSM_ROW_BLOCK)
    for g in range(gb):
        for r0 in range(0, n_rows, row_blk):
            rows = slice(r0, r0 + row_blk)
            ug = u_ref[g, rows, :]
            wy = jnp.dot(ug.astype(BF16), mp_ref[g], preferred_element_type=F32)
            y_ref[g, rows, :] = wy[:, :SSM_ROW] + ug * dvec_ref[g]
            w_ref[g, rows, :] = wy[:, SSM_ROW:]

    n_tiles = n_chunks // SUBLANES
    row = lax.broadcasted_iota(jnp.int32, (SUBLANES, LANES), 0)

    def tile_step(b, kt, carry):
        out = []
        for g in range(gb):
            cf, cfp, cb, cbp = carry[4 * g:4 * g + 4]
            rf = pl.multiple_of(b * n_chunks + kt * SUBLANES, SUBLANES)
            rb = pl.multiple_of(b * n_chunks + (n_tiles - 1 - kt) * SUBLANES, SUBLANES)
            wf = w_ref[g, pl.ds(rf, SUBLANES), :]
            wb = w_ref[g, pl.ds(rb, SUBLANES), :]
            cst_f = [cst_ref[g, 0, i] for i in range(N_SCAN_CONST)]
            cst_b = [cst_ref[g, 1, i] for i in range(N_SCAN_CONST)]
            seen_f, cf, cfp = _scan_tile(wf[:, 0:LANES], wf[:, LANES:2 * LANES], cf, cfp, cst_f, row, True)
            seen_b, cb, cbp = _scan_tile(wb[:, 2 * LANES:3 * LANES], wb[:, 3 * LANES:], cb, cbp, cst_b, row, False)
            sq_ref[g, pl.ds(rf, SUBLANES), 0:LANES] = seen_f
            sq_ref[g, pl.ds(rb, SUBLANES), LANES:2 * LANES] = seen_b
            out += [cf, cfp, cb, cbp]
        return tuple(out)

    def batch_body(b, _):
        carry = []
        for g in range(gb):
            for i in range(4):
                carry.append(jnp.broadcast_to(s0_ref[g, i, pl.ds(b, 1), :], (SUBLANES, LANES)))
        carry = tuple(carry)
        if n_tiles <= 2:
            for kt in range(n_tiles):
                carry = tile_step(b, kt, carry)
        else:
            carry = lax.fori_loop(0, n_tiles, lambda kt, c: tile_step(b, kt, c), carry, unroll=2)
        for g in range(gb):
            fin_ref[g, 0, pl.ds(b, 1), :] = carry[4 * g][0:1, :]
            fin_ref[g, 1, pl.ds(b, 1), :] = carry[4 * g + 2][0:1, :]
        return 0

    lax.fori_loop(0, n_batch, batch_body, 0, unroll=2 if n_tiles <= 2 else 1)

    for g in range(gb):
        for r0 in range(0, n_rows, row_blk):
            rows = slice(r0, r0 + row_blk)
            y_ref[g, rows, :] += jnp.dot(sq_ref[g, rows, :].astype(BF16), q_ref[g], preferred_element_type=F32)


def _ssm_core(u, mp, q, cst, dvec, s0, *, l, n_batch, n_chunks):
    n_rows = n_batch * n_chunks
    gb = SSM_GROUP_BLOCK
    g3 = lambda i: (i, 0, 0)
    g4 = lambda i: (i, 0, 0, 0)
    return pl.pallas_call(
        functools.partial(_ssm_kernel, n_batch=n_batch, n_chunks=n_chunks),
        grid=(SSM_GROUPS // gb,),
        in_specs=[
            pl.BlockSpec((gb, n_rows, SSM_ROW), g3),
            pl.BlockSpec((None, gb, SSM_ROW, SSM_ROW + SSM_W_COLS), lambda i: (l, i, 0, 0)),
            pl.BlockSpec((None, gb, 2 * LANES, SSM_ROW), lambda i: (l, i, 0, 0)),
            pl.BlockSpec((None, gb, 2, N_SCAN_CONST, SUBLANES, LANES), lambda i: (l, i, 0, 0, 0, 0)),
            pl.BlockSpec((None, gb, 1, SSM_ROW), lambda i: (l, i, 0, 0)),
            pl.BlockSpec((gb, 4, n_batch, LANES), g4),
        ],
        out_specs=[
            pl.BlockSpec((gb, n_rows, SSM_ROW), g3),
            pl.BlockSpec((gb, 2, n_batch, LANES), g4),
        ],
        out_shape=[
            jax.ShapeDtypeStruct((SSM_GROUPS, n_rows, SSM_ROW), F32),
            jax.ShapeDtypeStruct((SSM_GROUPS, 2, n_batch, LANES), F32),
        ],
        scratch_shapes=[pltpu.VMEM((gb, n_rows, SSM_W_COLS), F32), pltpu.VMEM((gb, n_rows, 2 * LANES), F32)],
        compiler_params=_cparams(("arbitrary",)),
        name="ssm_core",
    )(u, mp, q, cst, dvec, s0)


def _cmul(a, b):
    return a[0] * b[0] - a[1] * b[1], a[0] * b[1] + a[1] * b[0]


_CHUNK_IDX = np.arange(SSM_CHUNK)


def _ssm_exponents():
    t = SSM_CHUNK
    idx = _CHUNK_IDX.astype(np.float32)
    both = lambda a: np.stack([a, a], axis=1)
    lane_step = idx - (t - 1) / 2
    tile_rows = np.arange(1, SUBLANES + 1, dtype=np.float32) * t
    tables = dict(
        one=both(np.ones(1, np.float32)),
        incr=np.stack([t - 1 - idx, idx], axis=1),
        seen=np.stack([idx + 1, t - idx], axis=1),
        resp_in=np.stack([-lane_step, lane_step], axis=1),
        resp_out=np.stack([lane_step, -lane_step], axis=1),
        shift=both(np.asarray(SCAN_SHIFTS, np.float32) * t),
        tile=np.stack([tile_rows, tile_rows[::-1]], axis=1),
    )
    spans, start = {}, 0
    for name, tab in tables.items():
        spans[name] = slice(start, start + len(tab))
        start += len(tab)
    return np.concatenate(list(tables.values()), axis=0), spans


def _ssm_params(p):
    t = SSM_CHUNK
    n_l = p['ssm_lambda_re'].shape[0]
    lam = (p['ssm_lambda_re'].astype(F32), p['ssm_lambda_im'].astype(F32))
    dt = jnp.exp(p['ssm_log_dt'].astype(F32))[..., None]
    arg = (lam[0] * dt, lam[1] * dt)
    expo, span = _ssm_exponents()
    n = jnp.asarray(expo)[:, None, :, None, None]
    mag = jnp.exp(n * arg[0][None])
    pw = (mag * jnp.cos(n * arg[1][None]), mag * jnp.sin(n * arg[1][None]))
    take = lambda name: [x[span[name]] for x in pw]
    lanes_last = lambda xs: [jnp.moveaxis(x, 0, -1) for x in xs]

    lbar = [x[0] for x in take('one')]
    num = (lbar[0] - 1.0, lbar[1])
    den = lam[0] * lam[0] + lam[1] * lam[1]
    zoh = ((num[0] * lam[0] + num[1] * lam[1]) / den, (num[1] * lam[0] - num[0] * lam[1]) / den)
    b_mat = (p['ssm_b_re'].astype(F32), p['ssm_b_im'].astype(F32))
    bbar = _cmul((zoh[0][..., None], zoh[1][..., None]), b_mat)
    c_mat = [jnp.swapaxes(p[k].astype(F32), -1, -2) for k in ('ssm_c_re', 'ssm_c_im')]
    ct = [jnp.tile(x, (1, 1, 1, 1, t)) for x in c_mat]
    bt = [jnp.tile(x, (1, 1, 1, 1, t)) for x in bbar]

    per_lane = lambda name: [jnp.repeat(x, SSM_CH, axis=-1) for x in lanes_last(take(name))]
    xm = _cmul(per_lane('resp_in'), bt)
    ym = _cmul(per_lane('resp_out'), ct)
    resp = jnp.einsum('ldgpx,ldgpy->ldgxy', jnp.concatenate([xm[0], -xm[1]], axis=3),
                      jnp.concatenate([ym[0], ym[1]], axis=3), precision='highest')
    steps_x = np.repeat(_CHUNK_IDX, SSM_CH)
    causal = np.stack([steps_x[None, :] >= steps_x[:, None], steps_x[:, None] >= steps_x[None, :]])
    mats = jnp.sum(jnp.where(jnp.asarray(causal)[None, :, None], resp, 0.0), axis=1)

    incr = _cmul([x[..., None] for x in take('incr')], [x[None] for x in bbar])
    incr = [jnp.transpose(x, (1, 2, 3, 0, 5, 4)) for x in incr]
    pcat = jnp.concatenate([incr[0], incr[1], incr[1], incr[0]], axis=-1)
    pcat = jnp.concatenate([pcat[:, 0], pcat[:, 1]], axis=-1).reshape(n_l, SSM_GROUPS, SSM_ROW, SSM_W_COLS)
    mp = jnp.concatenate([mats, pcat], axis=-1).astype(BF16)

    qm = _cmul(ct, per_lane('seen'))
    q = jnp.concatenate([qm[0][:, 0], -qm[1][:, 0], qm[0][:, 1], -qm[1][:, 1]], axis=2).astype(BF16)

    def forms(name, rows):
        pr, pi = [jnp.moveaxis(x, 0, 3) for x in take(name)]
        f = jnp.stack([jnp.concatenate([pr, pr], -1), jnp.concatenate([-pi, pi], -1),
                       jnp.concatenate([pi, -pi], -1)], axis=4)
        if rows:
            return f
        r = np.arange(SUBLANES)[None, :]
        d = np.repeat(np.asarray(SCAN_SHIFTS), 3)[:, None]
        keep = np.stack([r >= d, r < SUBLANES - d]).astype(np.float32)
        return f.reshape(n_l, 2, SSM_GROUPS, 3 * len(SCAN_SHIFTS), 1, LANES) * keep[None, :, None, :, :, None]
    per_row = jnp.swapaxes(forms('tile', True), 3, 4)
    cst = jnp.swapaxes(jnp.concatenate([forms('shift', False), per_row], axis=3), 1, 2)
    dvec = jnp.tile(p['ssm_d'].astype(F32).reshape(n_l, SSM_GROUPS, 1, SSM_CH), (1, 1, 1, SSM_CHUNK))
    return mp, q, cst, dvec


def _ssm_states_in(state_ssm):
    st = jnp.transpose(state_ssm.astype(F32), (1, 3, 2, 0, 4, 5))
    re, im = st[..., 0], st[..., 1]
    ri, ir = jnp.concatenate([re, im], -1), jnp.concatenate([im, re], -1)
    return jnp.stack([ri[:, :, 0], ir[:, :, 0], ri[:, :, 1], ir[:, :, 1]], axis=2)


def _ssm_states_out(fins):
    fin = jnp.stack(fins, axis=0)
    fin = jnp.stack([fin[..., :SSM_STATE], fin[..., SSM_STATE:]], axis=-1)
    return jnp.transpose(fin, (3, 0, 2, 1, 4, 5))


def _head_lane_mask(hh):
    lane = lax.broadcasted_iota(jnp.int32, (1, LANES), 1)
    return (lane >= hh * NA_HEAD_DIM) & (lane < (hh + 1) * NA_HEAD_DIM)


_NT_DIMS = (((1,), (1,)), ((), ()))


def _ctx_attn_kernel(q_ref, k_ref, v_ref, o_ref):
    lm0 = _head_lane_mask(0)
    for b in range(CTX_ATTN_BATCH):
        rows = slice(b * SEQ, (b + 1) * SEQ)
        for p in range(HEAD_PAIRS):
            qp, kp, vp = q_ref[p, rows, :], k_ref[p, rows, :], v_ref[p, rows, :]
            zero = jnp.zeros_like(qp)
            q2 = jnp.concatenate([jnp.where(lm0, qp, zero), jnp.where(lm0, zero, qp)], axis=0)
            s = lax.dot_general(q2, kp, _NT_DIMS, preferred_element_type=F32)
            e = jnp.exp(s - jnp.max(s, axis=-1, keepdims=True))
            o = jnp.dot(e.astype(BF16), vp, preferred_element_type=F32) / jnp.sum(e, axis=-1, keepdims=True)
            o_ref[rows, p * LANES:(p + 1) * LANES] = jnp.where(lm0, o[:SEQ, :], o[SEQ:, :]).astype(BF16)


def _ctx_attn(q, k, v, n_batch):
    rows = CTX_ATTN_BATCH * SEQ
    spec = pl.BlockSpec((HEAD_PAIRS, rows, LANES), lambda b: (0, b, 0))
    return pl.pallas_call(
        _ctx_attn_kernel,
        grid=(n_batch // CTX_ATTN_BATCH,),
        in_specs=[spec, spec, spec],
        out_specs=pl.BlockSpec((rows, NA_WIDTH), lambda b: (b, 0)),
        out_shape=jax.ShapeDtypeStruct((n_batch * SEQ, NA_WIDTH), BF16),
        compiler_params=_cparams(("arbitrary",)),
        name="ctx_attn",
    )(q, k, v)


def _na_kernel(q_ref, k_ref, v_ref, kc_ref, vc_ref, t2_ref, rm_ref, o_ref, qb_ref, os_ref, tb_ref):
    rt = pl.program_id(1)
    n_rt = pl.num_programs(1)

    @pl.when((pl.program_id(0) == 0) & (rt == 0))
    def _():
        for h in range(NA_HEADS):
            for n in range(NA_COL_BLOCKS):
                for rl in range(NA_Q_ROWS):
                    lane0 = (NA_OFF_START - rl) * NA_KCOLS
                    tb_ref[h, n, rl * NA_COLS:(rl + 1) * NA_COLS, :] = (
                        t2_ref[h, n, :, lane0:lane0 + NA_K_ROWS * NA_KCOLS])

    r0 = rt * NA_Q_ROWS
    ttype = jnp.where(rt == 0, 0, jnp.where(rt == n_rt - 1, 2, 1))
    rmask = rm_ref[ttype]
    blk_q = NA_Q_ROWS * NA_COLS

    def pair_body(p, carry):
        lm0 = _head_lane_mask(0)
        for n in range(NA_COL_BLOCKS):
            for rl in range(NA_Q_ROWS):
                qrow = q_ref[p, rl * GRID_W + n * NA_COLS:rl * GRID_W + (n + 1) * NA_COLS, :]
                zero = jnp.zeros_like(qrow)
                base = 2 * n * blk_q + rl * NA_COLS
                qb_ref[base:base + NA_COLS, :] = jnp.where(lm0, qrow, zero)
                qb_ref[base + blk_q:base + blk_q + NA_COLS, :] = jnp.where(lm0, zero, qrow)
        kcp, vcp = kc_ref[0, p], vc_ref[0, p]
        n_win = NA_K_ROWS * NA_KCOLS
        for n in range(NA_COL_BLOCKS):
            kparts, vparts = [], []
            for i in range(NA_K_ROWS):
                kr = jnp.clip(r0 - (NA_K_ROWS - NA_Q_ROWS) // 2 + i, 0, GRID_W - 1)
                start = pl.multiple_of(kr * GRID_W + NA_KC0[n], 8)
                kparts.append(k_ref[p, pl.ds(start, NA_KCOLS), :])
                vparts.append(v_ref[p, pl.ds(start, NA_KCOLS), :])
            kall = jnp.concatenate([jnp.concatenate(kparts, axis=0).astype(BF16), kcp], axis=0)
            vall = jnp.concatenate([jnp.concatenate(vparts, axis=0).astype(BF16), vcp], axis=0)
            rows2 = slice(2 * n * blk_q, 2 * (n + 1) * blk_q)
            s = lax.dot_general(qb_ref[rows2, :], kall, _NT_DIMS, preferred_element_type=F32)
            bias = jnp.concatenate([tb_ref[2 * p, n] + rmask, tb_ref[2 * p + 1, n] + rmask], axis=0)
            s = jnp.concatenate([s[:, :n_win] + bias, s[:, n_win:]], axis=-1)
            e = jnp.exp(s - jnp.max(s, axis=-1, keepdims=True))
            o = jnp.dot(e.astype(BF16), vall, preferred_element_type=F32) / jnp.sum(e, axis=-1, keepdims=True)
            oblk = jnp.where(lm0, o[:blk_q, :], o[blk_q:, :])
            for rl in range(NA_Q_ROWS):
                os_ref[p, rl * GRID_W + n * NA_COLS:rl * GRID_W + (n + 1) * NA_COLS, :] = (
                    oblk[rl * NA_COLS:(rl + 1) * NA_COLS, :])
        return carry

    lax.fori_loop(0, HEAD_PAIRS, pair_body, 0, unroll=2)
    for p in range(HEAD_PAIRS):
        o_ref[:, p * LANES:(p + 1) * LANES] = os_ref[p].astype(BF16)


def _na_attn(q, k, v, kc, vc, tb, rm, n_batch, *, l):
    tile_tok = NA_Q_ROWS * GRID_W
    n_rt = DEC_SEQ // tile_tok
    blk_q, blk_k = NA_Q_ROWS * NA_COLS, NA_K_ROWS * NA_KCOLS
    return pl.pallas_call(
        _na_kernel,
        grid=(n_batch, n_rt),
        in_specs=[
            pl.BlockSpec((HEAD_PAIRS, tile_tok, LANES), lambda b, r: (0, b * n_rt + r, 0)),
            pl.BlockSpec((HEAD_PAIRS, DEC_SEQ, LANES), lambda b, r: (0, b, 0)),
            pl.BlockSpec((HEAD_PAIRS, DEC_SEQ, LANES), lambda b, r: (0, b, 0)),
            pl.BlockSpec((1, None, HEAD_PAIRS, PAST_LEN, LANES), lambda b, r: (b, l, 0, 0, 0)),
            pl.BlockSpec((1, None, HEAD_PAIRS, PAST_LEN, LANES), lambda b, r: (b, l, 0, 0, 0)),
            _layer_spec(tb, l),
            _const_spec(rm.shape),
        ],
        out_specs=pl.BlockSpec((tile_tok, NA_WIDTH), lambda b, r: (b * n_rt + r, 0)),
        out_shape=jax.ShapeDtypeStruct((n_batch * DEC_SEQ, NA_WIDTH), BF16),
        scratch_shapes=[
            pltpu.VMEM((2 * tile_tok, LANES), BF16),
            pltpu.VMEM((HEAD_PAIRS, tile_tok, LANES), F32),
            pltpu.VMEM((NA_HEADS, NA_COL_BLOCKS, blk_q, blk_k), F32)],
        compiler_params=_cparams(("arbitrary", "arbitrary")),
        name="na_attn",
    )(q, k, v, kc, vc, tb, rm)


def _na_tables(rpb):
    n = np.arange(NA_COL_BLOCKS)
    cc = np.arange(NA_COLS)
    kk = np.arange(NA_KCOLS)
    rl = np.arange(NA_Q_ROWS)
    ki = np.arange(NA_K_ROWS)
    n_dr, n_dc = 2 * NA_MAX_ROWS - 1, 2 * NA_COLS - 1
    c = n[:, None] * NA_COLS + cc[None, :]
    kcol = np.asarray(NA_KC0)[:, None] + kk[None, :]
    dc = np.clip(kcol[:, None, :] - c[:, :, None], -(NA_COLS - 1), NA_COLS - 1) + (NA_COLS - 1)
    cs = np.clip(c - NA_COLS // 2, 0, GRID_W - NA_COLS)
    col_ok = (kcol[:, None, :] >= cs[:, :, None]) & (kcol[:, None, :] < cs[:, :, None] + NA_COLS)
    half = (NA_K_ROWS - NA_Q_ROWS) // 2
    dr = np.clip(ki[None, :] - half - rl[:, None] + (NA_MAX_ROWS - 1), 0, n_dr - 1)
    oh_dc = jnp.asarray(dc[None] == np.arange(n_dc)[:, None, None, None], F32)
    off = np.clip(np.arange(NA_OFF_ROWS) - NA_OFF_PAD, 0, n_dr - 1)
    assert np.array_equal(dr, off[ki[None] + (NA_OFF_START - rl)[:, None]])
    oh_off = jnp.asarray(off[None, :] == np.arange(n_dr)[:, None], F32)
    rows_p = jnp.einsum('lhab,aA->lhAb', rpb.astype(F32), oh_off, precision='highest')
    cols = jnp.einsum('lhAb,bnck->lhncAk', rows_p, oh_dc, precision='highest')
    cols = jnp.where(jnp.asarray(col_ok)[None, None, :, :, None, :], cols, NEG_INF)
    tb = cols.reshape(-1, NA_HEADS, NA_COL_BLOCKS, NA_COLS, NA_OFF_ROWS * NA_KCOLS)
    rows = GRID_W
    masks = []
    for r0 in (0, NA_Q_ROWS, rows - NA_Q_ROWS):
        r = r0 + rl
        rs = np.clip(r - NA_MAX_ROWS // 2, 0, rows - NA_MAX_ROWS)
        kr = r0 - half + ki
        ok = (kr[None, :] >= rs[:, None]) & (kr[None, :] < rs[:, None] + NA_MAX_ROWS)
        full = np.broadcast_to(ok[:, None, :, None], (NA_Q_ROWS, NA_COLS, NA_K_ROWS, NA_KCOLS))
        masks.append(np.where(full, 0.0, NEG_INF).reshape(NA_Q_ROWS * NA_COLS, NA_K_ROWS * NA_KCOLS))
    rm = jnp.asarray(np.stack(masks), F32)
    return tb, rm


def _outffn_kernel(x_ref, mod_ref, y_ref, na_ref, gm_ref, gw_ref, gb_ref, wo_ref, g_ref, win_ref, wout_ref, o_ref,
                   yr_ref, ys_ref, cat_ref, xm_ref, h_ref, act_ref):
    gate = mod_ref[0, 5:6, :]
    n_rows = OUT_TILE // SSM_CHUNK
    for g in range(SSM_GROUPS):
        yr_ref[g] = _roll_lanes(y_ref[g], SSM_CH * g)
    for t in range(SSM_CHUNK):
        yt = _merge_segments(lambda src, lt: yr_ref[src, :, lt * LANES:(lt + 1) * LANES], t)
        yt = _roll_lanes(yt, -SSM_CH * t)
        for j in range(SSM_WIDTH // LANES):
            ys_ref[j, pl.ds(t, n_rows, stride=SSM_CHUNK), :] = yt[:, j * LANES:(j + 1) * LANES]
    y = _gelu(jnp.concatenate([ys_ref[j] for j in range(SSM_WIDTH // LANES)], axis=-1))
    z = jnp.dot(y.astype(BF16), gw_ref[...], preferred_element_type=F32) + gb_ref[...]
    cat_ref[:, 0:SSM_WIDTH] = (y * _sigmoid(z)).astype(BF16)
    cat_ref[:, SSM_WIDTH:SSM_WIDTH + NA_WIDTH] = na_ref[...]
    cat_ref[:, SSM_WIDTH + NA_WIDTH:] = gm_ref[...]
    mix = jnp.dot(cat_ref[...], wo_ref[...], preferred_element_type=F32)
    xm_ref[...] = x_ref[...] + gate * mix
    _ffn_kernel(xm_ref, mod_ref, g_ref, win_ref, wout_ref, o_ref, h_ref, act_ref, mod_base=6)


def _outffn(x, mod, y, na, gm, gw, gb, wo, g, w_in, w_out, *, l, seg0, seg_tokens):
    n_tok = x.shape[0]
    n_rows = OUT_TILE // SSM_CHUNK
    tok = lambda width: pl.BlockSpec((OUT_TILE, width), lambda i: (i, 0))
    return pl.pallas_call(
        _outffn_kernel,
        grid=(n_tok // OUT_TILE,),
        in_specs=[
            tok(D_MODEL),
            _mod_spec(l, seg0, seg_tokens, OUT_TILE),
            pl.BlockSpec((SSM_GROUPS, n_rows, SSM_ROW), lambda i: (0, i, 0)),
            tok(NA_WIDTH), tok(GM_WIDTH),
            _layer_spec(gw, l), _layer_spec(gb, l), _layer_spec(wo, l),
            _layer_spec(g, l), _layer_spec(w_in, l), _layer_spec(w_out, l),
        ],
        out_specs=tok(D_MODEL),
        out_shape=jax.ShapeDtypeStruct((n_tok, D_MODEL), F32),
        scratch_shapes=[pltpu.VMEM((SSM_GROUPS, n_rows, SSM_ROW), F32),
                        pltpu.VMEM((SSM_WIDTH // LANES, OUT_TILE, LANES), F32),
                        pltpu.VMEM((OUT_TILE, D_MODEL), BF16),
                        pltpu.VMEM((OUT_TILE, D_MODEL), F32),
                        pltpu.VMEM((OUT_TILE, D_MODEL), BF16), pltpu.VMEM((OUT_TILE, D_FF), BF16)],
        compiler_params=_cparams(("arbitrary",)),
        name="outffn",
    )(x, mod, y, na, gm, gw, gb, wo, g, w_in, w_out)


def _prep(p):
    heads = np.arange(MXU_DIM) // NA_HEAD_DIM
    hsum = jnp.asarray((heads[:, None] == heads[None, :]) / NA_HEAD_DIM, BF16)
    gmb = jnp.repeat(jnp.swapaxes(p['gm_bs'], 1, 2).astype(F32), GM_WIDTH // GM_GROUPS, axis=2)
    tb, rm = _na_tables(p['na_rpb'])
    row = lambda a: a.astype(F32).reshape(DEPTH, 1, -1)
    return dict(
        g1=row(p['norm_ffn1']), g2=row(p['norm_mix']), g3=row(p['norm_ffn2']),
        f1_in=p['ffn1_w_in'].astype(BF16), f1_out=p['ffn1_w_out'].astype(BF16),
        f2_in=p['ffn2_w_in'].astype(BF16), f2_out=p['ffn2_w_out'].astype(BF16),
        w_in=p['w_in'].astype(BF16), w_out=p['w_out'].astype(BF16),
        hsum=hsum,
        qg=row(jnp.tile(p['na_q_norm'], (1, NA_HEADS))), kg=row(jnp.tile(p['na_k_norm'], (1, NA_HEADS))),
        ws=p['gm_ws'].astype(BF16), gmb=gmb,
        ssm=_ssm_params(p),
        glu_w=p['ssm_glu_w'].astype(BF16), glu_b=row(p['ssm_glu_b']),
        tb=tb, rm=rm,
    )


def _pairs_major(t):
    b, n_l, s = t.shape[:3]
    return jnp.transpose(t.reshape(b, n_l, s, HEAD_PAIRS, LANES), (0, 1, 3, 2, 4)).astype(BF16)


def _trunk_layer(x, mod, w, l, *, n_batch, seq_len, caches=None, is_ctx=False, ctx_kv=None, ssm_init=None):
    seg = dict(l=l, seg0=0 if is_ctx else 1, seg_tokens=x.shape[0] if is_ctx else seq_len)
    x = _ffn(x, mod, w['g1'], w['f1_in'], w['f1_out'], mod_base=0, **seg)
    outs = _inproj(x, mod, w['g2'], w['w_in'], w['hsum'], w['qg'], w['kg'], w['ws'], w['gmb'],
                   is_ctx=is_ctx, caches=caches, **seg)
    u, q, k, v, gm = outs[:5]
    s0 = jnp.zeros((SSM_GROUPS, 4, n_batch, LANES), F32) if is_ctx else ssm_init[l]
    y, fin = _ssm_core(u, *w['ssm'], s0, l=l, n_batch=n_batch, n_chunks=seq_len // SSM_CHUNK)
    if is_ctx:
        na = _ctx_attn(q, k, v, n_batch)
    else:
        na = _na_attn(q, k, v, ctx_kv[0], ctx_kv[1], w['tb'], w['rm'], n_batch, l=l)
    x = _outffn(x, mod, y, na, gm, w['glu_w'], w['glu_b'], w['w_out'], w['g3'], w['f2_in'], w['f2_out'], **seg)
    return x, (tuple(outs[5:]), fin)


def kernel(x_prompt, x_sample, c, cache_k, cache_v, state_ssm, c_ctx, w_ada, b_ada, norm_ffn1, ffn1_w_in, ffn1_w_out, norm_mix, w_in, w_out, ssm_lambda_re, ssm_lambda_im, ssm_log_dt, ssm_b_re, ssm_b_im, ssm_c_re, ssm_c_im, ssm_d, ssm_glu_w, ssm_glu_b, na_q_norm, na_k_norm, na_rpb, gm_ws, gm_bs, norm_ffn2, ffn2_w_in, ffn2_w_out):
    p = dict(norm_ffn1=norm_ffn1, ffn1_w_in=ffn1_w_in, ffn1_w_out=ffn1_w_out, norm_mix=norm_mix, w_in=w_in,
             w_out=w_out, ssm_lambda_re=ssm_lambda_re, ssm_lambda_im=ssm_lambda_im, ssm_log_dt=ssm_log_dt,
             ssm_b_re=ssm_b_re, ssm_b_im=ssm_b_im, ssm_c_re=ssm_c_re, ssm_c_im=ssm_c_im, ssm_d=ssm_d,
             ssm_glu_w=ssm_glu_w, ssm_glu_b=ssm_glu_b, na_q_norm=na_q_norm, na_k_norm=na_k_norm, na_rpb=na_rpb,
             gm_ws=gm_ws, gm_bs=gm_bs, norm_ffn2=norm_ffn2, ffn2_w_in=ffn2_w_in, ffn2_w_out=ffn2_w_out)
    batch, dec_batch = x_prompt.shape[0], x_sample.shape[0]
    cond8 = jnp.zeros((8, D_MODEL), F32).at[0].set(c_ctx).at[1:1 + dec_batch].set(c)
    mod = _adaln(cond8, w_ada, b_ada).reshape(DEPTH, 8, N_MOD, D_MODEL)

    xp = x_prompt.reshape(batch * SEQ, D_MODEL)
    xs = x_sample.reshape(dec_batch * DEC_SEQ, D_MODEL)
    w = _prep(p)
    ctx_kv = (_pairs_major(cache_k), _pairs_major(cache_v))
    ssm_init = _ssm_states_in(state_ssm)
    caches, fins = None, []
    for l in range(DEPTH):
        xp, (caches, fin) = _trunk_layer(xp, mod, w, l, n_batch=batch, seq_len=SEQ, caches=caches, is_ctx=True)
        fins.append(fin)
        xs, _ = _trunk_layer(xs, mod, w, l, n_batch=dec_batch, seq_len=DEC_SEQ, ctx_kv=ctx_kv, ssm_init=ssm_init)
    cache_shape = (batch, DEPTH, SEQ, NA_HEADS, NA_HEAD_DIM)
    return (xp.reshape(batch, SEQ, D_MODEL), xs.reshape(dec_batch, DEC_SEQ, D_MODEL),
            caches[0].reshape(cache_shape), caches[1].reshape(cache_shape), _ssm_states_out(fins))
```

```python
import functools

import numpy as np
import jax
import jax.numpy as jnp
from jax import lax
from jax.experimental import pallas as pl
from jax.experimental.pallas import tpu as pltpu

D_MODEL = 1024
DEPTH = 2
SEQ = 256
DEC_SEQ = 4096
PAST_LEN = 256
GRID_W = 64
SSM_WIDTH = 256
SSM_CH = 16
SSM_GROUPS = 16
SSM_STATE = 64
NA_WIDTH = 512
NA_HEAD_DIM = 64
NA_HEADS = 8
NA_MAX_ROWS = 8
NA_COLS = 16
GM_WIDTH = 256
GM_GROUPS = 4
GM_CHUNK = 128
D_FF = 2816
N_MOD = 9
RMS_EPS = 1e-6
LN_EPS = 1e-5
NEG_INF = -1e30

F32 = jnp.float32
BF16 = jnp.bfloat16

LANES = 128
SUBLANES = 8
MXU_DIM = 256
VMEM_LIMIT_BYTES = 56 * 1024 * 1024
NA_VMEM_LIMIT_BYTES = 62 * 1024 * 1024

FFN_TILE = 1024
MIX_TILE = 1024
OUT_TILE = 512
FF_CHUNK = MXU_DIM
N_FF_CHUNKS = D_FF // FF_CHUNK
SSM_CHUNK = 16
SSM_ROW = SSM_CHUNK * SSM_CH
SSM_GROUP_BLOCK = 4
SSM_ROW_BLOCK = 512
SSM_W_COLS = 4 * LANES
SCAN_SHIFTS = (1, 2, 4)
N_SCAN_CONST = 3 * len(SCAN_SHIFTS) + 3
HEAD_PAIRS = NA_HEADS // 2
CTX_ATTN_BATCH = 4
NA_Q_ROWS = 8
NA_K_ROWS = 16
NA_KCOLS = 2 * NA_COLS
NA_COL_BLOCKS = GRID_W // NA_COLS
NA_KC0 = (0, 8, 24, 32)
SEGS_PER_TILE = LANES // SSM_CH
NA_OFF_PAD = (NA_Q_ROWS - 1) - ((NA_MAX_ROWS - 1) - (NA_K_ROWS - NA_Q_ROWS) // 2)
NA_OFF_START = NA_Q_ROWS - 1
NA_OFF_ROWS = NA_OFF_START + NA_K_ROWS


def _silu(x):
    return x * (1.0 / (1.0 + jnp.exp(-x)))


def _sigmoid(x):
    return 1.0 / (1.0 + jnp.exp(-x))


def _gelu(x):
    return 0.5 * x * (1.0 + jnp.tanh(0.7978845608028654 * (x + 0.044715 * (x * x * x))))


def _cparams(sem):
    return pltpu.CompilerParams(dimension_semantics=sem, vmem_limit_bytes=VMEM_LIMIT_BYTES)


def _const_spec(shape):
    nd = len(shape)
    return pl.BlockSpec(shape, lambda *_: (0,) * nd, pipeline_mode=pl.Buffered(1))


def _layer_spec(stacked, l):
    nd = stacked.ndim
    return pl.BlockSpec((None,) + stacked.shape[1:], lambda *_: (l,) + (0,) * (nd - 1), pipeline_mode=pl.Buffered(1))


def _ada_kernel(c_ref, w_ref, b_ref, o_ref):
    s = _silu(c_ref[...]).astype(BF16)
    w = w_ref[0].astype(BF16)
    o_ref[0] = jnp.dot(s, w, preferred_element_type=F32) + b_ref[0]


def _adaln(cond8, w_ada, b_ada):
    tn = D_MODEL
    ncol = N_MOD * D_MODEL
    return pl.pallas_call(
        _ada_kernel,
        grid=(DEPTH, ncol // tn),
        in_specs=[
            pl.BlockSpec((8, D_MODEL), lambda l, j: (0, 0)),
            pl.BlockSpec((1, D_MODEL, tn), lambda l, j: (l, 0, j)),
            pl.BlockSpec((1, 1, tn), lambda l, j: (l, 0, j)),
        ],
        out_specs=pl.BlockSpec((1, 8, tn), lambda l, j: (l, 0, j)),
        out_shape=jax.ShapeDtypeStruct((DEPTH, 8, ncol), F32),
        compiler_params=_cparams(("arbitrary", "arbitrary")),
        name="adaln",
    )(cond8, w_ada, b_ada.reshape(DEPTH, 1, ncol))


def _normed(x, g_row, shift, scale):
    ms = jnp.mean(x * x, axis=-1, keepdims=True)
    h = x * lax.rsqrt(ms + RMS_EPS) * g_row
    return h * (1.0 + scale) + shift


def _mod_spec(l, seg0, seg_tokens, tile):
    tiles_per_seg = seg_tokens // tile
    return pl.BlockSpec((None, 1, N_MOD, D_MODEL), lambda i: (l, seg0 + i // tiles_per_seg, 0, 0))


def _ffn_kernel(x_ref, mod_ref, g_ref, win_ref, wout_ref, o_ref, h_ref, act_ref, *, mod_base):
    shift = mod_ref[0, mod_base:mod_base + 1, :]
    scale = mod_ref[0, mod_base + 1:mod_base + 2, :]
    gate = mod_ref[0, mod_base + 2:mod_base + 3, :]
    h_ref[...] = _normed(x_ref[...], g_ref[...], shift, scale).astype(BF16)
    for j in range(N_FF_CHUNKS):
        cols = slice(j * FF_CHUNK, (j + 1) * FF_CHUNK)
        gate_j = jnp.dot(h_ref[...], win_ref[:, cols], preferred_element_type=F32)
        up_j = jnp.dot(h_ref[...], win_ref[:, D_FF + j * FF_CHUNK:D_FF + (j + 1) * FF_CHUNK], preferred_element_type=F32)
        act_ref[:, cols] = (_silu(gate_j) * up_j).astype(BF16)
    for n in range(D_MODEL // MXU_DIM):
        cols = slice(n * MXU_DIM, (n + 1) * MXU_DIM)
        y = jnp.dot(act_ref[...], wout_ref[:, cols], preferred_element_type=F32)
        o_ref[:, cols] = x_ref[:, cols] + 0.5 * gate[:, cols] * y


def _ffn(x, mod, g, w_in, w_out, *, l, mod_base, seg0, seg_tokens):
    n_tok = x.shape[0]
    return pl.pallas_call(
        functools.partial(_ffn_kernel, mod_base=mod_base),
        grid=(n_tok // FFN_TILE,),
        in_specs=[
            pl.BlockSpec((FFN_TILE, D_MODEL), lambda i: (i, 0)),
            _mod_spec(l, seg0, seg_tokens, FFN_TILE),
            _layer_spec(g, l), _layer_spec(w_in, l), _layer_spec(w_out, l),
        ],
        out_specs=pl.BlockSpec((FFN_TILE, D_MODEL), lambda i: (i, 0)),
        out_shape=jax.ShapeDtypeStruct((n_tok, D_MODEL), F32),
        scratch_shapes=[pltpu.VMEM((FFN_TILE, D_MODEL), BF16), pltpu.VMEM((FFN_TILE, D_FF), BF16)],
        compiler_params=_cparams(("arbitrary",)),
        name="ffn",
    )(x, mod, g, w_in, w_out)


def _segment_ids():
    return lax.broadcasted_iota(jnp.int32, (1, LANES), 1) // SSM_CH


def _merge_segments(load_piece, rot):
    seg = _segment_ids()
    tiles = []
    for lt in range(SSM_ROW // LANES):
        acc = None
        for jj in range(SEGS_PER_TILE):
            src = (lt * SEGS_PER_TILE + jj - rot) % SSM_CHUNK
            piece = load_piece(src, lt)
            acc = piece if acc is None else jnp.where(seg == jj, piece, acc)
        tiles.append(acc)
    return jnp.concatenate(tiles, axis=-1)


def _roll_lanes(x, shift):
    shift %= x.shape[-1]
    return pltpu.roll(x, shift, axis=x.ndim - 1) if shift else x


def _inproj_kernel(*refs, n_in, n_out):
    x_ref, mod_ref, g_ref, w_ref, hsum_ref, qg_ref, kg_ref, ws_ref, gmb_ref = refs[:9]
    u_ref, q_ref, k_ref, v_ref, gm_ref, *tok_refs = refs[n_in:n_in + n_out]
    h_ref, xs_ref, z_ref = refs[n_in + n_out:]
    shift = mod_ref[0, 3:4, :]
    scale = mod_ref[0, 4:5, :]
    h_ref[...] = _normed(x_ref[...], g_ref[...], shift, scale).astype(BF16)
    n_rows = MIX_TILE // SSM_CHUNK

    def proj(c0, width):
        return jnp.dot(h_ref[...], w_ref[:, c0:c0 + width], preferred_element_type=F32)

    xs_qk = proj(0, SSM_WIDTH + 2 * NA_WIDTH)
    for j in range(SSM_WIDTH // LANES):
        xs_ref[j] = xs_qk[:, j * LANES:(j + 1) * LANES]
    for t in range(SSM_CHUNK):
        zt = jnp.concatenate([xs_ref[j, pl.ds(t, n_rows, stride=SSM_CHUNK), :] for j in range(SSM_WIDTH // LANES)],
                             axis=-1)
        z_ref[t] = _roll_lanes(zt, SSM_CH * t)
    for g in range(SSM_GROUPS):
        ug = _merge_segments(lambda src, lt: z_ref[src, :, lt * LANES:(lt + 1) * LANES], g)
        u_ref[g] = _roll_lanes(ug, -SSM_CH * g)

    c0 = SSM_WIDTH
    qk = xs_qk[:, c0:]
    sq = (qk * qk).astype(BF16)
    groups = range(0, 2 * NA_WIDTH, MXU_DIM)
    ms = jnp.dot(jnp.concatenate([sq[:, c:c + MXU_DIM] for c in groups], axis=0), hsum_ref[...],
                 preferred_element_type=F32)
    ms = jnp.concatenate([ms[i * MIX_TILE:(i + 1) * MIX_TILE, :] for i in range(len(groups))], axis=-1)
    qk = qk * lax.rsqrt(ms + RMS_EPS)
    q = qk[:, :NA_WIDTH] * qg_ref[...] * (NA_HEAD_DIM ** -0.5)
    k = qk[:, NA_WIDTH:] * kg_ref[...]
    v_uv = proj(c0 + 2 * NA_WIDTH, NA_WIDTH + 2 * GM_WIDTH)
    v = v_uv[:, :NA_WIDTH]
    for p in range(HEAD_PAIRS):
        lanes = slice(p * LANES, (p + 1) * LANES)
        q_ref[p] = q[:, lanes].astype(q_ref.dtype)
        k_ref[p] = k[:, lanes].astype(k_ref.dtype)
        v_ref[p] = v[:, lanes].astype(v_ref.dtype)
    if tok_refs:
        for slot in range(tok_refs[0].shape[1]):
            tok_refs[0][:, slot] = k.reshape(MIX_TILE // SEQ, SEQ, NA_WIDTH)
            tok_refs[1][:, slot] = v.reshape(MIX_TILE // SEQ, SEQ, NA_WIDTH)

    a = _gelu(v_uv[:, NA_WIDTH:])
    u = a[:, :GM_WIDTH]
    vv = a[:, GM_WIDTH:]
    mu = jnp.mean(vv, axis=-1, keepdims=True)
    vc = vv - mu
    var = jnp.mean(vc * vc, axis=-1, keepdims=True)
    vln = (vc * lax.rsqrt(var + LN_EPS)).astype(BF16)
    lane = lax.broadcasted_iota(jnp.int32, (1, GM_WIDTH), 1)
    gw = GM_WIDTH // GM_GROUPS
    for ci in range(MIX_TILE // GM_CHUNK):
        rows = slice(ci * GM_CHUNK, (ci + 1) * GM_CHUNK)
        vch = vln[rows, :]
        sp = gmb_ref[...]
        for gi in range(GM_GROUPS):
            t = jnp.dot(ws_ref[gi], vch, preferred_element_type=F32)
            sp = sp + jnp.where((lane >= gi * gw) & (lane < (gi + 1) * gw), t, 0.0)
        gm_ref[rows, :] = (u[rows, :] * sp).astype(BF16)


def _inproj(x, mod, g, w, hsum, qg, kg, ws, gmb, *, l, seg0, seg_tokens, is_ctx, caches=None):
    n_tok = x.shape[0]
    kv_dtype = BF16 if is_ctx else F32
    n_rows = MIX_TILE // SSM_CHUNK
    pair_spec = pl.BlockSpec((HEAD_PAIRS, MIX_TILE, LANES), lambda i: (0, i, 0))
    out_specs = [
        pl.BlockSpec((SSM_GROUPS, n_rows, SSM_ROW), lambda i: (0, i, 0)),
        pair_spec, pair_spec, pair_spec,
        pl.BlockSpec((MIX_TILE, GM_WIDTH), lambda i: (i, 0)),
    ]
    out_shape = [
        jax.ShapeDtypeStruct((SSM_GROUPS, n_tok // SSM_CHUNK, SSM_ROW), F32),
        jax.ShapeDtypeStruct((HEAD_PAIRS, n_tok, LANES), BF16),
        jax.ShapeDtypeStruct((HEAD_PAIRS, n_tok, LANES), kv_dtype),
        jax.ShapeDtypeStruct((HEAD_PAIRS, n_tok, LANES), kv_dtype),
        jax.ShapeDtypeStruct((n_tok, GM_WIDTH), BF16),
    ]
    args = [x, mod, g, w, hsum, qg, kg, ws, gmb]
    in_specs = [
        pl.BlockSpec((MIX_TILE, D_MODEL), lambda i: (i, 0)),
        _mod_spec(l, seg0, seg_tokens, MIX_TILE),
        _layer_spec(g, l), _layer_spec(w, l),
        _const_spec((MXU_DIM, MXU_DIM)),
        _layer_spec(qg, l), _layer_spec(kg, l), _layer_spec(ws, l), _layer_spec(gmb, l),
    ]
    aliases = {}
    if is_ctx:
        bt = MIX_TILE // SEQ
        cache_shape = jax.ShapeDtypeStruct((n_tok // SEQ, DEPTH, SEQ, NA_WIDTH), F32)
        if caches is None:
            out_specs += [pl.BlockSpec((bt, DEPTH, SEQ, NA_WIDTH), lambda i: (i, 0, 0, 0))] * 2
        else:
            out_specs += [pl.BlockSpec((bt, 1, SEQ, NA_WIDTH), lambda i: (i, l, 0, 0))] * 2
        out_shape += [cache_shape] * 2
        if caches is not None:
            aliases = {len(args): len(out_shape) - 2, len(args) + 1: len(out_shape) - 1}
            args += list(caches)
            in_specs += [pl.BlockSpec(memory_space=pl.ANY)] * 2
    return pl.pallas_call(
        functools.partial(_inproj_kernel, n_in=len(args), n_out=len(out_shape)),
        grid=(n_tok // MIX_TILE,),
        in_specs=in_specs,
        out_specs=out_specs,
        out_shape=out_shape,
        input_output_aliases=aliases,
        scratch_shapes=[pltpu.VMEM((MIX_TILE, D_MODEL), BF16), pltpu.VMEM((SSM_WIDTH // LANES, MIX_TILE, LANES), F32),
                        pltpu.VMEM((SSM_CHUNK, n_rows, SSM_ROW), F32)],
        compiler_params=_cparams(("arbitrary",)),
        name="inproj_ctx" if is_ctx else "inproj_lat",
    )(*args)


def _roll_rows(x, d, down):
    return pltpu.roll(x, d if down else SUBLANES - d, axis=0)


def _scan_tile(w, wp, c, cp, cst, row, down):
    x, xp = w, wp
    for si, d in enumerate(SCAN_SHIFTS):
        a, b, bp = cst[3 * si], cst[3 * si + 1], cst[3 * si + 2]
        xs, xps = _roll_rows(x, d, down), _roll_rows(xp, d, down)
        x, xp = x + a * xs + b * xps, xp + a * xps + bp * xs
    a, b, bp = cst[N_SCAN_CONST - 3], cst[N_SCAN_CONST - 2], cst[N_SCAN_CONST - 1]
    s = x + a * c + b * cp
    sp = xp + a * cp + bp * c
    edge = 0 if down else SUBLANES - 1
    last = SUBLANES - 1 if down else 0
    seen = jnp.where(row == edge, c, _roll_rows(s, 1, down))
    c_new = jnp.broadcast_to(s[last:last + 1, :], s.shape)
    cp_new = jnp.broadcast_to(sp[last:last + 1, :], sp.shape)
    return seen, c_new, cp_new


def _ssm_kernel(u_ref, mp_ref, q_ref, cst_ref, dvec_ref, s0_ref, y_ref, fin_ref, w_ref, sq_ref, *, n_batch, n_chunks):
    gb = u_ref.shape[0]
    n_rows = n_batch * n_chunks
    row_blk = min(n_rows, SSM_ROW_BLOCK)
    for g in range(gb):
        for r0 in range(0, n_rows, row_blk):
            rows = slice(r0, r0 + row_blk)
            ug = u_ref[g, rows, :]
            wy = jnp.dot(ug.astype(BF16), mp_ref[g], preferred_element_type=F32)
            y_ref[g, rows, :] = wy[:, :SSM_ROW] + ug * dvec_ref[g]
            w_ref[g, rows, :] = wy[:, SSM_ROW:]

    n_tiles = n_chunks // SUBLANES
    row = lax.broadcasted_iota(jnp.int32, (SUBLANES, LANES), 0)

    def tile_step(b, kt, carry):
        out = []
        for g in range(gb):
            cf, cfp, cb, cbp = carry[4 * g:4 * g + 4]
            rf = pl.multiple_of(b * n_chunks + kt * SUBLANES, SUBLANES)
            rb = pl.multiple_of(b * n_chunks + (n_tiles - 1 - kt) * SUBLANES, SUBLANES)
            wf = w_ref[g, pl.ds(rf, SUBLANES), :]
            wb = w_ref[g, pl.ds(rb, SUBLANES), :]
            cst_f = [cst_ref[g, 0, i] for i in range(N_SCAN_CONST)]
            cst_b = [cst_ref[g, 1, i] for i in range(N_SCAN_CONST)]
            seen_f, cf, cfp = _scan_tile(wf[:, 0:LANES], wf[:, LANES:2 * LANES], cf, cfp, cst_f, row, True)
            seen_b, cb, cbp = _scan_tile(wb[:, 2 * LANES:3 * LANES], wb[:, 3 * LANES:], cb, cbp, cst_b, row, False)
            sq_ref[g, pl.ds(rf, SUBLANES), 0:LANES] = seen_f
            sq_ref[g, pl.ds(rb, SUBLANES), LANES:2 * LANES] = seen_b
            out += [cf, cfp, cb, cbp]
        return tuple(out)

    def batch_body(b, _):
        carry = []
        for g in range(gb):
            for i in range(4):
                carry.append(jnp.broadcast_to(s0_ref[g, i, pl.ds(b, 1), :], (SUBLANES, LANES)))
        carry = tuple(carry)
        if n_tiles <= 2:
            for kt in range(n_tiles):
                carry = tile_step(b, kt, carry)
        else:
            carry = lax.fori_loop(0, n_tiles, lambda kt, c: tile_step(b, kt, c), carry, unroll=2)
        for g in range(gb):
            fin_ref[g, 0, pl.ds(b, 1), :] = carry[4 * g][0:1, :]
            fin_ref[g, 1, pl.ds(b, 1), :] = carry[4 * g + 2][0:1, :]
        return 0

    lax.fori_loop(0, n_batch, batch_body, 0, unroll=2 if n_tiles <= 2 else 1)

    for g in range(gb):
        for r0 in range(0, n_rows, row_blk):
            rows = slice(r0, r0 + row_blk)
            y_ref[g, rows, :] += jnp.dot(sq_ref[g, rows, :].astype(BF16), q_ref[g], preferred_element_type=F32)


def _ssm_core(u, mp, q, cst, dvec, s0, *, l, n_batch, n_chunks):
    n_rows = n_batch * n_chunks
    gb = SSM_GROUP_BLOCK
    g3 = lambda i: (i, 0, 0)
    g4 = lambda i: (i, 0, 0, 0)
    return pl.pallas_call(
        functools.partial(_ssm_kernel, n_batch=n_batch, n_chunks=n_chunks),
        grid=(SSM_GROUPS // gb,),
        in_specs=[
            pl.BlockSpec((gb, n_rows, SSM_ROW), g3),
            pl.BlockSpec((None, gb, SSM_ROW, SSM_ROW + SSM_W_COLS), lambda i: (l, i, 0, 0)),
            pl.BlockSpec((None, gb, 2 * LANES, SSM_ROW), lambda i: (l, i, 0, 0)),
            pl.BlockSpec((None, gb, 2, N_SCAN_CONST, SUBLANES, LANES), lambda i: (l, i, 0, 0, 0, 0)),
            pl.BlockSpec((None, gb, 1, SSM_ROW), lambda i: (l, i, 0, 0)),
            pl.BlockSpec((gb, 4, n_batch, LANES), g4),
        ],
        out_specs=[
            pl.BlockSpec((gb, n_rows, SSM_ROW), g3),
            pl.BlockSpec((gb, 2, n_batch, LANES), g4),
        ],
        out_shape=[
            jax.ShapeDtypeStruct((SSM_GROUPS, n_rows, SSM_ROW), F32),
            jax.ShapeDtypeStruct((SSM_GROUPS, 2, n_batch, LANES), F32),
        ],
        scratch_shapes=[pltpu.VMEM((gb, n_rows, SSM_W_COLS), F32), pltpu.VMEM((gb, n_rows, 2 * LANES), F32)],
        compiler_params=_cparams(("arbitrary",)),
        name="ssm_core",
    )(u, mp, q, cst, dvec, s0)


def _cmul(a, b):
    return a[0] * b[0] - a[1] * b[1], a[0] * b[1] + a[1] * b[0]


_CHUNK_IDX = np.arange(SSM_CHUNK)


def _ssm_exponents():
    t = SSM_CHUNK
    idx = _CHUNK_IDX.astype(np.float32)
    both = lambda a: np.stack([a, a], axis=1)
    lane_step = idx - (t - 1) / 2
    tile_rows = np.arange(1, SUBLANES + 1, dtype=np.float32) * t
    tables = dict(
        one=both(np.ones(1, np.float32)),
        incr=np.stack([t - 1 - idx, idx], axis=1),
        seen=np.stack([idx + 1, t - idx], axis=1),
        resp_in=np.stack([-lane_step, lane_step], axis=1),
        resp_out=np.stack([lane_step, -lane_step], axis=1),
        shift=both(np.asarray(SCAN_SHIFTS, np.float32) * t),
        tile=np.stack([tile_rows, tile_rows[::-1]], axis=1),
    )
    spans, start = {}, 0
    for name, tab in tables.items():
        spans[name] = slice(start, start + len(tab))
        start += len(tab)
    return np.concatenate(list(tables.values()), axis=0), spans


def _ssm_params(p):
    t = SSM_CHUNK
    n_l = p['ssm_lambda_re'].shape[0]
    lam = (p['ssm_lambda_re'].astype(F32), p['ssm_lambda_im'].astype(F32))
    dt = jnp.exp(p['ssm_log_dt'].astype(F32))[..., None]
    arg = (lam[0] * dt, lam[1] * dt)
    expo, span = _ssm_exponents()
    n = jnp.asarray(expo)[:, None, :, None, None]
    mag = jnp.exp(n * arg[0][None])
    pw = (mag * jnp.cos(n * arg[1][None]), mag * jnp.sin(n * arg[1][None]))
    take = lambda name: [x[span[name]] for x in pw]
    lanes_last = lambda xs: [jnp.moveaxis(x, 0, -1) for x in xs]

    lbar = [x[0] for x in take('one')]
    num = (lbar[0] - 1.0, lbar[1])
    den = lam[0] * lam[0] + lam[1] * lam[1]
    zoh = ((num[0] * lam[0] + num[1] * lam[1]) / den, (num[1] * lam[0] - num[0] * lam[1]) / den)
    b_mat = (p['ssm_b_re'].astype(F32), p['ssm_b_im'].astype(F32))
    bbar = _cmul((zoh[0][..., None], zoh[1][..., None]), b_mat)
    c_mat = [jnp.swapaxes(p[k].astype(F32), -1, -2) for k in ('ssm_c_re', 'ssm_c_im')]
    ct = [jnp.tile(x, (1, 1, 1, 1, t)) for x in c_mat]
    bt = [jnp.tile(x, (1, 1, 1, 1, t)) for x in bbar]

    per_lane = lambda name: [jnp.repeat(x, SSM_CH, axis=-1) for x in lanes_last(take(name))]
    xm = _cmul(per_lane('resp_in'), bt)
    ym = _cmul(per_lane('resp_out'), ct)
    resp = jnp.einsum('ldgpx,ldgpy->ldgxy', jnp.concatenate([xm[0], -xm[1]], axis=3),
                      jnp.concatenate([ym[0], ym[1]], axis=3), precision='highest')
    steps_x = np.repeat(_CHUNK_IDX, SSM_CH)
    causal = np.stack([steps_x[None, :] >= steps_x[:, None], steps_x[:, None] >= steps_x[None, :]])
    mats = jnp.sum(jnp.where(jnp.asarray(causal)[None, :, None], resp, 0.0), axis=1)

    incr = _cmul([x[..., None] for x in take('incr')], [x[None] for x in bbar])
    incr = [jnp.transpose(x, (1, 2, 3, 0, 5, 4)) for x in incr]
    pcat = jnp.concatenate([incr[0], incr[1], incr[1], incr[0]], axis=-1)
    pcat = jnp.concatenate([pcat[:, 0], pcat[:, 1]], axis=-1).reshape(n_l, SSM_GROUPS, SSM_ROW, SSM_W_COLS)
    mp = jnp.concatenate([mats, pcat], axis=-1).astype(BF16)

    qm = _cmul(ct, per_lane('seen'))
    q = jnp.concatenate([qm[0][:, 0], -qm[1][:, 0], qm[0][:, 1], -qm[1][:, 1]], axis=2).astype(BF16)

    def forms(name, rows):
        pr, pi = [jnp.moveaxis(x, 0, 3) for x in take(name)]
        f = jnp.stack([jnp.concatenate([pr, pr], -1), jnp.concatenate([-pi, pi], -1),
                       jnp.concatenate([pi, -pi], -1)], axis=4)
        if rows:
            return f
        r = np.arange(SUBLANES)[None, :]
        d = np.repeat(np.asarray(SCAN_SHIFTS), 3)[:, None]
        keep = np.stack([r >= d, r < SUBLANES - d]).astype(np.float32)
        return f.reshape(n_l, 2, SSM_GROUPS, 3 * len(SCAN_SHIFTS), 1, LANES) * keep[None, :, None, :, :, None]
    per_row = jnp.swapaxes(forms('tile', True), 3, 4)
    cst = jnp.swapaxes(jnp.concatenate([forms('shift', False), per_row], axis=3), 1, 2)
    dvec = jnp.tile(p['ssm_d'].astype(F32).reshape(n_l, SSM_GROUPS, 1, SSM_CH), (1, 1, 1, SSM_CHUNK))
    return mp, q, cst, dvec


def _ssm_states_in(state_ssm):
    st = jnp.transpose(state_ssm.astype(F32), (1, 3, 2, 0, 4, 5))
    re, im = st[..., 0], st[..., 1]
    ri, ir = jnp.concatenate([re, im], -1), jnp.concatenate([im, re], -1)
    return jnp.stack([ri[:, :, 0], ir[:, :, 0], ri[:, :, 1], ir[:, :, 1]], axis=2)


def _ssm_states_out(fins):
    fin = jnp.stack(fins, axis=0)
    fin = jnp.stack([fin[..., :SSM_STATE], fin[..., SSM_STATE:]], axis=-1)
    return jnp.transpose(fin, (3, 0, 2, 1, 4, 5))


def _head_lane_mask(hh):
    lane = lax.broadcasted_iota(jnp.int32, (1, LANES), 1)
    return (lane >= hh * NA_HEAD_DIM) & (lane < (hh + 1) * NA_HEAD_DIM)


_NT_DIMS = (((1,), (1,)), ((), ()))


def _ctx_attn_kernel(q_ref, k_ref, v_ref, o_ref):
    lm0 = _head_lane_mask(0)
    for b in range(CTX_ATTN_BATCH):
        rows = slice(b * SEQ, (b + 1) * SEQ)
        for p in range(HEAD_PAIRS):
            qp, kp, vp = q_ref[p, rows, :], k_ref[p, rows, :], v_ref[p, rows, :]
            zero = jnp.zeros_like(qp)
            q2 = jnp.concatenate([jnp.where(lm0, qp, zero), jnp.where(lm0, zero, qp)], axis=0)
            s = lax.dot_general(q2, kp, _NT_DIMS, preferred_element_type=F32)
            e = jnp.exp(s - jnp.max(s, axis=-1, keepdims=True))
            o = jnp.dot(e.astype(BF16), vp, preferred_element_type=F32) / jnp.sum(e, axis=-1, keepdims=True)
            o_ref[rows, p * LANES:(p + 1) * LANES] = jnp.where(lm0, o[:SEQ, :], o[SEQ:, :]).astype(BF16)


def _ctx_attn(q, k, v, n_batch):
    rows = CTX_ATTN_BATCH * SEQ
    spec = pl.BlockSpec((HEAD_PAIRS, rows, LANES), lambda b: (0, b, 0))
    return pl.pallas_call(
        _ctx_attn_kernel,
        grid=(n_batch // CTX_ATTN_BATCH,),
        in_specs=[spec, spec, spec],
        out_specs=pl.BlockSpec((rows, NA_WIDTH), lambda b: (b, 0)),
        out_shape=jax.ShapeDtypeStruct((n_batch * SEQ, NA_WIDTH), BF16),
        compiler_params=_cparams(("arbitrary",)),
        name="ctx_attn",
    )(q, k, v)


def _na_kernel(q_ref, k_ref, v_ref, kc_ref, vc_ref, t2_ref, rm_ref, o_ref, qb_ref, os_ref, tb_ref):
    rt = pl.program_id(1)
    n_rt = pl.num_programs(1)

    @pl.when((pl.program_id(0) == 0) & (rt == 0))
    def _():
        for h in range(NA_HEADS):
            for n in range(NA_COL_BLOCKS):
                for rl in range(NA_Q_ROWS):
                    lane0 = (NA_OFF_START - rl) * NA_KCOLS
                    tb_ref[h, n, rl * NA_COLS:(rl + 1) * NA_COLS, :] = (
                        t2_ref[h, n, :, lane0:lane0 + NA_K_ROWS * NA_KCOLS])

    r0 = rt * NA_Q_ROWS
    ttype = jnp.where(rt == 0, 0, jnp.where(rt == n_rt - 1, 2, 1))
    rmask = rm_ref[ttype]
    blk_q = NA_Q_ROWS * NA_COLS

    def pair_body(p, carry):
        lm0 = _head_lane_mask(0)
        for n in range(NA_COL_BLOCKS):
            for rl in range(NA_Q_ROWS):
                qrow = q_ref[p, rl * GRID_W + n * NA_COLS:rl * GRID_W + (n + 1) * NA_COLS, :]
                zero = jnp.zeros_like(qrow)
                base = 2 * n * blk_q + rl * NA_COLS
                qb_ref[base:base + NA_COLS, :] = jnp.where(lm0, qrow, zero)
                qb_ref[base + blk_q:base + blk_q + NA_COLS, :] = jnp.where(lm0, zero, qrow)
        kcp, vcp = kc_ref[0, p], vc_ref[0, p]
        n_win = NA_K_ROWS * NA_KCOLS
        for n in range(NA_COL_BLOCKS):
            kparts, vparts = [], []
            for i in range(NA_K_ROWS):
                kr = jnp.clip(r0 - (NA_K_ROWS - NA_Q_ROWS) // 2 + i, 0, GRID_W - 1)
                start = pl.multiple_of(kr * GRID_W + NA_KC0[n], 8)
                kparts.append(k_ref[p, pl.ds(start, NA_KCOLS), :])
                vparts.append(v_ref[p, pl.ds(start, NA_KCOLS), :])
            kall = jnp.concatenate([jnp.concatenate(kparts, axis=0).astype(BF16), kcp], axis=0)
            vall = jnp.concatenate([jnp.concatenate(vparts, axis=0).astype(BF16), vcp], axis=0)
            rows2 = slice(2 * n * blk_q, 2 * (n + 1) * blk_q)
            s = lax.dot_general(qb_ref[rows2, :], kall, _NT_DIMS, preferred_element_type=F32)
            bias = jnp.concatenate([tb_ref[2 * p, n] + rmask, tb_ref[2 * p + 1, n] + rmask], axis=0)
            s = jnp.concatenate([s[:, :n_win] + bias, s[:, n_win:]], axis=-1)
            e = jnp.exp(s - jnp.max(s, axis=-1, keepdims=True))
            o = jnp.dot(e.astype(BF16), vall, preferred_element_type=F32) / jnp.sum(e, axis=-1, keepdims=True)
            oblk = jnp.where(lm0, o[:blk_q, :], o[blk_q:, :])
            for rl in range(NA_Q_ROWS):
                os_ref[p, rl * GRID_W + n * NA_COLS:rl * GRID_W + (n + 1) * NA_COLS, :] = (
                    oblk[rl * NA_COLS:(rl + 1) * NA_COLS, :])
        return carry

    lax.fori_loop(0, HEAD_PAIRS, pair_body, 0, unroll=True)
    for p in range(HEAD_PAIRS):
        o_ref[:, p * LANES:(p + 1) * LANES] = os_ref[p].astype(BF16)


def _na_attn(q, k, v, kc, vc, tb, rm, n_batch, *, l):
    tile_tok = NA_Q_ROWS * GRID_W
    n_rt = DEC_SEQ // tile_tok
    blk_q, blk_k = NA_Q_ROWS * NA_COLS, NA_K_ROWS * NA_KCOLS
    return pl.pallas_call(
        _na_kernel,
        grid=(n_batch, n_rt),
        in_specs=[
            pl.BlockSpec((HEAD_PAIRS, tile_tok, LANES), lambda b, r: (0, b * n_rt + r, 0)),
            pl.BlockSpec((HEAD_PAIRS, DEC_SEQ, LANES), lambda b, r: (0, b, 0)),
            pl.BlockSpec((HEAD_PAIRS, DEC_SEQ, LANES), lambda b, r: (0, b, 0)),
            pl.BlockSpec((1, None, HEAD_PAIRS, PAST_LEN, LANES), lambda b, r: (b, l, 0, 0, 0)),
            pl.BlockSpec((1, None, HEAD_PAIRS, PAST_LEN, LANES), lambda b, r: (b, l, 0, 0, 0)),
            _layer_spec(tb, l),
            _const_spec(rm.shape),
        ],
        out_specs=pl.BlockSpec((tile_tok, NA_WIDTH), lambda b, r: (b * n_rt + r, 0)),
        out_shape=jax.ShapeDtypeStruct((n_batch * DEC_SEQ, NA_WIDTH), BF16),
        scratch_shapes=[
            pltpu.VMEM((2 * tile_tok, LANES), BF16),
            pltpu.VMEM((HEAD_PAIRS, tile_tok, LANES), F32),
            pltpu.VMEM((NA_HEADS, NA_COL_BLOCKS, blk_q, blk_k), F32)],
        compiler_params=pltpu.CompilerParams(dimension_semantics=("arbitrary", "arbitrary"),
                                             vmem_limit_bytes=NA_VMEM_LIMIT_BYTES),
        name="na_attn",
    )(q, k, v, kc, vc, tb, rm)


def _na_tables(rpb):
    n = np.arange(NA_COL_BLOCKS)
    cc = np.arange(NA_COLS)
    kk = np.arange(NA_KCOLS)
    rl = np.arange(NA_Q_ROWS)
    ki = np.arange(NA_K_ROWS)
    n_dr, n_dc = 2 * NA_MAX_ROWS - 1, 2 * NA_COLS - 1
    c = n[:, None] * NA_COLS + cc[None, :]
    kcol = np.asarray(NA_KC0)[:, None] + kk[None, :]
    dc = np.clip(kcol[:, None, :] - c[:, :, None], -(NA_COLS - 1), NA_COLS - 1) + (NA_COLS - 1)
    cs = np.clip(c - NA_COLS // 2, 0, GRID_W - NA_COLS)
    col_ok = (kcol[:, None, :] >= cs[:, :, None]) & (kcol[:, None, :] < cs[:, :, None] + NA_COLS)
    half = (NA_K_ROWS - NA_Q_ROWS) // 2
    dr = np.clip(ki[None, :] - half - rl[:, None] + (NA_MAX_ROWS - 1), 0, n_dr - 1)
    oh_dc = jnp.asarray(dc[None] == np.arange(n_dc)[:, None, None, None], F32)
    off = np.clip(np.arange(NA_OFF_ROWS) - NA_OFF_PAD, 0, n_dr - 1)
    assert np.array_equal(dr, off[ki[None] + (NA_OFF_START - rl)[:, None]])
    oh_off = jnp.asarray(off[None, :] == np.arange(n_dr)[:, None], F32)
    rows_p = jnp.einsum('lhab,aA->lhAb', rpb.astype(F32), oh_off, precision='highest')
    cols = jnp.einsum('lhAb,bnck->lhncAk', rows_p, oh_dc, precision='highest')
    cols = jnp.where(jnp.asarray(col_ok)[None, None, :, :, None, :], cols, NEG_INF)
    tb = cols.reshape(-1, NA_HEADS, NA_COL_BLOCKS, NA_COLS, NA_OFF_ROWS * NA_KCOLS)
    rows = GRID_W
    masks = []
    for r0 in (0, NA_Q_ROWS, rows - NA_Q_ROWS):
        r = r0 + rl
        rs = np.clip(r - NA_MAX_ROWS // 2, 0, rows - NA_MAX_ROWS)
        kr = r0 - half + ki
        ok = (kr[None, :] >= rs[:, None]) & (kr[None, :] < rs[:, None] + NA_MAX_ROWS)
        full = np.broadcast_to(ok[:, None, :, None], (NA_Q_ROWS, NA_COLS, NA_K_ROWS, NA_KCOLS))
        masks.append(np.where(full, 0.0, NEG_INF).reshape(NA_Q_ROWS * NA_COLS, NA_K_ROWS * NA_KCOLS))
    rm = jnp.asarray(np.stack(masks), F32)
    return tb, rm


def _outffn_kernel(x_ref, mod_ref, y_ref, na_ref, gm_ref, gw_ref, gb_ref, wo_ref, g_ref, win_ref, wout_ref, o_ref,
                   yr_ref, ys_ref, cat_ref, xm_ref, h_ref, act_ref):
    gate = mod_ref[0, 5:6, :]
    n_rows = OUT_TILE // SSM_CHUNK
    for g in range(SSM_GROUPS):
        yr_ref[g] = _roll_lanes(y_ref[g], SSM_CH * g)
    for t in range(SSM_CHUNK):
        yt = _merge_segments(lambda src, lt: yr_ref[src, :, lt * LANES:(lt + 1) * LANES], t)
        yt = _roll_lanes(yt, -SSM_CH * t)
        for j in range(SSM_WIDTH // LANES):
            ys_ref[j, pl.ds(t, n_rows, stride=SSM_CHUNK), :] = yt[:, j * LANES:(j + 1) * LANES]
    y = _gelu(jnp.concatenate([ys_ref[j] for j in range(SSM_WIDTH // LANES)], axis=-1))
    z = jnp.dot(y.astype(BF16), gw_ref[...], preferred_element_type=F32) + gb_ref[...]
    cat_ref[:, 0:SSM_WIDTH] = (y * _sigmoid(z)).astype(BF16)
    cat_ref[:, SSM_WIDTH:SSM_WIDTH + NA_WIDTH] = na_ref[...]
    cat_ref[:, SSM_WIDTH + NA_WIDTH:] = gm_ref[...]
    mix = jnp.dot(cat_ref[...], wo_ref[...], preferred_element_type=F32)
    xm_ref[...] = x_ref[...] + gate * mix
    _ffn_kernel(xm_ref, mod_ref, g_ref, win_ref, wout_ref, o_ref, h_ref, act_ref, mod_base=6)


def _outffn(x, mod, y, na, gm, gw, gb, wo, g, w_in, w_out, *, l, seg0, seg_tokens):
    n_tok = x.shape[0]
    n_rows = OUT_TILE // SSM_CHUNK
    tok = lambda width: pl.BlockSpec((OUT_TILE, width), lambda i: (i, 0))
    return pl.pallas_call(
        _outffn_kernel,
        grid=(n_tok // OUT_TILE,),
        in_specs=[
            tok(D_MODEL),
            _mod_spec(l, seg0, seg_tokens, OUT_TILE),
            pl.BlockSpec((SSM_GROUPS, n_rows, SSM_ROW), lambda i: (0, i, 0)),
            tok(NA_WIDTH), tok(GM_WIDTH),
            _layer_spec(gw, l), _layer_spec(gb, l), _layer_spec(wo, l),
            _layer_spec(g, l), _layer_spec(w_in, l), _layer_spec(w_out, l),
        ],
        out_specs=tok(D_MODEL),
        out_shape=jax.ShapeDtypeStruct((n_tok, D_MODEL), F32),
        scratch_shapes=[pltpu.VMEM((SSM_GROUPS, n_rows, SSM_ROW), F32),
                        pltpu.VMEM((SSM_WIDTH // LANES, OUT_TILE, LANES), F32),
                        pltpu.VMEM((OUT_TILE, D_MODEL), BF16),
                        pltpu.VMEM((OUT_TILE, D_MODEL), F32),
                        pltpu.VMEM((OUT_TILE, D_MODEL), BF16), pltpu.VMEM((OUT_TILE, D_FF), BF16)],
        compiler_params=_cparams(("arbitrary",)),
        name="outffn",
    )(x, mod, y, na, gm, gw, gb, wo, g, w_in, w_out)


def _prep(p):
    heads = np.arange(MXU_DIM) // NA_HEAD_DIM
    hsum = jnp.asarray((heads[:, None] == heads[None, :]) / NA_HEAD_DIM, BF16)
    gmb = jnp.repeat(jnp.swapaxes(p['gm_bs'], 1, 2).astype(F32), GM_WIDTH // GM_GROUPS, axis=2)
    tb, rm = _na_tables(p['na_rpb'])
    row = lambda a: a.astype(F32).reshape(DEPTH, 1, -1)
    return dict(
        g1=row(p['norm_ffn1']), g2=row(p['norm_mix']), g3=row(p['norm_ffn2']),
        f1_in=p['ffn1_w_in'].astype(BF16), f1_out=p['ffn1_w_out'].astype(BF16),
        f2_in=p['ffn2_w_in'].astype(BF16), f2_out=p['ffn2_w_out'].astype(BF16),
        w_in=p['w_in'].astype(BF16), w_out=p['w_out'].astype(BF16),
        hsum=hsum,
        qg=row(jnp.tile(p['na_q_norm'], (1, NA_HEADS))), kg=row(jnp.tile(p['na_k_norm'], (1, NA_HEADS))),
        ws=p['gm_ws'].astype(BF16), gmb=gmb,
        ssm=_ssm_params(p),
        glu_w=p['ssm_glu_w'].astype(BF16), glu_b=row(p['ssm_glu_b']),
        tb=tb, rm=rm,
    )


def _pairs_major(t):
    b, n_l, s = t.shape[:3]
    return jnp.transpose(t.reshape(b, n_l, s, HEAD_PAIRS, LANES), (0, 1, 3, 2, 4)).astype(BF16)


def _trunk_layer(x, mod, w, l, *, n_batch, seq_len, caches=None, is_ctx=False, ctx_kv=None, ssm_init=None):
    seg = dict(l=l, seg0=0 if is_ctx else 1, seg_tokens=x.shape[0] if is_ctx else seq_len)
    x = _ffn(x, mod, w['g1'], w['f1_in'], w['f1_out'], mod_base=0, **seg)
    outs = _inproj(x, mod, w['g2'], w['w_in'], w['hsum'], w['qg'], w['kg'], w['ws'], w['gmb'],
                   is_ctx=is_ctx, caches=caches, **seg)
    u, q, k, v, gm = outs[:5]
    s0 = jnp.zeros((SSM_GROUPS, 4, n_batch, LANES), F32) if is_ctx else ssm_init[l]
    y, fin = _ssm_core(u, *w['ssm'], s0, l=l, n_batch=n_batch, n_chunks=seq_len // SSM_CHUNK)
    if is_ctx:
        na = _ctx_attn(q, k, v, n_batch)
    else:
        na = _na_attn(q, k, v, ctx_kv[0], ctx_kv[1], w['tb'], w['rm'], n_batch, l=l)
    x = _outffn(x, mod, y, na, gm, w['glu_w'], w['glu_b'], w['w_out'], w['g3'], w['f2_in'], w['f2_out'], **seg)
    return x, (tuple(outs[5:]), fin)


def kernel(x_prompt, x_sample, c, cache_k, cache_v, state_ssm, c_ctx, w_ada, b_ada, norm_ffn1, ffn1_w_in, ffn1_w_out, norm_mix, w_in, w_out, ssm_lambda_re, ssm_lambda_im, ssm_log_dt, ssm_b_re, ssm_b_im, ssm_c_re, ssm_c_im, ssm_d, ssm_glu_w, ssm_glu_b, na_q_norm, na_k_norm, na_rpb, gm_ws, gm_bs, norm_ffn2, ffn2_w_in, ffn2_w_out):
    p = dict(norm_ffn1=norm_ffn1, ffn1_w_in=ffn1_w_in, ffn1_w_out=ffn1_w_out, norm_mix=norm_mix, w_in=w_in,
             w_out=w_out, ssm_lambda_re=ssm_lambda_re, ssm_lambda_im=ssm_lambda_im, ssm_log_dt=ssm_log_dt,
             ssm_b_re=ssm_b_re, ssm_b_im=ssm_b_im, ssm_c_re=ssm_c_re, ssm_c_im=ssm_c_im, ssm_d=ssm_d,
             ssm_glu_w=ssm_glu_w, ssm_glu_b=ssm_glu_b, na_q_norm=na_q_norm, na_k_norm=na_k_norm, na_rpb=na_rpb,
             gm_ws=gm_ws, gm_bs=gm_bs, norm_ffn2=norm_ffn2, ffn2_w_in=ffn2_w_in, ffn2_w_out=ffn2_w_out)
    batch, dec_batch = x_prompt.shape[0], x_sample.shape[0]
    cond8 = jnp.zeros((8, D_MODEL), F32).at[0].set(c_ctx).at[1:1 + dec_batch].set(c)
    mod = _adaln(cond8, w_ada, b_ada).reshape(DEPTH, 8, N_MOD, D_MODEL)

    xp = x_prompt.reshape(batch * SEQ, D_MODEL)
    xs = x_sample.reshape(dec_batch * DEC_SEQ, D_MODEL)
    w = _prep(p)
    ctx_kv = (_pairs_major(cache_k), _pairs_major(cache_v))
    ssm_init = _ssm_states_in(state_ssm)
    caches, fins = None, []
    for l in range(DEPTH):
        xp, (caches, fin) = _trunk_layer(xp, mod, w, l, n_batch=batch, seq_len=SEQ, caches=caches, is_ctx=True)
        fins.append(fin)
        xs, _ = _trunk_layer(xs, mod, w, l, n_batch=dec_batch, seq_len=DEC_SEQ, ctx_kv=ctx_kv, ssm_init=ssm_init)
    cache_shape = (batch, DEPTH, SEQ, NA_HEADS, NA_HEAD_DIM)
    return (xp.reshape(batch, SEQ, D_MODEL), xs.reshape(dec_batch, DEC_SEQ, D_MODEL),
            caches[0].reshape(cache_shape), caches[1].reshape(cache_shape), _ssm_states_out(fins))
```

```python
import functools

import numpy as np
import jax
import jax.numpy as jnp
from jax import lax
from jax.experimental import pallas as pl
from jax.experimental.pallas import tpu as pltpu

D_MODEL = 1024
DEPTH = 2
SEQ = 256
DEC_SEQ = 4096
PAST_LEN = 256
GRID_W = 64
SSM_WIDTH = 256
SSM_CH = 16
SSM_GROUPS = 16
SSM_STATE = 64
NA_WIDTH = 512
NA_HEAD_DIM = 64
NA_HEADS = 8
NA_MAX_ROWS = 8
NA_COLS = 16
GM_WIDTH = 256
GM_GROUPS = 4
GM_CHUNK = 128
D_FF = 2816
N_MOD = 9
RMS_EPS = 1e-6
LN_EPS = 1e-5
NEG_INF = -1e30
LOG2_E = 1.4426950408889634

F32 = jnp.float32
BF16 = jnp.bfloat16

LANES = 128
SUBLANES = 8
MXU_DIM = 256
VMEM_LIMIT_BYTES = 56 * 1024 * 1024
NA_VMEM_LIMIT_BYTES = 62 * 1024 * 1024

FFN_TILE = 1024
MIX_TILE = 1024
OUT_TILE = 512
FF_CHUNK = MXU_DIM
N_FF_CHUNKS = D_FF // FF_CHUNK
SSM_CHUNK = 16
SSM_ROW = SSM_CHUNK * SSM_CH
SSM_GROUP_BLOCK = 4
SSM_ROW_BLOCK = 512
SSM_W_COLS = 4 * LANES
SCAN_SHIFTS = (1, 2, 4)
N_SCAN_CONST = 3 * len(SCAN_SHIFTS) + 3
HEAD_PAIRS = NA_HEADS // 2
CTX_ATTN_BATCH = 4
NA_Q_ROWS = 8
NA_K_ROWS = 16
NA_KCOLS = 2 * NA_COLS
NA_COL_BLOCKS = GRID_W // NA_COLS
NA_KC0 = (0, 8, 24, 32)
SEGS_PER_TILE = LANES // SSM_CH
NA_OFF_PAD = (NA_Q_ROWS - 1) - ((NA_MAX_ROWS - 1) - (NA_K_ROWS - NA_Q_ROWS) // 2)
NA_OFF_START = NA_Q_ROWS - 1
NA_OFF_ROWS = NA_OFF_START + NA_K_ROWS


def _silu(x):
    return x * (1.0 / (1.0 + jnp.exp(-x)))


def _sigmoid(x):
    return 1.0 / (1.0 + jnp.exp(-x))


def _gelu(x):
    return 0.5 * x * (1.0 + jnp.tanh(0.7978845608028654 * (x + 0.044715 * (x * x * x))))


def _cparams(sem):
    return pltpu.CompilerParams(dimension_semantics=sem, vmem_limit_bytes=VMEM_LIMIT_BYTES)


def _const_spec(shape):
    nd = len(shape)
    return pl.BlockSpec(shape, lambda *_: (0,) * nd, pipeline_mode=pl.Buffered(1))


def _layer_spec(stacked, l):
    nd = stacked.ndim
    return pl.BlockSpec((None,) + stacked.shape[1:], lambda *_: (l,) + (0,) * (nd - 1), pipeline_mode=pl.Buffered(1))


def _ada_kernel(c_ref, w_ref, b_ref, o_ref):
    s = _silu(c_ref[...]).astype(BF16)
    w = w_ref[0].astype(BF16)
    o_ref[0] = jnp.dot(s, w, preferred_element_type=F32) + b_ref[0]


def _adaln(cond8, w_ada, b_ada):
    tn = D_MODEL
    ncol = N_MOD * D_MODEL
    return pl.pallas_call(
        _ada_kernel,
        grid=(DEPTH, ncol // tn),
        in_specs=[
            pl.BlockSpec((8, D_MODEL), lambda l, j: (0, 0)),
            pl.BlockSpec((1, D_MODEL, tn), lambda l, j: (l, 0, j)),
            pl.BlockSpec((1, 1, tn), lambda l, j: (l, 0, j)),
        ],
        out_specs=pl.BlockSpec((1, 8, tn), lambda l, j: (l, 0, j)),
        out_shape=jax.ShapeDtypeStruct((DEPTH, 8, ncol), F32),
        compiler_params=_cparams(("arbitrary", "arbitrary")),
        name="adaln",
    )(cond8, w_ada, b_ada.reshape(DEPTH, 1, ncol))


def _normed(x, g_row, shift, scale):
    ms = jnp.mean(x * x, axis=-1, keepdims=True)
    h = x * lax.rsqrt(ms + RMS_EPS) * g_row
    return h * (1.0 + scale) + shift


def _mod_spec(l, seg0, seg_tokens, tile):
    tiles_per_seg = seg_tokens // tile
    return pl.BlockSpec((None, 1, N_MOD, D_MODEL), lambda i: (l, seg0 + i // tiles_per_seg, 0, 0))


def _ffn_kernel(x_ref, mod_ref, g_ref, win_ref, wout_ref, o_ref, h_ref, act_ref, *, mod_base):
    shift = mod_ref[0, mod_base:mod_base + 1, :]
    scale = mod_ref[0, mod_base + 1:mod_base + 2, :]
    gate = mod_ref[0, mod_base + 2:mod_base + 3, :]
    h_ref[...] = _normed(x_ref[...], g_ref[...], shift, scale).astype(BF16)
    for j in range(N_FF_CHUNKS):
        cols = slice(j * FF_CHUNK, (j + 1) * FF_CHUNK)
        gate_j = jnp.dot(h_ref[...], win_ref[:, cols], preferred_element_type=F32)
        up_j = jnp.dot(h_ref[...], win_ref[:, D_FF + j * FF_CHUNK:D_FF + (j + 1) * FF_CHUNK], preferred_element_type=F32)
        act_ref[:, cols] = (_silu(gate_j) * up_j).astype(BF16)
    for n in range(D_MODEL // MXU_DIM):
        cols = slice(n * MXU_DIM, (n + 1) * MXU_DIM)
        y = jnp.dot(act_ref[...], wout_ref[:, cols], preferred_element_type=F32)
        o_ref[:, cols] = x_ref[:, cols] + 0.5 * gate[:, cols] * y


def _ffn(x, mod, g, w_in, w_out, *, l, mod_base, seg0, seg_tokens):
    n_tok = x.shape[0]
    return pl.pallas_call(
        functools.partial(_ffn_kernel, mod_base=mod_base),
        grid=(n_tok // FFN_TILE,),
        in_specs=[
            pl.BlockSpec((FFN_TILE, D_MODEL), lambda i: (i, 0)),
            _mod_spec(l, seg0, seg_tokens, FFN_TILE),
            _layer_spec(g, l), _layer_spec(w_in, l), _layer_spec(w_out, l),
        ],
        out_specs=pl.BlockSpec((FFN_TILE, D_MODEL), lambda i: (i, 0)),
        out_shape=jax.ShapeDtypeStruct((n_tok, D_MODEL), F32),
        scratch_shapes=[pltpu.VMEM((FFN_TILE, D_MODEL), BF16), pltpu.VMEM((FFN_TILE, D_FF), BF16)],
        compiler_params=_cparams(("arbitrary",)),
        name="ffn",
    )(x, mod, g, w_in, w_out)


def _segment_ids():
    return lax.broadcasted_iota(jnp.int32, (1, LANES), 1) // SSM_CH


def _merge_segments(load_piece, rot):
    seg = _segment_ids()
    tiles = []
    for lt in range(SSM_ROW // LANES):
        acc = None
        for jj in range(SEGS_PER_TILE):
            src = (lt * SEGS_PER_TILE + jj - rot) % SSM_CHUNK
            piece = load_piece(src, lt)
            acc = piece if acc is None else jnp.where(seg == jj, piece, acc)
        tiles.append(acc)
    return jnp.concatenate(tiles, axis=-1)


def _roll_lanes(x, shift):
    shift %= x.shape[-1]
    return pltpu.roll(x, shift, axis=x.ndim - 1) if shift else x


def _inproj_kernel(*refs, n_in, n_out):
    x_ref, mod_ref, g_ref, w_ref, hsum_ref, qg_ref, kg_ref, ws_ref, gmb_ref = refs[:9]
    u_ref, q_ref, k_ref, v_ref, gm_ref, *tok_refs = refs[n_in:n_in + n_out]
    h_ref, xs_ref, z_ref = refs[n_in + n_out:]
    shift = mod_ref[0, 3:4, :]
    scale = mod_ref[0, 4:5, :]
    h_ref[...] = _normed(x_ref[...], g_ref[...], shift, scale).astype(BF16)
    n_rows = MIX_TILE // SSM_CHUNK

    def proj(c0, width):
        return jnp.dot(h_ref[...], w_ref[:, c0:c0 + width], preferred_element_type=F32)

    xs_qk = proj(0, SSM_WIDTH + 2 * NA_WIDTH)
    for j in range(SSM_WIDTH // LANES):
        xs_ref[j] = xs_qk[:, j * LANES:(j + 1) * LANES]
    for t in range(SSM_CHUNK):
        zt = jnp.concatenate([xs_ref[j, pl.ds(t, n_rows, stride=SSM_CHUNK), :] for j in range(SSM_WIDTH // LANES)],
                             axis=-1)
        z_ref[t] = _roll_lanes(zt, SSM_CH * t)
    for g in range(SSM_GROUPS):
        ug = _merge_segments(lambda src, lt: z_ref[src, :, lt * LANES:(lt + 1) * LANES], g)
        u_ref[g] = _roll_lanes(ug, -SSM_CH * g)

    c0 = SSM_WIDTH
    qk = xs_qk[:, c0:]
    sq = (qk * qk).astype(BF16)
    groups = range(0, 2 * NA_WIDTH, MXU_DIM)
    ms = jnp.dot(jnp.concatenate([sq[:, c:c + MXU_DIM] for c in groups], axis=0), hsum_ref[...],
                 preferred_element_type=F32)
    ms = jnp.concatenate([ms[i * MIX_TILE:(i + 1) * MIX_TILE, :] for i in range(len(groups))], axis=-1)
    qk = qk * lax.rsqrt(ms + RMS_EPS)
    q = qk[:, :NA_WIDTH] * qg_ref[...] * (NA_HEAD_DIM ** -0.5 * LOG2_E)
    k = qk[:, NA_WIDTH:] * kg_ref[...]
    v_uv = proj(c0 + 2 * NA_WIDTH, NA_WIDTH + 2 * GM_WIDTH)
    v = v_uv[:, :NA_WIDTH]
    for p in range(HEAD_PAIRS):
        lanes = slice(p * LANES, (p + 1) * LANES)
        q_ref[p] = q[:, lanes].astype(q_ref.dtype)
        k_ref[p] = k[:, lanes].astype(k_ref.dtype)
        v_ref[p] = v[:, lanes].astype(v_ref.dtype)
    if tok_refs:
        for slot in range(tok_refs[0].shape[1]):
            tok_refs[0][:, slot] = k.reshape(MIX_TILE // SEQ, SEQ, NA_WIDTH)
            tok_refs[1][:, slot] = v.reshape(MIX_TILE // SEQ, SEQ, NA_WIDTH)

    a = _gelu(v_uv[:, NA_WIDTH:])
    u = a[:, :GM_WIDTH]
    vv = a[:, GM_WIDTH:]
    mu = jnp.mean(vv, axis=-1, keepdims=True)
    vc = vv - mu
    var = jnp.mean(vc * vc, axis=-1, keepdims=True)
    vln = (vc * lax.rsqrt(var + LN_EPS)).astype(BF16)
    lane = lax.broadcasted_iota(jnp.int32, (1, GM_WIDTH), 1)
    gw = GM_WIDTH // GM_GROUPS
    for ci in range(MIX_TILE // GM_CHUNK):
        rows = slice(ci * GM_CHUNK, (ci + 1) * GM_CHUNK)
        vch = vln[rows, :]
        sp = gmb_ref[...]
        for gi in range(GM_GROUPS):
            t = jnp.dot(ws_ref[gi], vch, preferred_element_type=F32)
            sp = sp + jnp.where((lane >= gi * gw) & (lane < (gi + 1) * gw), t, 0.0)
        gm_ref[rows, :] = (u[rows, :] * sp).astype(BF16)


def _inproj(x, mod, g, w, hsum, qg, kg, ws, gmb, *, l, seg0, seg_tokens, is_ctx, caches=None):
    n_tok = x.shape[0]
    kv_dtype = BF16 if is_ctx else F32
    n_rows = MIX_TILE // SSM_CHUNK
    pair_spec = pl.BlockSpec((HEAD_PAIRS, MIX_TILE, LANES), lambda i: (0, i, 0))
    out_specs = [
        pl.BlockSpec((SSM_GROUPS, n_rows, SSM_ROW), lambda i: (0, i, 0)),
        pair_spec, pair_spec, pair_spec,
        pl.BlockSpec((MIX_TILE, GM_WIDTH), lambda i: (i, 0)),
    ]
    out_shape = [
        jax.ShapeDtypeStruct((SSM_GROUPS, n_tok // SSM_CHUNK, SSM_ROW), F32),
        jax.ShapeDtypeStruct((HEAD_PAIRS, n_tok, LANES), BF16),
        jax.ShapeDtypeStruct((HEAD_PAIRS, n_tok, LANES), kv_dtype),
        jax.ShapeDtypeStruct((HEAD_PAIRS, n_tok, LANES), kv_dtype),
        jax.ShapeDtypeStruct((n_tok, GM_WIDTH), BF16),
    ]
    args = [x, mod, g, w, hsum, qg, kg, ws, gmb]
    in_specs = [
        pl.BlockSpec((MIX_TILE, D_MODEL), lambda i: (i, 0)),
        _mod_spec(l, seg0, seg_tokens, MIX_TILE),
        _layer_spec(g, l), _layer_spec(w, l),
        _const_spec((MXU_DIM, MXU_DIM)),
        _layer_spec(qg, l), _layer_spec(kg, l), _layer_spec(ws, l), _layer_spec(gmb, l),
    ]
    aliases = {}
    if is_ctx:
        bt = MIX_TILE // SEQ
        cache_shape = jax.ShapeDtypeStruct((n_tok // SEQ, DEPTH, SEQ, NA_WIDTH), F32)
        if caches is None:
            out_specs += [pl.BlockSpec((bt, DEPTH, SEQ, NA_WIDTH), lambda i: (i, 0, 0, 0))] * 2
        else:
            out_specs += [pl.BlockSpec((bt, 1, SEQ, NA_WIDTH), lambda i: (i, l, 0, 0))] * 2
        out_shape += [cache_shape] * 2
        if caches is not None:
            aliases = {len(args): len(out_shape) - 2, len(args) + 1: len(out_shape) - 1}
            args += list(caches)
            in_specs += [pl.BlockSpec(memory_space=pl.ANY)] * 2
    return pl.pallas_call(
        functools.partial(_inproj_kernel, n_in=len(args), n_out=len(out_shape)),
        grid=(n_tok // MIX_TILE,),
        in_specs=in_specs,
        out_specs=out_specs,
        out_shape=out_shape,
        input_output_aliases=aliases,
        scratch_shapes=[pltpu.VMEM((MIX_TILE, D_MODEL), BF16), pltpu.VMEM((SSM_WIDTH // LANES, MIX_TILE, LANES), F32),
                        pltpu.VMEM((SSM_CHUNK, n_rows, SSM_ROW), F32)],
        compiler_params=_cparams(("arbitrary",)),
        name="inproj_ctx" if is_ctx else "inproj_lat",
    )(*args)


def _roll_rows(x, d, down):
    return pltpu.roll(x, d if down else SUBLANES - d, axis=0)


def _scan_tile(w, wp, c, cp, cst, row, down):
    x, xp = w, wp
    for si, d in enumerate(SCAN_SHIFTS):
        a, b, bp = cst[3 * si], cst[3 * si + 1], cst[3 * si + 2]
        xs, xps = _roll_rows(x, d, down), _roll_rows(xp, d, down)
        x, xp = x + a * xs + b * xps, xp + a * xps + bp * xs
    a, b, bp = cst[N_SCAN_CONST - 3], cst[N_SCAN_CONST - 2], cst[N_SCAN_CONST - 1]
    s = x + a * c + b * cp
    sp = xp + a * cp + bp * c
    edge = 0 if down else SUBLANES - 1
    last = SUBLANES - 1 if down else 0
    seen = jnp.where(row == edge, c, _roll_rows(s, 1, down))
    c_new = jnp.broadcast_to(s[last:last + 1, :], s.shape)
    cp_new = jnp.broadcast_to(sp[last:last + 1, :], sp.shape)
    return seen, c_new, cp_new


def _ssm_kernel(u_ref, mp_ref, q_ref, cst_ref, dvec_ref, s0_ref, y_ref, fin_ref, w_ref, sq_ref, *, n_batch, n_chunks):
    gb = u_ref.shape[0]
    n_rows = n_batch * n_chunks
    row_blk = min(n_rows, SSM_ROW_BLOCK)
    for g in range(gb):
        for r0 in range(0, n_rows, row_blk):
            rows = slice(r0, r0 + row_blk)
            ug = u_ref[g, rows, :]
            wy = jnp.dot(ug.astype(BF16), mp_ref[g], preferred_element_type=F32)
            y_ref[g, rows, :] = wy[:, :SSM_ROW] + ug * dvec_ref[g]
            w_ref[g, rows, :] = wy[:, SSM_ROW:]

    n_tiles = n_chunks // SUBLANES
    row = lax.broadcasted_iota(jnp.int32, (SUBLANES, LANES), 0)

    def tile_step(b, kt, carry):
        out = []
        for g in range(gb):
            cf, cfp, cb, cbp = carry[4 * g:4 * g + 4]
            rf = pl.multiple_of(b * n_chunks + kt * SUBLANES, SUBLANES)
            rb = pl.multiple_of(b * n_chunks + (n_tiles - 1 - kt) * SUBLANES, SUBLANES)
            wf = w_ref[g, pl.ds(rf, SUBLANES), :]
            wb = w_ref[g, pl.ds(rb, SUBLANES), :]
            cst_f = [cst_ref[g, 0, i] for i in range(N_SCAN_CONST)]
            cst_b = [cst_ref[g, 1, i] for i in range(N_SCAN_CONST)]
            seen_f, cf, cfp = _scan_tile(wf[:, 0:LANES], wf[:, LANES:2 * LANES], cf, cfp, cst_f, row, True)
            seen_b, cb, cbp = _scan_tile(wb[:, 2 * LANES:3 * LANES], wb[:, 3 * LANES:], cb, cbp, cst_b, row, False)
            sq_ref[g, pl.ds(rf, SUBLANES), 0:LANES] = seen_f
            sq_ref[g, pl.ds(rb, SUBLANES), LANES:2 * LANES] = seen_b
            out += [cf, cfp, cb, cbp]
        return tuple(out)

    def batch_body(b, _):
        carry = []
        for g in range(gb):
            for i in range(4):
                carry.append(jnp.broadcast_to(s0_ref[g, i, pl.ds(b, 1), :], (SUBLANES, LANES)))
        carry = tuple(carry)
        if n_tiles <= 2:
            for kt in range(n_tiles):
                carry = tile_step(b, kt, carry)
        else:
            carry = lax.fori_loop(0, n_tiles, lambda kt, c: tile_step(b, kt, c), carry, unroll=2)
        for g in range(gb):
            fin_ref[g, 0, pl.ds(b, 1), :] = carry[4 * g][0:1, :]
            fin_ref[g, 1, pl.ds(b, 1), :] = carry[4 * g + 2][0:1, :]
        return 0

    lax.fori_loop(0, n_batch, batch_body, 0, unroll=2 if n_tiles <= 2 else 1)

    for g in range(gb):
        for r0 in range(0, n_rows, row_blk):
            rows = slice(r0, r0 + row_blk)
            y_ref[g, rows, :] += jnp.dot(sq_ref[g, rows, :].astype(BF16), q_ref[g], preferred_element_type=F32)


def _ssm_core(u, mp, q, cst, dvec, s0, *, l, n_batch, n_chunks):
    n_rows = n_batch * n_chunks
    gb = SSM_GROUP_BLOCK
    g3 = lambda i: (i, 0, 0)
    g4 = lambda i: (i, 0, 0, 0)
    return pl.pallas_call(
        functools.partial(_ssm_kernel, n_batch=n_batch, n_chunks=n_chunks),
        grid=(SSM_GROUPS // gb,),
        in_specs=[
            pl.BlockSpec((gb, n_rows, SSM_ROW), g3),
            pl.BlockSpec((None, gb, SSM_ROW, SSM_ROW + SSM_W_COLS), lambda i: (l, i, 0, 0)),
            pl.BlockSpec((None, gb, 2 * LANES, SSM_ROW), lambda i: (l, i, 0, 0)),
            pl.BlockSpec((None, gb, 2, N_SCAN_CONST, SUBLANES, LANES), lambda i: (l, i, 0, 0, 0, 0)),
            pl.BlockSpec((None, gb, 1, SSM_ROW), lambda i: (l, i, 0, 0)),
            pl.BlockSpec((gb, 4, n_batch, LANES), g4),
        ],
        out_specs=[
            pl.BlockSpec((gb, n_rows, SSM_ROW), g3),
            pl.BlockSpec((gb, 2, n_batch, LANES), g4),
        ],
        out_shape=[
            jax.ShapeDtypeStruct((SSM_GROUPS, n_rows, SSM_ROW), F32),
            jax.ShapeDtypeStruct((SSM_GROUPS, 2, n_batch, LANES), F32),
        ],
        scratch_shapes=[pltpu.VMEM((gb, n_rows, SSM_W_COLS), F32), pltpu.VMEM((gb, n_rows, 2 * LANES), F32)],
        compiler_params=_cparams(("arbitrary",)),
        name="ssm_core",
    )(u, mp, q, cst, dvec, s0)


def _cmul(a, b):
    return a[0] * b[0] - a[1] * b[1], a[0] * b[1] + a[1] * b[0]


_CHUNK_IDX = np.arange(SSM_CHUNK)


def _ssm_exponents():
    t = SSM_CHUNK
    idx = _CHUNK_IDX.astype(np.float32)
    both = lambda a: np.stack([a, a], axis=1)
    lane_step = idx - (t - 1) / 2
    tile_rows = np.arange(1, SUBLANES + 1, dtype=np.float32) * t
    tables = dict(
        one=both(np.ones(1, np.float32)),
        incr=np.stack([t - 1 - idx, idx], axis=1),
        seen=np.stack([idx + 1, t - idx], axis=1),
        resp_in=np.stack([-lane_step, lane_step], axis=1),
        resp_out=np.stack([lane_step, -lane_step], axis=1),
        shift=both(np.asarray(SCAN_SHIFTS, np.float32) * t),
        tile=np.stack([tile_rows, tile_rows[::-1]], axis=1),
    )
    spans, start = {}, 0
    for name, tab in tables.items():
        spans[name] = slice(start, start + len(tab))
        start += len(tab)
    return np.concatenate(list(tables.values()), axis=0), spans


def _ssm_params(p):
    t = SSM_CHUNK
    n_l = p['ssm_lambda_re'].shape[0]
    lam = (p['ssm_lambda_re'].astype(F32), p['ssm_lambda_im'].astype(F32))
    dt = jnp.exp(p['ssm_log_dt'].astype(F32))[..., None]
    arg = (lam[0] * dt, lam[1] * dt)
    expo, span = _ssm_exponents()
    n = jnp.asarray(expo)[:, None, :, None, None]
    mag = jnp.exp(n * arg[0][None])
    pw = (mag * jnp.cos(n * arg[1][None]), mag * jnp.sin(n * arg[1][None]))
    take = lambda name: [x[span[name]] for x in pw]
    lanes_last = lambda xs: [jnp.moveaxis(x, 0, -1) for x in xs]

    lbar = [x[0] for x in take('one')]
    num = (lbar[0] - 1.0, lbar[1])
    den = lam[0] * lam[0] + lam[1] * lam[1]
    zoh = ((num[0] * lam[0] + num[1] * lam[1]) / den, (num[1] * lam[0] - num[0] * lam[1]) / den)
    b_mat = (p['ssm_b_re'].astype(F32), p['ssm_b_im'].astype(F32))
    bbar = _cmul((zoh[0][..., None], zoh[1][..., None]), b_mat)
    c_mat = [jnp.swapaxes(p[k].astype(F32), -1, -2) for k in ('ssm_c_re', 'ssm_c_im')]
    ct = [jnp.tile(x, (1, 1, 1, 1, t)) for x in c_mat]
    bt = [jnp.tile(x, (1, 1, 1, 1, t)) for x in bbar]

    per_lane = lambda name: [jnp.repeat(x, SSM_CH, axis=-1) for x in lanes_last(take(name))]
    xm = _cmul(per_lane('resp_in'), bt)
    ym = _cmul(per_lane('resp_out'), ct)
    resp = jnp.einsum('ldgpx,ldgpy->ldgxy', jnp.concatenate([xm[0], -xm[1]], axis=3),
                      jnp.concatenate([ym[0], ym[1]], axis=3), precision='highest')
    steps_x = np.repeat(_CHUNK_IDX, SSM_CH)
    causal = np.stack([steps_x[None, :] >= steps_x[:, None], steps_x[:, None] >= steps_x[None, :]])
    mats = jnp.sum(jnp.where(jnp.asarray(causal)[None, :, None], resp, 0.0), axis=1)

    incr = _cmul([x[..., None] for x in take('incr')], [x[None] for x in bbar])
    incr = [jnp.transpose(x, (1, 2, 3, 0, 5, 4)) for x in incr]
    pcat = jnp.concatenate([incr[0], incr[1], incr[1], incr[0]], axis=-1)
    pcat = jnp.concatenate([pcat[:, 0], pcat[:, 1]], axis=-1).reshape(n_l, SSM_GROUPS, SSM_ROW, SSM_W_COLS)
    mp = jnp.concatenate([mats, pcat], axis=-1).astype(BF16)

    qm = _cmul(ct, per_lane('seen'))
    q = jnp.concatenate([qm[0][:, 0], -qm[1][:, 0], qm[0][:, 1], -qm[1][:, 1]], axis=2).astype(BF16)

    def forms(name, rows):
        pr, pi = [jnp.moveaxis(x, 0, 3) for x in take(name)]
        f = jnp.stack([jnp.concatenate([pr, pr], -1), jnp.concatenate([-pi, pi], -1),
                       jnp.concatenate([pi, -pi], -1)], axis=4)
        if rows:
            return f
        r = np.arange(SUBLANES)[None, :]
        d = np.repeat(np.asarray(SCAN_SHIFTS), 3)[:, None]
        keep = np.stack([r >= d, r < SUBLANES - d]).astype(np.float32)
        return f.reshape(n_l, 2, SSM_GROUPS, 3 * len(SCAN_SHIFTS), 1, LANES) * keep[None, :, None, :, :, None]
    per_row = jnp.swapaxes(forms('tile', True), 3, 4)
    cst = jnp.swapaxes(jnp.concatenate([forms('shift', False), per_row], axis=3), 1, 2)
    dvec = jnp.tile(p['ssm_d'].astype(F32).reshape(n_l, SSM_GROUPS, 1, SSM_CH), (1, 1, 1, SSM_CHUNK))
    return mp, q, cst, dvec


def _ssm_states_in(state_ssm):
    st = jnp.transpose(state_ssm.astype(F32), (1, 3, 2, 0, 4, 5))
    re, im = st[..., 0], st[..., 1]
    ri, ir = jnp.concatenate([re, im], -1), jnp.concatenate([im, re], -1)
    return jnp.stack([ri[:, :, 0], ir[:, :, 0], ri[:, :, 1], ir[:, :, 1]], axis=2)


def _ssm_states_out(fins):
    fin = jnp.stack(fins, axis=0)
    fin = jnp.stack([fin[..., :SSM_STATE], fin[..., SSM_STATE:]], axis=-1)
    return jnp.transpose(fin, (3, 0, 2, 1, 4, 5))


def _head_lane_mask(hh):
    lane = lax.broadcasted_iota(jnp.int32, (1, LANES), 1)
    return (lane >= hh * NA_HEAD_DIM) & (lane < (hh + 1) * NA_HEAD_DIM)


_NT_DIMS = (((1,), (1,)), ((), ()))


def _ctx_attn_kernel(q_ref, k_ref, v_ref, o_ref):
    lm0 = _head_lane_mask(0)
    for b in range(CTX_ATTN_BATCH):
        rows = slice(b * SEQ, (b + 1) * SEQ)
        for p in range(HEAD_PAIRS):
            qp, kp, vp = q_ref[p, rows, :], k_ref[p, rows, :], v_ref[p, rows, :]
            zero = jnp.zeros_like(qp)
            q2 = jnp.concatenate([jnp.where(lm0, qp, zero), jnp.where(lm0, zero, qp)], axis=0)
            s = lax.dot_general(q2, kp, _NT_DIMS, preferred_element_type=F32)
            e = jnp.exp2(s - jnp.max(s, axis=-1, keepdims=True))
            o = jnp.dot(e.astype(BF16), vp, preferred_element_type=F32) / jnp.sum(e, axis=-1, keepdims=True)
            o_ref[rows, p * LANES:(p + 1) * LANES] = jnp.where(lm0, o[:SEQ, :], o[SEQ:, :]).astype(BF16)


def _ctx_attn(q, k, v, n_batch):
    rows = CTX_ATTN_BATCH * SEQ
    spec = pl.BlockSpec((HEAD_PAIRS, rows, LANES), lambda b: (0, b, 0))
    return pl.pallas_call(
        _ctx_attn_kernel,
        grid=(n_batch // CTX_ATTN_BATCH,),
        in_specs=[spec, spec, spec],
        out_specs=pl.BlockSpec((rows, NA_WIDTH), lambda b: (b, 0)),
        out_shape=jax.ShapeDtypeStruct((n_batch * SEQ, NA_WIDTH), BF16),
        compiler_params=_cparams(("arbitrary",)),
        name="ctx_attn",
    )(q, k, v)


def _na_kernel(q_ref, k_ref, v_ref, kc_ref, vc_ref, t2_ref, rm_ref, o_ref, qb_ref, os_ref, tb_ref):
    rt = pl.program_id(1)
    n_rt = pl.num_programs(1)

    @pl.when((pl.program_id(0) == 0) & (rt == 0))
    def _():
        for h in range(NA_HEADS):
            for n in range(NA_COL_BLOCKS):
                for rl in range(NA_Q_ROWS):
                    lane0 = (NA_OFF_START - rl) * NA_KCOLS
                    tb_ref[h, n, rl * NA_COLS:(rl + 1) * NA_COLS, :] = (
                        t2_ref[h, n, :, lane0:lane0 + NA_K_ROWS * NA_KCOLS])

    r0 = rt * NA_Q_ROWS
    ttype = jnp.where(rt == 0, 0, jnp.where(rt == n_rt - 1, 2, 1))
    rmask = rm_ref[ttype]
    blk_q = NA_Q_ROWS * NA_COLS

    def pair_body(p, carry):
        lm0 = _head_lane_mask(0)
        for n in range(NA_COL_BLOCKS):
            for rl in range(NA_Q_ROWS):
                qrow = q_ref[p, rl * GRID_W + n * NA_COLS:rl * GRID_W + (n + 1) * NA_COLS, :]
                zero = jnp.zeros_like(qrow)
                base = 2 * n * blk_q + rl * NA_COLS
                qb_ref[base:base + NA_COLS, :] = jnp.where(lm0, qrow, zero)
                qb_ref[base + blk_q:base + blk_q + NA_COLS, :] = jnp.where(lm0, zero, qrow)
        kcp, vcp = kc_ref[0, p], vc_ref[0, p]
        n_win = NA_K_ROWS * NA_KCOLS
        for n in range(NA_COL_BLOCKS):
            kparts, vparts = [], []
            for i in range(NA_K_ROWS):
                kr = jnp.clip(r0 - (NA_K_ROWS - NA_Q_ROWS) // 2 + i, 0, GRID_W - 1)
                start = pl.multiple_of(kr * GRID_W + NA_KC0[n], 8)
                kparts.append(k_ref[p, pl.ds(start, NA_KCOLS), :])
                vparts.append(v_ref[p, pl.ds(start, NA_KCOLS), :])
            kall = jnp.concatenate([jnp.concatenate(kparts, axis=0).astype(BF16), kcp], axis=0)
            vall = jnp.concatenate([jnp.concatenate(vparts, axis=0).astype(BF16), vcp], axis=0)
            rows2 = slice(2 * n * blk_q, 2 * (n + 1) * blk_q)
            s = lax.dot_general(qb_ref[rows2, :], kall, _NT_DIMS, preferred_element_type=F32)
            bias = jnp.concatenate([tb_ref[2 * p, n] + rmask, tb_ref[2 * p + 1, n] + rmask], axis=0)
            s = jnp.concatenate([s[:, :n_win] + bias, s[:, n_win:]], axis=-1)
            e = jnp.exp2(s - jnp.max(s, axis=-1, keepdims=True))
            o = jnp.dot(e.astype(BF16), vall, preferred_element_type=F32) / jnp.sum(e, axis=-1, keepdims=True)
            oblk = jnp.where(lm0, o[:blk_q, :], o[blk_q:, :])
            for rl in range(NA_Q_ROWS):
                os_ref[p, rl * GRID_W + n * NA_COLS:rl * GRID_W + (n + 1) * NA_COLS, :] = (
                    oblk[rl * NA_COLS:(rl + 1) * NA_COLS, :])
        return carry

    lax.fori_loop(0, HEAD_PAIRS, pair_body, 0, unroll=True)
    for p in range(HEAD_PAIRS):
        o_ref[:, p * LANES:(p + 1) * LANES] = os_ref[p].astype(BF16)


def _na_attn(q, k, v, kc, vc, tb, rm, n_batch, *, l):
    tile_tok = NA_Q_ROWS * GRID_W
    n_rt = DEC_SEQ // tile_tok
    blk_q, blk_k = NA_Q_ROWS * NA_COLS, NA_K_ROWS * NA_KCOLS
    return pl.pallas_call(
        _na_kernel,
        grid=(n_batch, n_rt),
        in_specs=[
            pl.BlockSpec((HEAD_PAIRS, tile_tok, LANES), lambda b, r: (0, b * n_rt + r, 0)),
            pl.BlockSpec((HEAD_PAIRS, DEC_SEQ, LANES), lambda b, r: (0, b, 0)),
            pl.BlockSpec((HEAD_PAIRS, DEC_SEQ, LANES), lambda b, r: (0, b, 0)),
            pl.BlockSpec((1, None, HEAD_PAIRS, PAST_LEN, LANES), lambda b, r: (b, l, 0, 0, 0)),
            pl.BlockSpec((1, None, HEAD_PAIRS, PAST_LEN, LANES), lambda b, r: (b, l, 0, 0, 0)),
            _layer_spec(tb, l),
            _const_spec(rm.shape),
        ],
        out_specs=pl.BlockSpec((tile_tok, NA_WIDTH), lambda b, r: (b * n_rt + r, 0)),
        out_shape=jax.ShapeDtypeStruct((n_batch * DEC_SEQ, NA_WIDTH), BF16),
        scratch_shapes=[
            pltpu.VMEM((2 * tile_tok, LANES), BF16),
            pltpu.VMEM((HEAD_PAIRS, tile_tok, LANES), F32),
            pltpu.VMEM((NA_HEADS, NA_COL_BLOCKS, blk_q, blk_k), F32)],
        compiler_params=pltpu.CompilerParams(dimension_semantics=("arbitrary", "arbitrary"),
                                             vmem_limit_bytes=NA_VMEM_LIMIT_BYTES),
        name="na_attn",
    )(q, k, v, kc, vc, tb, rm)


def _na_tables(rpb):
    n = np.arange(NA_COL_BLOCKS)
    cc = np.arange(NA_COLS)
    kk = np.arange(NA_KCOLS)
    rl = np.arange(NA_Q_ROWS)
    ki = np.arange(NA_K_ROWS)
    n_dr, n_dc = 2 * NA_MAX_ROWS - 1, 2 * NA_COLS - 1
    c = n[:, None] * NA_COLS + cc[None, :]
    kcol = np.asarray(NA_KC0)[:, None] + kk[None, :]
    dc = np.clip(kcol[:, None, :] - c[:, :, None], -(NA_COLS - 1), NA_COLS - 1) + (NA_COLS - 1)
    cs = np.clip(c - NA_COLS // 2, 0, GRID_W - NA_COLS)
    col_ok = (kcol[:, None, :] >= cs[:, :, None]) & (kcol[:, None, :] < cs[:, :, None] + NA_COLS)
    half = (NA_K_ROWS - NA_Q_ROWS) // 2
    dr = np.clip(ki[None, :] - half - rl[:, None] + (NA_MAX_ROWS - 1), 0, n_dr - 1)
    oh_dc = jnp.asarray(dc[None] == np.arange(n_dc)[:, None, None, None], F32)
    off = np.clip(np.arange(NA_OFF_ROWS) - NA_OFF_PAD, 0, n_dr - 1)
    assert np.array_equal(dr, off[ki[None] + (NA_OFF_START - rl)[:, None]])
    oh_off = jnp.asarray(off[None, :] == np.arange(n_dr)[:, None], F32)
    rows_p = jnp.einsum('lhab,aA->lhAb', rpb.astype(F32) * LOG2_E, oh_off, precision='highest')
    cols = jnp.einsum('lhAb,bnck->lhncAk', rows_p, oh_dc, precision='highest')
    cols = jnp.where(jnp.asarray(col_ok)[None, None, :, :, None, :], cols, NEG_INF)
    tb = cols.reshape(-1, NA_HEADS, NA_COL_BLOCKS, NA_COLS, NA_OFF_ROWS * NA_KCOLS)
    rows = GRID_W
    masks = []
    for r0 in (0, NA_Q_ROWS, rows - NA_Q_ROWS):
        r = r0 + rl
        rs = np.clip(r - NA_MAX_ROWS // 2, 0, rows - NA_MAX_ROWS)
        kr = r0 - half + ki
        ok = (kr[None, :] >= rs[:, None]) & (kr[None, :] < rs[:, None] + NA_MAX_ROWS)
        full = np.broadcast_to(ok[:, None, :, None], (NA_Q_ROWS, NA_COLS, NA_K_ROWS, NA_KCOLS))
        masks.append(np.where(full, 0.0, NEG_INF).reshape(NA_Q_ROWS * NA_COLS, NA_K_ROWS * NA_KCOLS))
    rm = jnp.asarray(np.stack(masks), F32)
    return tb, rm


def _outffn_kernel(x_ref, mod_ref, y_ref, na_ref, gm_ref, gw_ref, gb_ref, wo_ref, g_ref, win_ref, wout_ref, o_ref,
                   yr_ref, ys_ref, cat_ref, xm_ref, h_ref, act_ref):
    gate = mod_ref[0, 5:6, :]
    n_rows = OUT_TILE // SSM_CHUNK
    for g in range(SSM_GROUPS):
        yr_ref[g] = _roll_lanes(y_ref[g], SSM_CH * g)
    for t in range(SSM_CHUNK):
        yt = _merge_segments(lambda src, lt: yr_ref[src, :, lt * LANES:(lt + 1) * LANES], t)
        yt = _roll_lanes(yt, -SSM_CH * t)
        for j in range(SSM_WIDTH // LANES):
            ys_ref[j, pl.ds(t, n_rows, stride=SSM_CHUNK), :] = yt[:, j * LANES:(j + 1) * LANES]
    y = _gelu(jnp.concatenate([ys_ref[j] for j in range(SSM_WIDTH // LANES)], axis=-1))
    z = jnp.dot(y.astype(BF16), gw_ref[...], preferred_element_type=F32) + gb_ref[...]
    cat_ref[:, 0:SSM_WIDTH] = (y * _sigmoid(z)).astype(BF16)
    cat_ref[:, SSM_WIDTH:SSM_WIDTH + NA_WIDTH] = na_ref[...]
    cat_ref[:, SSM_WIDTH + NA_WIDTH:] = gm_ref[...]
    mix = jnp.dot(cat_ref[...], wo_ref[...], preferred_element_type=F32)
    xm_ref[...] = x_ref[...] + gate * mix
    _ffn_kernel(xm_ref, mod_ref, g_ref, win_ref, wout_ref, o_ref, h_ref, act_ref, mod_base=6)


def _outffn(x, mod, y, na, gm, gw, gb, wo, g, w_in, w_out, *, l, seg0, seg_tokens):
    n_tok = x.shape[0]
    n_rows = OUT_TILE // SSM_CHUNK
    tok = lambda width: pl.BlockSpec((OUT_TILE, width), lambda i: (i, 0))
    return pl.pallas_call(
        _outffn_kernel,
        grid=(n_tok // OUT_TILE,),
        in_specs=[
            tok(D_MODEL),
            _mod_spec(l, seg0, seg_tokens, OUT_TILE),
            pl.BlockSpec((SSM_GROUPS, n_rows, SSM_ROW), lambda i: (0, i, 0)),
            tok(NA_WIDTH), tok(GM_WIDTH),
            _layer_spec(gw, l), _layer_spec(gb, l), _layer_spec(wo, l),
            _layer_spec(g, l), _layer_spec(w_in, l), _layer_spec(w_out, l),
        ],
        out_specs=tok(D_MODEL),
        out_shape=jax.ShapeDtypeStruct((n_tok, D_MODEL), F32),
        scratch_shapes=[pltpu.VMEM((SSM_GROUPS, n_rows, SSM_ROW), F32),
                        pltpu.VMEM((SSM_WIDTH // LANES, OUT_TILE, LANES), F32),
                        pltpu.VMEM((OUT_TILE, D_MODEL), BF16),
                        pltpu.VMEM((OUT_TILE, D_MODEL), F32),
                        pltpu.VMEM((OUT_TILE, D_MODEL), BF16), pltpu.VMEM((OUT_TILE, D_FF), BF16)],
        compiler_params=_cparams(("arbitrary",)),
        name="outffn",
    )(x, mod, y, na, gm, gw, gb, wo, g, w_in, w_out)


def _prep(p):
    heads = np.arange(MXU_DIM) // NA_HEAD_DIM
    hsum = jnp.asarray((heads[:, None] == heads[None, :]) / NA_HEAD_DIM, BF16)
    gmb = jnp.repeat(jnp.swapaxes(p['gm_bs'], 1, 2).astype(F32), GM_WIDTH // GM_GROUPS, axis=2)
    tb, rm = _na_tables(p['na_rpb'])
    row = lambda a: a.astype(F32).reshape(DEPTH, 1, -1)
    return dict(
        g1=row(p['norm_ffn1']), g2=row(p['norm_mix']), g3=row(p['norm_ffn2']),
        f1_in=p['ffn1_w_in'].astype(BF16), f1_out=p['ffn1_w_out'].astype(BF16),
        f2_in=p['ffn2_w_in'].astype(BF16), f2_out=p['ffn2_w_out'].astype(BF16),
        w_in=p['w_in'].astype(BF16), w_out=p['w_out'].astype(BF16),
        hsum=hsum,
        qg=row(jnp.tile(p['na_q_norm'], (1, NA_HEADS))), kg=row(jnp.tile(p['na_k_norm'], (1, NA_HEADS))),
        ws=p['gm_ws'].astype(BF16), gmb=gmb,
        ssm=_ssm_params(p),
        glu_w=p['ssm_glu_w'].astype(BF16), glu_b=row(p['ssm_glu_b']),
        tb=tb, rm=rm,
    )


def _pairs_major(t):
    b, n_l, s = t.shape[:3]
    return jnp.transpose(t.reshape(b, n_l, s, HEAD_PAIRS, LANES), (0, 1, 3, 2, 4)).astype(BF16)


def _trunk_layer(x, mod, w, l, *, n_batch, seq_len, caches=None, is_ctx=False, ctx_kv=None, ssm_init=None):
    seg = dict(l=l, seg0=0 if is_ctx else 1, seg_tokens=x.shape[0] if is_ctx else seq_len)
    x = _ffn(x, mod, w['g1'], w['f1_in'], w['f1_out'], mod_base=0, **seg)
    outs = _inproj(x, mod, w['g2'], w['w_in'], w['hsum'], w['qg'], w['kg'], w['ws'], w['gmb'],
                   is_ctx=is_ctx, caches=caches, **seg)
    u, q, k, v, gm = outs[:5]
    s0 = jnp.zeros((SSM_GROUPS, 4, n_batch, LANES), F32) if is_ctx else ssm_init[l]
    y, fin = _ssm_core(u, *w['ssm'], s0, l=l, n_batch=n_batch, n_chunks=seq_len // SSM_CHUNK)
    if is_ctx:
        na = _ctx_attn(q, k, v, n_batch)
    else:
        na = _na_attn(q, k, v, ctx_kv[0], ctx_kv[1], w['tb'], w['rm'], n_batch, l=l)
    x = _outffn(x, mod, y, na, gm, w['glu_w'], w['glu_b'], w['w_out'], w['g3'], w['f2_in'], w['f2_out'], **seg)
    return x, (tuple(outs[5:]), fin)


def kernel(x_prompt, x_sample, c, cache_k, cache_v, state_ssm, c_ctx, w_ada, b_ada, norm_ffn1, ffn1_w_in, ffn1_w_out, norm_mix, w_in, w_out, ssm_lambda_re, ssm_lambda_im, ssm_log_dt, ssm_b_re, ssm_b_im, ssm_c_re, ssm_c_im, ssm_d, ssm_glu_w, ssm_glu_b, na_q_norm, na_k_norm, na_rpb, gm_ws, gm_bs, norm_ffn2, ffn2_w_in, ffn2_w_out):
    p = dict(norm_ffn1=norm_ffn1, ffn1_w_in=ffn1_w_in, ffn1_w_out=ffn1_w_out, norm_mix=norm_mix, w_in=w_in,
             w_out=w_out, ssm_lambda_re=ssm_lambda_re, ssm_lambda_im=ssm_lambda_im, ssm_log_dt=ssm_log_dt,
             ssm_b_re=ssm_b_re, ssm_b_im=ssm_b_im, ssm_c_re=ssm_c_re, ssm_c_im=ssm_c_im, ssm_d=ssm_d,
             ssm_glu_w=ssm_glu_w, ssm_glu_b=ssm_glu_b, na_q_norm=na_q_norm, na_k_norm=na_k_norm, na_rpb=na_rpb,
             gm_ws=gm_ws, gm_bs=gm_bs, norm_ffn2=norm_ffn2, ffn2_w_in=ffn2_w_in, ffn2_w_out=ffn2_w_out)
    batch, dec_batch = x_prompt.shape[0], x_sample.shape[0]
    cond8 = jnp.zeros((8, D_MODEL), F32).at[0].set(c_ctx).at[1:1 + dec_batch].set(c)
    mod = _adaln(cond8, w_ada, b_ada).reshape(DEPTH, 8, N_MOD, D_MODEL)

    xp = x_prompt.reshape(batch * SEQ, D_MODEL)
    xs = x_sample.reshape(dec_batch * DEC_SEQ, D_MODEL)
    w = _prep(p)
    ctx_kv = (_pairs_major(cache_k), _pairs_major(cache_v))
    ssm_init = _ssm_states_in(state_ssm)
    caches, fins = None, []
    for l in range(DEPTH):
        xp, (caches, fin) = _trunk_layer(xp, mod, w, l, n_batch=batch, seq_len=SEQ, caches=caches, is_ctx=True)
        fins.append(fin)
        xs, _ = _trunk_layer(xs, mod, w, l, n_batch=dec_batch, seq_len=DEC_SEQ, ctx_kv=ctx_kv, ssm_init=ssm_init)
    cache_shape = (batch, DEPTH, SEQ, NA_HEADS, NA_HEAD_DIM)
    return (xp.reshape(batch, SEQ, D_MODEL), xs.reshape(dec_batch, DEC_SEQ, D_MODEL),
            caches[0].reshape(cache_shape), caches[1].reshape(cache_shape), _ssm_states_out(fins))
```
